```python
import math
import jax
import jax.numpy as jnp
from jax import lax
import numpy as np

D_MODEL = 1024
BATCH = 4
SEQ = 8192
DEPTH = 2

GRID_W = 64
CTX_LEN = 256
N_ADA = 9
EPS = 1e-6
NEG_INF = -1e30
ROPE_BASE = 10000.0
D_FF = ((8 * D_MODEL // 3 + 127) // 128) * 128

HEAD_DIM = 64
NA_HEADS = 4
NA_WIN_ROWS = 8
NA_WIN_COLS = 16
NA_QBLOCK_COLS = 16
NA_KBLOCK_COLS = 32
D_A = NA_HEADS * HEAD_DIM
GM_GROUPS = 4
GM_WIDTH = 64
GM_CHUNK = 128
D_B = GM_GROUPS * GM_WIDTH
DA_HEADS = 4
DA_QK_DIM = 64
DA_V_DIM = 2 * DA_QK_DIM
DA_QBLOCK = 128
D_C = DA_HEADS * DA_V_DIM
D_QK_C = DA_HEADS * 2 * DA_QK_DIM

D_MIX = D_A + D_B + D_C
SPLIT_SIZES = (D_A, D_A, D_A, D_B, D_B, D_QK_C, D_QK_C, D_C)
SPLIT_POINTS = (D_A, 2 * D_A, 3 * D_A, 3 * D_A + D_B, 3 * D_A + 2 * D_B,
                3 * D_A + 2 * D_B + D_QK_C, 3 * D_A + 2 * D_B + 2 * D_QK_C)
D_IN = 3 * D_A + 2 * D_B + 2 * D_QK_C + D_C

kernel_name = "hybrid_natten_gmlp_diffattn_dit_block"


def rms_norm(x, gain=None, eps=EPS):
    xf = x.astype(jnp.float32)
    y = xf * lax.rsqrt(jnp.mean(xf * xf, axis=-1, keepdims=True) + eps)
    if gain is not None:
        y = y * gain.astype(jnp.float32)
    return y.astype(x.dtype)


def modulate(x, shift, scale):
    return rms_norm(x) * (1 + scale) + shift


def swiglu(h, w1, w3, w2):
    return (jax.nn.silu(h @ w1) * (h @ w3)) @ w2


def axial_rope_tables(n_tokens):
    t = jnp.arange(n_tokens)
    row = (t // GRID_W).astype(jnp.float32)
    col = (t % GRID_W).astype(jnp.float32)
    n_freq = DA_QK_DIM // 4
    freqs = ROPE_BASE ** (-jnp.arange(n_freq, dtype=jnp.float32) / n_freq)
    ang = jnp.concatenate([row[:, None] * freqs, col[:, None] * freqs], axis=-1)
    return jnp.cos(ang), jnp.sin(ang)


def apply_rope(x, cos, sin):
    half = x.shape[-1] // 2
    xf = x.astype(jnp.float32)
    x1, x2 = xf[..., :half], xf[..., half:]
    out = jnp.concatenate([x1 * cos - x2 * sin, x1 * sin + x2 * cos], axis=-1)
    return out.astype(x.dtype)


def split_projection(z):
    B, n, _ = z.shape
    qa, ka, va, ub, vb, qd, kd, vd = jnp.split(z, SPLIT_POINTS, axis=-1)
    heads = lambda t: t.reshape(B, n, NA_HEADS, HEAD_DIM).transpose(0, 2, 1, 3)
    groups = lambda t: jax.nn.gelu(t, approximate=False).reshape(B, n, GM_GROUPS, GM_WIDTH)
    qk_c = lambda t: t.reshape(B, n, DA_HEADS, 2, DA_QK_DIM)
    return (heads(qa), heads(ka), heads(va), groups(ub), groups(vb),
            qk_c(qd), qk_c(kd), vd.reshape(B, n, DA_HEADS, DA_V_DIM))


def neighborhood_attention(q, k, v, kc, vc, rpb):
    B, H, L, d = q.shape
    rows = L // GRID_W
    kh = min(NA_WIN_ROWS, rows)
    scale = d ** -0.5
    qg = q.reshape(B, H, rows, GRID_W, d)
    kg = k.reshape(B, H, rows, GRID_W, d)
    vg = v.reshape(B, H, rows, GRID_W, d)
    r = jnp.arange(rows)
    key_rows = jnp.clip(r - kh // 2, 0, rows - kh)[:, None] + jnp.arange(kh)[None, :]
    dr_idx = key_rows - r[:, None] + (NA_WIN_ROWS - 1)
    s_ctx = (jnp.einsum('bhld,bhcd->bhlc', q, kc).astype(jnp.float32) * scale
             ).reshape(B, H, rows, GRID_W, -1)
    n_win = kh * NA_KBLOCK_COLS
    outs = []
    for j in range(GRID_W // NA_QBLOCK_COLS):
        qc0 = j * NA_QBLOCK_COLS
        kc0 = min(max(qc0 - NA_WIN_COLS // 2, 0), GRID_W - NA_KBLOCK_COLS)
        qcols = np.arange(qc0, qc0 + NA_QBLOCK_COLS)
        kcols = np.arange(kc0, kc0 + NA_KBLOCK_COLS)
        w0 = np.clip(qcols - NA_WIN_COLS // 2, 0, GRID_W - NA_WIN_COLS)
        in_win = (kcols[None, :] >= w0[:, None]) & (kcols[None, :] < w0[:, None] + NA_WIN_COLS)
        mask = np.broadcast_to(in_win[:, None, :], (NA_QBLOCK_COLS, kh, NA_KBLOCK_COLS)
                               ).reshape(NA_QBLOCK_COLS, n_win)
        dc_idx = np.clip(kcols[None, :] - qcols[:, None] + NA_WIN_COLS - 1, 0, 2 * NA_WIN_COLS - 2)
        bias = rpb[:, dr_idx[:, :, None, None], dc_idx[None, None, :, :]]
        bias = bias.transpose(0, 1, 3, 2, 4).reshape(H, rows, NA_QBLOCK_COLS, n_win)
        kb = kg[:, :, :, kc0:kc0 + NA_KBLOCK_COLS][:, :, key_rows].reshape(B, H, rows, n_win, d)
        vb = vg[:, :, :, kc0:kc0 + NA_KBLOCK_COLS][:, :, key_rows].reshape(B, H, rows, n_win, d)
        qb = qg[:, :, :, qc0:qc0 + NA_QBLOCK_COLS]
        s_win = jnp.einsum('bhrqd,bhrkd->bhrqk', qb, kb).astype(jnp.float32) * scale \
            + bias.astype(jnp.float32)[None]
        s_win = jnp.where(mask, s_win, NEG_INF)
        s = jnp.concatenate([s_win, s_ctx[:, :, :, qc0:qc0 + NA_QBLOCK_COLS]], axis=-1)
        p = jax.nn.softmax(s, axis=-1).astype(v.dtype)
        o = jnp.einsum('bhrqk,bhrkd->bhrqd', p[..., :n_win], vb) \
            + jnp.einsum('bhrqc,bhcd->bhrqd', p[..., n_win:], vc)
        outs.append(o)
    return jnp.concatenate(outs, axis=3).reshape(B, H, L, d)


def dense_attention(q, k, v):
    s = jnp.einsum('bhqd,bhkd->bhqk', q, k).astype(jnp.float32) * (q.shape[-1] ** -0.5)
    p = jax.nn.softmax(s, axis=-1).astype(v.dtype)
    return jnp.einsum('bhqk,bhkd->bhqd', p, v)


def chunk_gmlp(u, v, w_s, b_s, g_norm):
    B, n, G, W = u.shape
    vn = rms_norm(v, g_norm).reshape(B, n // GM_CHUNK, GM_CHUNK, G, W)
    s = jnp.einsum('gpq,bcqgw->bcpgw', w_s, vn) + b_s.T[None, None, :, :, None]
    return u * s.reshape(B, n, G, W)


def diff_weighted_values(q, k, v, lam, scale):
    s = jnp.einsum('bqhcd,bkhcd->bhcqk', q, k).astype(jnp.float32) * scale
    p = jax.nn.softmax(s, axis=-1)
    w = p[:, :, 0] - lam * p[:, :, 1]
    return jnp.einsum('bhqk,bkhd->bqhd', w.astype(v.dtype), v)


def diff_attention_latent(q, k, v, kc, vc, lam):
    B, L, H, _, dq = q.shape
    scale = dq ** -0.5
    keys = jnp.concatenate([k, kc], axis=1)
    vals = jnp.concatenate([v, vc], axis=1)
    nb = L // DA_QBLOCK
    q_blocks = q.reshape(B, nb, DA_QBLOCK, H, 2, dq).transpose(1, 0, 2, 3, 4, 5)
    out = lax.map(lambda qb: diff_weighted_values(qb, keys, vals, lam, scale), q_blocks)
    return out.transpose(1, 0, 2, 3, 4).reshape(B, L, H, v.shape[-1])


def merge_mixers(out_a, out_b, out_c, w_out, da_subln, lambda_init):
    B, n = out_b.shape[0], out_b.shape[1]
    a = out_a.transpose(0, 2, 1, 3).reshape(B, n, D_A)
    b = out_b.reshape(B, n, D_B)
    cmix = (rms_norm(out_c, da_subln) * (1.0 - lambda_init)).reshape(B, n, D_C)
    return jnp.concatenate([a, b, cmix], axis=-1) @ w_out


def hybrid_layer(x, xc, c_silu, cc_silu, rope_cos, rope_sin, layer_idx, ctx_out,
                 w_ada, b_ada, ffn1_w1, ffn1_w3, ffn1_w2, w_in, w_out, na_rpb,
                 gm_ws, gm_bs, gm_norm, da_lq1, da_lk1, da_lq2, da_lk2, da_subln,
                 ffn2_w1, ffn2_w3, ffn2_w2):
    mod = jnp.split((c_silu @ w_ada + b_ada)[:, None, :], N_ADA, axis=-1)
    mod_c = jnp.split((cc_silu @ w_ada + b_ada)[None, None, :], N_ADA, axis=-1)

    x = x + 0.5 * mod[2] * swiglu(modulate(x, mod[0], mod[1]), ffn1_w1, ffn1_w3, ffn1_w2)
    xc = xc + 0.5 * mod_c[2] * swiglu(modulate(xc, mod_c[0], mod_c[1]), ffn1_w1, ffn1_w3, ffn1_w2)

    qa, ka, va, ub, vb, qd, kd, vd = split_projection(modulate(x, mod[3], mod[4]) @ w_in)
    qac, kac, vac, ubc, vbc, qdc, kdc, vdc = split_projection(modulate(xc, mod_c[3], mod_c[4]) @ w_in)

    lambda_init = 0.8 - 0.6 * math.exp(-0.3 * layer_idx)
    lam = (jnp.exp(jnp.sum(da_lq1.astype(jnp.float32) * da_lk1.astype(jnp.float32)))
           - jnp.exp(jnp.sum(da_lq2.astype(jnp.float32) * da_lk2.astype(jnp.float32))) + lambda_init)

    out_a = neighborhood_attention(qa, ka, va, kac, vac, na_rpb)
    out_b = chunk_gmlp(ub, vb, gm_ws, gm_bs, gm_norm)
    cos = rope_cos[None, :, None, None, :]
    sin = rope_sin[None, :, None, None, :]
    out_c = diff_attention_latent(apply_rope(qd, cos, sin), apply_rope(kd, cos, sin), vd, kdc, vdc, lam)
    x = x + mod[5] * merge_mixers(out_a, out_b, out_c, w_out, da_subln, lambda_init)

    if ctx_out:
        out_ac = dense_attention(qac, kac, vac)
        out_bc = chunk_gmlp(ubc, vbc, gm_ws, gm_bs, gm_norm)
        out_cc = diff_weighted_values(qdc, kdc, vdc, lam, DA_QK_DIM ** -0.5)
        xc = xc + mod_c[5] * merge_mixers(out_ac, out_bc, out_cc, w_out, da_subln, lambda_init)
        xc = xc + 0.5 * mod_c[8] * swiglu(modulate(xc, mod_c[6], mod_c[7]), ffn2_w1, ffn2_w3, ffn2_w2)

    x = x + 0.5 * mod[8] * swiglu(modulate(x, mod[6], mod[7]), ffn2_w1, ffn2_w3, ffn2_w2)
    return x, xc


def setup_inputs(seed: int = 0) -> dict:
    key = jax.random.key(seed)
    ks = jax.random.split(key, 24)
    f32 = jnp.float32

    def nrm(k, shape, scale):
        return jax.random.normal(k, shape, f32) * scale

    return {
        "x": nrm(ks[0], (BATCH, SEQ, D_MODEL), 1.0),
        "c": nrm(ks[1], (BATCH, D_MODEL), 1.0),
        "ctx": nrm(ks[2], (BATCH, CTX_LEN, D_MODEL), 1.0),
        "c_ctx": nrm(ks[3], (D_MODEL,), 1.0),
        "w_ada": nrm(ks[4], (DEPTH, D_MODEL, N_ADA * D_MODEL), D_MODEL ** -0.5),
        "b_ada": nrm(ks[5], (DEPTH, N_ADA * D_MODEL), 0.02),
        "ffn1_w1": nrm(ks[6], (DEPTH, D_MODEL, D_FF), D_MODEL ** -0.5),
        "ffn1_w3": nrm(ks[7], (DEPTH, D_MODEL, D_FF), D_MODEL ** -0.5),
        "ffn1_w2": nrm(ks[8], (DEPTH, D_FF, D_MODEL), D_FF ** -0.5),
        "w_in": nrm(ks[9], (DEPTH, D_MODEL, D_IN), D_MODEL ** -0.5),
        "w_out": nrm(ks[10], (DEPTH, D_MIX, D_MODEL), D_MIX ** -0.5),
        "na_rpb": nrm(ks[11], (DEPTH, NA_HEADS, 2 * NA_WIN_ROWS - 1, 2 * NA_WIN_COLS - 1), 0.5),
        "gm_ws": nrm(ks[12], (DEPTH, GM_GROUPS, GM_CHUNK, GM_CHUNK), GM_CHUNK ** -0.5),
        "gm_bs": 1.0 + nrm(ks[13], (DEPTH, GM_GROUPS, GM_CHUNK), 0.02),
        "gm_norm": 1.0 + nrm(ks[14], (DEPTH, GM_GROUPS, GM_WIDTH), 0.02),
        "da_lq1": nrm(ks[15], (DEPTH, DA_QK_DIM), 0.1),
        "da_lk1": nrm(ks[16], (DEPTH, DA_QK_DIM), 0.1),
        "da_lq2": nrm(ks[17], (DEPTH, DA_QK_DIM), 0.1),
        "da_lk2": nrm(ks[18], (DEPTH, DA_QK_DIM), 0.1),
        "da_subln": 1.0 + nrm(ks[19], (DEPTH, DA_V_DIM), 0.02),
        "ffn2_w1": nrm(ks[20], (DEPTH, D_MODEL, D_FF), D_MODEL ** -0.5),
        "ffn2_w3": nrm(ks[21], (DEPTH, D_MODEL, D_FF), D_MODEL ** -0.5),
        "ffn2_w2": nrm(ks[22], (DEPTH, D_FF, D_MODEL), D_FF ** -0.5),
        "final_norm": 1.0 + nrm(ks[23], (D_MODEL,), 0.02),
    }


def reference(x, c, ctx, c_ctx, w_ada, b_ada, ffn1_w1, ffn1_w3, ffn1_w2, w_in, w_out, na_rpb,
              gm_ws, gm_bs, gm_norm, da_lq1, da_lk1, da_lq2, da_lk2, da_subln,
              ffn2_w1, ffn2_w3, ffn2_w2, final_norm):
    n_tokens = x.shape[1]
    rope_cos, rope_sin = axial_rope_tables(n_tokens)
    c_silu = jax.nn.silu(c)
    cc_silu = jax.nn.silu(c_ctx)
    xc = ctx
    for l in range(DEPTH):
        x, xc = hybrid_layer(
            x, xc, c_silu, cc_silu, rope_cos, rope_sin, l, l < DEPTH - 1,
            w_ada[l], b_ada[l], ffn1_w1[l], ffn1_w3[l], ffn1_w2[l], w_in[l], w_out[l], na_rpb[l],
            gm_ws[l], gm_bs[l], gm_norm[l], da_lq1[l], da_lk1[l], da_lq2[l], da_lk2[l], da_subln[l],
            ffn2_w1[l], ffn2_w3[l], ffn2_w2[l])
    return rms_norm(x, final_norm)
```

```python
import functools
import math

import numpy as np
import jax
import jax.numpy as jnp
from jax import lax
from jax.experimental import pallas as pl
from jax.experimental.pallas import tpu as pltpu

F32 = jnp.float32
BF16 = jnp.bfloat16

D_MODEL = 1024
DEPTH = 2
GRID_W = 64
CTX_LEN = 256
N_ADA = 9
EPS = 1e-6
NEG_INF = -1e30
ROPE_BASE = 10000.0
D_FF = 2816
HEAD_DIM = 64
NA_HEADS = 4
NA_WIN_ROWS = 8
NA_WIN_COLS = 16
D_A = NA_HEADS * HEAD_DIM
GM_GROUPS = 4
GM_WIDTH = 64
GM_CHUNK = 128
D_B = GM_GROUPS * GM_WIDTH
DA_HEADS = 4
DA_QK_DIM = 64
DA_V_DIM = 128
D_C = DA_HEADS * DA_V_DIM
D_QK_C = DA_HEADS * 2 * DA_QK_DIM
D_IN = 3 * D_A + 2 * D_B + 2 * D_QK_C + D_C
QK_SCALE = HEAD_DIM ** -0.5

LANES = 128
VMEM_BYTES_V7X = 64 * 1024 * 1024
VMEM_LIMIT = VMEM_BYTES_V7X - 8 * 1024 * 1024

TOKEN_TILE = 256
ATT_Q_TILE = 256
NA_Q_ROWS = ATT_Q_TILE // GRID_W
NA_K_ROWS = NA_Q_ROWS + NA_WIN_ROWS
NA_K_TOK = NA_K_ROWS * GRID_W
DA_KV_CHUNK = 768
FF_CHUNKS = ((0, 768), (768, 1536), (1536, 2304), (2304, 2816))


def _params(sem):
    return pltpu.CompilerParams(dimension_semantics=sem, vmem_limit_bytes=VMEM_LIMIT)


def _resident(shape):
    nd = len(shape)
    return pl.BlockSpec(shape, lambda *_: (0,) * nd, pipeline_mode=pl.Buffered(1))


def _rms(x):
    return x * lax.rsqrt(jnp.mean(x * x, axis=-1, keepdims=True) + EPS)


def _mod_row(mod_ref, idx, is_ctx):
    return jnp.where(is_ctx, mod_ref[0, 1, idx:idx + 1, :], mod_ref[0, 0, idx:idx + 1, :])


def _is_ctx_rows(tm, n_latent):
    row0 = pl.program_id(1) * tm
    return (row0 + lax.broadcasted_iota(jnp.int32, (tm, 1), 0)) >= n_latent


def _ada_kernel(c_ref, w_ref, b_ref, o_ref):
    c = c_ref[...]
    cs = c / (1.0 + jnp.exp(-c))
    o_ref[0] = jnp.dot(cs, w_ref[0], preferred_element_type=F32,
                       precision=lax.Precision.HIGHEST) + b_ref[0]


def _ada(cvec, w_ada, b_ada):
    depth, d, n = w_ada.shape
    tn = 1024
    return pl.pallas_call(
        _ada_kernel,
        grid=(depth, n // tn),
        in_specs=[
            pl.BlockSpec((8, d), lambda l, j: (0, 0)),
            pl.BlockSpec((1, d, tn), lambda l, j: (l, 0, j)),
            pl.BlockSpec((1, 1, tn), lambda l, j: (l, 0, j)),
        ],
        out_specs=pl.BlockSpec((1, 8, tn), lambda l, j: (l, 0, j)),
        out_shape=jax.ShapeDtypeStruct((depth, 8, n), F32),
        name="ada",
        compiler_params=_params(("arbitrary", "arbitrary")),
    )(cvec, w_ada, b_ada.reshape(depth, 1, n))


def _ffn_kernel(x_ref, mod_ref, w1_ref, w3_ref, w2_ref, *rest, tm, n_latent, mi, final):
    if final:
        gain_ref, o_ref = rest
    else:
        (o_ref,) = rest
    x = x_ref[0]
    is_ctx = _is_ctx_rows(tm, n_latent)
    shift = _mod_row(mod_ref, mi, is_ctx)
    scale = _mod_row(mod_ref, mi + 1, is_ctx)
    gate = _mod_row(mod_ref, mi + 2, is_ctx)
    h = (_rms(x) * (1.0 + scale) + shift).astype(BF16)
    acc = jnp.zeros((tm, D_MODEL), F32)
    for c0, c1 in FF_CHUNKS:
        a = jnp.dot(h, w1_ref[:, c0:c1], preferred_element_type=F32)
        g = jnp.dot(h, w3_ref[:, c0:c1], preferred_element_type=F32)
        y = (a / (1.0 + jnp.exp(-a)) * g).astype(BF16)
        acc = acc + jnp.dot(y, w2_ref[c0:c1, :], preferred_element_type=F32)
    out = x + 0.5 * gate * acc
    if final:
        out = _rms(out) * gain_ref[...]
    o_ref[0] = out


def _ffn(x, modt, w1, w3, w2, *, n_latent, mi, final_gain=None):
    b, t, d = x.shape
    tm = TOKEN_TILE
    final = final_gain is not None
    t_out = n_latent if final else t
    in_specs = [
        pl.BlockSpec((1, tm, d), lambda i, j: (i, j, 0)),
        pl.BlockSpec((1, 2, N_ADA, d), lambda i, j: (i, 0, 0, 0)),
        _resident((d, D_FF)), _resident((d, D_FF)), _resident((D_FF, d)),
    ]
    args = [x, modt, w1, w3, w2]
    if final:
        in_specs.append(_resident((1, d)))
        args.append(final_gain.reshape(1, d))
    return pl.pallas_call(
        functools.partial(_ffn_kernel, tm=tm, n_latent=n_latent, mi=mi, final=final),
        grid=(b, t_out // tm),
        in_specs=in_specs,
        out_specs=pl.BlockSpec((1, tm, d), lambda i, j: (i, j, 0)),
        out_shape=jax.ShapeDtypeStruct((b, t_out, d), F32),
        name="ffn_final" if final else "ffn",
        compiler_params=_params(("arbitrary", "arbitrary")),
    )(*args)


def _swap_rope_halves(x, first_half):
    return jnp.where(first_half, pltpu.roll(x, 96, 1), pltpu.roll(x, 32, 1))


def _proj_in_kernel(x_ref, mod_ref, w_ref, cos_ref, sin_ref, ws_ref, bs_ref, gn_ref, gmat_ref,
                    za_ref, ob_ref, zd_ref, *, tm, n_latent):
    x = x_ref[0]
    is_ctx = _is_ctx_rows(tm, n_latent)
    shift = _mod_row(mod_ref, 3, is_ctx)
    scale = _mod_row(mod_ref, 4, is_ctx)
    h = (_rms(x) * (1.0 + scale) + shift).astype(BF16)
    z = jnp.dot(h, w_ref[...], preferred_element_type=F32)

    za_ref[0, :, 0:D_A] = (z[:, 0:D_A] * QK_SCALE).astype(BF16)
    za_ref[0, :, D_A:3 * D_A] = z[:, D_A:3 * D_A].astype(BF16)

    o = 3 * D_A
    u = z[:, o:o + D_B]
    v = z[:, o + D_B:o + 2 * D_B]
    u = 0.5 * u * (1.0 + lax.erf(u * (2.0 ** -0.5)))
    v = 0.5 * v * (1.0 + lax.erf(v * (2.0 ** -0.5)))
    v2 = v * v
    v2_hi = v2.astype(BF16)
    v2_lo = (v2 - v2_hi.astype(F32)).astype(BF16)
    ms = (jnp.dot(v2_hi, gmat_ref[...], preferred_element_type=F32)
          + jnp.dot(v2_lo, gmat_ref[...], preferred_element_type=F32)) * (1.0 / GM_WIDTH)
    vn = (v * lax.rsqrt(ms + EPS) * gn_ref[...]).astype(BF16)
    lane_group = lax.broadcasted_iota(jnp.int32, (1, D_B), 1) // GM_WIDTH
    for c in range(tm // GM_CHUNK):
        rows = slice(c * GM_CHUNK, (c + 1) * GM_CHUNK)
        r = jnp.dot(ws_ref[...], vn[rows, :], preferred_element_type=F32)
        s = bs_ref[...]
        for g in range(GM_GROUPS):
            s = s + jnp.where(lane_group == g, r[g * GM_CHUNK:(g + 1) * GM_CHUNK, :], 0.0)
        ob_ref[0, rows, :] = (u[rows, :] * s).astype(BF16)

    o = 3 * D_A + 2 * D_B
    cos = cos_ref[...]
    sin = sin_ref[...]
    first_half = (lax.broadcasted_iota(jnp.int32, (1, LANES), 1) % DA_QK_DIM) < (DA_QK_DIM // 2)
    for j in range(2 * D_QK_C // LANES):
        zz = z[:, o + j * LANES:o + (j + 1) * LANES]
        r = zz * cos + _swap_rope_halves(zz, first_half) * sin
        if j < D_QK_C // LANES:
            r = r * QK_SCALE
        zd_ref[0, :, j * LANES:(j + 1) * LANES] = r.astype(BF16)
    o = o + 2 * D_QK_C
    zd_ref[0, :, 2 * D_QK_C:] = z[:, o:].astype(BF16)


def _proj_in(x, modt, w_in, cos_t, sin_t, ws_stack, bs_full, gn, gmat, *, n_latent):
    b, t, d = x.shape
    tm = TOKEN_TILE
    return pl.pallas_call(
        functools.partial(_proj_in_kernel, tm=tm, n_latent=n_latent),
        grid=(b, t // tm),
        in_specs=[
            pl.BlockSpec((1, tm, d), lambda i, j: (i, j, 0)),
            pl.BlockSpec((1, 2, N_ADA, d), lambda i, j: (i, 0, 0, 0)),
            _resident((d, D_IN)),
            pl.BlockSpec((tm, LANES), lambda i, j: (j, 0)),
            pl.BlockSpec((tm, LANES), lambda i, j: (j, 0)),
            _resident((GM_GROUPS * GM_CHUNK, GM_CHUNK)),
            _resident((GM_CHUNK, D_B)),
            _resident((1, D_B)),
            _resident((D_B, D_B)),
        ],
        out_specs=[
            pl.BlockSpec((1, tm, 3 * D_A), lambda i, j: (i, j, 0)),
            pl.BlockSpec((1, tm, D_B), lambda i, j: (i, j, 0)),
            pl.BlockSpec((1, tm, 2 * D_QK_C + D_C), lambda i, j: (i, j, 0)),
        ],
        out_shape=[
            jax.ShapeDtypeStruct((b, t, 3 * D_A), BF16),
            jax.ShapeDtypeStruct((b, t, D_B), BF16),
            jax.ShapeDtypeStruct((b, t, 2 * D_QK_C + D_C), BF16),
        ],
        name="proj_in",
        compiler_params=_params(("arbitrary", "arbitrary")),
    )(x, modt, w_in, cos_t, sin_t, ws_stack, bs_full, gn, gmat)


_NT = (((1,), (1,)), ((), ()))


def _softmax_pv(parts):
    m = parts[0][0].max(axis=-1, keepdims=True)
    for s, _ in parts[1:]:
        m = jnp.maximum(m, s.max(axis=-1, keepdims=True))
    l = 0.0
    o = 0.0
    for s, v in parts:
        p = jnp.exp(s - m)
        l = l + p.sum(axis=-1, keepdims=True)
        o = o + jnp.dot(p.astype(BF16), v, preferred_element_type=F32)
    return o / l


def _na_kernel(q_ref, k_ref, v_ref, bias_ref, o_ref, *, n_latent):
    g = pl.program_id(2)
    n_groups = n_latent // ATT_Q_TILE
    rows = n_latent // GRID_W
    lane = lax.broadcasted_iota(jnp.int32, (1, LANES), 1)
    head_lanes = (lane < HEAD_DIM, lane >= HEAD_DIM)
    q = q_ref[0]
    kc = k_ref[0, n_latent:, :]
    vc = v_ref[0, n_latent:, :]

    def heads(parts_of):
        outs = []
        for hh in range(2):
            qh = jnp.where(head_lanes[hh], q, jnp.zeros_like(q))
            outs.append(_softmax_pv(parts_of(hh, qh)))
        o_ref[0] = jnp.where(head_lanes[0], outs[0], outs[1]).astype(BF16)

    @pl.when(g < n_groups)
    def _():
        base_row = jnp.clip(g * NA_Q_ROWS - NA_WIN_ROWS // 2, 0, rows - NA_K_ROWS)
        base = pl.multiple_of(base_row * GRID_W, GRID_W)
        kw = k_ref[0, pl.ds(base, NA_K_TOK), :]
        vw = v_ref[0, pl.ds(base, NA_K_TOK), :]

        def parts_of(hh, qh):
            sw = lax.dot_general(qh, kw, _NT, preferred_element_type=F32) + bias_ref[0, hh]
            sc = lax.dot_general(qh, kc, _NT, preferred_element_type=F32)
            return [(sw, vw), (sc, vc)]

        heads(parts_of)

    @pl.when(g == n_groups)
    def _():
        heads(lambda hh, qh: [(lax.dot_general(qh, kc, _NT, preferred_element_type=F32), vc)])


def _na_attention(za, bias, *, n_latent):
    b, t, _ = za.shape
    n_groups = n_latent // ATT_Q_TILE
    assert t == n_latent + CTX_LEN and CTX_LEN == ATT_Q_TILE
    assert n_groups >= 3 and n_latent // GRID_W >= NA_K_ROWS

    def bias_map(i, hp, g):
        kind = jnp.where(g == 0, 0, jnp.where(g >= n_groups - 1, 2, 1))
        return (kind, hp, 0, 0)

    return pl.pallas_call(
        functools.partial(_na_kernel, n_latent=n_latent),
        grid=(b, NA_HEADS // 2, n_groups + 1),
        in_specs=[
            pl.BlockSpec((1, ATT_Q_TILE, LANES), lambda i, hp, g: (i, g, hp)),
            pl.BlockSpec((1, t, LANES), lambda i, hp, g: (i, 0, 2 + hp)),
            pl.BlockSpec((1, t, LANES), lambda i, hp, g: (i, 0, 4 + hp)),
            pl.BlockSpec((1, 2, ATT_Q_TILE, NA_K_TOK), bias_map),
        ],
        out_specs=pl.BlockSpec((1, ATT_Q_TILE, LANES), lambda i, hp, g: (i, g, hp)),
        out_shape=jax.ShapeDtypeStruct((b, t, D_A), BF16),
        name="na_attention",
        compiler_params=_params(("arbitrary", "arbitrary", "arbitrary")),
    )(za, za, za, bias)


def _na_bias_tables(rpb, n_latent):
    rows = n_latent // GRID_W
    n_groups = n_latent // ATT_Q_TILE
    qc = np.arange(GRID_W)
    kcol = np.arange(GRID_W)
    w0 = np.clip(qc - NA_WIN_COLS // 2, 0, GRID_W - NA_WIN_COLS)
    col_ok = (kcol[None, :] >= w0[:, None]) & (kcol[None, :] < w0[:, None] + NA_WIN_COLS)
    dc = np.clip(kcol[None, :] - qc[:, None] + NA_WIN_COLS - 1, 0, 2 * NA_WIN_COLS - 2)
    dr_all, ok_all = [], []
    for g in (0, 1, n_groups - 1):
        r = g * NA_Q_ROWS + np.arange(NA_Q_ROWS)
        base = int(np.clip(g * NA_Q_ROWS - NA_WIN_ROWS // 2, 0, rows - NA_K_ROWS))
        kr = base + np.arange(NA_K_ROWS)
        start = np.clip(r - NA_WIN_ROWS // 2, 0, rows - NA_WIN_ROWS)
        row_ok = (kr[None, :] >= start[:, None]) & (kr[None, :] < start[:, None] + NA_WIN_ROWS)
        dr = np.clip(kr[None, :] - r[:, None] + NA_WIN_ROWS - 1, 0, 2 * NA_WIN_ROWS - 2)
        shape = (NA_Q_ROWS, GRID_W, NA_K_ROWS, GRID_W)
        dr_all.append(np.broadcast_to(dr[:, None, :, None], shape).reshape(ATT_Q_TILE, NA_K_TOK))
        ok_all.append((row_ok[:, None, :, None] & col_ok[None, :, None, :]).reshape(ATT_Q_TILE, NA_K_TOK))
    dc_full = np.broadcast_to(dc[None, :, None, :], (NA_Q_ROWS, GRID_W, NA_K_ROWS, GRID_W)
                              ).reshape(ATT_Q_TILE, NA_K_TOK)
    dr_idx = np.stack(dr_all)
    ok = np.stack(ok_all)
    vals = rpb[:, dr_idx, dc_full[None]]
    vals = jnp.where(ok[None], vals.astype(F32), NEG_INF)
    return jnp.transpose(vals, (1, 0, 2, 3))


def _diff_kernel(q_ref, k_ref, v_ref, lqk_ref, subln_ref, o_ref, *, n_latent, lambda_init):
    i = pl.program_id(2)
    n_q_tiles = n_latent // ATT_Q_TILE
    lane = lax.broadcasted_iota(jnp.int32, (1, LANES), 1)
    q = q_ref[0]
    qs = (jnp.where(lane < DA_QK_DIM, q, jnp.zeros_like(q)),
          jnp.where(lane >= DA_QK_DIM, q, jnp.zeros_like(q)))
    lqk = lqk_ref[...]
    lam = (jnp.exp(jnp.sum(lqk[0:1] * lqk[1:2], axis=-1, keepdims=True))
           - jnp.exp(jnp.sum(lqk[2:3] * lqk[3:4], axis=-1, keepdims=True)) + lambda_init)

    def step(k, v, state):
        new = []
        for c in range(2):
            m, l, acc = state[c]
            s = lax.dot_general(qs[c], k, _NT, preferred_element_type=F32)
            m_new = jnp.maximum(m, s.max(axis=-1, keepdims=True))
            alpha = jnp.exp(m - m_new)
            p = jnp.exp(s - m_new)
            l = alpha * l + p.sum(axis=-1, keepdims=True)
            acc = alpha * acc + jnp.dot(p.astype(BF16), v, preferred_element_type=F32)
            new.append((m_new, l, acc))
        return tuple(new)

    def init():
        one = (jnp.full((ATT_Q_TILE, 1), NEG_INF, F32), jnp.zeros((ATT_Q_TILE, 1), F32),
               jnp.zeros((ATT_Q_TILE, DA_V_DIM), F32))
        return (one, one)

    def finish(state):
        (_, l1, a1), (_, l2, a2) = state
        o = a1 / l1 - lam * (a2 / l2)
        o_ref[0] = (_rms(o) * subln_ref[...] * (1.0 - lambda_init)).astype(BF16)

    @pl.when(i < n_q_tiles)
    def _():
        def body(c, state):
            start = pl.multiple_of(c * DA_KV_CHUNK, DA_KV_CHUNK)
            return step(k_ref[0, pl.ds(start, DA_KV_CHUNK), :],
                        v_ref[0, pl.ds(start, DA_KV_CHUNK), :], state)
        finish(lax.fori_loop(0, (n_latent + CTX_LEN) // DA_KV_CHUNK, body, init()))

    @pl.when(i == n_q_tiles)
    def _():
        finish(step(k_ref[0, n_latent:, :], v_ref[0, n_latent:, :], init()))


def _diff_attention(zd, lqk, subln, *, n_latent, lambda_init):
    b, t, _ = zd.shape
    n_q_tiles = n_latent // ATT_Q_TILE
    return pl.pallas_call(
        functools.partial(_diff_kernel, n_latent=n_latent, lambda_init=lambda_init),
        grid=(b, DA_HEADS, n_q_tiles + 1),
        in_specs=[
            pl.BlockSpec((1, ATT_Q_TILE, LANES), lambda bi, h, i: (bi, i, h)),
            pl.BlockSpec((1, t, LANES), lambda bi, h, i: (bi, 0, DA_HEADS + h)),
            pl.BlockSpec((1, t, LANES), lambda bi, h, i: (bi, 0, 2 * DA_HEADS + h)),
            pl.BlockSpec((4, DA_QK_DIM), lambda bi, h, i: (0, 0)),
            pl.BlockSpec((1, DA_V_DIM), lambda bi, h, i: (0, 0)),
        ],
        out_specs=pl.BlockSpec((1, ATT_Q_TILE, LANES), lambda bi, h, i: (bi, i, h)),
        out_shape=jax.ShapeDtypeStruct((b, t, D_C), BF16),
        name="diff_attention",
        compiler_params=_params(("arbitrary", "arbitrary", "arbitrary")),
    )(zd, zd, zd, lqk, subln)


def _proj_out_kernel(x_ref, mod_ref, oa_ref, ob_ref, oc_ref, w_ref, o_ref, *, tm, n_latent):
    is_ctx = _is_ctx_rows(tm, n_latent)
    gate = _mod_row(mod_ref, 5, is_ctx)
    cat = jnp.concatenate([oa_ref[0], ob_ref[0], oc_ref[0]], axis=-1)
    o_ref[0] = x_ref[0] + gate * jnp.dot(cat, w_ref[...], preferred_element_type=F32)


def _proj_out(x, modt, oa, ob, oc, w_out, *, n_latent):
    b, t, d = x.shape
    tm = TOKEN_TILE
    tile = lambda width: pl.BlockSpec((1, tm, width), lambda i, j: (i, j, 0))
    return pl.pallas_call(
        functools.partial(_proj_out_kernel, tm=tm, n_latent=n_latent),
        grid=(b, t // tm),
        in_specs=[tile(d), pl.BlockSpec((1, 2, N_ADA, d), lambda i, j: (i, 0, 0, 0)),
                  tile(D_A), tile(D_B), tile(D_C), _resident((d, d))],
        out_specs=tile(d),
        out_shape=jax.ShapeDtypeStruct((b, t, d), F32),
        name="proj_out",
        compiler_params=_params(("arbitrary", "arbitrary")),
    )(x, modt, oa, ob, oc, w_out)


def _rope_tables(n_latent):
    tok = jnp.arange(n_latent)
    row = (tok // GRID_W).astype(F32)
    col = (tok % GRID_W).astype(F32)
    n_freq = DA_QK_DIM // 4
    freqs = ROPE_BASE ** (-jnp.arange(n_freq, dtype=F32) / n_freq)
    ang = jnp.concatenate([row[:, None] * freqs, col[:, None] * freqs], axis=-1)
    cos, sin = jnp.cos(ang), jnp.sin(ang)
    cos_t = jnp.concatenate([cos, cos, cos, cos], axis=-1)
    sin_t = jnp.concatenate([-sin, sin, -sin, sin], axis=-1)
    cos_t = jnp.concatenate([cos_t, jnp.ones((CTX_LEN, LANES), F32)], axis=0)
    sin_t = jnp.concatenate([sin_t, jnp.zeros((CTX_LEN, LANES), F32)], axis=0)
    return cos_t, sin_t


def kernel(x, c, ctx, c_ctx, w_ada, b_ada, ffn1_w1, ffn1_w3, ffn1_w2, w_in, w_out, na_rpb,
           gm_ws, gm_bs, gm_norm, da_lq1, da_lk1, da_lq2, da_lk2, da_subln,
           ffn2_w1, ffn2_w3, ffn2_w2, final_norm):
    b, n_latent, d = x.shape
    assert d == D_MODEL and ctx.shape[1] == CTX_LEN and b < 8
    assert (n_latent + CTX_LEN) % DA_KV_CHUNK == 0 and n_latent % ATT_Q_TILE == 0

    cvec = jnp.zeros((8, d), F32).at[:b].set(c).at[b].set(c_ctx)
    mod = _ada(cvec, w_ada, b_ada).reshape(DEPTH, 8, N_ADA, d)
    cos_t, sin_t = _rope_tables(n_latent)
    gmat = jnp.asarray(np.kron(np.eye(GM_GROUPS), np.ones((GM_WIDTH, GM_WIDTH))), BF16)

    xs = jnp.concatenate([x, ctx], axis=1)
    for l in range(DEPTH):
        last = l == DEPTH - 1
        lambda_init = 0.8 - 0.6 * math.exp(-0.3 * l)
        modt = jnp.stack([mod[l, :b], jnp.broadcast_to(mod[l, b], (b, N_ADA, d))], axis=1)
        bias = _na_bias_tables(na_rpb[l], n_latent)
        ws_stack = gm_ws[l].reshape(GM_GROUPS * GM_CHUNK, GM_CHUNK).astype(BF16)
        bs_full = jnp.repeat(gm_bs[l].T, GM_WIDTH, axis=1)
        gn = gm_norm[l].reshape(1, D_B)
        lqk = jnp.stack([da_lq1[l], da_lk1[l], da_lq2[l], da_lk2[l]])

        xs = _ffn(xs, modt, ffn1_w1[l].astype(BF16), ffn1_w3[l].astype(BF16),
                  ffn1_w2[l].astype(BF16), n_latent=n_latent, mi=0)
        za, ob, zd = _proj_in(xs, modt, w_in[l].astype(BF16), cos_t, sin_t, ws_stack, bs_full,
                              gn, gmat, n_latent=n_latent)
        oa = _na_attention(za, bias, n_latent=n_latent)
        oc = _diff_attention(zd, lqk, da_subln[l].reshape(1, DA_V_DIM),
                             n_latent=n_latent, lambda_init=lambda_init)
        xs = _proj_out(xs, modt, oa, ob, oc, w_out[l].astype(BF16), n_latent=n_latent)
        xs = _ffn(xs, modt, ffn2_w1[l].astype(BF16), ffn2_w3[l].astype(BF16),
                  ffn2_w2[l].astype(BF16), n_latent=n_latent, mi=6,
                  final_gain=final_norm if last else None)
    return xs
```

```python
import functools
import math

import numpy as np
import jax
import jax.numpy as jnp
from jax import lax
from jax.experimental import pallas as pl
from jax.experimental.pallas import tpu as pltpu

F32 = jnp.float32
BF16 = jnp.bfloat16

D_MODEL = 1024
DEPTH = 2
GRID_W = 64
CTX_LEN = 256
N_ADA = 9
EPS = 1e-6
NEG_INF = -1e30
ROPE_BASE = 10000.0
D_FF = 2816
HEAD_DIM = 64
NA_HEADS = 4
NA_WIN_ROWS = 8
NA_WIN_COLS = 16
D_A = NA_HEADS * HEAD_DIM
GM_GROUPS = 4
GM_WIDTH = 64
GM_CHUNK = 128
D_B = GM_GROUPS * GM_WIDTH
DA_HEADS = 4
DA_QK_DIM = 64
DA_V_DIM = 128
D_C = DA_HEADS * DA_V_DIM
D_QK_C = DA_HEADS * 2 * DA_QK_DIM
D_IN = 3 * D_A + 2 * D_B + 2 * D_QK_C + D_C
QK_SCALE = HEAD_DIM ** -0.5

LANES = 128
VMEM_BYTES_V7X = 64 * 1024 * 1024
VMEM_LIMIT = VMEM_BYTES_V7X - 8 * 1024 * 1024

TOKEN_TILE = 256
ATT_Q_TILE = 256
NA_Q_ROWS = ATT_Q_TILE // GRID_W
NA_K_ROWS = NA_Q_ROWS + NA_WIN_ROWS
NA_K_TOK = NA_K_ROWS * GRID_W
DA_KV_CHUNK = 768
FF_CHUNKS = ((0, 768), (768, 1536), (1536, 2304), (2304, 2816))


def _params(sem):
    return pltpu.CompilerParams(dimension_semantics=sem, vmem_limit_bytes=VMEM_LIMIT)


def _resident(shape):
    nd = len(shape)
    return pl.BlockSpec(shape, lambda *_: (0,) * nd, pipeline_mode=pl.Buffered(1))


def _rms(x):
    return x * lax.rsqrt(jnp.mean(x * x, axis=-1, keepdims=True) + EPS)


def _mod_row(mod_ref, idx, is_ctx):
    return jnp.where(is_ctx, mod_ref[0, 1, idx:idx + 1, :], mod_ref[0, 0, idx:idx + 1, :])


def _is_ctx_rows(tm, n_latent):
    row0 = pl.program_id(1) * tm
    return (row0 + lax.broadcasted_iota(jnp.int32, (tm, 1), 0)) >= n_latent


def _ada_kernel(c_ref, w_ref, b_ref, o_ref):
    c = c_ref[...]
    cs = c / (1.0 + jnp.exp(-c))
    o_ref[0] = jnp.dot(cs, w_ref[0], preferred_element_type=F32,
                       precision=lax.Precision.HIGHEST) + b_ref[0]


def _ada(cvec, w_ada, b_ada):
    depth, d, n = w_ada.shape
    tn = 1024
    return pl.pallas_call(
        _ada_kernel,
        grid=(depth, n // tn),
        in_specs=[
            pl.BlockSpec((8, d), lambda l, j: (0, 0)),
            pl.BlockSpec((1, d, tn), lambda l, j: (l, 0, j)),
            pl.BlockSpec((1, 1, tn), lambda l, j: (l, 0, j)),
        ],
        out_specs=pl.BlockSpec((1, 8, tn), lambda l, j: (l, 0, j)),
        out_shape=jax.ShapeDtypeStruct((depth, 8, n), F32),
        name="ada",
        compiler_params=_params(("arbitrary", "arbitrary")),
    )(cvec, w_ada, b_ada.reshape(depth, 1, n))


def _ffn_kernel(x_ref, mod_ref, w1_ref, w3_ref, w2_ref, *rest, tm, n_latent, mi, final):
    if final:
        gain_ref, o_ref = rest
    else:
        (o_ref,) = rest
    x = x_ref[0]
    is_ctx = _is_ctx_rows(tm, n_latent)
    shift = _mod_row(mod_ref, mi, is_ctx)
    scale = _mod_row(mod_ref, mi + 1, is_ctx)
    gate = _mod_row(mod_ref, mi + 2, is_ctx)
    h = (_rms(x) * (1.0 + scale) + shift).astype(BF16)
    acc = jnp.zeros((tm, D_MODEL), F32)
    for c0, c1 in FF_CHUNKS:
        a = jnp.dot(h, w1_ref[:, c0:c1], preferred_element_type=F32)
        g = jnp.dot(h, w3_ref[:, c0:c1], preferred_element_type=F32)
        y = (a / (1.0 + jnp.exp(-a)) * g).astype(BF16)
        acc = acc + jnp.dot(y, w2_ref[c0:c1, :], preferred_element_type=F32)
    out = x + 0.5 * gate * acc
    if final:
        out = _rms(out) * gain_ref[...]
    o_ref[0] = out


def _ffn(x, modt, w1, w3, w2, *, n_latent, mi, final_gain=None):
    b, t, d = x.shape
    tm = TOKEN_TILE
    final = final_gain is not None
    t_out = n_latent if final else t
    in_specs = [
        pl.BlockSpec((1, tm, d), lambda i, j: (i, j, 0)),
        pl.BlockSpec((1, 2, N_ADA, d), lambda i, j: (i, 0, 0, 0)),
        _resident((d, D_FF)), _resident((d, D_FF)), _resident((D_FF, d)),
    ]
    args = [x, modt, w1, w3, w2]
    if final:
        in_specs.append(_resident((1, d)))
        args.append(final_gain.reshape(1, d))
    return pl.pallas_call(
        functools.partial(_ffn_kernel, tm=tm, n_latent=n_latent, mi=mi, final=final),
        grid=(b, t_out // tm),
        in_specs=in_specs,
        out_specs=pl.BlockSpec((1, tm, d), lambda i, j: (i, j, 0)),
        out_shape=jax.ShapeDtypeStruct((b, t_out, d), F32),
        name="ffn_final" if final else "ffn",
        compiler_params=_params(("arbitrary", "arbitrary")),
    )(*args)


def _swap_rope_halves(x, first_half):
    return jnp.where(first_half, pltpu.roll(x, 96, 1), pltpu.roll(x, 32, 1))


def _proj_in_kernel(x_ref, mod_ref, w_ref, cos_ref, sin_ref, ws_ref, bs_ref, gn_ref, gmat_ref,
                    za_ref, ob_ref, zd_ref, *, tm, n_latent):
    x = x_ref[0]
    is_ctx = _is_ctx_rows(tm, n_latent)
    shift = _mod_row(mod_ref, 3, is_ctx)
    scale = _mod_row(mod_ref, 4, is_ctx)
    h = (_rms(x) * (1.0 + scale) + shift).astype(BF16)
    z = jnp.dot(h, w_ref[...], preferred_element_type=F32)

    za_ref[0, :, 0:D_A] = (z[:, 0:D_A] * QK_SCALE).astype(BF16)
    za_ref[0, :, D_A:3 * D_A] = z[:, D_A:3 * D_A].astype(BF16)

    o = 3 * D_A
    u = z[:, o:o + D_B]
    v = z[:, o + D_B:o + 2 * D_B]
    u = 0.5 * u * (1.0 + lax.erf(u * (2.0 ** -0.5)))
    v = 0.5 * v * (1.0 + lax.erf(v * (2.0 ** -0.5)))
    v2 = v * v
    v2_hi = v2.astype(BF16)
    v2_lo = (v2 - v2_hi.astype(F32)).astype(BF16)
    ms = (jnp.dot(v2_hi, gmat_ref[...], preferred_element_type=F32)
          + jnp.dot(v2_lo, gmat_ref[...], preferred_element_type=F32)) * (1.0 / GM_WIDTH)
    vn = (v * lax.rsqrt(ms + EPS) * gn_ref[...]).astype(BF16)
    lane_group = lax.broadcasted_iota(jnp.int32, (1, D_B), 1) // GM_WIDTH
    for c in range(tm // GM_CHUNK):
        rows = slice(c * GM_CHUNK, (c + 1) * GM_CHUNK)
        r = jnp.dot(ws_ref[...], vn[rows, :], preferred_element_type=F32)
        s = bs_ref[...]
        for g in range(GM_GROUPS):
            s = s + jnp.where(lane_group == g, r[g * GM_CHUNK:(g + 1) * GM_CHUNK, :], 0.0)
        ob_ref[0, rows, :] = (u[rows, :] * s).astype(BF16)

    o = 3 * D_A + 2 * D_B
    cos = cos_ref[...]
    sin = sin_ref[...]
    first_half = (lax.broadcasted_iota(jnp.int32, (1, LANES), 1) % DA_QK_DIM) < (DA_QK_DIM // 2)
    for j in range(2 * D_QK_C // LANES):
        zz = z[:, o + j * LANES:o + (j + 1) * LANES]
        r = zz * cos + _swap_rope_halves(zz, first_half) * sin
        if j < D_QK_C // LANES:
            r = r * QK_SCALE
        zd_ref[0, :, j * LANES:(j + 1) * LANES] = r.astype(BF16)
    o = o + 2 * D_QK_C
    zd_ref[0, :, 2 * D_QK_C:] = z[:, o:].astype(BF16)


def _proj_in(x, modt, w_in, cos_t, sin_t, ws_stack, bs_full, gn, gmat, *, n_latent):
    b, t, d = x.shape
    tm = TOKEN_TILE
    return pl.pallas_call(
        functools.partial(_proj_in_kernel, tm=tm, n_latent=n_latent),
        grid=(b, t // tm),
        in_specs=[
            pl.BlockSpec((1, tm, d), lambda i, j: (i, j, 0)),
            pl.BlockSpec((1, 2, N_ADA, d), lambda i, j: (i, 0, 0, 0)),
            _resident((d, D_IN)),
            pl.BlockSpec((tm, LANES), lambda i, j: (j, 0)),
            pl.BlockSpec((tm, LANES), lambda i, j: (j, 0)),
            _resident((GM_GROUPS * GM_CHUNK, GM_CHUNK)),
            _resident((GM_CHUNK, D_B)),
            _resident((1, D_B)),
            _resident((D_B, D_B)),
        ],
        out_specs=[
            pl.BlockSpec((1, tm, 3 * D_A), lambda i, j: (i, j, 0)),
            pl.BlockSpec((1, tm, D_B), lambda i, j: (i, j, 0)),
            pl.BlockSpec((1, tm, 2 * D_QK_C + D_C), lambda i, j: (i, j, 0)),
        ],
        out_shape=[
            jax.ShapeDtypeStruct((b, t, 3 * D_A), BF16),
            jax.ShapeDtypeStruct((b, t, D_B), BF16),
            jax.ShapeDtypeStruct((b, t, 2 * D_QK_C + D_C), BF16),
        ],
        name="proj_in",
        compiler_params=_params(("arbitrary", "arbitrary")),
    )(x, modt, w_in, cos_t, sin_t, ws_stack, bs_full, gn, gmat)


_NT = (((1,), (1,)), ((), ()))


def _softmax_pv(parts):
    m = parts[0][0].max(axis=-1, keepdims=True)
    for s, _ in parts[1:]:
        m = jnp.maximum(m, s.max(axis=-1, keepdims=True))
    l = 0.0
    o = 0.0
    for s, v in parts:
        p = jnp.exp(s - m)
        l = l + p.sum(axis=-1, keepdims=True)
        o = o + jnp.dot(p.astype(BF16), v, preferred_element_type=F32)
    return o / l


def _na_kernel(q_ref, k_ref, v_ref, bias_ref, o_ref, *, n_latent):
    g = pl.program_id(2)
    n_groups = n_latent // ATT_Q_TILE
    rows = n_latent // GRID_W
    lane = lax.broadcasted_iota(jnp.int32, (1, LANES), 1)
    head_lanes = (lane < HEAD_DIM, lane >= HEAD_DIM)
    q = q_ref[0]
    kc = k_ref[0, n_latent:, :]
    vc = v_ref[0, n_latent:, :]

    def heads(parts_of):
        outs = []
        for hh in range(2):
            qh = jnp.where(head_lanes[hh], q, jnp.zeros_like(q))
            outs.append(_softmax_pv(parts_of(hh, qh)))
        o_ref[0] = jnp.where(head_lanes[0], outs[0], outs[1]).astype(BF16)

    @pl.when(g < n_groups)
    def _():
        base_row = jnp.clip(g * NA_Q_ROWS - NA_WIN_ROWS // 2, 0, rows - NA_K_ROWS)
        base = pl.multiple_of(base_row * GRID_W, GRID_W)
        kw = k_ref[0, pl.ds(base, NA_K_TOK), :]
        vw = v_ref[0, pl.ds(base, NA_K_TOK), :]

        def parts_of(hh, qh):
            sw = lax.dot_general(qh, kw, _NT, preferred_element_type=F32) + bias_ref[0, hh]
            sc = lax.dot_general(qh, kc, _NT, preferred_element_type=F32)
            return [(sw, vw), (sc, vc)]

        heads(parts_of)

    @pl.when(g == n_groups)
    def _():
        heads(lambda hh, qh: [(lax.dot_general(qh, kc, _NT, preferred_element_type=F32), vc)])


def _na_attention(za, bias, *, n_latent):
    b, t, _ = za.shape
    n_groups = n_latent // ATT_Q_TILE
    assert t == n_latent + CTX_LEN and CTX_LEN == ATT_Q_TILE
    assert n_groups >= 3 and n_latent // GRID_W >= NA_K_ROWS

    def bias_map(i, hp, g):
        kind = jnp.where(g == 0, 0, jnp.where(g >= n_groups - 1, 2, 1))
        return (kind, hp, 0, 0)

    return pl.pallas_call(
        functools.partial(_na_kernel, n_latent=n_latent),
        grid=(b, NA_HEADS // 2, n_groups + 1),
        in_specs=[
            pl.BlockSpec((1, ATT_Q_TILE, LANES), lambda i, hp, g: (i, g, hp)),
            pl.BlockSpec((1, t, LANES), lambda i, hp, g: (i, 0, 2 + hp)),
            pl.BlockSpec((1, t, LANES), lambda i, hp, g: (i, 0, 4 + hp)),
            pl.BlockSpec((1, 2, ATT_Q_TILE, NA_K_TOK), bias_map),
        ],
        out_specs=pl.BlockSpec((1, ATT_Q_TILE, LANES), lambda i, hp, g: (i, g, hp)),
        out_shape=jax.ShapeDtypeStruct((b, t, D_A), BF16),
        name="na_attention",
        compiler_params=_params(("arbitrary", "arbitrary", "arbitrary")),
    )(za, za, za, bias)


NA_BIAS_PAD = NA_WIN_ROWS // 2
NA_BIAS_W = 11 * LANES


def _na_bias_plan(n_latent):
    rows = n_latent // GRID_W
    n_groups = n_latent // ATT_Q_TILE
    plan = []
    for g in (0, 1, n_groups - 1):
        base = int(np.clip(g * NA_Q_ROWS - NA_WIN_ROWS // 2, 0, rows - NA_K_ROWS))
        kind = []
        for i in range(NA_Q_ROWS):
            r = g * NA_Q_ROWS + i
            start = int(np.clip(r - NA_WIN_ROWS // 2, 0, rows - NA_WIN_ROWS))
            e0 = base - r + NA_WIN_ROWS - 1 + NA_BIAS_PAD
            assert 0 <= e0 and (e0 - e0 % 2) * GRID_W + NA_K_TOK <= NA_BIAS_W
            kind.append((e0, start - base, start - base + NA_WIN_ROWS - 1))
        plan.append(tuple(kind))
    return tuple(plan)


def _na_bias_kernel(r_ref, o_ref, *, plan):
    shape = (GRID_W, NA_BIAS_W)
    qc = lax.broadcasted_iota(jnp.int32, shape, 0)
    col = lax.broadcasted_iota(jnp.int32, shape, 1)
    kc = col % GRID_W
    dc = kc - qc + (NA_WIN_COLS - 1)
    w0 = jnp.clip(qc - NA_WIN_COLS // 2, 0, GRID_W - NA_WIN_COLS)
    col_ok = (kc >= w0) & (kc < w0 + NA_WIN_COLS)
    tables = []
    for p in range(2):
        a = col // GRID_W + (p - NA_BIAS_PAD)
        acc = jnp.zeros(shape, F32)
        for b in range(2 * NA_WIN_COLS - 1):
            acc = acc + jnp.where(dc == b, r_ref[0, p, b:b + 1, :], 0.0)
        ok = col_ok & (a >= 0) & (a <= 2 * NA_WIN_ROWS - 2)
        tables.append(jnp.where(ok, acc, NEG_INF))
    kj = lax.broadcasted_iota(jnp.int32, (GRID_W, NA_K_TOK), 1) // GRID_W
    for k, kind in enumerate(plan):
        for i, (e0, jlo, jhi) in enumerate(kind):
            p = e0 % 2
            off = (e0 - p) * GRID_W
            slab = tables[p][:, off:off + NA_K_TOK]
            o_ref[k, 0, i * GRID_W:(i + 1) * GRID_W, :] = jnp.where(
                (kj >= jlo) & (kj <= jhi), slab, NEG_INF)


def _na_bias_tables(rpb, n_latent):
    h = rpb.shape[0]
    n_blocks = NA_BIAS_W // GRID_W + 2
    rp = jnp.pad(rpb, ((0, 0), (NA_BIAS_PAD, n_blocks - NA_BIAS_PAD - rpb.shape[1]), (0, 0)))
    rep = jnp.transpose(jnp.repeat(rp, GRID_W, axis=1), (0, 2, 1))
    r = jnp.stack([rep[:, :, p * GRID_W:p * GRID_W + NA_BIAS_W] for p in range(2)], axis=1)
    n_dc = rpb.shape[2]
    return pl.pallas_call(
        functools.partial(_na_bias_kernel, plan=_na_bias_plan(n_latent)),
        grid=(h,),
        in_specs=[pl.BlockSpec((1, 2, n_dc, NA_BIAS_W), lambda i: (i, 0, 0, 0))],
        out_specs=pl.BlockSpec((3, 1, ATT_Q_TILE, NA_K_TOK), lambda i: (0, i, 0, 0)),
        out_shape=jax.ShapeDtypeStruct((3, h, ATT_Q_TILE, NA_K_TOK), F32),
        name="na_bias",
        compiler_params=_params(("arbitrary",)),
    )(r)


def _diff_kernel(q_ref, k_ref, v_ref, lqk_ref, subln_ref, o_ref, *, n_latent, lambda_init):
    i = pl.program_id(2)
    n_q_tiles = n_latent // ATT_Q_TILE
    lane = lax.broadcasted_iota(jnp.int32, (1, LANES), 1)
    q = q_ref[0]
    qs = (jnp.where(lane < DA_QK_DIM, q, jnp.zeros_like(q)),
          jnp.where(lane >= DA_QK_DIM, q, jnp.zeros_like(q)))
    lqk = lqk_ref[...]
    lam = (jnp.exp(jnp.sum(lqk[0:1] * lqk[1:2], axis=-1, keepdims=True))
           - jnp.exp(jnp.sum(lqk[2:3] * lqk[3:4], axis=-1, keepdims=True)) + lambda_init)

    def step(k, v, state):
        new = []
        for c in range(2):
            m, l, acc = state[c]
            s = lax.dot_general(qs[c], k, _NT, preferred_element_type=F32)
            m_new = jnp.maximum(m, s.max(axis=-1, keepdims=True))
            alpha = jnp.exp(m - m_new)
            p = jnp.exp(s - m_new)
            l = alpha * l + p.sum(axis=-1, keepdims=True)
            acc = alpha * acc + jnp.dot(p.astype(BF16), v, preferred_element_type=F32)
            new.append((m_new, l, acc))
        return tuple(new)

    def init():
        one = (jnp.full((ATT_Q_TILE, 1), NEG_INF, F32), jnp.zeros((ATT_Q_TILE, 1), F32),
               jnp.zeros((ATT_Q_TILE, DA_V_DIM), F32))
        return (one, one)

    def finish(state):
        (_, l1, a1), (_, l2, a2) = state
        o = a1 / l1 - lam * (a2 / l2)
        o_ref[0] = (_rms(o) * subln_ref[...] * (1.0 - lambda_init)).astype(BF16)

    @pl.when(i < n_q_tiles)
    def _():
        def body(c, state):
            start = pl.multiple_of(c * DA_KV_CHUNK, DA_KV_CHUNK)
            return step(k_ref[0, pl.ds(start, DA_KV_CHUNK), :],
                        v_ref[0, pl.ds(start, DA_KV_CHUNK), :], state)
        finish(lax.fori_loop(0, (n_latent + CTX_LEN) // DA_KV_CHUNK, body, init(), unroll=True))

    @pl.when(i == n_q_tiles)
    def _():
        finish(step(k_ref[0, n_latent:, :], v_ref[0, n_latent:, :], init()))


def _diff_attention(zd, lqk, subln, *, n_latent, lambda_init):
    b, t, _ = zd.shape
    n_q_tiles = n_latent // ATT_Q_TILE
    return pl.pallas_call(
        functools.partial(_diff_kernel, n_latent=n_latent, lambda_init=lambda_init),
        grid=(b, DA_HEADS, n_q_tiles + 1),
        in_specs=[
            pl.BlockSpec((1, ATT_Q_TILE, LANES), lambda bi, h, i: (bi, i, h)),
            pl.BlockSpec((1, t, LANES), lambda bi, h, i: (bi, 0, DA_HEADS + h)),
            pl.BlockSpec((1, t, LANES), lambda bi, h, i: (bi, 0, 2 * DA_HEADS + h)),
            pl.BlockSpec((4, DA_QK_DIM), lambda bi, h, i: (0, 0)),
            pl.BlockSpec((1, DA_V_DIM), lambda bi, h, i: (0, 0)),
        ],
        out_specs=pl.BlockSpec((1, ATT_Q_TILE, LANES), lambda bi, h, i: (bi, i, h)),
        out_shape=jax.ShapeDtypeStruct((b, t, D_C), BF16),
        name="diff_attention",
        compiler_params=_params(("arbitrary", "arbitrary", "arbitrary")),
    )(zd, zd, zd, lqk, subln)


def _proj_out_kernel(x_ref, mod_ref, oa_ref, ob_ref, oc_ref, w_ref, o_ref, *, tm, n_latent):
    is_ctx = _is_ctx_rows(tm, n_latent)
    gate = _mod_row(mod_ref, 5, is_ctx)
    cat = jnp.concatenate([oa_ref[0], ob_ref[0], oc_ref[0]], axis=-1)
    o_ref[0] = x_ref[0] + gate * jnp.dot(cat, w_ref[...], preferred_element_type=F32)


def _proj_out(x, modt, oa, ob, oc, w_out, *, n_latent):
    b, t, d = x.shape
    tm = TOKEN_TILE
    tile = lambda width: pl.BlockSpec((1, tm, width), lambda i, j: (i, j, 0))
    return pl.pallas_call(
        functools.partial(_proj_out_kernel, tm=tm, n_latent=n_latent),
        grid=(b, t // tm),
        in_specs=[tile(d), pl.BlockSpec((1, 2, N_ADA, d), lambda i, j: (i, 0, 0, 0)),
                  tile(D_A), tile(D_B), tile(D_C), _resident((d, d))],
        out_specs=tile(d),
        out_shape=jax.ShapeDtypeStruct((b, t, d), F32),
        name="proj_out",
        compiler_params=_params(("arbitrary", "arbitrary")),
    )(x, modt, oa, ob, oc, w_out)


def _rope_tables(n_latent):
    tok = jnp.arange(n_latent)
    row = (tok // GRID_W).astype(F32)
    col = (tok % GRID_W).astype(F32)
    n_freq = DA_QK_DIM // 4
    freqs = ROPE_BASE ** (-jnp.arange(n_freq, dtype=F32) / n_freq)
    ang = jnp.concatenate([row[:, None] * freqs, col[:, None] * freqs], axis=-1)
    cos, sin = jnp.cos(ang), jnp.sin(ang)
    cos_t = jnp.concatenate([cos, cos, cos, cos], axis=-1)
    sin_t = jnp.concatenate([-sin, sin, -sin, sin], axis=-1)
    cos_t = jnp.concatenate([cos_t, jnp.ones((CTX_LEN, LANES), F32)], axis=0)
    sin_t = jnp.concatenate([sin_t, jnp.zeros((CTX_LEN, LANES), F32)], axis=0)
    return cos_t, sin_t


def kernel(x, c, ctx, c_ctx, w_ada, b_ada, ffn1_w1, ffn1_w3, ffn1_w2, w_in, w_out, na_rpb,
           gm_ws, gm_bs, gm_norm, da_lq1, da_lk1, da_lq2, da_lk2, da_subln,
           ffn2_w1, ffn2_w3, ffn2_w2, final_norm):
    b, n_latent, d = x.shape
    assert d == D_MODEL and ctx.shape[1] == CTX_LEN and b < 8
    assert (n_latent + CTX_LEN) % DA_KV_CHUNK == 0 and n_latent % ATT_Q_TILE == 0

    cvec = jnp.zeros((8, d), F32).at[:b].set(c).at[b].set(c_ctx)
    mod = _ada(cvec, w_ada, b_ada).reshape(DEPTH, 8, N_ADA, d)
    cos_t, sin_t = _rope_tables(n_latent)
    gmat = jnp.asarray(np.kron(np.eye(GM_GROUPS), np.ones((GM_WIDTH, GM_WIDTH))), BF16)

    xs = jnp.concatenate([x, ctx], axis=1)
    for l in range(DEPTH):
        last = l == DEPTH - 1
        lambda_init = 0.8 - 0.6 * math.exp(-0.3 * l)
        modt = jnp.stack([mod[l, :b], jnp.broadcast_to(mod[l, b], (b, N_ADA, d))], axis=1)
        bias = _na_bias_tables(na_rpb[l], n_latent)
        ws_stack = gm_ws[l].reshape(GM_GROUPS * GM_CHUNK, GM_CHUNK).astype(BF16)
        bs_full = jnp.repeat(gm_bs[l].T, GM_WIDTH, axis=1)
        gn = gm_norm[l].reshape(1, D_B)
        lqk = jnp.stack([da_lq1[l], da_lk1[l], da_lq2[l], da_lk2[l]])

        xs = _ffn(xs, modt, ffn1_w1[l].astype(BF16), ffn1_w3[l].astype(BF16),
                  ffn1_w2[l].astype(BF16), n_latent=n_latent, mi=0)
        za, ob, zd = _proj_in(xs, modt, w_in[l].astype(BF16), cos_t, sin_t, ws_stack, bs_full,
                              gn, gmat, n_latent=n_latent)
        oa = _na_attention(za, bias, n_latent=n_latent)
        oc = _diff_attention(zd, lqk, da_subln[l].reshape(1, DA_V_DIM),
                             n_latent=n_latent, lambda_init=lambda_init)
        xs = _proj_out(xs, modt, oa, ob, oc, w_out[l].astype(BF16), n_latent=n_latent)
        xs = _ffn(xs, modt, ffn2_w1[l].astype(BF16), ffn2_w3[l].astype(BF16),
                  ffn2_w2[l].astype(BF16), n_latent=n_latent, mi=6,
                  final_gain=final_norm if last else None)
    return xs
```

```python
import functools
import math

import numpy as np
import jax
import jax.numpy as jnp
from jax import lax
from jax.experimental import pallas as pl
from jax.experimental.pallas import tpu as pltpu

F32 = jnp.float32
BF16 = jnp.bfloat16

D_MODEL = 1024
DEPTH = 2
GRID_W = 64
CTX_LEN = 256
N_ADA = 9
EPS = 1e-6
NEG_INF = -1e30
ROPE_BASE = 10000.0
D_FF = 2816
HEAD_DIM = 64
NA_HEADS = 4
NA_WIN_ROWS = 8
NA_WIN_COLS = 16
D_A = NA_HEADS * HEAD_DIM
GM_GROUPS = 4
GM_WIDTH = 64
GM_CHUNK = 128
D_B = GM_GROUPS * GM_WIDTH
DA_HEADS = 4
DA_QK_DIM = 64
DA_V_DIM = 128
D_C = DA_HEADS * DA_V_DIM
D_QK_C = DA_HEADS * 2 * DA_QK_DIM
D_IN = 3 * D_A + 2 * D_B + 2 * D_QK_C + D_C
QK_SCALE = HEAD_DIM ** -0.5
LOG2_E = math.log2(math.e)

LANES = 128
VMEM_BYTES_V7X = 64 * 1024 * 1024
VMEM_LIMIT = VMEM_BYTES_V7X - 8 * 1024 * 1024

TOKEN_TILE = 256
ATT_Q_TILE = 256
NA_Q_ROWS = ATT_Q_TILE // GRID_W
NA_K_ROWS = NA_Q_ROWS + NA_WIN_ROWS
NA_K_TOK = NA_K_ROWS * GRID_W
DA_KV_CHUNK = 768
FF_CHUNKS = ((0, 768), (768, 1536), (1536, 2304), (2304, 2816))


def _params(sem):
    return pltpu.CompilerParams(dimension_semantics=sem, vmem_limit_bytes=VMEM_LIMIT)


def _resident(shape):
    nd = len(shape)
    return pl.BlockSpec(shape, lambda *_: (0,) * nd, pipeline_mode=pl.Buffered(1))


def _rms(x):
    return x * lax.rsqrt(jnp.mean(x * x, axis=-1, keepdims=True) + EPS)


def _mod_row(mod_ref, idx, is_ctx):
    return jnp.where(is_ctx, mod_ref[0, 1, idx:idx + 1, :], mod_ref[0, 0, idx:idx + 1, :])


def _is_ctx_rows(tm, n_latent):
    row0 = pl.program_id(1) * tm
    return (row0 + lax.broadcasted_iota(jnp.int32, (tm, 1), 0)) >= n_latent


def _ada_kernel(c_ref, w_ref, b_ref, o_ref):
    c = c_ref[...]
    cs = c / (1.0 + jnp.exp(-c))
    o_ref[0] = jnp.dot(cs, w_ref[0], preferred_element_type=F32,
                       precision=lax.Precision.HIGHEST) + b_ref[0]


def _ada(cvec, w_ada, b_ada):
    depth, d, n = w_ada.shape
    tn = 1024
    return pl.pallas_call(
        _ada_kernel,
        grid=(depth, n // tn),
        in_specs=[
            pl.BlockSpec((8, d), lambda l, j: (0, 0)),
            pl.BlockSpec((1, d, tn), lambda l, j: (l, 0, j)),
            pl.BlockSpec((1, 1, tn), lambda l, j: (l, 0, j)),
        ],
        out_specs=pl.BlockSpec((1, 8, tn), lambda l, j: (l, 0, j)),
        out_shape=jax.ShapeDtypeStruct((depth, 8, n), F32),
        name="ada",
        compiler_params=_params(("arbitrary", "arbitrary")),
    )(cvec, w_ada, b_ada.reshape(depth, 1, n))


def _ffn_kernel(x_ref, mod_ref, w1_ref, w3_ref, w2_ref, *rest, tm, n_latent, mi, final):
    if final:
        gain_ref, o_ref = rest
    else:
        (o_ref,) = rest
    x = x_ref[0]
    is_ctx = _is_ctx_rows(tm, n_latent)
    shift = _mod_row(mod_ref, mi, is_ctx)
    scale = _mod_row(mod_ref, mi + 1, is_ctx)
    gate = _mod_row(mod_ref, mi + 2, is_ctx)
    h = (_rms(x) * (1.0 + scale) + shift).astype(BF16)
    acc = jnp.zeros((tm, D_MODEL), F32)
    for c0, c1 in FF_CHUNKS:
        a = jnp.dot(h, w1_ref[:, c0:c1], preferred_element_type=F32)
        g = jnp.dot(h, w3_ref[:, c0:c1], preferred_element_type=F32)
        y = (a / (1.0 + jnp.exp(-a)) * g).astype(BF16)
        acc = acc + jnp.dot(y, w2_ref[c0:c1, :], preferred_element_type=F32)
    out = x + 0.5 * gate * acc
    if final:
        out = _rms(out) * gain_ref[...]
    o_ref[0] = out


def _ffn(x, modt, w1, w3, w2, *, n_latent, mi, final_gain=None):
    b, t, d = x.shape
    tm = TOKEN_TILE
    final = final_gain is not None
    t_out = n_latent if final else t
    in_specs = [
        pl.BlockSpec((1, tm, d), lambda i, j: (i, j, 0)),
        pl.BlockSpec((1, 2, N_ADA, d), lambda i, j: (i, 0, 0, 0)),
        _resident((d, D_FF)), _resident((d, D_FF)), _resident((D_FF, d)),
    ]
    args = [x, modt, w1, w3, w2]
    if final:
        in_specs.append(_resident((1, d)))
        args.append(final_gain.reshape(1, d))
    return pl.pallas_call(
        functools.partial(_ffn_kernel, tm=tm, n_latent=n_latent, mi=mi, final=final),
        grid=(b, t_out // tm),
        in_specs=in_specs,
        out_specs=pl.BlockSpec((1, tm, d), lambda i, j: (i, j, 0)),
        out_shape=jax.ShapeDtypeStruct((b, t_out, d), F32),
        name="ffn_final" if final else "ffn",
        compiler_params=_params(("arbitrary", "arbitrary")),
    )(*args)


def _swap_rope_halves(x, first_half):
    return jnp.where(first_half, pltpu.roll(x, 96, 1), pltpu.roll(x, 32, 1))


def _proj_in_kernel(x_ref, mod_ref, w_ref, cos_ref, sin_ref, ws_ref, bs_ref, gn_ref, gmat_ref,
                    za_ref, kat_ref, ob_ref, zd_ref, kt_ref, *, tm, n_latent):
    x = x_ref[0]
    is_ctx = _is_ctx_rows(tm, n_latent)
    shift = _mod_row(mod_ref, 3, is_ctx)
    scale = _mod_row(mod_ref, 4, is_ctx)
    h = (_rms(x) * (1.0 + scale) + shift).astype(BF16)
    z = jnp.dot(h, w_ref[...], preferred_element_type=F32)

    za_ref[0, :, 0:D_A] = (z[:, 0:D_A] * (QK_SCALE * LOG2_E)).astype(BF16)
    kat_ref[0] = z[:, D_A:2 * D_A].T.astype(BF16)
    za_ref[0, :, D_A:2 * D_A] = z[:, 2 * D_A:3 * D_A].astype(BF16)

    o = 3 * D_A
    u = z[:, o:o + D_B]
    v = z[:, o + D_B:o + 2 * D_B]
    u = 0.5 * u * (1.0 + lax.erf(u * (2.0 ** -0.5)))
    v = 0.5 * v * (1.0 + lax.erf(v * (2.0 ** -0.5)))
    v2 = v * v
    v2_hi = v2.astype(BF16)
    v2_lo = (v2 - v2_hi.astype(F32)).astype(BF16)
    ms = (jnp.dot(v2_hi, gmat_ref[...], preferred_element_type=F32)
          + jnp.dot(v2_lo, gmat_ref[...], preferred_element_type=F32)) * (1.0 / GM_WIDTH)
    vn = (v * lax.rsqrt(ms + EPS) * gn_ref[...]).astype(BF16)
    lane_group = lax.broadcasted_iota(jnp.int32, (1, D_B), 1) // GM_WIDTH
    for c in range(tm // GM_CHUNK):
        rows = slice(c * GM_CHUNK, (c + 1) * GM_CHUNK)
        r = jnp.dot(ws_ref[...], vn[rows, :], preferred_element_type=F32)
        s = bs_ref[...]
        for g in range(GM_GROUPS):
            s = s + jnp.where(lane_group == g, r[g * GM_CHUNK:(g + 1) * GM_CHUNK, :], 0.0)
        ob_ref[0, rows, :] = (u[rows, :] * s).astype(BF16)

    o = 3 * D_A + 2 * D_B
    cos = cos_ref[...]
    sin = sin_ref[...]
    first_half = (lax.broadcasted_iota(jnp.int32, (1, LANES), 1) % DA_QK_DIM) < (DA_QK_DIM // 2)
    n_q_blocks = D_QK_C // LANES
    for j in range(2 * n_q_blocks):
        zz = z[:, o + j * LANES:o + (j + 1) * LANES]
        r = zz * cos + _swap_rope_halves(zz, first_half) * sin
        if j < n_q_blocks:
            zd_ref[0, :, j * LANES:(j + 1) * LANES] = (r * (QK_SCALE * LOG2_E)).astype(BF16)
        else:
            jk = j - n_q_blocks
            kt_ref[0, jk * LANES:(jk + 1) * LANES, :] = r.T.astype(BF16)
    o = o + 2 * D_QK_C
    zd_ref[0, :, D_QK_C:] = z[:, o:].astype(BF16)


def _proj_in(x, modt, w_in, cos_t, sin_t, ws_stack, bs_full, gn, gmat, *, n_latent):
    b, t, d = x.shape
    tm = TOKEN_TILE
    return pl.pallas_call(
        functools.partial(_proj_in_kernel, tm=tm, n_latent=n_latent),
        grid=(b, t // tm),
        in_specs=[
            pl.BlockSpec((1, tm, d), lambda i, j: (i, j, 0)),
            pl.BlockSpec((1, 2, N_ADA, d), lambda i, j: (i, 0, 0, 0)),
            _resident((d, D_IN)),
            pl.BlockSpec((tm, LANES), lambda i, j: (j, 0)),
            pl.BlockSpec((tm, LANES), lambda i, j: (j, 0)),
            _resident((GM_GROUPS * GM_CHUNK, GM_CHUNK)),
            _resident((GM_CHUNK, D_B)),
            _resident((1, D_B)),
            _resident((D_B, D_B)),
        ],
        out_specs=[
            pl.BlockSpec((1, tm, 2 * D_A), lambda i, j: (i, j, 0)),
            pl.BlockSpec((1, D_A, tm), lambda i, j: (i, 0, j)),
            pl.BlockSpec((1, tm, D_B), lambda i, j: (i, j, 0)),
            pl.BlockSpec((1, tm, D_QK_C + D_C), lambda i, j: (i, j, 0)),
            pl.BlockSpec((1, D_QK_C, tm), lambda i, j: (i, 0, j)),
        ],
        out_shape=[
            jax.ShapeDtypeStruct((b, t, 2 * D_A), BF16),
            jax.ShapeDtypeStruct((b, D_A, t), BF16),
            jax.ShapeDtypeStruct((b, t, D_B), BF16),
            jax.ShapeDtypeStruct((b, t, D_QK_C + D_C), BF16),
            jax.ShapeDtypeStruct((b, D_QK_C, t), BF16),
        ],
        name="proj_in",
        compiler_params=_params(("arbitrary", "arbitrary")),
    )(x, modt, w_in, cos_t, sin_t, ws_stack, bs_full, gn, gmat)


_NT = (((1,), (1,)), ((), ()))


def _softmax_pv(parts):
    m = parts[0][0].max(axis=-1, keepdims=True)
    for s, _ in parts[1:]:
        m = jnp.maximum(m, s.max(axis=-1, keepdims=True))
    l = 0.0
    o = 0.0
    for s, v in parts:
        p = jnp.exp2(s - m)
        l = l + p.sum(axis=-1, keepdims=True)
        o = o + jnp.dot(p.astype(BF16), v, preferred_element_type=F32)
    return o / l


def _na_kernel(q_ref, kt_ref, v_ref, bias_ref, o_ref, *, n_latent):
    g = pl.program_id(2)
    n_groups = n_latent // ATT_Q_TILE
    rows = n_latent // GRID_W
    lane = lax.broadcasted_iota(jnp.int32, (1, LANES), 1)
    q = q_ref[0]
    qs = jnp.concatenate([jnp.where(lane < HEAD_DIM, q, jnp.zeros_like(q)),
                          jnp.where(lane >= HEAD_DIM, q, jnp.zeros_like(q))], axis=0)
    sc = jnp.dot(qs, kt_ref[0, :, n_latent:], preferred_element_type=F32)
    vc = v_ref[0, n_latent:, :]

    def finish(parts):
        o = _softmax_pv(parts)
        o_ref[0] = jnp.where(lane < HEAD_DIM, o[:ATT_Q_TILE], o[ATT_Q_TILE:]).astype(BF16)

    @pl.when(g < n_groups)
    def _():
        base_row = jnp.clip(g * NA_Q_ROWS - NA_WIN_ROWS // 2, 0, rows - NA_K_ROWS)
        base = pl.multiple_of(base_row * GRID_W, ATT_Q_TILE)
        sw = jnp.dot(qs, kt_ref[0, :, pl.ds(base, NA_K_TOK)], preferred_element_type=F32)
        sw = sw + bias_ref[0].reshape(2 * ATT_Q_TILE, NA_K_TOK)
        finish([(sw, v_ref[0, pl.ds(base, NA_K_TOK), :]), (sc, vc)])

    @pl.when(g == n_groups)
    def _():
        finish([(sc, vc)])


def _na_attention(za, kat, bias, *, n_latent):
    b, t, _ = za.shape
    n_groups = n_latent // ATT_Q_TILE
    assert t == n_latent + CTX_LEN and CTX_LEN == ATT_Q_TILE
    assert n_groups >= 3 and n_latent // GRID_W >= NA_K_ROWS

    def bias_map(i, hp, g):
        kind = jnp.where(g == 0, 0, jnp.where(g >= n_groups - 1, 2, 1))
        return (kind, hp, 0, 0)

    return pl.pallas_call(
        functools.partial(_na_kernel, n_latent=n_latent),
        grid=(b, NA_HEADS // 2, n_groups + 1),
        in_specs=[
            pl.BlockSpec((1, ATT_Q_TILE, LANES), lambda i, hp, g: (i, g, hp)),
            pl.BlockSpec((1, LANES, t), lambda i, hp, g: (i, hp, 0)),
            pl.BlockSpec((1, t, LANES), lambda i, hp, g: (i, 0, 2 + hp)),
            pl.BlockSpec((1, 2, ATT_Q_TILE, NA_K_TOK), bias_map),
        ],
        out_specs=pl.BlockSpec((1, ATT_Q_TILE, LANES), lambda i, hp, g: (i, g, hp)),
        out_shape=jax.ShapeDtypeStruct((b, t, D_A), BF16),
        name="na_attention",
        compiler_params=_params(("arbitrary", "arbitrary", "arbitrary")),
    )(za, kat, za, bias)


NA_BIAS_PAD = NA_WIN_ROWS // 2
NA_BIAS_W = 11 * LANES


def _na_bias_plan(n_latent):
    rows = n_latent // GRID_W
    n_groups = n_latent // ATT_Q_TILE
    plan = []
    for g in (0, 1, n_groups - 1):
        base = int(np.clip(g * NA_Q_ROWS - NA_WIN_ROWS // 2, 0, rows - NA_K_ROWS))
        kind = []
        for i in range(NA_Q_ROWS):
            r = g * NA_Q_ROWS + i
            start = int(np.clip(r - NA_WIN_ROWS // 2, 0, rows - NA_WIN_ROWS))
            e0 = base - r + NA_WIN_ROWS - 1 + NA_BIAS_PAD
            assert 0 <= e0 and (e0 - e0 % 2) * GRID_W + NA_K_TOK <= NA_BIAS_W
            kind.append((e0, start - base, start - base + NA_WIN_ROWS - 1))
        plan.append(tuple(kind))
    return tuple(plan)


def _na_bias_kernel(r_ref, o_ref, *, plan):
    shape = (GRID_W, NA_BIAS_W)
    qc = lax.broadcasted_iota(jnp.int32, shape, 0)
    col = lax.broadcasted_iota(jnp.int32, shape, 1)
    kc = col % GRID_W
    dc = kc - qc + (NA_WIN_COLS - 1)
    w0 = jnp.clip(qc - NA_WIN_COLS // 2, 0, GRID_W - NA_WIN_COLS)
    col_ok = (kc >= w0) & (kc < w0 + NA_WIN_COLS)
    tables = []
    for p in range(2):
        a = col // GRID_W + (p - NA_BIAS_PAD)
        acc = jnp.zeros(shape, F32)
        for b in range(2 * NA_WIN_COLS - 1):
            acc = acc + jnp.where(dc == b, r_ref[0, p, b:b + 1, :], 0.0)
        ok = col_ok & (a >= 0) & (a <= 2 * NA_WIN_ROWS - 2)
        tables.append(jnp.where(ok, acc * LOG2_E, NEG_INF))
    kj = lax.broadcasted_iota(jnp.int32, (GRID_W, NA_K_TOK), 1) // GRID_W
    for k, kind in enumerate(plan):
        for i, (e0, jlo, jhi) in enumerate(kind):
            p = e0 % 2
            off = (e0 - p) * GRID_W
            slab = tables[p][:, off:off + NA_K_TOK]
            o_ref[k, 0, i * GRID_W:(i + 1) * GRID_W, :] = jnp.where(
                (kj >= jlo) & (kj <= jhi), slab, NEG_INF)


def _na_bias_tables(rpb, n_latent):
    h = rpb.shape[0]
    n_blocks = NA_BIAS_W // GRID_W + 2
    rp = jnp.pad(rpb, ((0, 0), (NA_BIAS_PAD, n_blocks - NA_BIAS_PAD - rpb.shape[1]), (0, 0)))
    rep = jnp.transpose(jnp.repeat(rp, GRID_W, axis=1), (0, 2, 1))
    r = jnp.stack([rep[:, :, p * GRID_W:p * GRID_W + NA_BIAS_W] for p in range(2)], axis=1)
    n_dc = rpb.shape[2]
    return pl.pallas_call(
        functools.partial(_na_bias_kernel, plan=_na_bias_plan(n_latent)),
        grid=(h,),
        in_specs=[pl.BlockSpec((1, 2, n_dc, NA_BIAS_W), lambda i: (i, 0, 0, 0))],
        out_specs=pl.BlockSpec((3, 1, ATT_Q_TILE, NA_K_TOK), lambda i: (0, i, 0, 0)),
        out_shape=jax.ShapeDtypeStruct((3, h, ATT_Q_TILE, NA_K_TOK), F32),
        name="na_bias",
        compiler_params=_params(("arbitrary",)),
    )(r)


def _diff_kernel(q_ref, kt_ref, v_ref, lqk_ref, subln_ref, o_ref, *, n_latent, lambda_init):
    i = pl.program_id(2)
    n_q_tiles = n_latent // ATT_Q_TILE
    lane = lax.broadcasted_iota(jnp.int32, (1, LANES), 1)
    q = q_ref[0]
    qs = jnp.concatenate([jnp.where(lane < DA_QK_DIM, q, jnp.zeros_like(q)),
                          jnp.where(lane >= DA_QK_DIM, q, jnp.zeros_like(q))], axis=0)
    lqk = lqk_ref[...]
    lam = (jnp.exp(jnp.sum(lqk[0:1] * lqk[1:2], axis=-1, keepdims=True))
           - jnp.exp(jnp.sum(lqk[2:3] * lqk[3:4], axis=-1, keepdims=True)) + lambda_init)

    def step(kt, v, state):
        m, l, acc = state
        s = jnp.dot(qs, kt, preferred_element_type=F32)
        m_new = jnp.maximum(m, s.max(axis=-1, keepdims=True))
        alpha = jnp.exp2(m - m_new)
        p = jnp.exp2(s - m_new)
        l = alpha * l + p.sum(axis=-1, keepdims=True)
        acc = alpha * acc + jnp.dot(p.astype(BF16), v, preferred_element_type=F32)
        return (m_new, l, acc)

    def init():
        return (jnp.full((2 * ATT_Q_TILE, 1), NEG_INF, F32), jnp.zeros((2 * ATT_Q_TILE, 1), F32),
                jnp.zeros((2 * ATT_Q_TILE, DA_V_DIM), F32))

    def finish(state):
        _, l, acc = state
        o = acc / l
        o = o[:ATT_Q_TILE] - lam * o[ATT_Q_TILE:]
        o_ref[0] = (_rms(o) * subln_ref[...] * (1.0 - lambda_init)).astype(BF16)

    @pl.when(i < n_q_tiles)
    def _():
        def body(c, state):
            start = pl.multiple_of(c * DA_KV_CHUNK, DA_KV_CHUNK)
            return step(kt_ref[0, :, pl.ds(start, DA_KV_CHUNK)],
                        v_ref[0, pl.ds(start, DA_KV_CHUNK), :], state)
        finish(lax.fori_loop(0, (n_latent + CTX_LEN) // DA_KV_CHUNK, body, init(), unroll=True))

    @pl.when(i == n_q_tiles)
    def _():
        finish(step(kt_ref[0, :, n_latent:], v_ref[0, n_latent:, :], init()))


def _diff_attention(zd, kt, lqk, subln, *, n_latent, lambda_init):
    b, t, _ = zd.shape
    n_q_tiles = n_latent // ATT_Q_TILE
    return pl.pallas_call(
        functools.partial(_diff_kernel, n_latent=n_latent, lambda_init=lambda_init),
        grid=(b, DA_HEADS, n_q_tiles + 1),
        in_specs=[
            pl.BlockSpec((1, ATT_Q_TILE, LANES), lambda bi, h, i: (bi, i, h)),
            pl.BlockSpec((1, LANES, t), lambda bi, h, i: (bi, h, 0)),
            pl.BlockSpec((1, t, LANES), lambda bi, h, i: (bi, 0, DA_HEADS + h)),
            pl.BlockSpec((4, DA_QK_DIM), lambda bi, h, i: (0, 0)),
            pl.BlockSpec((1, DA_V_DIM), lambda bi, h, i: (0, 0)),
        ],
        out_specs=pl.BlockSpec((1, ATT_Q_TILE, LANES), lambda bi, h, i: (bi, i, h)),
        out_shape=jax.ShapeDtypeStruct((b, t, D_C), BF16),
        name="diff_attention",
        compiler_params=_params(("arbitrary", "arbitrary", "arbitrary")),
    )(zd, kt, zd, lqk, subln)


def _proj_out_kernel(x_ref, mod_ref, oa_ref, ob_ref, oc_ref, w_ref, o_ref, *, tm, n_latent):
    is_ctx = _is_ctx_rows(tm, n_latent)
    gate = _mod_row(mod_ref, 5, is_ctx)
    cat = jnp.concatenate([oa_ref[0], ob_ref[0], oc_ref[0]], axis=-1)
    o_ref[0] = x_ref[0] + gate * jnp.dot(cat, w_ref[...], preferred_element_type=F32)


def _proj_out(x, modt, oa, ob, oc, w_out, *, n_latent):
    b, t, d = x.shape
    tm = TOKEN_TILE
    tile = lambda width: pl.BlockSpec((1, tm, width), lambda i, j: (i, j, 0))
    return pl.pallas_call(
        functools.partial(_proj_out_kernel, tm=tm, n_latent=n_latent),
        grid=(b, t // tm),
        in_specs=[tile(d), pl.BlockSpec((1, 2, N_ADA, d), lambda i, j: (i, 0, 0, 0)),
                  tile(D_A), tile(D_B), tile(D_C), _resident((d, d))],
        out_specs=tile(d),
        out_shape=jax.ShapeDtypeStruct((b, t, d), F32),
        name="proj_out",
        compiler_params=_params(("arbitrary", "arbitrary")),
    )(x, modt, oa, ob, oc, w_out)


def _rope_tables(n_latent):
    tok = jnp.arange(n_latent)
    row = (tok // GRID_W).astype(F32)
    col = (tok % GRID_W).astype(F32)
    n_freq = DA_QK_DIM // 4
    freqs = ROPE_BASE ** (-jnp.arange(n_freq, dtype=F32) / n_freq)
    ang = jnp.concatenate([row[:, None] * freqs, col[:, None] * freqs], axis=-1)
    cos, sin = jnp.cos(ang), jnp.sin(ang)
    cos_t = jnp.concatenate([cos, cos, cos, cos], axis=-1)
    sin_t = jnp.concatenate([-sin, sin, -sin, sin], axis=-1)
    cos_t = jnp.concatenate([cos_t, jnp.ones((CTX_LEN, LANES), F32)], axis=0)
    sin_t = jnp.concatenate([sin_t, jnp.zeros((CTX_LEN, LANES), F32)], axis=0)
    return cos_t, sin_t


def kernel(x, c, ctx, c_ctx, w_ada, b_ada, ffn1_w1, ffn1_w3, ffn1_w2, w_in, w_out, na_rpb,
           gm_ws, gm_bs, gm_norm, da_lq1, da_lk1, da_lq2, da_lk2, da_subln,
           ffn2_w1, ffn2_w3, ffn2_w2, final_norm):
    b, n_latent, d = x.shape
    assert d == D_MODEL and ctx.shape[1] == CTX_LEN and b < 8
    assert (n_latent + CTX_LEN) % DA_KV_CHUNK == 0 and n_latent % ATT_Q_TILE == 0

    cvec = jnp.zeros((8, d), F32).at[:b].set(c).at[b].set(c_ctx)
    mod = _ada(cvec, w_ada, b_ada).reshape(DEPTH, 8, N_ADA, d)
    cos_t, sin_t = _rope_tables(n_latent)
    gmat = jnp.asarray(np.kron(np.eye(GM_GROUPS), np.ones((GM_WIDTH, GM_WIDTH))), BF16)

    xs = jnp.concatenate([x, ctx], axis=1)
    for l in range(DEPTH):
        last = l == DEPTH - 1
        lambda_init = 0.8 - 0.6 * math.exp(-0.3 * l)
        modt = jnp.stack([mod[l, :b], jnp.broadcast_to(mod[l, b], (b, N_ADA, d))], axis=1)
        bias = _na_bias_tables(na_rpb[l], n_latent)
        ws_stack = gm_ws[l].reshape(GM_GROUPS * GM_CHUNK, GM_CHUNK).astype(BF16)
        bs_full = jnp.repeat(gm_bs[l].T, GM_WIDTH, axis=1)
        gn = gm_norm[l].reshape(1, D_B)
        lqk = jnp.stack([da_lq1[l], da_lk1[l], da_lq2[l], da_lk2[l]])

        xs = _ffn(xs, modt, ffn1_w1[l].astype(BF16), ffn1_w3[l].astype(BF16),
                  ffn1_w2[l].astype(BF16), n_latent=n_latent, mi=0)
        za, kat, ob, zd, kt = _proj_in(xs, modt, w_in[l].astype(BF16), cos_t, sin_t, ws_stack,
                                       bs_full, gn, gmat, n_latent=n_latent)
        oa = _na_attention(za, kat, bias, n_latent=n_latent)
        oc = _diff_attention(zd, kt, lqk, da_subln[l].reshape(1, DA_V_DIM),
                             n_latent=n_latent, lambda_init=lambda_init)
        xs = _proj_out(xs, modt, oa, ob, oc, w_out[l].astype(BF16), n_latent=n_latent)
        xs = _ffn(xs, modt, ffn2_w1[l].astype(BF16), ffn2_w3[l].astype(BF16),
                  ffn2_w2[l].astype(BF16), n_latent=n_latent, mi=6,
                  final_gain=final_norm if last else None)
    return xs
```

```python
import functools
import math

import numpy as np
import jax
import jax.numpy as jnp
from jax import lax
from jax.experimental import pallas as pl
from jax.experimental.pallas import tpu as pltpu

F32 = jnp.float32
BF16 = jnp.bfloat16

D_MODEL = 1024
DEPTH = 2
GRID_W = 64
CTX_LEN = 256
N_ADA = 9
EPS = 1e-6
NEG_INF = -1e30
ROPE_BASE = 10000.0
D_FF = 2816
HEAD_DIM = 64
NA_HEADS = 4
NA_WIN_ROWS = 8
NA_WIN_COLS = 16
D_A = NA_HEADS * HEAD_DIM
GM_GROUPS = 4
GM_WIDTH = 64
GM_CHUNK = 128
D_B = GM_GROUPS * GM_WIDTH
DA_HEADS = 4
DA_QK_DIM = 64
DA_V_DIM = 128
D_C = DA_HEADS * DA_V_DIM
D_QK_C = DA_HEADS * 2 * DA_QK_DIM
D_IN = 3 * D_A + 2 * D_B + 2 * D_QK_C + D_C
QK_SCALE = HEAD_DIM ** -0.5
LOG2_E = math.log2(math.e)

LANES = 128
VMEM_BYTES_V7X = 64 * 1024 * 1024
VMEM_LIMIT = VMEM_BYTES_V7X - 8 * 1024 * 1024

TOKEN_TILE = 256
ATT_Q_TILE = 256
NA_Q_ROWS = ATT_Q_TILE // GRID_W
NA_K_ROWS = NA_Q_ROWS + NA_WIN_ROWS
NA_K_TOK = NA_K_ROWS * GRID_W
DA_KV_CHUNK = 1408
FF_CHUNKS = ((0, 768), (768, 1536), (1536, 2304), (2304, 2816))


def _params(sem):
    return pltpu.CompilerParams(dimension_semantics=sem, vmem_limit_bytes=VMEM_LIMIT)


def _resident(shape):
    nd = len(shape)
    return pl.BlockSpec(shape, lambda *_: (0,) * nd, pipeline_mode=pl.Buffered(1))


def _rms(x):
    return x * lax.rsqrt(jnp.mean(x * x, axis=-1, keepdims=True) + EPS)


def _mod_row(mod_ref, idx, is_ctx):
    return jnp.where(is_ctx, mod_ref[0, 1, idx:idx + 1, :], mod_ref[0, 0, idx:idx + 1, :])


def _is_ctx_rows(tm, n_latent):
    row0 = pl.program_id(1) * tm
    return (row0 + lax.broadcasted_iota(jnp.int32, (tm, 1), 0)) >= n_latent


def _ada_kernel(c_ref, w_ref, b_ref, o_ref):
    c = c_ref[...]
    cs = c / (1.0 + jnp.exp(-c))
    o_ref[0] = jnp.dot(cs, w_ref[0], preferred_element_type=F32,
                       precision=lax.Precision.HIGHEST) + b_ref[0]


def _ada(cvec, w_ada, b_ada):
    depth, d, n = w_ada.shape
    tn = 1024
    return pl.pallas_call(
        _ada_kernel,
        grid=(depth, n // tn),
        in_specs=[
            pl.BlockSpec((8, d), lambda l, j: (0, 0)),
            pl.BlockSpec((1, d, tn), lambda l, j: (l, 0, j)),
            pl.BlockSpec((1, 1, tn), lambda l, j: (l, 0, j)),
        ],
        out_specs=pl.BlockSpec((1, 8, tn), lambda l, j: (l, 0, j)),
        out_shape=jax.ShapeDtypeStruct((depth, 8, n), F32),
        name="ada",
        compiler_params=_params(("arbitrary", "arbitrary")),
    )(cvec, w_ada, b_ada.reshape(depth, 1, n))


def _ffn_kernel(x_ref, mod_ref, w1_ref, w3_ref, w2_ref, *rest, tm, n_latent, mi, final):
    if final:
        gain_ref, o_ref = rest
    else:
        (o_ref,) = rest
    x = x_ref[0]
    is_ctx = _is_ctx_rows(tm, n_latent)
    shift = _mod_row(mod_ref, mi, is_ctx)
    scale = _mod_row(mod_ref, mi + 1, is_ctx)
    gate = _mod_row(mod_ref, mi + 2, is_ctx)
    h = (_rms(x) * (1.0 + scale) + shift).astype(BF16)
    acc = jnp.zeros((tm, D_MODEL), F32)
    for c0, c1 in FF_CHUNKS:
        a = jnp.dot(h, w1_ref[:, c0:c1], preferred_element_type=F32)
        g = jnp.dot(h, w3_ref[:, c0:c1], preferred_element_type=F32)
        y = (a / (1.0 + jnp.exp(-a)) * g).astype(BF16)
        acc = acc + jnp.dot(y, w2_ref[c0:c1, :], preferred_element_type=F32)
    out = x + 0.5 * gate * acc
    if final:
        out = _rms(out) * gain_ref[...]
    o_ref[0] = out


def _ffn(x, modt, w1, w3, w2, *, n_latent, mi, final_gain=None):
    b, t, d = x.shape
    tm = TOKEN_TILE
    final = final_gain is not None
    t_out = n_latent if final else t
    in_specs = [
        pl.BlockSpec((1, tm, d), lambda i, j: (i, j, 0)),
        pl.BlockSpec((1, 2, N_ADA, d), lambda i, j: (i, 0, 0, 0)),
        _resident((d, D_FF)), _resident((d, D_FF)), _resident((D_FF, d)),
    ]
    args = [x, modt, w1, w3, w2]
    if final:
        in_specs.append(_resident((1, d)))
        args.append(final_gain.reshape(1, d))
    return pl.pallas_call(
        functools.partial(_ffn_kernel, tm=tm, n_latent=n_latent, mi=mi, final=final),
        grid=(b, t_out // tm),
        in_specs=in_specs,
        out_specs=pl.BlockSpec((1, tm, d), lambda i, j: (i, j, 0)),
        out_shape=jax.ShapeDtypeStruct((b, t_out, d), F32),
        name="ffn_final" if final else "ffn",
        compiler_params=_params(("arbitrary", "arbitrary")),
    )(*args)


def _swap_rope_halves(x, first_half):
    return jnp.where(first_half, pltpu.roll(x, 96, 1), pltpu.roll(x, 32, 1))


def _proj_in_kernel(x_ref, mod_ref, w_ref, cos_ref, sin_ref, ws_ref, bs_ref, gn_ref, gmat_ref,
                    za_ref, kat_ref, ob_ref, kd_ref, qt_ref, vt_ref, *, tm, n_latent):
    x = x_ref[0]
    is_ctx = _is_ctx_rows(tm, n_latent)
    shift = _mod_row(mod_ref, 3, is_ctx)
    scale = _mod_row(mod_ref, 4, is_ctx)
    h = (_rms(x) * (1.0 + scale) + shift).astype(BF16)
    z = jnp.dot(h, w_ref[...], preferred_element_type=F32)

    za_ref[0, :, 0:D_A] = (z[:, 0:D_A] * (QK_SCALE * LOG2_E)).astype(BF16)
    kat_ref[0] = z[:, D_A:2 * D_A].T.astype(BF16)
    za_ref[0, :, D_A:2 * D_A] = z[:, 2 * D_A:3 * D_A].astype(BF16)

    o = 3 * D_A
    u = z[:, o:o + D_B]
    v = z[:, o + D_B:o + 2 * D_B]
    u = 0.5 * u * (1.0 + lax.erf(u * (2.0 ** -0.5)))
    v = 0.5 * v * (1.0 + lax.erf(v * (2.0 ** -0.5)))
    v2 = v * v
    v2_hi = v2.astype(BF16)
    v2_lo = (v2 - v2_hi.astype(F32)).astype(BF16)
    ms = (jnp.dot(v2_hi, gmat_ref[...], preferred_element_type=F32)
          + jnp.dot(v2_lo, gmat_ref[...], preferred_element_type=F32)) * (1.0 / GM_WIDTH)
    vn = (v * lax.rsqrt(ms + EPS) * gn_ref[...]).astype(BF16)
    lane_group = lax.broadcasted_iota(jnp.int32, (1, D_B), 1) // GM_WIDTH
    for c in range(tm // GM_CHUNK):
        rows = slice(c * GM_CHUNK, (c + 1) * GM_CHUNK)
        r = jnp.dot(ws_ref[...], vn[rows, :], preferred_element_type=F32)
        s = bs_ref[...]
        for g in range(GM_GROUPS):
            s = s + jnp.where(lane_group == g, r[g * GM_CHUNK:(g + 1) * GM_CHUNK, :], 0.0)
        ob_ref[0, rows, :] = (u[rows, :] * s).astype(BF16)

    o = 3 * D_A + 2 * D_B
    cos = cos_ref[...]
    sin = sin_ref[...]
    first_half = (lax.broadcasted_iota(jnp.int32, (1, LANES), 1) % DA_QK_DIM) < (DA_QK_DIM // 2)
    n_q_blocks = D_QK_C // LANES
    for j in range(2 * n_q_blocks):
        zz = z[:, o + j * LANES:o + (j + 1) * LANES]
        r = zz * cos + _swap_rope_halves(zz, first_half) * sin
        if j < n_q_blocks:
            qt_ref[0, j * LANES:(j + 1) * LANES, :] = (r * (QK_SCALE * LOG2_E)).T.astype(BF16)
        else:
            jk = j - n_q_blocks
            kd_ref[0, :, jk * LANES:(jk + 1) * LANES] = r.astype(BF16)
    o = o + 2 * D_QK_C
    for j in range(D_C // LANES):
        vt_ref[0, j * LANES:(j + 1) * LANES, :] = z[:, o + j * LANES:o + (j + 1) * LANES].T.astype(BF16)


def _proj_in(x, modt, w_in, cos_t, sin_t, ws_stack, bs_full, gn, gmat, *, n_latent):
    b, t, d = x.shape
    tm = TOKEN_TILE
    return pl.pallas_call(
        functools.partial(_proj_in_kernel, tm=tm, n_latent=n_latent),
        grid=(b, t // tm),
        in_specs=[
            pl.BlockSpec((1, tm, d), lambda i, j: (i, j, 0)),
            pl.BlockSpec((1, 2, N_ADA, d), lambda i, j: (i, 0, 0, 0)),
            _resident((d, D_IN)),
            pl.BlockSpec((tm, LANES), lambda i, j: (j, 0)),
            pl.BlockSpec((tm, LANES), lambda i, j: (j, 0)),
            _resident((GM_GROUPS * GM_CHUNK, GM_CHUNK)),
            _resident((GM_CHUNK, D_B)),
            _resident((1, D_B)),
            _resident((D_B, D_B)),
        ],
        out_specs=[
            pl.BlockSpec((1, tm, 2 * D_A), lambda i, j: (i, j, 0)),
            pl.BlockSpec((1, D_A, tm), lambda i, j: (i, 0, j)),
            pl.BlockSpec((1, tm, D_B), lambda i, j: (i, j, 0)),
            pl.BlockSpec((1, tm, D_QK_C), lambda i, j: (i, j, 0)),
            pl.BlockSpec((1, D_QK_C, tm), lambda i, j: (i, 0, j)),
            pl.BlockSpec((1, D_C, tm), lambda i, j: (i, 0, j)),
        ],
        out_shape=[
            jax.ShapeDtypeStruct((b, t, 2 * D_A), BF16),
            jax.ShapeDtypeStruct((b, D_A, t), BF16),
            jax.ShapeDtypeStruct((b, t, D_B), BF16),
            jax.ShapeDtypeStruct((b, t, D_QK_C), BF16),
            jax.ShapeDtypeStruct((b, D_QK_C, t), BF16),
            jax.ShapeDtypeStruct((b, D_C, t), BF16),
        ],
        name="proj_in",
        compiler_params=_params(("arbitrary", "arbitrary")),
    )(x, modt, w_in, cos_t, sin_t, ws_stack, bs_full, gn, gmat)


_NT = (((1,), (1,)), ((), ()))


def _softmax_pv(parts):
    m = parts[0][0].max(axis=-1, keepdims=True)
    for s, _ in parts[1:]:
        m = jnp.maximum(m, s.max(axis=-1, keepdims=True))
    l = 0.0
    o = 0.0
    for s, v in parts:
        p = jnp.exp2(s - m)
        l = l + p.sum(axis=-1, keepdims=True)
        o = o + jnp.dot(p.astype(BF16), v, preferred_element_type=F32)
    return o / l


def _na_kernel(q_ref, kt_ref, v_ref, bias_ref, o_ref, *, n_latent):
    g = pl.program_id(2)
    n_groups = n_latent // ATT_Q_TILE
    rows = n_latent // GRID_W
    lane = lax.broadcasted_iota(jnp.int32, (1, LANES), 1)
    q = q_ref[0]
    qs = jnp.concatenate([jnp.where(lane < HEAD_DIM, q, jnp.zeros_like(q)),
                          jnp.where(lane >= HEAD_DIM, q, jnp.zeros_like(q))], axis=0)
    sc = jnp.dot(qs, kt_ref[0, :, n_latent:], preferred_element_type=F32)
    vc = v_ref[0, n_latent:, :]

    def finish(parts):
        o = _softmax_pv(parts)
        o_ref[0] = jnp.where(lane < HEAD_DIM, o[:ATT_Q_TILE], o[ATT_Q_TILE:]).astype(BF16)

    @pl.when(g < n_groups)
    def _():
        base_row = jnp.clip(g * NA_Q_ROWS - NA_WIN_ROWS // 2, 0, rows - NA_K_ROWS)
        base = pl.multiple_of(base_row * GRID_W, ATT_Q_TILE)
        sw = jnp.dot(qs, kt_ref[0, :, pl.ds(base, NA_K_TOK)], preferred_element_type=F32)
        sw = sw + bias_ref[0].reshape(2 * ATT_Q_TILE, NA_K_TOK)
        finish([(sw, v_ref[0, pl.ds(base, NA_K_TOK), :]), (sc, vc)])

    @pl.when(g == n_groups)
    def _():
        finish([(sc, vc)])


def _na_attention(za, kat, bias, *, n_latent):
    b, t, _ = za.shape
    n_groups = n_latent // ATT_Q_TILE
    assert t == n_latent + CTX_LEN and CTX_LEN == ATT_Q_TILE
    assert n_groups >= 3 and n_latent // GRID_W >= NA_K_ROWS

    def bias_map(i, hp, g):
        kind = jnp.where(g == 0, 0, jnp.where(g >= n_groups - 1, 2, 1))
        return (kind, hp, 0, 0)

    return pl.pallas_call(
        functools.partial(_na_kernel, n_latent=n_latent),
        grid=(b, NA_HEADS // 2, n_groups + 1),
        in_specs=[
            pl.BlockSpec((1, ATT_Q_TILE, LANES), lambda i, hp, g: (i, g, hp)),
            pl.BlockSpec((1, LANES, t), lambda i, hp, g: (i, hp, 0)),
            pl.BlockSpec((1, t, LANES), lambda i, hp, g: (i, 0, 2 + hp)),
            pl.BlockSpec((1, 2, ATT_Q_TILE, NA_K_TOK), bias_map),
        ],
        out_specs=pl.BlockSpec((1, ATT_Q_TILE, LANES), lambda i, hp, g: (i, g, hp)),
        out_shape=jax.ShapeDtypeStruct((b, t, D_A), BF16),
        name="na_attention",
        compiler_params=_params(("arbitrary", "arbitrary", "arbitrary")),
    )(za, kat, za, bias)


NA_BIAS_PAD = NA_WIN_ROWS // 2
NA_BIAS_W = 11 * LANES


def _na_bias_plan(n_latent):
    rows = n_latent // GRID_W
    n_groups = n_latent // ATT_Q_TILE
    plan = []
    for g in (0, 1, n_groups - 1):
        base = int(np.clip(g * NA_Q_ROWS - NA_WIN_ROWS // 2, 0, rows - NA_K_ROWS))
        kind = []
        for i in range(NA_Q_ROWS):
            r = g * NA_Q_ROWS + i
            start = int(np.clip(r - NA_WIN_ROWS // 2, 0, rows - NA_WIN_ROWS))
            e0 = base - r + NA_WIN_ROWS - 1 + NA_BIAS_PAD
            assert 0 <= e0 and (e0 - e0 % 2) * GRID_W + NA_K_TOK <= NA_BIAS_W
            kind.append((e0, start - base, start - base + NA_WIN_ROWS - 1))
        plan.append(tuple(kind))
    return tuple(plan)


def _na_bias_kernel(r_ref, o_ref, *, plan):
    shape = (GRID_W, NA_BIAS_W)
    qc = lax.broadcasted_iota(jnp.int32, shape, 0)
    col = lax.broadcasted_iota(jnp.int32, shape, 1)
    kc = col % GRID_W
    dc = kc - qc + (NA_WIN_COLS - 1)
    w0 = jnp.clip(qc - NA_WIN_COLS // 2, 0, GRID_W - NA_WIN_COLS)
    col_ok = (kc >= w0) & (kc < w0 + NA_WIN_COLS)
    tables = []
    for p in range(2):
        a = col // GRID_W + (p - NA_BIAS_PAD)
        acc = jnp.zeros(shape, F32)
        for b in range(2 * NA_WIN_COLS - 1):
            acc = acc + jnp.where(dc == b, r_ref[0, p, b:b + 1, :], 0.0)
        ok = col_ok & (a >= 0) & (a <= 2 * NA_WIN_ROWS - 2)
        tables.append(jnp.where(ok, acc * LOG2_E, NEG_INF))
    kj = lax.broadcasted_iota(jnp.int32, (GRID_W, NA_K_TOK), 1) // GRID_W
    for k, kind in enumerate(plan):
        for i, (e0, jlo, jhi) in enumerate(kind):
            p = e0 % 2
            off = (e0 - p) * GRID_W
            slab = tables[p][:, off:off + NA_K_TOK]
            o_ref[k, 0, i * GRID_W:(i + 1) * GRID_W, :] = jnp.where(
                (kj >= jlo) & (kj <= jhi), slab, NEG_INF)


def _na_bias_tables(rpb, n_latent):
    h = rpb.shape[0]
    n_blocks = NA_BIAS_W // GRID_W + 2
    rp = jnp.pad(rpb, ((0, 0), (NA_BIAS_PAD, n_blocks - NA_BIAS_PAD - rpb.shape[1]), (0, 0)))
    rep = jnp.transpose(jnp.repeat(rp, GRID_W, axis=1), (0, 2, 1))
    r = jnp.stack([rep[:, :, p * GRID_W:p * GRID_W + NA_BIAS_W] for p in range(2)], axis=1)
    n_dc = rpb.shape[2]
    return pl.pallas_call(
        functools.partial(_na_bias_kernel, plan=_na_bias_plan(n_latent)),
        grid=(h,),
        in_specs=[pl.BlockSpec((1, 2, n_dc, NA_BIAS_W), lambda i: (i, 0, 0, 0))],
        out_specs=pl.BlockSpec((3, 1, ATT_Q_TILE, NA_K_TOK), lambda i: (0, i, 0, 0)),
        out_shape=jax.ShapeDtypeStruct((3, h, ATT_Q_TILE, NA_K_TOK), F32),
        name="na_bias",
        compiler_params=_params(("arbitrary",)),
    )(r)


def _diff_kernel(qt_ref, k_ref, vt_ref, lqk_ref, subln_ref, o_ref, s_ref, *, n_latent, lambda_init):
    i = pl.program_id(2)
    tq = ATT_Q_TILE
    n_q_tiles = n_latent // tq
    feat = lax.broadcasted_iota(jnp.int32, (LANES, 1), 0)
    qt = qt_ref[0]
    qst = jnp.concatenate([jnp.where(feat < DA_QK_DIM, qt, jnp.zeros_like(qt)),
                           jnp.where(feat >= DA_QK_DIM, qt, jnp.zeros_like(qt))], axis=1)
    lqk = lqk_ref[...]
    lam = (jnp.exp(jnp.sum(lqk[0:1] * lqk[1:2], axis=-1, keepdims=True))
           - jnp.exp(jnp.sum(lqk[2:3] * lqk[3:4], axis=-1, keepdims=True)) + lambda_init)

    def col_reduce(x, op):
        r, c = x.shape
        groups = 8 if r % 64 == 0 else 1
        x = x.reshape(groups, r // (8 * groups), 8, c)
        return op(op(op(x, axis=1), axis=0), axis=0, keepdims=True)

    def scores(k):
        return jnp.dot(k, qst, preferred_element_type=F32)

    def step(s, vt, state):
        m, l, acc = state
        m_new = jnp.maximum(m, col_reduce(s, jnp.max))
        alpha = jnp.exp2(m - m_new)
        p = jnp.exp2(s - m_new)
        l = alpha * l + col_reduce(p, jnp.sum)
        acc = alpha * acc + jnp.dot(vt, p.astype(BF16), preferred_element_type=F32)
        return (m_new, l, acc)

    def init():
        return (jnp.full((1, 2 * tq), NEG_INF, F32), jnp.zeros((1, 2 * tq), F32),
                jnp.zeros((DA_V_DIM, 2 * tq), F32))

    def finish(state):
        _, l, acc = state
        o = acc / l
        o = o[:, :tq] - lam * o[:, tq:]
        o = o * lax.rsqrt(jnp.mean(o * o, axis=0, keepdims=True) + EPS)
        o = o * (subln_ref[...] * (1.0 - lambda_init))
        o_ref[0] = o.T.astype(BF16)

    @pl.when(i < n_q_tiles)
    def _():
        n_chunks = (n_latent + CTX_LEN) // DA_KV_CHUNK
        chunk = lambda c: slice(c * DA_KV_CHUNK, (c + 1) * DA_KV_CHUNK)
        s_ref[0] = scores(k_ref[0, chunk(0), :])
        state = init()
        for c in range(n_chunks):
            if c + 1 < n_chunks:
                s_ref[(c + 1) % 2] = scores(k_ref[0, chunk(c + 1), :])
            state = step(s_ref[c % 2], vt_ref[0, :, chunk(c)], state)
        finish(state)

    @pl.when(i == n_q_tiles)
    def _():
        finish(step(scores(k_ref[0, n_latent:, :]), vt_ref[0, :, n_latent:], init()))


def _diff_attention(kd, qt, vt, lqk, subln, *, n_latent, lambda_init):
    b, t, _ = kd.shape
    n_q_tiles = n_latent // ATT_Q_TILE
    return pl.pallas_call(
        functools.partial(_diff_kernel, n_latent=n_latent, lambda_init=lambda_init),
        grid=(b, DA_HEADS, n_q_tiles + 1),
        in_specs=[
            pl.BlockSpec((1, LANES, ATT_Q_TILE), lambda bi, h, i: (bi, h, i)),
            pl.BlockSpec((1, t, LANES), lambda bi, h, i: (bi, 0, h)),
            pl.BlockSpec((1, LANES, t), lambda bi, h, i: (bi, h, 0)),
            pl.BlockSpec((4, DA_QK_DIM), lambda bi, h, i: (0, 0)),
            pl.BlockSpec((DA_V_DIM, 1), lambda bi, h, i: (0, 0)),
        ],
        out_specs=pl.BlockSpec((1, ATT_Q_TILE, LANES), lambda bi, h, i: (bi, i, h)),
        out_shape=jax.ShapeDtypeStruct((b, t, D_C), BF16),
        scratch_shapes=[pltpu.VMEM((2, DA_KV_CHUNK, 2 * ATT_Q_TILE), F32)],
        name="diff_attention",
        compiler_params=_params(("arbitrary", "arbitrary", "arbitrary")),
    )(qt, kd, vt, lqk, subln)


def _proj_out_kernel(x_ref, mod_ref, oa_ref, ob_ref, oc_ref, w_ref, o_ref, *, tm, n_latent):
    is_ctx = _is_ctx_rows(tm, n_latent)
    gate = _mod_row(mod_ref, 5, is_ctx)
    cat = jnp.concatenate([oa_ref[0], ob_ref[0], oc_ref[0]], axis=-1)
    o_ref[0] = x_ref[0] + gate * jnp.dot(cat, w_ref[...], preferred_element_type=F32)


def _proj_out(x, modt, oa, ob, oc, w_out, *, n_latent):
    b, t, d = x.shape
    tm = TOKEN_TILE
    tile = lambda width: pl.BlockSpec((1, tm, width), lambda i, j: (i, j, 0))
    return pl.pallas_call(
        functools.partial(_proj_out_kernel, tm=tm, n_latent=n_latent),
        grid=(b, t // tm),
        in_specs=[tile(d), pl.BlockSpec((1, 2, N_ADA, d), lambda i, j: (i, 0, 0, 0)),
                  tile(D_A), tile(D_B), tile(D_C), _resident((d, d))],
        out_specs=tile(d),
        out_shape=jax.ShapeDtypeStruct((b, t, d), F32),
        name="proj_out",
        compiler_params=_params(("arbitrary", "arbitrary")),
    )(x, modt, oa, ob, oc, w_out)


def _rope_tables(n_latent):
    tok = jnp.arange(n_latent)
    row = (tok // GRID_W).astype(F32)
    col = (tok % GRID_W).astype(F32)
    n_freq = DA_QK_DIM // 4
    freqs = ROPE_BASE ** (-jnp.arange(n_freq, dtype=F32) / n_freq)
    ang = jnp.concatenate([row[:, None] * freqs, col[:, None] * freqs], axis=-1)
    cos, sin = jnp.cos(ang), jnp.sin(ang)
    cos_t = jnp.concatenate([cos, cos, cos, cos], axis=-1)
    sin_t = jnp.concatenate([-sin, sin, -sin, sin], axis=-1)
    cos_t = jnp.concatenate([cos_t, jnp.ones((CTX_LEN, LANES), F32)], axis=0)
    sin_t = jnp.concatenate([sin_t, jnp.zeros((CTX_LEN, LANES), F32)], axis=0)
    return cos_t, sin_t


def kernel(x, c, ctx, c_ctx, w_ada, b_ada, ffn1_w1, ffn1_w3, ffn1_w2, w_in, w_out, na_rpb,
           gm_ws, gm_bs, gm_norm, da_lq1, da_lk1, da_lq2, da_lk2, da_subln,
           ffn2_w1, ffn2_w3, ffn2_w2, final_norm):
    b, n_latent, d = x.shape
    assert d == D_MODEL and ctx.shape[1] == CTX_LEN and b < 8
    assert (n_latent + CTX_LEN) % DA_KV_CHUNK == 0 and n_latent % ATT_Q_TILE == 0

    cvec = jnp.zeros((8, d), F32).at[:b].set(c).at[b].set(c_ctx)
    mod = _ada(cvec, w_ada, b_ada).reshape(DEPTH, 8, N_ADA, d)
    cos_t, sin_t = _rope_tables(n_latent)
    gmat = jnp.asarray(np.kron(np.eye(GM_GROUPS), np.ones((GM_WIDTH, GM_WIDTH))), BF16)

    xs = jnp.concatenate([x, ctx], axis=1)
    for l in range(DEPTH):
        last = l == DEPTH - 1
        lambda_init = 0.8 - 0.6 * math.exp(-0.3 * l)
        modt = jnp.stack([mod[l, :b], jnp.broadcast_to(mod[l, b], (b, N_ADA, d))], axis=1)
        bias = _na_bias_tables(na_rpb[l], n_latent)
        ws_stack = gm_ws[l].reshape(GM_GROUPS * GM_CHUNK, GM_CHUNK).astype(BF16)
        bs_full = jnp.repeat(gm_bs[l].T, GM_WIDTH, axis=1)
        gn = gm_norm[l].reshape(1, D_B)
        lqk = jnp.stack([da_lq1[l], da_lk1[l], da_lq2[l], da_lk2[l]])

        xs = _ffn(xs, modt, ffn1_w1[l].astype(BF16), ffn1_w3[l].astype(BF16),
                  ffn1_w2[l].astype(BF16), n_latent=n_latent, mi=0)
        za, kat, ob, kd, qt, vt = _proj_in(xs, modt, w_in[l].astype(BF16), cos_t, sin_t, ws_stack,
                                           bs_full, gn, gmat, n_latent=n_latent)
        oa = _na_attention(za, kat, bias, n_latent=n_latent)
        oc = _diff_attention(kd, qt, vt, lqk, da_subln[l].reshape(DA_V_DIM, 1),
                             n_latent=n_latent, lambda_init=lambda_init)
        xs = _proj_out(xs, modt, oa, ob, oc, w_out[l].astype(BF16), n_latent=n_latent)
        xs = _ffn(xs, modt, ffn2_w1[l].astype(BF16), ffn2_w3[l].astype(BF16),
                  ffn2_w2[l].astype(BF16), n_latent=n_latent, mi=6,
                  final_gain=final_norm if last else None)
    return xs
```

```python
import functools
import math

import numpy as np
import jax
import jax.numpy as jnp
from jax import lax
from jax.experimental import pallas as pl
from jax.experimental.pallas import tpu as pltpu

F32 = jnp.float32
BF16 = jnp.bfloat16

D_MODEL = 1024
DEPTH = 2
GRID_W = 64
CTX_LEN = 256
N_ADA = 9
EPS = 1e-6
NEG_INF = -1e30
ROPE_BASE = 10000.0
D_FF = 2816
HEAD_DIM = 64
NA_HEADS = 4
NA_WIN_ROWS = 8
NA_WIN_COLS = 16
D_A = NA_HEADS * HEAD_DIM
GM_GROUPS = 4
GM_WIDTH = 64
GM_CHUNK = 128
D_B = GM_GROUPS * GM_WIDTH
DA_HEADS = 4
DA_QK_DIM = 64
DA_V_DIM = 128
D_C = DA_HEADS * DA_V_DIM
D_QK_C = DA_HEADS * 2 * DA_QK_DIM
D_IN = 3 * D_A + 2 * D_B + 2 * D_QK_C + D_C
QK_SCALE = HEAD_DIM ** -0.5
LOG2_E = math.log2(math.e)

LANES = 128
VMEM_BYTES_V7X = 64 * 1024 * 1024
VMEM_LIMIT = VMEM_BYTES_V7X - 8 * 1024 * 1024

TOKEN_TILE = 256
ATT_Q_TILE = 256
NA_Q_ROWS = ATT_Q_TILE // GRID_W
NA_K_ROWS = NA_Q_ROWS + NA_WIN_ROWS
NA_K_TOK = NA_K_ROWS * GRID_W
DA_KV_CHUNK = 1408
DA_KV_RAMP_UP = (256, 768)
DA_KV_RAMP_DOWN = (1024, 512, 256)
DA_KV_SMALL_CHUNK = 768
FF_CHUNKS =((0, 768), (768, 1536), (1536, 2304), (2304, 2816))


def _params(sem):
    return pltpu.CompilerParams(dimension_semantics=sem, vmem_limit_bytes=VMEM_LIMIT)


def _resident(shape):
    nd = len(shape)
    return pl.BlockSpec(shape, lambda *_: (0,) * nd, pipeline_mode=pl.Buffered(1))


def _rms(x):
    return x * lax.rsqrt(jnp.mean(x * x, axis=-1, keepdims=True) + EPS)


def _mod_row(mod_ref, idx, is_ctx):
    return jnp.where(is_ctx, mod_ref[0, 1, idx:idx + 1, :], mod_ref[0, 0, idx:idx + 1, :])


def _is_ctx_rows(tm, n_latent):
    row0 = pl.program_id(1) * tm
    return (row0 + lax.broadcasted_iota(jnp.int32, (tm, 1), 0)) >= n_latent


def _ada_kernel(c_ref, w_ref, b_ref, o_ref):
    c = c_ref[...]
    cs = c / (1.0 + jnp.exp(-c))
    o_ref[0] = jnp.dot(cs, w_ref[0], preferred_element_type=F32,
                       precision=lax.Precision.HIGHEST) + b_ref[0]


def _ada(cvec, w_ada, b_ada):
    depth, d, n = w_ada.shape
    tn = 1024
    return pl.pallas_call(
        _ada_kernel,
        grid=(depth, n // tn),
        in_specs=[
            pl.BlockSpec((8, d), lambda l, j: (0, 0)),
            pl.BlockSpec((1, d, tn), lambda l, j: (l, 0, j)),
            pl.BlockSpec((1, 1, tn), lambda l, j: (l, 0, j)),
        ],
        out_specs=pl.BlockSpec((1, 8, tn), lambda l, j: (l, 0, j)),
        out_shape=jax.ShapeDtypeStruct((depth, 8, n), F32),
        name="ada",
        compiler_params=_params(("arbitrary", "arbitrary")),
    )(cvec, w_ada, b_ada.reshape(depth, 1, n))


def _ffn_kernel(*refs, tm, n_latent, mi, split_ctx, mix, final):
    refs = list(refs)
    x_ref, mod_ref = refs[:2]
    del refs[:2]
    if split_ctx:
        ctx_ref = refs.pop(0)
    if mix:
        oa_ref, ob_ref, oc_ref, wout_ref = refs[:4]
        del refs[:4]
    w1_ref, w3_ref, w2_ref = refs[:3]
    del refs[:3]
    if final:
        gain_ref = refs.pop(0)
    (o_ref,) = refs
    x = x_ref[0]
    is_ctx = _is_ctx_rows(tm, n_latent)
    if split_ctx:
        x = jnp.where(is_ctx, ctx_ref[0], x)
    if mix:
        cat =jnp.concatenate([oa_ref[0], ob_ref[0], oc_ref[0]], axis=-1)
        x = x + _mod_row(mod_ref, mi - 1, is_ctx) * jnp.dot(cat, wout_ref[...],
                                                            preferred_element_type=F32)
    shift = _mod_row(mod_ref, mi, is_ctx)
    scale = _mod_row(mod_ref, mi + 1, is_ctx)
    gate = _mod_row(mod_ref, mi + 2, is_ctx)
    h = (_rms(x) * (1.0 + scale) + shift).astype(BF16)
    acc = jnp.zeros((tm, D_MODEL), F32)
    for c0, c1 in FF_CHUNKS:
        a = jnp.dot(h, w1_ref[:, c0:c1], preferred_element_type=F32)
        g = jnp.dot(h, w3_ref[:, c0:c1], preferred_element_type=F32)
        y = (a / (1.0 + jnp.exp(-a)) * g).astype(BF16)
        acc = acc + jnp.dot(y, w2_ref[c0:c1, :], preferred_element_type=F32)
    out = x + 0.5 * gate * acc
    if final:
        out = _rms(out) * gain_ref[...]
    o_ref[0] = out


def _ffn(x, modt, w1, w3, w2, *, n_latent, mi, ctx=None, mix=None, final_gain=None):
    b, _, d = x.shape
    t = n_latent + CTX_LEN
    tm = TOKEN_TILE
    final = final_gain is not None
    t_out = n_latent if final else t
    tile = lambda width: pl.BlockSpec((1, tm, width), lambda i, j: (i, j, 0))
    in_specs = [tile(d), pl.BlockSpec((1, 2, N_ADA, d), lambda i, j: (i, 0, 0, 0))]
    args = [x, modt]
    if ctx is not None:
        assert tm == CTX_LEN and not final
        last_latent = n_latent // tm - 1
        in_specs[0] = pl.BlockSpec((1, tm, d), lambda i, j: (i, jnp.minimum(j, last_latent), 0))
        in_specs.append(pl.BlockSpec((1, CTX_LEN, d), lambda i, j: (i, 0, 0)))
        args.append(ctx)
    if mix is not None:
        in_specs += [tile(D_A), tile(D_B), tile(D_C), _resident((d, d))]
        args += list(mix)
    in_specs += [_resident((d, D_FF)), _resident((d, D_FF)), _resident((D_FF, d))]
    args += [w1, w3, w2]
    if final:
        in_specs.append(_resident((1, d)))
        args.append(final_gain.reshape(1, d))
    return pl.pallas_call(
        functools.partial(_ffn_kernel, tm=tm, n_latent=n_latent, mi=mi, split_ctx=ctx is not None,
                          mix=mix is not None, final=final),
        grid=(b, t_out // tm),
        in_specs=in_specs,
        out_specs=pl.BlockSpec((1, tm, d), lambda i, j: (i, j, 0)),
        out_shape=jax.ShapeDtypeStruct((b, t_out, d), F32),
        name="ffn_final" if final else ("ffn_mix" if mix is not None else "ffn"),
        compiler_params=_params(("arbitrary", "arbitrary")),
    )(*args)


def _swap_rope_halves(x, first_half):
    return jnp.where(first_half, pltpu.roll(x, 96, 1), pltpu.roll(x, 32, 1))


def _proj_in_kernel(x_ref, mod_ref, w_ref, cos_ref, sin_ref, ws_ref, bs_ref, gn_ref, gmat_ref,
                    za_ref, kat_ref, ob_ref, kd_ref, qt_ref, vt_ref, *, tm, n_latent):
    x = x_ref[0]
    is_ctx = _is_ctx_rows(tm, n_latent)
    shift = _mod_row(mod_ref, 3, is_ctx)
    scale = _mod_row(mod_ref, 4, is_ctx)
    h = (_rms(x) * (1.0 + scale) + shift).astype(BF16)
    z = jnp.dot(h, w_ref[...], preferred_element_type=F32)

    za_ref[0, :, 0:D_A] = (z[:, 0:D_A] * (QK_SCALE * LOG2_E)).astype(BF16)
    kat_ref[0] = z[:, D_A:2 * D_A].T.astype(BF16)
    za_ref[0, :, D_A:2 * D_A] = z[:, 2 * D_A:3 * D_A].astype(BF16)

    o = 3 * D_A
    u = z[:, o:o + D_B]
    v = z[:, o + D_B:o + 2 * D_B]
    u = 0.5 * u * (1.0 + lax.erf(u * (2.0 ** -0.5)))
    v = 0.5 * v * (1.0 + lax.erf(v * (2.0 ** -0.5)))
    v2 = v * v
    v2_hi = v2.astype(BF16)
    v2_lo = (v2 - v2_hi.astype(F32)).astype(BF16)
    ms = (jnp.dot(v2_hi, gmat_ref[...], preferred_element_type=F32)
          + jnp.dot(v2_lo, gmat_ref[...], preferred_element_type=F32)) * (1.0 / GM_WIDTH)
    vn = (v * lax.rsqrt(ms + EPS) * gn_ref[...]).astype(BF16)
    lane_group = lax.broadcasted_iota(jnp.int32, (1, D_B), 1) // GM_WIDTH
    for c in range(tm // GM_CHUNK):
        rows = slice(c * GM_CHUNK, (c + 1) * GM_CHUNK)
        r = jnp.dot(ws_ref[...], vn[rows, :], preferred_element_type=F32)
        s = bs_ref[...]
        for g in range(GM_GROUPS):
            s = s + jnp.where(lane_group == g, r[g * GM_CHUNK:(g + 1) * GM_CHUNK, :], 0.0)
        ob_ref[0, rows, :] = (u[rows, :] * s).astype(BF16)

    o = 3 * D_A + 2 * D_B
    cos = cos_ref[...]
    sin = sin_ref[...]
    first_half = (lax.broadcasted_iota(jnp.int32, (1, LANES), 1) % DA_QK_DIM) < (DA_QK_DIM // 2)
    n_q_blocks = D_QK_C // LANES
    for j in range(2 * n_q_blocks):
        zz = z[:, o + j * LANES:o + (j + 1) * LANES]
        r = zz * cos + _swap_rope_halves(zz, first_half) * sin
        if j < n_q_blocks:
            qt_ref[0, j * LANES:(j + 1) * LANES, :] = (r * (QK_SCALE * LOG2_E)).T.astype(BF16)
        else:
            jk = j - n_q_blocks
            kd_ref[0, :, jk * LANES:(jk + 1) * LANES] = r.astype(BF16)
    o = o + 2 * D_QK_C
    for j in range(D_C // LANES):
        vt_ref[0, j * LANES:(j + 1) * LANES, :] = z[:, o + j * LANES:o + (j + 1) * LANES].T.astype(BF16)


def _proj_in(x, modt, w_in, cos_t, sin_t, ws_stack, bs_full, gn, gmat, *, n_latent):
    b, t, d = x.shape
    tm = TOKEN_TILE
    return pl.pallas_call(
        functools.partial(_proj_in_kernel, tm=tm, n_latent=n_latent),
        grid=(b, t // tm),
        in_specs=[
            pl.BlockSpec((1, tm, d), lambda i, j: (i, j, 0)),
            pl.BlockSpec((1, 2, N_ADA, d), lambda i, j: (i, 0, 0, 0)),
            _resident((d, D_IN)),
            pl.BlockSpec((tm, LANES), lambda i, j: (j, 0)),
            pl.BlockSpec((tm, LANES), lambda i, j: (j, 0)),
            _resident((GM_GROUPS * GM_CHUNK, GM_CHUNK)),
            _resident((GM_CHUNK, D_B)),
            _resident((1, D_B)),
            _resident((D_B, D_B)),
        ],
        out_specs=[
            pl.BlockSpec((1, tm, 2 * D_A), lambda i, j: (i, j, 0)),
            pl.BlockSpec((1, D_A, tm), lambda i, j: (i, 0, j)),
            pl.BlockSpec((1, tm, D_B), lambda i, j: (i, j, 0)),
            pl.BlockSpec((1, tm, D_QK_C), lambda i, j: (i, j, 0)),
            pl.BlockSpec((1, D_QK_C, tm), lambda i, j: (i, 0, j)),
            pl.BlockSpec((1, D_C, tm), lambda i, j: (i, 0, j)),
        ],
        out_shape=[
            jax.ShapeDtypeStruct((b, t, 2 * D_A), BF16),
            jax.ShapeDtypeStruct((b, D_A, t), BF16),
            jax.ShapeDtypeStruct((b, t, D_B), BF16),
            jax.ShapeDtypeStruct((b, t, D_QK_C), BF16),
            jax.ShapeDtypeStruct((b, D_QK_C, t), BF16),
            jax.ShapeDtypeStruct((b, D_C, t), BF16),
        ],
        name="proj_in",
        compiler_params=_params(("arbitrary", "arbitrary")),
    )(x, modt, w_in, cos_t, sin_t, ws_stack, bs_full, gn, gmat)


def _na_kernel(q_ref, qn_ref, kt_ref, v_ref, bias_ref, biasn_ref, o_ref, sa_ref, sb_ref, *, n_latent):
    g = pl.program_id(2)
    n_groups = n_latent // ATT_Q_TILE
    rows = n_latent // GRID_W
    lane = lax.broadcasted_iota(jnp.int32, (1, LANES), 1)

    def window_start(grp):
        base_row = jnp.clip(grp * NA_Q_ROWS - NA_WIN_ROWS // 2, 0, rows - NA_K_ROWS)
        return pl.multiple_of(base_row * GRID_W, ATT_Q_TILE)

    def write_scores(s_ref, q, bias, grp):
        qs = jnp.concatenate([jnp.where(lane < HEAD_DIM, q, jnp.zeros_like(q)),
                              jnp.where(lane >= HEAD_DIM, q, jnp.zeros_like(q))], axis=0)
        sw = jnp.dot(qs, kt_ref[0, :, pl.ds(window_start(grp), NA_K_TOK)],
                     preferred_element_type=F32)
        s_ref[:, 0:NA_K_TOK] = sw + bias.reshape(2 * ATT_Q_TILE, NA_K_TOK)
        s_ref[:, NA_K_TOK:] = jnp.dot(qs, kt_ref[0, :, n_latent:], preferred_element_type=F32)

    def softmax_pv(s_ref):
        s = s_ref[...]
        p = jnp.exp2(s - s.max(axis=-1, keepdims=True))
        l = p.sum(axis=-1, keepdims=True)
        p = p.astype(BF16)
        o = (jnp.dot(p[:, 0:NA_K_TOK], v_ref[0, pl.ds(window_start(g), NA_K_TOK), :],
                     preferred_element_type=F32)
             + jnp.dot(p[:, NA_K_TOK:], v_ref[0, n_latent:, :], preferred_element_type=F32)) / l
        o_ref[0] = jnp.where(lane < HEAD_DIM, o[:ATT_Q_TILE], o[ATT_Q_TILE:]).astype(BF16)

    @pl.when(g == 0)
    def _():
        write_scores(sa_ref, q_ref[0], bias_ref[0], g)

    g_next = jnp.minimum(g + 1, n_groups)

    @pl.when(g % 2 == 0)
    def _():
        write_scores(sb_ref, qn_ref[0], biasn_ref[0], g_next)
        softmax_pv(sa_ref)

    @pl.when(g % 2 == 1)
    def _():
        write_scores(sa_ref, qn_ref[0], biasn_ref[0], g_next)
        softmax_pv(sb_ref)


def _na_attention(za, kat, bias, *, n_latent):
    b, t, _ = za.shape
    n_groups = n_latent // ATT_Q_TILE
    assert t == n_latent + CTX_LEN and CTX_LEN == ATT_Q_TILE
    assert n_groups >= 3 and n_latent // GRID_W >= NA_K_ROWS

    def kind(g):
        return jnp.where(g == 0, 0, jnp.where(g < n_groups - 1, 1, jnp.where(g == n_groups - 1, 2, 3)))

    nxt = lambda g: jnp.minimum(g + 1, n_groups)
    q_spec = lambda at: pl.BlockSpec((1, ATT_Q_TILE, LANES), lambda i, hp, g: (i, at(g), hp))
    bias_spec = lambda at: pl.BlockSpec((1, 2, ATT_Q_TILE, NA_K_TOK),
                                        lambda i, hp, g: (kind(at(g)), hp, 0, 0))
    same = lambda g: g
    score_buf = pltpu.VMEM((2 * ATT_Q_TILE, NA_K_TOK + CTX_LEN), F32)
    return pl.pallas_call(
        functools.partial(_na_kernel, n_latent=n_latent),
        grid=(b, NA_HEADS // 2, n_groups + 1),
        in_specs=[
            q_spec(same), q_spec(nxt),
            pl.BlockSpec((1, LANES, t), lambda i, hp, g: (i, hp, 0)),
            pl.BlockSpec((1, t, LANES), lambda i, hp, g: (i, 0, 2 + hp)),
            bias_spec(same), bias_spec(nxt),
        ],
        out_specs=pl.BlockSpec((1, ATT_Q_TILE, LANES), lambda i, hp, g: (i, g, hp)),
        out_shape=jax.ShapeDtypeStruct((b, t, D_A), BF16),
        scratch_shapes=[score_buf, score_buf],
        name="na_attention",
        compiler_params=_params(("arbitrary", "arbitrary", "arbitrary")),
    )(za, za, kat, za, bias, bias)


NA_BIAS_PAD = NA_WIN_ROWS // 2
NA_BIAS_W = 11 * LANES


def _na_bias_plan(n_latent):
    rows = n_latent // GRID_W
    n_groups = n_latent // ATT_Q_TILE
    plan = []
    for g in (0, 1, n_groups - 1):
        base = int(np.clip(g * NA_Q_ROWS - NA_WIN_ROWS // 2, 0, rows - NA_K_ROWS))
        kind = []
        for i in range(NA_Q_ROWS):
            r = g * NA_Q_ROWS + i
            start = int(np.clip(r - NA_WIN_ROWS // 2, 0, rows - NA_WIN_ROWS))
            e0 = base - r + NA_WIN_ROWS - 1 + NA_BIAS_PAD
            assert 0 <= e0 and (e0 - e0 % 2) * GRID_W + NA_K_TOK <= NA_BIAS_W
            kind.append((e0, start - base, start - base + NA_WIN_ROWS - 1))
        plan.append(tuple(kind))
    return tuple(plan)


def _na_bias_kernel(r_ref, o_ref, *, plan):
    shape = (GRID_W, NA_BIAS_W)
    qc = lax.broadcasted_iota(jnp.int32, shape, 0)
    col = lax.broadcasted_iota(jnp.int32, shape, 1)
    kc = col % GRID_W
    dc = kc - qc + (NA_WIN_COLS - 1)
    w0 = jnp.clip(qc - NA_WIN_COLS // 2, 0, GRID_W - NA_WIN_COLS)
    col_ok = (kc >= w0) & (kc < w0 + NA_WIN_COLS)
    tables = []
    for p in range(2):
        a = col // GRID_W + (p - NA_BIAS_PAD)
        acc = jnp.zeros(shape, F32)
        for b in range(2 * NA_WIN_COLS - 1):
            acc = acc + jnp.where(dc == b, r_ref[0, p, b:b + 1, :], 0.0)
        ok = col_ok & (a >= 0) & (a <= 2 * NA_WIN_ROWS - 2)
        tables.append(jnp.where(ok, acc * LOG2_E, NEG_INF))
    kj = lax.broadcasted_iota(jnp.int32, (GRID_W, NA_K_TOK), 1) // GRID_W
    for k, kind in enumerate(plan):
        for i, (e0, jlo, jhi) in enumerate(kind):
            p = e0 % 2
            off = (e0 - p) * GRID_W
            slab = tables[p][:, off:off + NA_K_TOK]
            o_ref[k, 0, i * GRID_W:(i + 1) * GRID_W, :] = jnp.where(
                (kj >= jlo) & (kj <= jhi), slab, NEG_INF)
    o_ref[len(plan), 0] = jnp.full((ATT_Q_TILE, NA_K_TOK), NEG_INF, F32)


def _na_bias_tables(rpb, n_latent):
    h = rpb.shape[0]
    n_blocks = NA_BIAS_W // GRID_W + 2
    rp = jnp.pad(rpb, ((0, 0), (NA_BIAS_PAD, n_blocks - NA_BIAS_PAD - rpb.shape[1]), (0, 0)))
    rep = jnp.transpose(jnp.repeat(rp, GRID_W, axis=1), (0, 2, 1))
    r = jnp.stack([rep[:, :, p * GRID_W:p * GRID_W + NA_BIAS_W] for p in range(2)], axis=1)
    n_dc = rpb.shape[2]
    return pl.pallas_call(
        functools.partial(_na_bias_kernel, plan=_na_bias_plan(n_latent)),
        grid=(h,),
        in_specs=[pl.BlockSpec((1, 2, n_dc, NA_BIAS_W), lambda i: (i, 0, 0, 0))],
        out_specs=pl.BlockSpec((4, 1, ATT_Q_TILE, NA_K_TOK), lambda i: (0, i, 0, 0)),
        out_shape=jax.ShapeDtypeStruct((4, h, ATT_Q_TILE, NA_K_TOK), F32),
        name="na_bias",
        compiler_params=_params(("arbitrary",)),
    )(r)


def _kv_chunk_bounds(t):
    head, tail = DA_KV_RAMP_UP, DA_KV_RAMP_DOWN
    middle = t - sum(head) - sum(tail)
    if middle < 0 or middle % DA_KV_CHUNK:
        assert t % DA_KV_SMALL_CHUNK == 0
        sizes = (DA_KV_SMALL_CHUNK,) * (t // DA_KV_SMALL_CHUNK)
    else:
        sizes = head + (DA_KV_CHUNK,) * (middle // DA_KV_CHUNK) + tail
    return tuple(int(v) for v in np.cumsum((0,) + sizes))


def _diff_kernel(qt_ref, k_ref, vt_ref, lqk_ref, subln_ref, o_ref, s_ref, *, n_latent, lambda_init):
    i = pl.program_id(2)
    tq = ATT_Q_TILE
    n_q_tiles = n_latent // tq
    feat = lax.broadcasted_iota(jnp.int32, (LANES, 1), 0)
    qt = qt_ref[0]
    qst = jnp.concatenate([jnp.where(feat < DA_QK_DIM, qt, jnp.zeros_like(qt)),
                           jnp.where(feat >= DA_QK_DIM, qt, jnp.zeros_like(qt))], axis=1)
    lqk = lqk_ref[...]
    lam = (jnp.exp(jnp.sum(lqk[0:1] * lqk[1:2], axis=-1, keepdims=True))
           - jnp.exp(jnp.sum(lqk[2:3] * lqk[3:4], axis=-1, keepdims=True)) + lambda_init)

    def col_reduce(x, op):
        r, c = x.shape
        groups = 8 if r % 64 == 0 else 1
        x = x.reshape(groups, r // (8 * groups), 8, c)
        return op(op(op(x, axis=1), axis=0), axis=0, keepdims=True)

    def scores(k):
        s = jnp.dot(k, qst, preferred_element_type=F32)
        return s, col_reduce(s, jnp.max)

    def step(s, s_max, vt, state):
        m, l, acc = state
        m_new = jnp.maximum(m, s_max)
        alpha = jnp.exp2(m - m_new)
        p = jnp.exp2(s - m_new)
        l = alpha * l + col_reduce(p, jnp.sum)
        acc = alpha * acc + jnp.dot(vt, p.astype(BF16), preferred_element_type=F32)
        return (m_new, l, acc)

    def init():
        return (jnp.full((1, 2 * tq), NEG_INF, F32), jnp.zeros((1, 2 * tq), F32),
                jnp.zeros((DA_V_DIM, 2 * tq), F32))

    def finish(state):
        _, l, acc = state
        o = acc / l
        o = o[:, :tq] - lam * o[:, tq:]
        o = o * lax.rsqrt(jnp.mean(o * o, axis=0, keepdims=True) + EPS)
        o = o * (subln_ref[...] * (1.0 - lambda_init))
        o_ref[0] = o.T.astype(BF16)

    @pl.when(i < n_q_tiles)
    def _():
        bounds = _kv_chunk_bounds(n_latent + CTX_LEN)
        chunk = lambda c: slice(bounds[c], bounds[c + 1])
        size = lambda c: bounds[c + 1] - bounds[c]
        n_chunks = len(bounds) - 1
        s_ref[0, 0:size(0)], s_max = scores(k_ref[0, chunk(0), :])
        state = init()
        for c in range(n_chunks):
            if c + 1 < n_chunks:
                s_ref[(c + 1) % 2, 0:size(c + 1)], next_max = scores(k_ref[0, chunk(c + 1), :])
            state = step(s_ref[c % 2, 0:size(c)], s_max, vt_ref[0, :, chunk(c)], state)
            s_max = next_max
        finish(state)

    @pl.when(i == n_q_tiles)
    def _():
        s, s_max = scores(k_ref[0, n_latent:, :])
        finish(step(s, s_max, vt_ref[0, :, n_latent:], init()))


def _diff_attention(kd, qt, vt, lqk, subln, *, n_latent, lambda_init):
    b, t, _ = kd.shape
    n_q_tiles = n_latent // ATT_Q_TILE
    return pl.pallas_call(
        functools.partial(_diff_kernel, n_latent=n_latent, lambda_init=lambda_init),
        grid=(b, DA_HEADS, n_q_tiles + 1),
        in_specs=[
            pl.BlockSpec((1, LANES, ATT_Q_TILE), lambda bi, h, i: (bi, h, i)),
            pl.BlockSpec((1, t, LANES), lambda bi, h, i: (bi, 0, h)),
            pl.BlockSpec((1, LANES, t), lambda bi, h, i: (bi, h, 0)),
            pl.BlockSpec((4, DA_QK_DIM), lambda bi, h, i: (0, 0)),
            pl.BlockSpec((DA_V_DIM, 1), lambda bi, h, i: (0, 0)),
        ],
        out_specs=pl.BlockSpec((1, ATT_Q_TILE, LANES), lambda bi, h, i: (bi, i, h)),
        out_shape=jax.ShapeDtypeStruct((b, t, D_C), BF16),
        scratch_shapes=[pltpu.VMEM((2, DA_KV_CHUNK, 2 * ATT_Q_TILE), F32)],
        name="diff_attention",
        compiler_params=_params(("arbitrary", "arbitrary", "arbitrary")),
    )(qt, kd, vt, lqk, subln)


def _rope_tables(n_latent):
    tok = jnp.arange(n_latent)
    row = (tok // GRID_W).astype(F32)
    col = (tok % GRID_W).astype(F32)
    n_freq = DA_QK_DIM // 4
    freqs = ROPE_BASE ** (-jnp.arange(n_freq, dtype=F32) / n_freq)
    ang = jnp.concatenate([row[:, None] * freqs, col[:, None] * freqs], axis=-1)
    cos, sin = jnp.cos(ang), jnp.sin(ang)
    cos_t = jnp.concatenate([cos, cos, cos, cos], axis=-1)
    sin_t = jnp.concatenate([-sin, sin, -sin, sin], axis=-1)
    cos_t = jnp.concatenate([cos_t, jnp.ones((CTX_LEN, LANES), F32)], axis=0)
    sin_t = jnp.concatenate([sin_t, jnp.zeros((CTX_LEN, LANES), F32)], axis=0)
    return cos_t, sin_t


def kernel(x, c, ctx, c_ctx, w_ada, b_ada, ffn1_w1, ffn1_w3, ffn1_w2, w_in, w_out, na_rpb,
           gm_ws, gm_bs, gm_norm, da_lq1, da_lk1, da_lq2, da_lk2, da_subln,
           ffn2_w1, ffn2_w3, ffn2_w2, final_norm):
    b, n_latent, d = x.shape
    assert d == D_MODEL and ctx.shape[1] == CTX_LEN and b < 8
    assert n_latent % ATT_Q_TILE == 0

    cvec = jnp.zeros((8, d), F32).at[:b].set(c).at[b].set(c_ctx)
    mod = _ada(cvec, w_ada, b_ada).reshape(DEPTH, 8, N_ADA, d)
    cos_t, sin_t = _rope_tables(n_latent)
    gmat = jnp.asarray(np.kron(np.eye(GM_GROUPS), np.ones((GM_WIDTH, GM_WIDTH))), BF16)

    xs = x
    for l in range(DEPTH):
        last = l == DEPTH - 1
        lambda_init = 0.8 - 0.6 * math.exp(-0.3 * l)
        modt = jnp.stack([mod[l, :b], jnp.broadcast_to(mod[l, b], (b, N_ADA, d))], axis=1)
        bias = _na_bias_tables(na_rpb[l], n_latent)
        ws_stack = gm_ws[l].reshape(GM_GROUPS * GM_CHUNK, GM_CHUNK).astype(BF16)
        bs_full = jnp.repeat(gm_bs[l].T, GM_WIDTH, axis=1)
        gn = gm_norm[l].reshape(1, D_B)
        lqk = jnp.stack([da_lq1[l], da_lk1[l], da_lq2[l], da_lk2[l]])

        xs = _ffn(xs, modt, ffn1_w1[l].astype(BF16), ffn1_w3[l].astype(BF16),
                  ffn1_w2[l].astype(BF16), n_latent=n_latent, mi=0, ctx=ctx if l == 0 else None)
        za, kat, ob, kd, qt, vt = _proj_in(xs, modt, w_in[l].astype(BF16), cos_t, sin_t, ws_stack,
                                           bs_full, gn, gmat, n_latent=n_latent)
        oa = _na_attention(za, kat, bias, n_latent=n_latent)
        oc = _diff_attention(kd, qt, vt, lqk, da_subln[l].reshape(DA_V_DIM, 1),
                             n_latent=n_latent, lambda_init=lambda_init)
        xs = _ffn(xs, modt, ffn2_w1[l].astype(BF16), ffn2_w3[l].astype(BF16),
                  ffn2_w2[l].astype(BF16), n_latent=n_latent, mi=6,
                  mix=(oa, ob, oc, w_out[l].astype(BF16)),
                  final_gain=final_norm if last else None)
    return xs
```

```python
import functools
import math

import numpy as np
import jax
import jax.numpy as jnp
from jax import lax
from jax.experimental import pallas as pl
from jax.experimental.pallas import tpu as pltpu

F32 = jnp.float32
BF16 = jnp.bfloat16

D_MODEL = 1024
DEPTH = 2
GRID_W = 64
CTX_LEN = 256
N_ADA = 9
EPS = 1e-6
NEG_INF = -1e30
ROPE_BASE = 10000.0
D_FF = 2816
HEAD_DIM = 64
NA_HEADS = 4
NA_WIN_ROWS = 8
NA_WIN_COLS = 16
D_A = NA_HEADS * HEAD_DIM
GM_GROUPS = 4
GM_WIDTH = 64
GM_CHUNK = 128
D_B = GM_GROUPS * GM_WIDTH
DA_HEADS = 4
DA_QK_DIM = 64
DA_V_DIM = 128
D_C = DA_HEADS * DA_V_DIM
D_QK_C = DA_HEADS * 2 * DA_QK_DIM
D_IN = 3 * D_A + 2 * D_B + 2 * D_QK_C + D_C
QK_SCALE = HEAD_DIM ** -0.5
LOG2_E = math.log2(math.e)

LANES = 128
VMEM_BYTES_V7X = 64 * 1024 * 1024
VMEM_LIMIT = VMEM_BYTES_V7X - 8 * 1024 * 1024

TOKEN_TILE = 256
ATT_Q_TILE = 256
NA_Q_ROWS = ATT_Q_TILE // GRID_W
NA_K_ROWS = NA_Q_ROWS + NA_WIN_ROWS
NA_K_TOK = NA_K_ROWS * GRID_W
DA_KV_CHUNK = 1408
DA_KV_RAMP_UP = (256, 768)
DA_KV_RAMP_DOWN = (768, 512, 256, 256)
DA_KV_SMALL_CHUNK = 384
FF_CHUNKS =((0, 768), (768, 1536), (1536, 2304), (2304, 2816))


def _params(sem):
    return pltpu.CompilerParams(dimension_semantics=sem, vmem_limit_bytes=VMEM_LIMIT)


def _resident(shape):
    nd = len(shape)
    return pl.BlockSpec(shape, lambda *_: (0,) * nd, pipeline_mode=pl.Buffered(1))


def _rms(x):
    return x * lax.rsqrt(jnp.mean(x * x, axis=-1, keepdims=True) + EPS)


def _mod_row(mod_ref, idx):
    return mod_ref[0, 0, idx:idx + 1, :]


def _mod_spec(tm, n_latent, d):
    assert n_latent % tm == 0
    return pl.BlockSpec((1, 1, N_ADA, d), lambda i, j: (i, (j * tm >= n_latent).astype(jnp.int32), 0, 0))


def _ada_kernel(c_ref, w_ref, b_ref, o_ref):
    c = c_ref[...]
    cs = c / (1.0 + jnp.exp(-c))
    o_ref[0] = jnp.dot(cs, w_ref[0], preferred_element_type=F32,
                       precision=lax.Precision.HIGHEST) + b_ref[0]


def _ada(cvec, w_ada, b_ada):
    depth, d, n = w_ada.shape
    tn = 1024
    return pl.pallas_call(
        _ada_kernel,
        grid=(depth, n // tn),
        in_specs=[
            pl.BlockSpec((8, d), lambda l, j: (0, 0)),
            pl.BlockSpec((1, d, tn), lambda l, j: (l, 0, j)),
            pl.BlockSpec((1, 1, tn), lambda l, j: (l, 0, j)),
        ],
        out_specs=pl.BlockSpec((1, 8, tn), lambda l, j: (l, 0, j)),
        out_shape=jax.ShapeDtypeStruct((depth, 8, n), F32),
        name="ada",
        compiler_params=_params(("arbitrary", "arbitrary")),
    )(cvec, w_ada, b_ada.reshape(depth, 1, n))


def _ffn_kernel(*refs, tm, n_latent, mi, split_ctx, mix, final):
    refs = list(refs)
    x_ref, mod_ref = refs[:2]
    del refs[:2]
    if split_ctx:
        ctx_ref = refs.pop(0)
    if mix:
        oa_ref, ob_ref, oc_ref, wout_ref = refs[:4]
        del refs[:4]
    w1_ref, w3_ref, w2_ref = refs[:3]
    del refs[:3]
    if final:
        gain_ref = refs.pop(0)
    (o_ref,) = refs
    x = x_ref[0]
    if split_ctx:
        x = jnp.where(pl.program_id(1) * tm >= n_latent, ctx_ref[0], x)
    if mix:
        cat = jnp.concatenate([oa_ref[0], ob_ref[0], oc_ref[0]], axis=-1)
        x = x + _mod_row(mod_ref, mi - 1) * jnp.dot(cat, wout_ref[...],
                                                    preferred_element_type=F32)
    shift = _mod_row(mod_ref, mi)
    scale1 = 1.0 + _mod_row(mod_ref, mi + 1)
    half_gate = 0.5 * _mod_row(mod_ref, mi + 2)
    h = (_rms(x) * scale1 + shift).astype(BF16)
    acc = jnp.zeros((tm, D_MODEL), F32)
    for c0, c1 in FF_CHUNKS:
        a = jnp.dot(h, w1_ref[:, c0:c1], preferred_element_type=F32)
        g = jnp.dot(h, w3_ref[:, c0:c1], preferred_element_type=F32)
        y = (a / (1.0 + jnp.exp(-a)) * g).astype(BF16)
        acc = acc + jnp.dot(y, w2_ref[c0:c1, :], preferred_element_type=F32)
    out = x + half_gate * acc
    if final:
        out = _rms(out) * gain_ref[...]
    o_ref[0] = out


def _ffn(x, modt, w1, w3, w2, *, n_latent, mi, ctx=None, mix=None, final_gain=None):
    b, _, d = x.shape
    t = n_latent + CTX_LEN
    tm = TOKEN_TILE
    final = final_gain is not None
    t_out = n_latent if final else t
    tile = lambda width: pl.BlockSpec((1, tm, width), lambda i, j: (i, j, 0))
    in_specs = [tile(d), _mod_spec(tm, n_latent, d)]
    args = [x, modt]
    if ctx is not None:
        assert tm == CTX_LEN and not final
        last_latent = n_latent // tm - 1
        in_specs[0] = pl.BlockSpec((1, tm, d), lambda i, j: (i, jnp.minimum(j, last_latent), 0))
        in_specs.append(pl.BlockSpec((1, CTX_LEN, d), lambda i, j: (i, 0, 0)))
        args.append(ctx)
    if mix is not None:
        in_specs += [tile(D_A), tile(D_B), tile(D_C), _resident((d, d))]
        args += list(mix)
    in_specs += [_resident((d, D_FF)), _resident((d, D_FF)), _resident((D_FF, d))]
    args += [w1, w3, w2]
    if final:
        in_specs.append(_resident((1, d)))
        args.append(final_gain.reshape(1, d))
    return pl.pallas_call(
        functools.partial(_ffn_kernel, tm=tm, n_latent=n_latent, mi=mi, split_ctx=ctx is not None,
                          mix=mix is not None, final=final),
        grid=(b, t_out // tm),
        in_specs=in_specs,
        out_specs=pl.BlockSpec((1, tm, d), lambda i, j: (i, j, 0)),
        out_shape=jax.ShapeDtypeStruct((b, t_out, d), F32),
        name="ffn_final" if final else ("ffn_mix" if mix is not None else "ffn"),
        compiler_params=_params(("arbitrary", "arbitrary")),
    )(*args)


def _swap_rope_halves(x, first_half):
    return jnp.where(first_half, pltpu.roll(x, 96, 1), pltpu.roll(x, 32, 1))


def _proj_in_kernel(x_ref, mod_ref, w_ref, cos_ref, sin_ref, ws_ref, bs_ref, gn_ref, gmat_ref,
                    za_ref, kat_ref, ob_ref, kd_ref, qt_ref, vt_ref, *, tm):
    x = x_ref[0]
    h = (_rms(x) * (1.0 + _mod_row(mod_ref, 4)) + _mod_row(mod_ref, 3)).astype(BF16)
    z = jnp.dot(h, w_ref[...], preferred_element_type=F32)

    za_ref[0, :, 0:D_A] = (z[:, 0:D_A] * (QK_SCALE * LOG2_E)).astype(BF16)
    kat_ref[0] = z[:, D_A:2 * D_A].T.astype(BF16)
    za_ref[0, :, D_A:2 * D_A] = z[:, 2 * D_A:3 * D_A].astype(BF16)

    o = 3 * D_A
    u = z[:, o:o + D_B]
    v = z[:, o + D_B:o + 2 * D_B]
    u = 0.5 * u * (1.0 + lax.erf(u * (2.0 ** -0.5)))
    v = 0.5 * v * (1.0 + lax.erf(v * (2.0 ** -0.5)))
    v2 = v * v
    v2_hi = v2.astype(BF16)
    v2_lo = (v2 - v2_hi.astype(F32)).astype(BF16)
    ms = (jnp.dot(v2_hi, gmat_ref[...], preferred_element_type=F32)
          + jnp.dot(v2_lo, gmat_ref[...], preferred_element_type=F32)) * (1.0 / GM_WIDTH)
    vn = (v * lax.rsqrt(ms + EPS) * gn_ref[...]).astype(BF16)
    lane_group = lax.broadcasted_iota(jnp.int32, (1, D_B), 1) // GM_WIDTH
    for c in range(tm // GM_CHUNK):
        rows = slice(c * GM_CHUNK, (c + 1) * GM_CHUNK)
        r = jnp.dot(ws_ref[...], vn[rows, :], preferred_element_type=F32)
        s = bs_ref[...]
        for g in range(GM_GROUPS):
            s = s + jnp.where(lane_group == g, r[g * GM_CHUNK:(g + 1) * GM_CHUNK, :], 0.0)
        ob_ref[0, rows, :] = (u[rows, :] * s).astype(BF16)

    o = 3 * D_A + 2 * D_B
    cos = cos_ref[...]
    sin = sin_ref[...]
    first_half = (lax.broadcasted_iota(jnp.int32, (1, LANES), 1) % DA_QK_DIM) < (DA_QK_DIM // 2)
    n_q_blocks = D_QK_C // LANES
    for j in range(2 * n_q_blocks):
        zz = z[:, o + j * LANES:o + (j + 1) * LANES]
        r = zz * cos + _swap_rope_halves(zz, first_half) * sin
        if j < n_q_blocks:
            qt_ref[0, j * LANES:(j + 1) * LANES, :] = (r * (QK_SCALE * LOG2_E)).T.astype(BF16)
        else:
            jk = j - n_q_blocks
            kd_ref[0, :, jk * LANES:(jk + 1) * LANES] = r.astype(BF16)
    o = o + 2 * D_QK_C
    for j in range(D_C // LANES):
        vt_ref[0, j * LANES:(j + 1) * LANES, :] = z[:, o + j * LANES:o + (j + 1) * LANES].T.astype(BF16)


def _proj_in(x, modt, w_in, cos_t, sin_t, ws_stack, bs_full, gn, gmat, *, n_latent):
    b, t, d = x.shape
    tm = TOKEN_TILE
    return pl.pallas_call(
        functools.partial(_proj_in_kernel, tm=tm),
        grid=(b, t // tm),
        in_specs=[
            pl.BlockSpec((1, tm, d), lambda i, j: (i, j, 0)),
            _mod_spec(tm, n_latent, d),
            _resident((d, D_IN)),
            pl.BlockSpec((tm, LANES), lambda i, j: (j, 0)),
            pl.BlockSpec((tm, LANES), lambda i, j: (j, 0)),
            _resident((GM_GROUPS * GM_CHUNK, GM_CHUNK)),
            _resident((GM_CHUNK, D_B)),
            _resident((1, D_B)),
            _resident((D_B, D_B)),
        ],
        out_specs=[
            pl.BlockSpec((1, tm, 2 * D_A), lambda i, j: (i, j, 0)),
            pl.BlockSpec((1, D_A, tm), lambda i, j: (i, 0, j)),
            pl.BlockSpec((1, tm, D_B), lambda i, j: (i, j, 0)),
            pl.BlockSpec((1, tm, D_QK_C), lambda i, j: (i, j, 0)),
            pl.BlockSpec((1, D_QK_C, tm), lambda i, j: (i, 0, j)),
            pl.BlockSpec((1, D_C, tm), lambda i, j: (i, 0, j)),
        ],
        out_shape=[
            jax.ShapeDtypeStruct((b, t, 2 * D_A), BF16),
            jax.ShapeDtypeStruct((b, D_A, t), BF16),
            jax.ShapeDtypeStruct((b, t, D_B), BF16),
            jax.ShapeDtypeStruct((b, t, D_QK_C), BF16),
            jax.ShapeDtypeStruct((b, D_QK_C, t), BF16),
            jax.ShapeDtypeStruct((b, D_C, t), BF16),
        ],
        name="proj_in",
        compiler_params=_params(("arbitrary", "arbitrary")),
    )(x, modt, w_in, cos_t, sin_t, ws_stack, bs_full, gn, gmat)


def _na_kernel(q_ref, qn_ref, kt_ref, v_ref, bias_ref, biasn_ref, o_ref, sa_ref, sb_ref, *, n_latent):
    g = pl.program_id(2)
    n_groups = n_latent // ATT_Q_TILE
    rows = n_latent // GRID_W
    lane = lax.broadcasted_iota(jnp.int32, (1, LANES), 1)

    def window_start(grp):
        base_row = jnp.clip(grp * NA_Q_ROWS - NA_WIN_ROWS // 2, 0, rows - NA_K_ROWS)
        return pl.multiple_of(base_row * GRID_W, ATT_Q_TILE)

    def write_scores(s_ref, q, bias, grp):
        qs = jnp.concatenate([jnp.where(lane < HEAD_DIM, q, jnp.zeros_like(q)),
                              jnp.where(lane >= HEAD_DIM, q, jnp.zeros_like(q))], axis=0)
        sw = jnp.dot(qs, kt_ref[0, :, pl.ds(window_start(grp), NA_K_TOK)],
                     preferred_element_type=F32)
        s_ref[:, 0:NA_K_TOK] = sw + bias.reshape(2 * ATT_Q_TILE, NA_K_TOK)
        s_ref[:, NA_K_TOK:] = jnp.dot(qs, kt_ref[0, :, n_latent:], preferred_element_type=F32)

    def softmax_pv(s_ref):
        s = s_ref[...]
        p = jnp.exp2(s - s.max(axis=-1, keepdims=True))
        l = p.sum(axis=-1, keepdims=True)
        p = p.astype(BF16)
        o = (jnp.dot(p[:, 0:NA_K_TOK], v_ref[0, pl.ds(window_start(g), NA_K_TOK), :],
                     preferred_element_type=F32)
             + jnp.dot(p[:, NA_K_TOK:], v_ref[0, n_latent:, :], preferred_element_type=F32)) / l
        o_ref[0] = jnp.where(lane < HEAD_DIM, o[:ATT_Q_TILE], o[ATT_Q_TILE:]).astype(BF16)

    @pl.when(g == 0)
    def _():
        write_scores(sa_ref, q_ref[0], bias_ref[0], g)

    g_next = jnp.minimum(g + 1, n_groups)

    @pl.when(g % 2 == 0)
    def _():
        write_scores(sb_ref, qn_ref[0], biasn_ref[0], g_next)
        softmax_pv(sa_ref)

    @pl.when(g % 2 == 1)
    def _():
        write_scores(sa_ref, qn_ref[0], biasn_ref[0], g_next)
        softmax_pv(sb_ref)


def _na_attention(za, kat, bias, *, n_latent):
    b, t, _ = za.shape
    n_groups = n_latent // ATT_Q_TILE
    assert t == n_latent + CTX_LEN and CTX_LEN == ATT_Q_TILE
    assert n_groups >= 3 and n_latent // GRID_W >= NA_K_ROWS

    def kind(g):
        return jnp.where(g == 0, 0, jnp.where(g < n_groups - 1, 1, jnp.where(g == n_groups - 1, 2, 3)))

    nxt = lambda g: jnp.minimum(g + 1, n_groups)
    q_spec = lambda at: pl.BlockSpec((1, ATT_Q_TILE, LANES), lambda i, hp, g: (i, at(g), hp))
    bias_spec = lambda at: pl.BlockSpec((1, 2, ATT_Q_TILE, NA_K_TOK),
                                        lambda i, hp, g: (kind(at(g)), hp, 0, 0))
    same = lambda g: g
    score_buf = pltpu.VMEM((2 * ATT_Q_TILE, NA_K_TOK + CTX_LEN), F32)
    return pl.pallas_call(
        functools.partial(_na_kernel, n_latent=n_latent),
        grid=(b, NA_HEADS // 2, n_groups + 1),
        in_specs=[
            q_spec(same), q_spec(nxt),
            pl.BlockSpec((1, LANES, t), lambda i, hp, g: (i, hp, 0)),
            pl.BlockSpec((1, t, LANES), lambda i, hp, g: (i, 0, 2 + hp)),
            bias_spec(same), bias_spec(nxt),
        ],
        out_specs=pl.BlockSpec((1, ATT_Q_TILE, LANES), lambda i, hp, g: (i, g, hp)),
        out_shape=jax.ShapeDtypeStruct((b, t, D_A), BF16),
        scratch_shapes=[score_buf, score_buf],
        name="na_attention",
        compiler_params=_params(("arbitrary", "arbitrary", "arbitrary")),
    )(za, za, kat, za, bias, bias)


NA_BIAS_PAD = NA_WIN_ROWS // 2
NA_BIAS_W = 11 * LANES


def _na_bias_plan(n_latent):
    rows = n_latent // GRID_W
    n_groups = n_latent // ATT_Q_TILE
    plan = []
    for g in (0, 1, n_groups - 1):
        base = int(np.clip(g * NA_Q_ROWS - NA_WIN_ROWS // 2, 0, rows - NA_K_ROWS))
        kind = []
        for i in range(NA_Q_ROWS):
            r = g * NA_Q_ROWS + i
            start = int(np.clip(r - NA_WIN_ROWS // 2, 0, rows - NA_WIN_ROWS))
            e0 = base - r + NA_WIN_ROWS - 1 + NA_BIAS_PAD
            assert 0 <= e0 and (e0 - e0 % 2) * GRID_W + NA_K_TOK <= NA_BIAS_W
            kind.append((e0, start - base, start - base + NA_WIN_ROWS - 1))
        plan.append(tuple(kind))
    return tuple(plan)


def _na_bias_kernel(r_ref, o_ref, *, plan):
    shape = (GRID_W, NA_BIAS_W)
    qc = lax.broadcasted_iota(jnp.int32, shape, 0)
    col = lax.broadcasted_iota(jnp.int32, shape, 1)
    kc = col % GRID_W
    dc = kc - qc + (NA_WIN_COLS - 1)
    w0 = jnp.clip(qc - NA_WIN_COLS // 2, 0, GRID_W - NA_WIN_COLS)
    col_ok = (kc >= w0) & (kc < w0 + NA_WIN_COLS)
    tables = []
    for p in range(2):
        a = col // GRID_W + (p - NA_BIAS_PAD)
        acc = jnp.zeros(shape, F32)
        for b in range(2 * NA_WIN_COLS - 1):
            acc = acc + jnp.where(dc == b, r_ref[0, p, b:b + 1, :], 0.0)
        ok = col_ok & (a >= 0) & (a <= 2 * NA_WIN_ROWS - 2)
        tables.append(jnp.where(ok, acc * LOG2_E, NEG_INF))
    kj = lax.broadcasted_iota(jnp.int32, (GRID_W, NA_K_TOK), 1) // GRID_W
    for k, kind in enumerate(plan):
        for i, (e0, jlo, jhi) in enumerate(kind):
            p = e0 % 2
            off = (e0 - p) * GRID_W
            slab = tables[p][:, off:off + NA_K_TOK]
            o_ref[k, 0, i * GRID_W:(i + 1) * GRID_W, :] = jnp.where(
                (kj >= jlo) & (kj <= jhi), slab, NEG_INF)
    o_ref[len(plan), 0] = jnp.full((ATT_Q_TILE, NA_K_TOK), NEG_INF, F32)


def _na_bias_tables(rpb, n_latent):
    h = rpb.shape[0]
    n_blocks = NA_BIAS_W // GRID_W + 2
    rp = jnp.pad(rpb, ((0, 0), (NA_BIAS_PAD, n_blocks - NA_BIAS_PAD - rpb.shape[1]), (0, 0)))
    rep = jnp.transpose(jnp.repeat(rp, GRID_W, axis=1), (0, 2, 1))
    r = jnp.stack([rep[:, :, p * GRID_W:p * GRID_W + NA_BIAS_W] for p in range(2)], axis=1)
    n_dc = rpb.shape[2]
    return pl.pallas_call(
        functools.partial(_na_bias_kernel, plan=_na_bias_plan(n_latent)),
        grid=(h,),
        in_specs=[pl.BlockSpec((1, 2, n_dc, NA_BIAS_W), lambda i: (i, 0, 0, 0))],
        out_specs=pl.BlockSpec((4, 1, ATT_Q_TILE, NA_K_TOK), lambda i: (0, i, 0, 0)),
        out_shape=jax.ShapeDtypeStruct((4, h, ATT_Q_TILE, NA_K_TOK), F32),
        name="na_bias",
        compiler_params=_params(("arbitrary",)),
    )(r)


def _kv_chunk_bounds(t):
    head, tail = DA_KV_RAMP_UP, DA_KV_RAMP_DOWN
    middle = t - sum(head) - sum(tail)
    if middle < 0 or middle % DA_KV_CHUNK:
        assert t % DA_KV_SMALL_CHUNK == 0
        sizes = (DA_KV_SMALL_CHUNK,) * (t // DA_KV_SMALL_CHUNK)
    else:
        sizes = head + (DA_KV_CHUNK,) * (middle // DA_KV_CHUNK) + tail
    return tuple(int(v) for v in np.cumsum((0,) + sizes))


def _diff_kernel(qt_ref, qtn_ref, k_ref, vt_ref, lqk_ref, subln_ref, o_ref, s_ref, smax_ref, *,
                 n_latent, lambda_init):
    i = pl.program_id(2)
    tq = ATT_Q_TILE
    n_q_tiles = n_latent // tq
    feat = lax.broadcasted_iota(jnp.int32, (LANES, 1), 0)

    def stack_components(qt):
        return jnp.concatenate([jnp.where(feat < DA_QK_DIM, qt, jnp.zeros_like(qt)),
                                jnp.where(feat >= DA_QK_DIM, qt, jnp.zeros_like(qt))], axis=1)

    qst = stack_components(qt_ref[0])
    lqk = lqk_ref[...]
    lam = (jnp.exp(jnp.sum(lqk[0:1] * lqk[1:2], axis=-1, keepdims=True))
           - jnp.exp(jnp.sum(lqk[2:3] * lqk[3:4], axis=-1, keepdims=True)) + lambda_init)

    def col_reduce(x, op):
        r, c = x.shape
        groups = 8 if r % 64 == 0 else 1
        x = x.reshape(groups, r // (8 * groups), 8, c)
        return op(op(op(x, axis=1), axis=0), axis=0, keepdims=True)

    def scores(k, q=qst):
        s = jnp.dot(k, q, preferred_element_type=F32)
        return s, col_reduce(s, jnp.max)

    def step(s, s_max, vt, state):
        m, l, acc = state
        m_new = jnp.maximum(m, s_max)
        alpha = jnp.exp2(m - m_new)
        p = jnp.exp2(s - m_new)
        l = alpha * l + col_reduce(p, jnp.sum)
        acc = alpha * acc + jnp.dot(vt, p.astype(BF16), preferred_element_type=F32)
        return (m_new, l, acc)

    def init():
        return (jnp.full((1, 2 * tq), NEG_INF, F32), jnp.zeros((1, 2 * tq), F32),
                jnp.zeros((DA_V_DIM, 2 * tq), F32))

    def finish(state):
        _, l, acc = state
        o = acc / l
        o = o[:, :tq] - lam * o[:, tq:]
        o = o * lax.rsqrt(jnp.mean(o * o, axis=0, keepdims=True) + EPS)
        o = o * (subln_ref[...] * (1.0 - lambda_init))
        o_ref[0] = o.T.astype(BF16)

    @pl.when(i < n_q_tiles)
    def _():
        bounds = _kv_chunk_bounds(n_latent + CTX_LEN)
        chunk = lambda c: slice(bounds[c], bounds[c + 1])
        size = lambda c: bounds[c + 1] - bounds[c]
        n_chunks = len(bounds) - 1
        assert n_chunks % 2 == 0

        @pl.when(i == 0)
        def _():
            s_ref[0, 0:size(0)], smax_ref[...] = scores(k_ref[0, chunk(0), :])

        s_max = smax_ref[...]
        state = init()
        for c in range(n_chunks):
            if c + 1 < n_chunks:
                s_ref[(c + 1) % 2, 0:size(c + 1)], next_max = scores(k_ref[0, chunk(c + 1), :])
            else:
                s_ref[0, 0:size(0)], smax_ref[...] = scores(k_ref[0, chunk(0), :],
                                                            stack_components(qtn_ref[0]))
            state = step(s_ref[c % 2, 0:size(c)], s_max, vt_ref[0, :, chunk(c)], state)
            s_max = next_max
        finish(state)

    @pl.when(i == n_q_tiles)
    def _():
        s, s_max = scores(k_ref[0, n_latent:, :])
        finish(step(s, s_max, vt_ref[0, :, n_latent:], init()))


def _diff_attention(kd, qt, vt, lqk, subln, *, n_latent, lambda_init):
    b, t, _ = kd.shape
    n_q_tiles = n_latent // ATT_Q_TILE
    return pl.pallas_call(
        functools.partial(_diff_kernel, n_latent=n_latent, lambda_init=lambda_init),
        grid=(b, DA_HEADS, n_q_tiles + 1),
        in_specs=[
            pl.BlockSpec((1, LANES, ATT_Q_TILE), lambda bi, h, i: (bi, h, i)),
            pl.BlockSpec((1, LANES, ATT_Q_TILE),
                         lambda bi, h, i: (bi, h, jnp.minimum(i + 1, n_q_tiles))),
            pl.BlockSpec((1, t, LANES), lambda bi, h, i: (bi, 0, h)),
            pl.BlockSpec((1, LANES, t), lambda bi, h, i: (bi, h, 0)),
            pl.BlockSpec((4, DA_QK_DIM), lambda bi, h, i: (0, 0)),
            pl.BlockSpec((DA_V_DIM, 1), lambda bi, h, i: (0, 0)),
        ],
        out_specs=pl.BlockSpec((1, ATT_Q_TILE, LANES), lambda bi, h, i: (bi, i, h)),
        out_shape=jax.ShapeDtypeStruct((b, t, D_C), BF16),
        scratch_shapes=[pltpu.VMEM((2, DA_KV_CHUNK, 2 * ATT_Q_TILE), F32),
                        pltpu.VMEM((1, 2 * ATT_Q_TILE), F32)],
        name="diff_attention",
        compiler_params=_params(("arbitrary", "arbitrary", "arbitrary")),
    )(qt, qt, kd, vt, lqk, subln)


def _rope_tables(n_latent):
    tok = jnp.arange(n_latent)
    row = (tok // GRID_W).astype(F32)
    col = (tok % GRID_W).astype(F32)
    n_freq = DA_QK_DIM // 4
    freqs = ROPE_BASE ** (-jnp.arange(n_freq, dtype=F32) / n_freq)
    ang = jnp.concatenate([row[:, None] * freqs, col[:, None] * freqs], axis=-1)
    cos, sin = jnp.cos(ang), jnp.sin(ang)
    cos_t = jnp.concatenate([cos, cos, cos, cos], axis=-1)
    sin_t = jnp.concatenate([-sin, sin, -sin, sin], axis=-1)
    cos_t = jnp.concatenate([cos_t, jnp.ones((CTX_LEN, LANES), F32)], axis=0)
    sin_t = jnp.concatenate([sin_t, jnp.zeros((CTX_LEN, LANES), F32)], axis=0)
    return cos_t, sin_t


def kernel(x, c, ctx, c_ctx, w_ada, b_ada, ffn1_w1, ffn1_w3, ffn1_w2, w_in, w_out, na_rpb,
           gm_ws, gm_bs, gm_norm, da_lq1, da_lk1, da_lq2, da_lk2, da_subln,
           ffn2_w1, ffn2_w3, ffn2_w2, final_norm):
    b, n_latent, d = x.shape
    assert d == D_MODEL and ctx.shape[1] == CTX_LEN and b < 8
    assert n_latent % ATT_Q_TILE == 0

    cvec = jnp.zeros((8, d), F32).at[:b].set(c).at[b].set(c_ctx)
    mod = _ada(cvec, w_ada, b_ada).reshape(DEPTH, 8, N_ADA, d)
    cos_t, sin_t = _rope_tables(n_latent)
    gmat = jnp.asarray(np.kron(np.eye(GM_GROUPS), np.ones((GM_WIDTH, GM_WIDTH))), BF16)

    xs = x
    for l in range(DEPTH):
        last = l == DEPTH - 1
        lambda_init = 0.8 - 0.6 * math.exp(-0.3 * l)
        modt = jnp.stack([mod[l, :b], jnp.broadcast_to(mod[l, b], (b, N_ADA, d))], axis=1)
        bias = _na_bias_tables(na_rpb[l], n_latent)
        ws_stack = gm_ws[l].reshape(GM_GROUPS * GM_CHUNK, GM_CHUNK).astype(BF16)
        bs_full = jnp.repeat(gm_bs[l].T, GM_WIDTH, axis=1)
        gn = gm_norm[l].reshape(1, D_B)
        lqk = jnp.stack([da_lq1[l], da_lk1[l], da_lq2[l], da_lk2[l]])

        xs = _ffn(xs, modt, ffn1_w1[l].astype(BF16), ffn1_w3[l].astype(BF16),
                  ffn1_w2[l].astype(BF16), n_latent=n_latent, mi=0, ctx=ctx if l == 0 else None)
        za, kat, ob, kd, qt, vt = _proj_in(xs, modt, w_in[l].astype(BF16), cos_t, sin_t, ws_stack,
                                           bs_full, gn, gmat, n_latent=n_latent)
        oa = _na_attention(za, kat, bias, n_latent=n_latent)
        oc = _diff_attention(kd, qt, vt, lqk, da_subln[l].reshape(DA_V_DIM, 1),
                             n_latent=n_latent, lambda_init=lambda_init)
        xs = _ffn(xs, modt, ffn2_w1[l].astype(BF16), ffn2_w3[l].astype(BF16),
                  ffn2_w2[l].astype(BF16), n_latent=n_latent, mi=6,
                  mix=(oa, ob, oc, w_out[l].astype(BF16)),
                  final_gain=final_norm if last else None)
    return xs
```

```python
import functools
import math

import numpy as np
import jax
import jax.numpy as jnp
from jax import lax
from jax.experimental import pallas as pl
from jax.experimental.pallas import tpu as pltpu

F32 = jnp.float32
BF16 = jnp.bfloat16

D_MODEL = 1024
DEPTH = 2
GRID_W = 64
CTX_LEN = 256
N_ADA = 9
EPS = 1e-6
NEG_INF = -1e30
ROPE_BASE = 10000.0
D_FF = 2816
HEAD_DIM = 64
NA_HEADS = 4
NA_WIN_ROWS = 8
NA_WIN_COLS = 16
D_A = NA_HEADS * HEAD_DIM
GM_GROUPS = 4
GM_WIDTH = 64
GM_CHUNK = 128
D_B = GM_GROUPS * GM_WIDTH
DA_HEADS = 4
DA_QK_DIM = 64
DA_V_DIM = 128
DA_VT_ROWS = DA_V_DIM + 16
D_C = DA_HEADS * DA_V_DIM
D_QK_C = DA_HEADS * 2 * DA_QK_DIM
D_IN = 3 * D_A + 2 * D_B + 2 * D_QK_C + D_C
QK_SCALE = HEAD_DIM ** -0.5
LOG2_E = math.log2(math.e)

LANES = 128
VMEM_BYTES_V7X = 64 * 1024 * 1024
VMEM_LIMIT = VMEM_BYTES_V7X - 8 * 1024 * 1024

TOKEN_TILE = 256
ATT_Q_TILE = 256
NA_Q_ROWS = ATT_Q_TILE // GRID_W
NA_K_ROWS = NA_Q_ROWS + NA_WIN_ROWS
NA_K_TOK = NA_K_ROWS * GRID_W
DA_KV_CHUNK = 1408
DA_KV_RAMP_UP = (256, 768)
DA_KV_RAMP_DOWN = (768, 512, 256, 256)
DA_KV_SMALL_CHUNK = 384
FF_CHUNKS =((0, 768), (768, 1536), (1536, 2304), (2304, 2816))


def _params(sem):
    return pltpu.CompilerParams(dimension_semantics=sem, vmem_limit_bytes=VMEM_LIMIT)


def _resident(shape):
    nd = len(shape)
    return pl.BlockSpec(shape, lambda *_: (0,) * nd, pipeline_mode=pl.Buffered(1))


def _layer_resident(shape, layer):
    nd = len(shape)
    return pl.BlockSpec((1,) + shape, lambda *_: (layer,) + (0,) * nd,
                        pipeline_mode=pl.Buffered(1))


def _rms(x):
    return x * lax.rsqrt(jnp.mean(x * x, axis=-1, keepdims=True) + EPS)


def _mod_row(mod_ref, idx):
    return mod_ref[0, 0, idx:idx + 1, :]


def _mod_spec(tm, n_latent, d):
    assert n_latent % tm == 0
    return pl.BlockSpec((1, 1, N_ADA, d), lambda i, j: (i, (j * tm >= n_latent).astype(jnp.int32), 0, 0))


def _ada_kernel(c_ref, w_ref, b_ref, o_ref):
    c = c_ref[...]
    cs = c / (1.0 + jnp.exp(-c))
    o_ref[0] = jnp.dot(cs, w_ref[0], preferred_element_type=F32,
                       precision=lax.Precision.HIGHEST) + b_ref[0]


def _ada(cvec, w_ada, b_ada):
    depth, d, n = w_ada.shape
    tn = 1024
    return pl.pallas_call(
        _ada_kernel,
        grid=(depth, n // tn),
        in_specs=[
            pl.BlockSpec((8, d), lambda l, j: (0, 0)),
            pl.BlockSpec((1, d, tn), lambda l, j: (l, 0, j)),
            pl.BlockSpec((1, 1, tn), lambda l, j: (l, 0, j)),
        ],
        out_specs=pl.BlockSpec((1, 8, tn), lambda l, j: (l, 0, j)),
        out_shape=jax.ShapeDtypeStruct((depth, 8, n), F32),
        name="ada",
        compiler_params=_params(("arbitrary", "arbitrary")),
    )(cvec, w_ada, b_ada.reshape(depth, 1, n))


def _ffn_kernel(*refs, tm, n_latent, mi, split_ctx, mix, final):
    refs = list(refs)
    x_ref, mod_ref = refs[:2]
    del refs[:2]
    if split_ctx:
        ctx_ref = refs.pop(0)
    if mix:
        oa_ref, ob_ref, oc_ref, wout_ref = refs[:4]
        del refs[:4]
    w1_ref, w3_ref, w2_ref = refs[:3]
    del refs[:3]
    if final:
        gain_ref = refs.pop(0)
    (o_ref,) = refs
    x = x_ref[0]
    if split_ctx:
        x = jnp.where(pl.program_id(1) * tm >= n_latent, ctx_ref[0], x)
    if mix:
        cat = jnp.concatenate([oa_ref[0], ob_ref[0], oc_ref[0]], axis=-1)
        x = x + _mod_row(mod_ref, mi - 1) * jnp.dot(cat, wout_ref[0],
                                                    preferred_element_type=F32)
    shift = _mod_row(mod_ref, mi)
    scale1 = 1.0 + _mod_row(mod_ref, mi + 1)
    half_gate = 0.5 * _mod_row(mod_ref, mi + 2)
    h = (_rms(x) * scale1 + shift).astype(BF16)
    acc = jnp.zeros((tm, D_MODEL), F32)
    for c0, c1 in FF_CHUNKS:
        a = jnp.dot(h, w1_ref[0, :, c0:c1], preferred_element_type=F32)
        g = jnp.dot(h, w3_ref[0, :, c0:c1], preferred_element_type=F32)
        y = (a / (1.0 + jnp.exp(-a)) * g).astype(BF16)
        acc = acc + jnp.dot(y, w2_ref[0, c0:c1, :], preferred_element_type=F32)
    out = x + half_gate * acc
    if final:
        out = _rms(out) * gain_ref[...]
    o_ref[0] = out


def _ffn(x, modt, w1, w3, w2, *, layer, n_latent, mi, ctx=None, mix=None, final_gain=None):
    b, _, d = x.shape
    t = n_latent + CTX_LEN
    tm = TOKEN_TILE
    final = final_gain is not None
    t_out = n_latent if final else t
    tile = lambda width: pl.BlockSpec((1, tm, width), lambda i, j: (i, j, 0))
    in_specs = [tile(d), _mod_spec(tm, n_latent, d)]
    args = [x, modt]
    if ctx is not None:
        assert tm == CTX_LEN and not final
        last_latent = n_latent // tm - 1
        in_specs[0] = pl.BlockSpec((1, tm, d), lambda i, j: (i, jnp.minimum(j, last_latent), 0))
        in_specs.append(pl.BlockSpec((1, CTX_LEN, d), lambda i, j: (i, 0, 0)))
        args.append(ctx)
    if mix is not None:
        in_specs += [tile(D_A), tile(D_B), tile(D_C), _layer_resident((d, d), layer)]
        args += list(mix)
    in_specs += [_layer_resident((d, D_FF), layer), _layer_resident((d, D_FF), layer),
                 _layer_resident((D_FF, d), layer)]
    args += [w1, w3, w2]
    if final:
        in_specs.append(_resident((1, d)))
        args.append(final_gain.reshape(1, d))
    return pl.pallas_call(
        functools.partial(_ffn_kernel, tm=tm, n_latent=n_latent, mi=mi, split_ctx=ctx is not None,
                          mix=mix is not None, final=final),
        grid=(b, t_out // tm),
        in_specs=in_specs,
        out_specs=pl.BlockSpec((1, tm, d), lambda i, j: (i, j, 0)),
        out_shape=jax.ShapeDtypeStruct((b, t_out, d), F32),
        name="ffn_final" if final else ("ffn_mix" if mix is not None else "ffn"),
        compiler_params=_params(("arbitrary", "arbitrary")),
    )(*args)


def _swap_rope_halves(x, first_half):
    return jnp.where(first_half, pltpu.roll(x, 96, 1), pltpu.roll(x, 32, 1))


def _proj_in_kernel(x_ref, mod_ref, w_ref, cos_ref, sin_ref, ws_ref, bs_ref, gn_ref, gmat_ref,
                    za_ref, kat_ref, ob_ref, kd_ref, qt_ref, vt_ref, *, tm):
    x = x_ref[0]
    h = (_rms(x) * (1.0 + _mod_row(mod_ref, 4)) + _mod_row(mod_ref, 3)).astype(BF16)

    z_all = jnp.dot(h, w_ref[0], preferred_element_type=F32)
    project = lambda c0, width: z_all[:, c0:c0 + width]

    z = project(3 * D_A, 2 * D_B)
    u = z[:, 0:D_B]
    v = z[:, D_B:]
    u = 0.5 * u * (1.0 + lax.erf(u * (2.0 ** -0.5)))
    v = 0.5 * v * (1.0 + lax.erf(v * (2.0 ** -0.5)))
    v2 = v * v
    v2_hi = v2.astype(BF16)
    v2_lo = (v2 - v2_hi.astype(F32)).astype(BF16)
    ms = (jnp.dot(v2_hi, gmat_ref[...], preferred_element_type=F32)
          + jnp.dot(v2_lo, gmat_ref[...], preferred_element_type=F32)) * (1.0 / GM_WIDTH)
    vn = (v * lax.rsqrt(ms + EPS) * gn_ref[...]).astype(BF16)
    lane_group = lax.broadcasted_iota(jnp.int32, (1, D_B), 1) // GM_WIDTH
    for c in range(tm // GM_CHUNK):
        rows = slice(c * GM_CHUNK, (c + 1) * GM_CHUNK)
        r = jnp.dot(ws_ref[...], vn[rows, :], preferred_element_type=F32)
        s = bs_ref[...]
        for g in range(GM_GROUPS):
            s = s + jnp.where(lane_group == g, r[g * GM_CHUNK:(g + 1) * GM_CHUNK, :], 0.0)
        ob_ref[0, rows, :] = (u[rows, :] * s).astype(BF16)

    z = project(0, 3 * D_A)
    za_ref[0, :, 0:D_A] = (z[:, 0:D_A] * (QK_SCALE * LOG2_E)).astype(BF16)
    kat_ref[0] = z[:, D_A:2 * D_A].T.astype(BF16)
    za_ref[0, :, D_A:2 * D_A] = z[:, 2 * D_A:3 * D_A].astype(BF16)

    o = 3 * D_A + 2 * D_B
    cos = cos_ref[...]
    sin = sin_ref[...]
    first_half = (lax.broadcasted_iota(jnp.int32, (1, LANES), 1) % DA_QK_DIM) < (DA_QK_DIM // 2)
    n_q_blocks = D_QK_C // LANES
    for j in range(2 * n_q_blocks):
        if j % n_q_blocks == 0:
            z = project(o + j * LANES, D_QK_C)
        zz = z[:, (j % n_q_blocks) * LANES:(j % n_q_blocks + 1) * LANES]
        r = zz * cos + _swap_rope_halves(zz, first_half) * sin
        if j < n_q_blocks:
            qt_ref[0, j * LANES:(j + 1) * LANES, :] = (r * (QK_SCALE * LOG2_E)).T.astype(BF16)
        else:
            jk = j - n_q_blocks
            kd_ref[0, :, jk * LANES:(jk + 1) * LANES] = r.astype(BF16)
    z = project(o + 2 * D_QK_C, D_C)
    for j in range(D_C // LANES):
        r0 = j * DA_VT_ROWS
        vt_ref[0, r0:r0 + DA_V_DIM, :] = z[:, j * LANES:(j + 1) * LANES].T.astype(BF16)
        vt_ref[0, r0 + DA_V_DIM:r0 + DA_VT_ROWS, :] = jnp.ones((DA_VT_ROWS - DA_V_DIM, tm), BF16)


def _proj_in(x, modt, w_in, cos_t, sin_t, ws_stack, bs_full, gn, gmat, *, layer, n_latent):
    b, t, d = x.shape
    tm = TOKEN_TILE
    return pl.pallas_call(
        functools.partial(_proj_in_kernel, tm=tm),
        grid=(b, t // tm),
        in_specs=[
            pl.BlockSpec((1, tm, d), lambda i, j: (i, j, 0)),
            _mod_spec(tm, n_latent, d),
            _layer_resident((d, D_IN), layer),
            pl.BlockSpec((tm, LANES), lambda i, j: (j, 0)),
            pl.BlockSpec((tm, LANES), lambda i, j: (j, 0)),
            _resident((GM_GROUPS * GM_CHUNK, GM_CHUNK)),
            _resident((GM_CHUNK, D_B)),
            _resident((1, D_B)),
            _resident((D_B, D_B)),
        ],
        out_specs=[
            pl.BlockSpec((1, tm, 2 * D_A), lambda i, j: (i, j, 0)),
            pl.BlockSpec((1, D_A, tm), lambda i, j: (i, 0, j)),
            pl.BlockSpec((1, tm, D_B), lambda i, j: (i, j, 0)),
            pl.BlockSpec((1, tm, D_QK_C), lambda i, j: (i, j, 0)),
            pl.BlockSpec((1, D_QK_C, tm), lambda i, j: (i, 0, j)),
            pl.BlockSpec((1, DA_HEADS * DA_VT_ROWS, tm), lambda i, j: (i, 0, j)),
        ],
        out_shape=[
            jax.ShapeDtypeStruct((b, t, 2 * D_A), BF16),
            jax.ShapeDtypeStruct((b, D_A, t), BF16),
            jax.ShapeDtypeStruct((b, t, D_B), BF16),
            jax.ShapeDtypeStruct((b, t, D_QK_C), BF16),
            jax.ShapeDtypeStruct((b, D_QK_C, t), BF16),
            jax.ShapeDtypeStruct((b, DA_HEADS * DA_VT_ROWS, t), BF16),
        ],
        name="proj_in",
        compiler_params=_params(("arbitrary", "arbitrary")),
    )(x, modt, w_in, cos_t, sin_t, ws_stack, bs_full, gn, gmat)


def _na_kernel(q_ref, qn_ref, kt_ref, v_ref, bias_ref, biasn_ref, o_ref, sa_ref, sb_ref, *, n_latent):
    g = pl.program_id(2)
    n_groups = n_latent // ATT_Q_TILE
    rows = n_latent // GRID_W
    lane = lax.broadcasted_iota(jnp.int32, (1, LANES), 1)

    def window_start(grp):
        base_row = jnp.clip(grp * NA_Q_ROWS - NA_WIN_ROWS // 2, 0, rows - NA_K_ROWS)
        return pl.multiple_of(base_row * GRID_W, ATT_Q_TILE)

    def write_scores(s_ref, q, bias, grp):
        qs = jnp.concatenate([jnp.where(lane < HEAD_DIM, q, jnp.zeros_like(q)),
                              jnp.where(lane >= HEAD_DIM, q, jnp.zeros_like(q))], axis=0)
        sw = jnp.dot(qs, kt_ref[0, :, pl.ds(window_start(grp), NA_K_TOK)],
                     preferred_element_type=F32)
        s_ref[:, 0:NA_K_TOK] = sw + bias.reshape(2 * ATT_Q_TILE, NA_K_TOK)
        s_ref[:, NA_K_TOK:] = jnp.dot(qs, kt_ref[0, :, n_latent:], preferred_element_type=F32)

    def softmax_pv(s_ref):
        s = s_ref[...]
        p = jnp.exp2(s - s.max(axis=-1, keepdims=True))
        l = p.sum(axis=-1, keepdims=True)
        p = p.astype(BF16)
        o = (jnp.dot(p[:, 0:NA_K_TOK], v_ref[0, pl.ds(window_start(g), NA_K_TOK), :],
                     preferred_element_type=F32)
             + jnp.dot(p[:, NA_K_TOK:], v_ref[0, n_latent:, :], preferred_element_type=F32)) / l
        o_ref[0] = jnp.where(lane < HEAD_DIM, o[:ATT_Q_TILE], o[ATT_Q_TILE:]).astype(BF16)

    @pl.when(g == 0)
    def _():
        write_scores(sa_ref, q_ref[0], bias_ref[0], g)

    g_next = jnp.minimum(g + 1, n_groups)

    @pl.when(g % 2 == 0)
    def _():
        write_scores(sb_ref, qn_ref[0], biasn_ref[0], g_next)
        softmax_pv(sa_ref)

    @pl.when(g % 2 == 1)
    def _():
        write_scores(sa_ref, qn_ref[0], biasn_ref[0], g_next)
        softmax_pv(sb_ref)


def _na_attention(za, kat, bias, *, n_latent):
    b, t, _ = za.shape
    n_groups = n_latent // ATT_Q_TILE
    assert t == n_latent + CTX_LEN and CTX_LEN == ATT_Q_TILE
    assert n_groups >= 3 and n_latent // GRID_W >= NA_K_ROWS

    def kind(g):
        return jnp.where(g == 0, 0, jnp.where(g < n_groups - 1, 1, jnp.where(g == n_groups - 1, 2, 3)))

    nxt = lambda g: jnp.minimum(g + 1, n_groups)
    q_spec = lambda at: pl.BlockSpec((1, ATT_Q_TILE, LANES), lambda i, hp, g: (i, at(g), hp))
    bias_spec = lambda at: pl.BlockSpec((1, 2, ATT_Q_TILE, NA_K_TOK),
                                        lambda i, hp, g: (kind(at(g)), hp, 0, 0))
    same = lambda g: g
    score_buf = pltpu.VMEM((2 * ATT_Q_TILE, NA_K_TOK + CTX_LEN), F32)
    return pl.pallas_call(
        functools.partial(_na_kernel, n_latent=n_latent),
        grid=(b, NA_HEADS // 2, n_groups + 1),
        in_specs=[
            q_spec(same), q_spec(nxt),
            pl.BlockSpec((1, LANES, t), lambda i, hp, g: (i, hp, 0)),
            pl.BlockSpec((1, t, LANES), lambda i, hp, g: (i, 0, 2 + hp)),
            bias_spec(same), bias_spec(nxt),
        ],
        out_specs=pl.BlockSpec((1, ATT_Q_TILE, LANES), lambda i, hp, g: (i, g, hp)),
        out_shape=jax.ShapeDtypeStruct((b, t, D_A), BF16),
        scratch_shapes=[score_buf, score_buf],
        name="na_attention",
        compiler_params=_params(("arbitrary", "arbitrary", "arbitrary")),
    )(za, za, kat, za, bias, bias)


NA_BIAS_PAD = NA_WIN_ROWS // 2
NA_BIAS_W = 11 * LANES


def _na_bias_plan(n_latent):
    rows = n_latent // GRID_W
    n_groups = n_latent // ATT_Q_TILE
    plan = []
    for g in (0, 1, n_groups - 1):
        base = int(np.clip(g * NA_Q_ROWS - NA_WIN_ROWS // 2, 0, rows - NA_K_ROWS))
        kind = []
        for i in range(NA_Q_ROWS):
            r = g * NA_Q_ROWS + i
            start = int(np.clip(r - NA_WIN_ROWS // 2, 0, rows - NA_WIN_ROWS))
            e0 = base - r + NA_WIN_ROWS - 1 + NA_BIAS_PAD
            assert 0 <= e0 and (e0 - e0 % 2) * GRID_W + NA_K_TOK <= NA_BIAS_W
            kind.append((e0, start - base, start - base + NA_WIN_ROWS - 1))
        plan.append(tuple(kind))
    return tuple(plan)


def _na_bias_kernel(r_ref, o_ref, *, plan):
    shape = (GRID_W, NA_BIAS_W)
    qc = lax.broadcasted_iota(jnp.int32, shape, 0)
    col = lax.broadcasted_iota(jnp.int32, shape, 1)
    kc = col % GRID_W
    dc = kc - qc + (NA_WIN_COLS - 1)
    w0 = jnp.clip(qc - NA_WIN_COLS // 2, 0, GRID_W - NA_WIN_COLS)
    col_ok = (kc >= w0) & (kc < w0 + NA_WIN_COLS)
    tables = []
    for p in range(2):
        a = col // GRID_W + (p - NA_BIAS_PAD)
        acc = jnp.zeros(shape, F32)
        for b in range(2 * NA_WIN_COLS - 1):
            acc = acc + jnp.where(dc == b, r_ref[0, p, b:b + 1, :], 0.0)
        ok = col_ok & (a >= 0) & (a <= 2 * NA_WIN_ROWS - 2)
        tables.append(jnp.where(ok, acc * LOG2_E, NEG_INF))
    kj = lax.broadcasted_iota(jnp.int32, (GRID_W, NA_K_TOK), 1) // GRID_W
    for k, kind in enumerate(plan):
        for i, (e0, jlo, jhi) in enumerate(kind):
            p = e0 % 2
            off = (e0 - p) * GRID_W
            slab = tables[p][:, off:off + NA_K_TOK]
            o_ref[k, 0, i * GRID_W:(i + 1) * GRID_W, :] = jnp.where(
                (kj >= jlo) & (kj <= jhi), slab, NEG_INF)
    o_ref[len(plan), 0] = jnp.full((ATT_Q_TILE, NA_K_TOK), NEG_INF, F32)


def _na_bias_tables(rpb, n_latent):
    h = rpb.shape[0]
    n_blocks = NA_BIAS_W // GRID_W + 2
    rp = jnp.pad(rpb, ((0, 0), (NA_BIAS_PAD, n_blocks - NA_BIAS_PAD - rpb.shape[1]), (0, 0)))
    rep = jnp.transpose(jnp.repeat(rp, GRID_W, axis=1), (0, 2, 1))
    r = jnp.stack([rep[:, :, p * GRID_W:p * GRID_W + NA_BIAS_W] for p in range(2)], axis=1)
    n_dc = rpb.shape[2]
    return pl.pallas_call(
        functools.partial(_na_bias_kernel, plan=_na_bias_plan(n_latent)),
        grid=(h,),
        in_specs=[pl.BlockSpec((1, 2, n_dc, NA_BIAS_W), lambda i: (i, 0, 0, 0))],
        out_specs=pl.BlockSpec((4, 1, ATT_Q_TILE, NA_K_TOK), lambda i: (0, i, 0, 0)),
        out_shape=jax.ShapeDtypeStruct((4, h, ATT_Q_TILE, NA_K_TOK), F32),
        name="na_bias",
        compiler_params=_params(("arbitrary",)),
    )(r)


def _kv_chunk_bounds(t):
    head, tail = DA_KV_RAMP_UP, DA_KV_RAMP_DOWN
    middle = t - sum(head) - sum(tail)
    if middle < 0 or middle % DA_KV_CHUNK:
        assert t % DA_KV_SMALL_CHUNK == 0
        sizes = (DA_KV_SMALL_CHUNK,) * (t // DA_KV_SMALL_CHUNK)
    else:
        sizes = head + (DA_KV_CHUNK,) * (middle // DA_KV_CHUNK) + tail
    return tuple(int(v) for v in np.cumsum((0,) + sizes))


def _diff_kernel(qt_ref, qtn_ref, k_ref, vt_ref, lqk_ref, subln_ref, o_ref, s_ref, smax_ref, *,
                 n_latent, lambda_init):
    i = pl.program_id(2)
    tq = ATT_Q_TILE
    n_q_tiles = n_latent // tq
    feat = lax.broadcasted_iota(jnp.int32, (LANES, 1), 0)

    def stack_components(qt):
        return jnp.concatenate([jnp.where(feat < DA_QK_DIM, qt, jnp.zeros_like(qt)),
                                jnp.where(feat >= DA_QK_DIM, qt, jnp.zeros_like(qt))], axis=1)

    qst = stack_components(qt_ref[0])
    lqk = lqk_ref[...]
    lam = (jnp.exp(jnp.sum(lqk[0:1] * lqk[1:2], axis=-1, keepdims=True))
           - jnp.exp(jnp.sum(lqk[2:3] * lqk[3:4], axis=-1, keepdims=True)) + lambda_init)

    def col_reduce(x, op):
        r, c = x.shape
        groups = 8 if r % 64 == 0 else 1
        x = x.reshape(groups, r // (8 * groups), 8, c)
        return op(op(op(x, axis=1), axis=0), axis=0, keepdims=True)

    def scores(k, q=qst):
        s = jnp.dot(k, q, preferred_element_type=F32)
        return s, col_reduce(s, jnp.max)

    def step(s, s_max, vt1, state):
        m, acc = state
        m_new = jnp.maximum(m, s_max)
        alpha = jnp.exp2(m - m_new)
        p = jnp.exp2(s - m_new)
        acc = alpha * acc + jnp.dot(vt1, p.astype(BF16), preferred_element_type=F32)
        return (m_new, acc)

    def init():
        return (jnp.full((1, 2 * tq), NEG_INF, F32), jnp.zeros((DA_VT_ROWS, 2 * tq), F32))

    def finish(state):
        _, acc = state
        o = acc[:DA_V_DIM] / acc[DA_V_DIM:DA_V_DIM + 1]
        o = o[:, :tq] - lam * o[:, tq:]
        o = o * lax.rsqrt(jnp.mean(o * o, axis=0, keepdims=True) + EPS)
        o = o * (subln_ref[...] * (1.0 - lambda_init))
        o_ref[0] = o.T.astype(BF16)

    @pl.when(i < n_q_tiles)
    def _():
        bounds = _kv_chunk_bounds(n_latent + CTX_LEN)
        chunk = lambda c: slice(bounds[c], bounds[c + 1])
        size = lambda c: bounds[c + 1] - bounds[c]
        n_chunks = len(bounds) - 1
        assert n_chunks % 2 == 0

        @pl.when(i == 0)
        def _():
            s_ref[0, 0:size(0)], smax_ref[...] = scores(k_ref[0, chunk(0), :])

        s_max = smax_ref[...]
        state = init()
        for c in range(n_chunks):
            if c + 1 < n_chunks:
                s_ref[(c + 1) % 2, 0:size(c + 1)], next_max = scores(k_ref[0, chunk(c + 1), :])
            else:
                s_ref[0, 0:size(0)], smax_ref[...] = scores(k_ref[0, chunk(0), :],
                                                            stack_components(qtn_ref[0]))
            state = step(s_ref[c % 2, 0:size(c)], s_max, vt_ref[0, :, chunk(c)], state)
            s_max = next_max
        finish(state)

    @pl.when(i == n_q_tiles)
    def _():
        s, s_max = scores(k_ref[0, n_latent:, :])
        finish(step(s, s_max, vt_ref[0, :, n_latent:], init()))


def _diff_attention(kd, qt, vt, lqk, subln, *, n_latent, lambda_init):
    b, t, _ = kd.shape
    n_q_tiles = n_latent // ATT_Q_TILE
    return pl.pallas_call(
        functools.partial(_diff_kernel, n_latent=n_latent, lambda_init=lambda_init),
        grid=(b, DA_HEADS, n_q_tiles + 1),
        in_specs=[
            pl.BlockSpec((1, LANES, ATT_Q_TILE), lambda bi, h, i: (bi, h, i)),
            pl.BlockSpec((1, LANES, ATT_Q_TILE),
                         lambda bi, h, i: (bi, h, jnp.minimum(i + 1, n_q_tiles))),
            pl.BlockSpec((1, t, LANES), lambda bi, h, i: (bi, 0, h)),
            pl.BlockSpec((1, DA_VT_ROWS, t), lambda bi, h, i: (bi, h, 0)),
            pl.BlockSpec((4, DA_QK_DIM), lambda bi, h, i: (0, 0)),
            pl.BlockSpec((DA_V_DIM, 1), lambda bi, h, i: (0, 0)),
        ],
        out_specs=pl.BlockSpec((1, ATT_Q_TILE, LANES), lambda bi, h, i: (bi, i, h)),
        out_shape=jax.ShapeDtypeStruct((b, t, D_C), BF16),
        scratch_shapes=[pltpu.VMEM((2, DA_KV_CHUNK, 2 * ATT_Q_TILE), F32),
                        pltpu.VMEM((1, 2 * ATT_Q_TILE), F32)],
        name="diff_attention",
        compiler_params=_params(("arbitrary", "arbitrary", "arbitrary")),
    )(qt, qt, kd, vt, lqk, subln)


def _rope_tables(n_latent):
    tok = jnp.arange(n_latent)
    row = (tok // GRID_W).astype(F32)
    col = (tok % GRID_W).astype(F32)
    n_freq = DA_QK_DIM // 4
    freqs = ROPE_BASE ** (-jnp.arange(n_freq, dtype=F32) / n_freq)
    ang = jnp.concatenate([row[:, None] * freqs, col[:, None] * freqs], axis=-1)
    cos, sin = jnp.cos(ang), jnp.sin(ang)
    cos_t = jnp.concatenate([cos, cos, cos, cos], axis=-1)
    sin_t = jnp.concatenate([-sin, sin, -sin, sin], axis=-1)
    cos_t = jnp.concatenate([cos_t, jnp.ones((CTX_LEN, LANES), F32)], axis=0)
    sin_t = jnp.concatenate([sin_t, jnp.zeros((CTX_LEN, LANES), F32)], axis=0)
    return cos_t, sin_t


def kernel(x, c, ctx, c_ctx, w_ada, b_ada, ffn1_w1, ffn1_w3, ffn1_w2, w_in, w_out, na_rpb,
           gm_ws, gm_bs, gm_norm, da_lq1, da_lk1, da_lq2, da_lk2, da_subln,
           ffn2_w1, ffn2_w3, ffn2_w2, final_norm):
    b, n_latent, d = x.shape
    assert d == D_MODEL and ctx.shape[1] == CTX_LEN and b < 8
    assert n_latent % ATT_Q_TILE == 0

    cvec = jnp.zeros((8, d), F32).at[:b].set(c).at[b].set(c_ctx)
    mod = _ada(cvec, w_ada, b_ada).reshape(DEPTH, 8, N_ADA, d)
    cos_t, sin_t = _rope_tables(n_latent)
    gmat = jnp.asarray(np.kron(np.eye(GM_GROUPS), np.ones((GM_WIDTH, GM_WIDTH))), BF16)

    bf16 = lambda w: w.astype(BF16)
    ffn1 = (bf16(ffn1_w1), bf16(ffn1_w3), bf16(ffn1_w2))
    ffn2 = (bf16(ffn2_w1), bf16(ffn2_w3), bf16(ffn2_w2))
    w_in, w_out = bf16(w_in), bf16(w_out)

    xs = x
    for l in range(DEPTH):
        last = l == DEPTH - 1
        lambda_init = 0.8 - 0.6 * math.exp(-0.3 * l)
        modt = jnp.stack([mod[l, :b], jnp.broadcast_to(mod[l, b], (b, N_ADA, d))], axis=1)
        bias = _na_bias_tables(na_rpb[l], n_latent)
        ws_stack = gm_ws[l].reshape(GM_GROUPS * GM_CHUNK, GM_CHUNK).astype(BF16)
        bs_full = jnp.repeat(gm_bs[l].T, GM_WIDTH, axis=1)
        gn = gm_norm[l].reshape(1, D_B)
        lqk = jnp.stack([da_lq1[l], da_lk1[l], da_lq2[l], da_lk2[l]])

        xs = _ffn(xs, modt, *ffn1, layer=l, n_latent=n_latent, mi=0, ctx=ctx if l == 0 else None)
        za, kat, ob, kd, qt, vt = _proj_in(xs, modt, w_in, cos_t, sin_t, ws_stack, bs_full, gn,
                                           gmat, layer=l, n_latent=n_latent)
        oa = _na_attention(za, kat, bias, n_latent=n_latent)
        oc = _diff_attention(kd, qt, vt, lqk, da_subln[l].reshape(DA_V_DIM, 1),
                             n_latent=n_latent, lambda_init=lambda_init)
        xs = _ffn(xs, modt, *ffn2, layer=l, n_latent=n_latent, mi=6, mix=(oa, ob, oc, w_out),
                  final_gain=final_norm if last else None)
    return xs
```

```python
import functools
import math

import numpy as np
import jax
import jax.numpy as jnp
from jax import lax
from jax.experimental import pallas as pl
from jax.experimental.pallas import tpu as pltpu

F32 = jnp.float32
BF16 = jnp.bfloat16

D_MODEL = 1024
DEPTH = 2
GRID_W = 64
CTX_LEN = 256
N_ADA = 9
EPS = 1e-6
NEG_INF = -1e30
ROPE_BASE = 10000.0
D_FF = 2816
HEAD_DIM = 64
NA_HEADS = 4
NA_WIN_ROWS = 8
NA_WIN_COLS = 16
D_A = NA_HEADS * HEAD_DIM
GM_GROUPS = 4
GM_WIDTH = 64
GM_CHUNK = 128
D_B = GM_GROUPS * GM_WIDTH
DA_HEADS = 4
DA_QK_DIM = 64
DA_V_DIM = 128
DA_VT_ROWS = DA_V_DIM + 16
D_C = DA_HEADS * DA_V_DIM
D_QK_C = DA_HEADS * 2 * DA_QK_DIM
D_IN = 3 * D_A + 2 * D_B + 2 * D_QK_C + D_C
QK_SCALE = HEAD_DIM ** -0.5
LOG2_E = math.log2(math.e)

LANES = 128
MXU_DIM_V7X = 256
VMEM_BYTES_V7X = 64 * 1024 * 1024
VMEM_LIMIT = VMEM_BYTES_V7X - 8 * 1024 * 1024

TOKEN_TILE = 256
ATT_Q_TILE = 256
NA_Q_ROWS = ATT_Q_TILE // GRID_W
NA_K_ROWS = NA_Q_ROWS + NA_WIN_ROWS
NA_K_TOK = NA_K_ROWS * GRID_W
DA_Q_TILE = 512
DA_KV_CHUNK = 1536
DA_KV_RAMP_UP = (256, 768)
DA_KV_RAMP_DOWN = (768, 512, 256, 256)
DA_KV_SMALL_CHUNK = 384
FF_CHUNKS =((0, 768), (768, 1536), (1536, 2304), (2304, 2816))


def _params(sem):
    return pltpu.CompilerParams(dimension_semantics=sem, vmem_limit_bytes=VMEM_LIMIT)


def _resident(shape):
    nd = len(shape)
    return pl.BlockSpec(shape, lambda *_: (0,) * nd, pipeline_mode=pl.Buffered(1))


def _layer_resident(shape, layer):
    nd = len(shape)
    return pl.BlockSpec((1,) + shape, lambda *_: (layer,) + (0,) * nd,
                        pipeline_mode=pl.Buffered(1))


def _rms(x):
    return x * lax.rsqrt(jnp.mean(x * x, axis=-1, keepdims=True) + EPS)


def _mod_row(mod_ref, idx):
    return mod_ref[0, 0, idx:idx + 1, :]


def _mod_spec(tm, n_latent, d):
    assert n_latent % tm == 0
    return pl.BlockSpec((1, 1, N_ADA, d), lambda i, j: (i, (j * tm >= n_latent).astype(jnp.int32), 0, 0))


def _ada_kernel(c_ref, w_ref, b_ref, o_ref):
    c = c_ref[...]
    cs = c / (1.0 + jnp.exp(-c))
    o_ref[0] = jnp.dot(cs, w_ref[0], preferred_element_type=F32,
                       precision=lax.Precision.HIGHEST) + b_ref[0]


def _ada(cvec, w_ada, b_ada):
    depth, d, n = w_ada.shape
    tn = 1024
    return pl.pallas_call(
        _ada_kernel,
        grid=(depth, n // tn),
        in_specs=[
            pl.BlockSpec((8, d), lambda l, j: (0, 0)),
            pl.BlockSpec((1, d, tn), lambda l, j: (l, 0, j)),
            pl.BlockSpec((1, 1, tn), lambda l, j: (l, 0, j)),
        ],
        out_specs=pl.BlockSpec((1, 8, tn), lambda l, j: (l, 0, j)),
        out_shape=jax.ShapeDtypeStruct((depth, 8, n), F32),
        name="ada",
        compiler_params=_params(("arbitrary", "arbitrary")),
    )(cvec, w_ada, b_ada.reshape(depth, 1, n))


def _ffn_kernel(*refs, tm, n_latent, mi, split_ctx, mix, final):
    refs = list(refs)
    x_ref, mod_ref = refs[:2]
    del refs[:2]
    if split_ctx:
        ctx_ref = refs.pop(0)
    if mix:
        oa_ref, ob_ref, oc_ref, wout_ref = refs[:4]
        del refs[:4]
    w1_ref, w3_ref, w2_ref = refs[:3]
    del refs[:3]
    if final:
        gain_ref = refs.pop(0)
    (o_ref,) = refs
    x = x_ref[0]
    if split_ctx:
        x = jnp.where(pl.program_id(1) * tm >= n_latent, ctx_ref[0], x)
    if mix:
        cat = jnp.concatenate([oa_ref[0], ob_ref[0], oc_ref[0]], axis=-1)
        x = x + _mod_row(mod_ref, mi - 1) * jnp.dot(cat, wout_ref[0],
                                                    preferred_element_type=F32)
    shift = _mod_row(mod_ref, mi)
    scale1 = 1.0 + _mod_row(mod_ref, mi + 1)
    half_gate = 0.5 * _mod_row(mod_ref, mi + 2)
    h = (_rms(x) * scale1 + shift).astype(BF16)
    acc = jnp.zeros((tm, D_MODEL), F32)
    for c0, c1 in FF_CHUNKS:
        a = jnp.dot(h, w1_ref[0, :, c0:c1], preferred_element_type=F32)
        g = jnp.dot(h, w3_ref[0, :, c0:c1], preferred_element_type=F32)
        y = (a / (1.0 + jnp.exp(-a)) * g).astype(BF16)
        acc = acc + jnp.dot(y, w2_ref[0, c0:c1, :], preferred_element_type=F32)
    out = x + half_gate * acc
    if final:
        out = _rms(out) * gain_ref[...]
    o_ref[0] = out


def _ffn(x, modt, w1, w3, w2, *, layer, n_latent, mi, ctx=None, mix=None, final_gain=None):
    b, _, d = x.shape
    t = n_latent + CTX_LEN
    tm = TOKEN_TILE
    final = final_gain is not None
    t_out = n_latent if final else t
    tile = lambda width: pl.BlockSpec((1, tm, width), lambda i, j: (i, j, 0))
    in_specs = [tile(d), _mod_spec(tm, n_latent, d)]
    args = [x, modt]
    if ctx is not None:
        assert tm == CTX_LEN and not final
        last_latent = n_latent // tm - 1
        in_specs[0] = pl.BlockSpec((1, tm, d), lambda i, j: (i, jnp.minimum(j, last_latent), 0))
        in_specs.append(pl.BlockSpec((1, CTX_LEN, d), lambda i, j: (i, 0, 0)))
        args.append(ctx)
    if mix is not None:
        in_specs += [tile(D_A), tile(D_B), tile(D_C), _layer_resident((d, d), layer)]
        args += list(mix)
    in_specs += [_layer_resident((d, D_FF), layer), _layer_resident((d, D_FF), layer),
                 _layer_resident((D_FF, d), layer)]
    args += [w1, w3, w2]
    if final:
        in_specs.append(_resident((1, d)))
        args.append(final_gain.reshape(1, d))
    return pl.pallas_call(
        functools.partial(_ffn_kernel, tm=tm, n_latent=n_latent, mi=mi, split_ctx=ctx is not None,
                          mix=mix is not None, final=final),
        grid=(b, t_out // tm),
        in_specs=in_specs,
        out_specs=pl.BlockSpec((1, tm, d), lambda i, j: (i, j, 0)),
        out_shape=jax.ShapeDtypeStruct((b, t_out, d), F32),
        name="ffn_final" if final else ("ffn_mix" if mix is not None else "ffn"),
        compiler_params=_params(("arbitrary", "arbitrary")),
    )(*args)


def _swap_rope_halves(x, first_half):
    return jnp.where(first_half, pltpu.roll(x, 96, 1), pltpu.roll(x, 32, 1))


def _proj_in_kernel(x_ref, mod_ref, w_ref, cos_ref, sin_ref, ws_ref, bs_ref, gn_ref, gmat_ref,
                    za_ref, kat_ref, ob_ref, kd_ref, qt_ref, vt_ref, *, tm):
    x = x_ref[0]
    h = (_rms(x) * (1.0 + _mod_row(mod_ref, 4)) + _mod_row(mod_ref, 3)).astype(BF16)

    z_all = jnp.dot(h, w_ref[0], preferred_element_type=F32)
    project = lambda c0, width: z_all[:, c0:c0 + width]

    z = project(3 * D_A, 2 * D_B)
    u = z[:, 0:D_B]
    v = z[:, D_B:]
    u = 0.5 * u * (1.0 + lax.erf(u * (2.0 ** -0.5)))
    v = 0.5 * v * (1.0 + lax.erf(v * (2.0 ** -0.5)))
    v2 = v * v
    v2_hi = v2.astype(BF16)
    v2_lo = (v2 - v2_hi.astype(F32)).astype(BF16)
    ms = (jnp.dot(v2_hi, gmat_ref[...], preferred_element_type=F32)
          + jnp.dot(v2_lo, gmat_ref[...], preferred_element_type=F32)) * (1.0 / GM_WIDTH)
    vn = (v * lax.rsqrt(ms + EPS) * gn_ref[...]).astype(BF16)
    lane_group = lax.broadcasted_iota(jnp.int32, (1, D_B), 1) // GM_WIDTH
    for c in range(tm // GM_CHUNK):
        rows = slice(c * GM_CHUNK, (c + 1) * GM_CHUNK)
        r = jnp.dot(ws_ref[...], vn[rows, :], preferred_element_type=F32)
        s = bs_ref[...]
        for g in range(GM_GROUPS):
            s = s + jnp.where(lane_group == g, r[g * GM_CHUNK:(g + 1) * GM_CHUNK, :], 0.0)
        ob_ref[0, rows, :] = (u[rows, :] * s).astype(BF16)

    z = project(0, 3 * D_A)
    za_ref[0, :, 0:D_A] = (z[:, 0:D_A] * (QK_SCALE * LOG2_E)).astype(BF16)
    kat_ref[0] = z[:, D_A:2 * D_A].T.astype(BF16)
    za_ref[0, :, D_A:2 * D_A] = z[:, 2 * D_A:3 * D_A].astype(BF16)

    o = 3 * D_A + 2 * D_B
    cos = cos_ref[...]
    sin = sin_ref[...]
    first_half = (lax.broadcasted_iota(jnp.int32, (1, LANES), 1) % DA_QK_DIM) < (DA_QK_DIM // 2)
    n_q_blocks = D_QK_C // LANES
    for j in range(2 * n_q_blocks):
        if j % n_q_blocks == 0:
            z = project(o + j * LANES, D_QK_C)
        zz = z[:, (j % n_q_blocks) * LANES:(j % n_q_blocks + 1) * LANES]
        r = zz * cos + _swap_rope_halves(zz, first_half) * sin
        if j < n_q_blocks:
            qt_ref[0, j * LANES:(j + 1) * LANES, :] = (r * (QK_SCALE * LOG2_E)).T.astype(BF16)
        else:
            jk = j - n_q_blocks
            kd_ref[0, :, jk * LANES:(jk + 1) * LANES] = r.astype(BF16)
    z = project(o + 2 * D_QK_C, D_C)
    for j in range(D_C // LANES):
        r0 = j * DA_VT_ROWS
        vt_ref[0, r0:r0 + DA_V_DIM, :] = z[:, j * LANES:(j + 1) * LANES].T.astype(BF16)
        vt_ref[0, r0 + DA_V_DIM:r0 + DA_VT_ROWS, :] = jnp.ones((DA_VT_ROWS - DA_V_DIM, tm), BF16)


def _proj_in(x, modt, w_in, cos_t, sin_t, ws_stack, bs_full, gn, gmat, *, layer, n_latent):
    b, t, d = x.shape
    tm = TOKEN_TILE
    return pl.pallas_call(
        functools.partial(_proj_in_kernel, tm=tm),
        grid=(b, t // tm),
        in_specs=[
            pl.BlockSpec((1, tm, d), lambda i, j: (i, j, 0)),
            _mod_spec(tm, n_latent, d),
            _layer_resident((d, D_IN), layer),
            pl.BlockSpec((tm, LANES), lambda i, j: (j, 0)),
            pl.BlockSpec((tm, LANES), lambda i, j: (j, 0)),
            _resident((GM_GROUPS * GM_CHUNK, GM_CHUNK)),
            _resident((GM_CHUNK, D_B)),
            _resident((1, D_B)),
            _resident((D_B, D_B)),
        ],
        out_specs=[
            pl.BlockSpec((1, tm, 2 * D_A), lambda i, j: (i, j, 0)),
            pl.BlockSpec((1, D_A, tm), lambda i, j: (i, 0, j)),
            pl.BlockSpec((1, tm, D_B), lambda i, j: (i, j, 0)),
            pl.BlockSpec((1, tm, D_QK_C), lambda i, j: (i, j, 0)),
            pl.BlockSpec((1, D_QK_C, tm), lambda i, j: (i, 0, j)),
            pl.BlockSpec((1, DA_HEADS * DA_VT_ROWS, tm), lambda i, j: (i, 0, j)),
        ],
        out_shape=[
            jax.ShapeDtypeStruct((b, t, 2 * D_A), BF16),
            jax.ShapeDtypeStruct((b, D_A, t), BF16),
            jax.ShapeDtypeStruct((b, t, D_B), BF16),
            jax.ShapeDtypeStruct((b, t, D_QK_C), BF16),
            jax.ShapeDtypeStruct((b, D_QK_C, t), BF16),
            jax.ShapeDtypeStruct((b, DA_HEADS * DA_VT_ROWS, t), BF16),
        ],
        name="proj_in",
        compiler_params=_params(("arbitrary", "arbitrary")),
    )(x, modt, w_in, cos_t, sin_t, ws_stack, bs_full, gn, gmat)


def _na_kernel(q_ref, qn_ref, kt_ref, v_ref, bias_ref, biasn_ref, o_ref, sa_ref, sb_ref, *, n_latent):
    g = pl.program_id(2)
    n_groups = n_latent // ATT_Q_TILE
    rows = n_latent // GRID_W
    lane = lax.broadcasted_iota(jnp.int32, (1, LANES), 1)

    def window_start(grp):
        base_row = jnp.clip(grp * NA_Q_ROWS - NA_WIN_ROWS // 2, 0, rows - NA_K_ROWS)
        return pl.multiple_of(base_row * GRID_W, ATT_Q_TILE)

    def write_scores(s_ref, q, bias, grp):
        qs = jnp.concatenate([jnp.where(lane < HEAD_DIM, q, jnp.zeros_like(q)),
                              jnp.where(lane >= HEAD_DIM, q, jnp.zeros_like(q))], axis=0)
        sw = jnp.dot(qs, kt_ref[0, :, pl.ds(window_start(grp), NA_K_TOK)],
                     preferred_element_type=F32)
        s_ref[:, 0:NA_K_TOK] = sw + bias.reshape(2 * ATT_Q_TILE, NA_K_TOK)
        s_ref[:, NA_K_TOK:] = jnp.dot(qs, kt_ref[0, :, n_latent:], preferred_element_type=F32)

    def softmax_pv(s_ref):
        s = s_ref[...]
        p = jnp.exp2(s - s.max(axis=-1, keepdims=True))
        l = p.sum(axis=-1, keepdims=True)
        p = p.astype(BF16)
        o = (jnp.dot(p[:, 0:NA_K_TOK], v_ref[0, pl.ds(window_start(g), NA_K_TOK), :],
                     preferred_element_type=F32)
             + jnp.dot(p[:, NA_K_TOK:], v_ref[0, n_latent:, :], preferred_element_type=F32)) / l
        o_ref[0] = jnp.where(lane < HEAD_DIM, o[:ATT_Q_TILE], o[ATT_Q_TILE:]).astype(BF16)

    @pl.when(g == 0)
    def _():
        write_scores(sa_ref, q_ref[0], bias_ref[0], g)

    g_next = jnp.minimum(g + 1, n_groups)

    @pl.when(g % 2 == 0)
    def _():
        write_scores(sb_ref, qn_ref[0], biasn_ref[0], g_next)
        softmax_pv(sa_ref)

    @pl.when(g % 2 == 1)
    def _():
        write_scores(sa_ref, qn_ref[0], biasn_ref[0], g_next)
        softmax_pv(sb_ref)


def _na_attention(za, kat, bias, *, n_latent, ctx_queries):
    b, t, _ = za.shape
    n_groups = n_latent // ATT_Q_TILE
    n_steps = n_groups + 1 if ctx_queries else n_groups
    assert t == n_latent + CTX_LEN and CTX_LEN == ATT_Q_TILE
    assert n_groups >= 3 and n_latent // GRID_W >= NA_K_ROWS

    def kind(g):
        return jnp.where(g == 0, 0, jnp.where(g < n_groups - 1, 1, jnp.where(g == n_groups - 1, 2, 3)))

    nxt = lambda g: jnp.minimum(g + 1, n_groups)
    q_spec = lambda at: pl.BlockSpec((1, ATT_Q_TILE, LANES), lambda i, hp, g: (i, at(g), hp))
    bias_spec = lambda at: pl.BlockSpec((1, 2, ATT_Q_TILE, NA_K_TOK),
                                        lambda i, hp, g: (kind(at(g)), hp, 0, 0))
    same = lambda g: g
    score_buf = pltpu.VMEM((2 * ATT_Q_TILE, NA_K_TOK + CTX_LEN), F32)
    return pl.pallas_call(
        functools.partial(_na_kernel, n_latent=n_latent),
        grid=(b, NA_HEADS // 2, n_steps),
        in_specs=[
            q_spec(same), q_spec(nxt),
            pl.BlockSpec((1, LANES, t), lambda i, hp, g: (i, hp, 0)),
            pl.BlockSpec((1, t, LANES), lambda i, hp, g: (i, 0, 2 + hp)),
            bias_spec(same), bias_spec(nxt),
        ],
        out_specs=pl.BlockSpec((1, ATT_Q_TILE, LANES), lambda i, hp, g: (i, g, hp)),
        out_shape=jax.ShapeDtypeStruct((b, n_steps * ATT_Q_TILE, D_A), BF16),
        scratch_shapes=[score_buf, score_buf],
        name="na_attention",
        compiler_params=_params(("arbitrary", "arbitrary", "arbitrary")),
    )(za, za, kat, za, bias, bias)


NA_BIAS_PAD = NA_WIN_ROWS // 2
NA_BIAS_W = 11 * LANES


def _na_bias_plan(n_latent):
    rows = n_latent // GRID_W
    n_groups = n_latent // ATT_Q_TILE
    plan = []
    for g in (0, 1, n_groups - 1):
        base = int(np.clip(g * NA_Q_ROWS - NA_WIN_ROWS // 2, 0, rows - NA_K_ROWS))
        kind = []
        for i in range(NA_Q_ROWS):
            r = g * NA_Q_ROWS + i
            start = int(np.clip(r - NA_WIN_ROWS // 2, 0, rows - NA_WIN_ROWS))
            e0 = base - r + NA_WIN_ROWS - 1 + NA_BIAS_PAD
            assert 0 <= e0 and (e0 - e0 % 2) * GRID_W + NA_K_TOK <= NA_BIAS_W
            kind.append((e0, start - base, start - base + NA_WIN_ROWS - 1))
        plan.append(tuple(kind))
    return tuple(plan)


def _na_bias_kernel(r_ref, o_ref, *, plan):
    shape = (GRID_W, NA_BIAS_W)
    qc = lax.broadcasted_iota(jnp.int32, shape, 0)
    col = lax.broadcasted_iota(jnp.int32, shape, 1)
    kc = col % GRID_W
    dc = kc - qc + (NA_WIN_COLS - 1)
    w0 = jnp.clip(qc - NA_WIN_COLS // 2, 0, GRID_W - NA_WIN_COLS)
    col_ok = (kc >= w0) & (kc < w0 + NA_WIN_COLS)
    tables = []
    for p in range(2):
        a = col // GRID_W + (p - NA_BIAS_PAD)
        acc = jnp.zeros(shape, F32)
        for b in range(2 * NA_WIN_COLS - 1):
            acc = acc + jnp.where(dc == b, r_ref[0, p, b:b + 1, :], 0.0)
        ok = col_ok & (a >= 0) & (a <= 2 * NA_WIN_ROWS - 2)
        tables.append(jnp.where(ok, acc * LOG2_E, NEG_INF))
    kj = lax.broadcasted_iota(jnp.int32, (GRID_W, NA_K_TOK), 1) // GRID_W
    for k, kind in enumerate(plan):
        for i, (e0, jlo, jhi) in enumerate(kind):
            p = e0 % 2
            off = (e0 - p) * GRID_W
            slab = tables[p][:, off:off + NA_K_TOK]
            o_ref[k, 0, i * GRID_W:(i + 1) * GRID_W, :] = jnp.where(
                (kj >= jlo) & (kj <= jhi), slab, NEG_INF)
    o_ref[len(plan), 0] = jnp.full((ATT_Q_TILE, NA_K_TOK), NEG_INF, F32)


def _na_bias_tables(rpb, n_latent):
    h = rpb.shape[0]
    n_blocks = NA_BIAS_W // GRID_W + 2
    rp = jnp.pad(rpb, ((0, 0), (NA_BIAS_PAD, n_blocks - NA_BIAS_PAD - rpb.shape[1]), (0, 0)))
    rep = jnp.transpose(jnp.repeat(rp, GRID_W, axis=1), (0, 2, 1))
    r = jnp.stack([rep[:, :, p * GRID_W:p * GRID_W + NA_BIAS_W] for p in range(2)], axis=1)
    n_dc = rpb.shape[2]
    return pl.pallas_call(
        functools.partial(_na_bias_kernel, plan=_na_bias_plan(n_latent)),
        grid=(h,),
        in_specs=[pl.BlockSpec((1, 2, n_dc, NA_BIAS_W), lambda i: (i, 0, 0, 0))],
        out_specs=pl.BlockSpec((4, 1, ATT_Q_TILE, NA_K_TOK), lambda i: (0, i, 0, 0)),
        out_shape=jax.ShapeDtypeStruct((4, h, ATT_Q_TILE, NA_K_TOK), F32),
        name="na_bias",
        compiler_params=_params(("arbitrary",)),
    )(r)


def _kv_chunk_bounds(t):
    head, tail = DA_KV_RAMP_UP, DA_KV_RAMP_DOWN
    middle = t - sum(head) - sum(tail)
    if middle < 0 or middle % MXU_DIM_V7X:
        assert t % DA_KV_SMALL_CHUNK == 0
        sizes = (DA_KV_SMALL_CHUNK,) * (t // DA_KV_SMALL_CHUNK)
    else:
        units = middle // MXU_DIM_V7X
        n_mid = -(-units // (DA_KV_CHUNK // MXU_DIM_V7X))
        n_mid += n_mid % 2
        mid = tuple((units // n_mid + (j < units % n_mid)) * MXU_DIM_V7X for j in range(n_mid))
        sizes = head + mid + tail
    assert max(sizes) <= DA_KV_CHUNK and len(sizes) % 2 == 0
    return tuple(int(v) for v in np.cumsum((0,) + sizes))


def _diff_kernel(*refs, n_keys, lambda_init, latent):
    if latent:
        qt_ref, qtn_ref, k_ref, vt_ref, lqk_ref, subln_ref, o_ref, s_ref, smax_ref = refs
    else:
        qt_ref, k_ref, vt_ref, lqk_ref, subln_ref, _, o_ref = refs
    tq = qt_ref.shape[2]
    feat = lax.broadcasted_iota(jnp.int32, (LANES, 1), 0)

    def stack_components(qt):
        return jnp.concatenate([jnp.where(feat < DA_QK_DIM, qt, jnp.zeros_like(qt)),
                                jnp.where(feat >= DA_QK_DIM, qt, jnp.zeros_like(qt))], axis=1)

    qst = stack_components(qt_ref[0])
    lqk = lqk_ref[...]
    lam = (jnp.exp(jnp.sum(lqk[0:1] * lqk[1:2], axis=-1, keepdims=True))
           - jnp.exp(jnp.sum(lqk[2:3] * lqk[3:4], axis=-1, keepdims=True)) + lambda_init)

    def col_reduce(x, op):
        r, c = x.shape
        groups = 8 if r % 64 == 0 else 1
        x = x.reshape(groups, r // (8 * groups), 8, c)
        return op(op(op(x, axis=1), axis=0), axis=0, keepdims=True)

    def scores(k, q=qst):
        s = jnp.dot(k, q, preferred_element_type=F32)
        return s, col_reduce(s, jnp.max)

    def step(s, s_max, vt1, state):
        m, acc = state
        m_new = jnp.maximum(m, s_max)
        alpha = jnp.exp2(m - m_new)
        p = jnp.exp2(s - m_new)
        acc = alpha * acc + jnp.dot(vt1, p.astype(BF16), preferred_element_type=F32)
        return (m_new, acc)

    def init():
        return (jnp.full((1, 2 * tq), NEG_INF, F32), jnp.zeros((DA_VT_ROWS, 2 * tq), F32))

    def finish(state):
        _, acc = state
        o = acc[:DA_V_DIM] / acc[DA_V_DIM:DA_V_DIM + 1]
        o = o[:, :tq] - lam * o[:, tq:]
        o = o * lax.rsqrt(jnp.mean(o * o, axis=0, keepdims=True) + EPS)
        o = o * (subln_ref[...] * (1.0 - lambda_init))
        o_ref[0] = o.T.astype(BF16)

    if not latent:
        s, s_max = scores(k_ref[0])
        finish(step(s, s_max, vt_ref[0], init()))
        return

    bounds = _kv_chunk_bounds(n_keys)
    chunk = lambda c: slice(bounds[c], bounds[c + 1])
    size = lambda c: bounds[c + 1] - bounds[c]
    n_chunks = len(bounds) - 1
    assert n_chunks % 2 == 0

    @pl.when(pl.program_id(2) == 0)
    def _():
        s_ref[0, 0:size(0)], smax_ref[...] = scores(k_ref[0, chunk(0), :])

    s_max = smax_ref[...]
    state = init()
    for c in range(n_chunks):
        if c + 1 < n_chunks:
            s_ref[(c + 1) % 2, 0:size(c + 1)], next_max = scores(k_ref[0, chunk(c + 1), :])
        else:
            s_ref[0, 0:size(0)], smax_ref[...] = scores(k_ref[0, chunk(0), :],
                                                        stack_components(qtn_ref[0]))
        state = step(s_ref[c % 2, 0:size(c)], s_max, vt_ref[0, :, chunk(c)], state)
        s_max = next_max
    finish(state)


def _diff_attention(kd, qt, vt, lqk, subln, *, n_latent, lambda_init, ctx_queries):
    b, t, _ = kd.shape
    tq = DA_Q_TILE
    n_q_tiles = n_latent // tq
    consts = [pl.BlockSpec((4, DA_QK_DIM), lambda *_: (0, 0)),
              pl.BlockSpec((DA_V_DIM, 1), lambda *_: (0, 0))]
    out_rows = t if ctx_queries else n_latent
    oc = pl.pallas_call(
        functools.partial(_diff_kernel, n_keys=t, lambda_init=lambda_init, latent=True),
        grid=(b, DA_HEADS, n_q_tiles),
        in_specs=[
            pl.BlockSpec((1, LANES, tq), lambda bi, h, i: (bi, h, i)),
            pl.BlockSpec((1, LANES, tq), lambda bi, h, i: (bi, h, jnp.minimum(i + 1, n_q_tiles - 1))),
            pl.BlockSpec((1, t, LANES), lambda bi, h, i: (bi, 0, h)),
            pl.BlockSpec((1, DA_VT_ROWS, t), lambda bi, h, i: (bi, h, 0)),
        ] + consts,
        out_specs=pl.BlockSpec((1, tq, LANES), lambda bi, h, i: (bi, i, h)),
        out_shape=jax.ShapeDtypeStruct((b, out_rows, D_C), BF16),
        scratch_shapes=[pltpu.VMEM((2, DA_KV_CHUNK, 2 * tq), F32), pltpu.VMEM((1, 2 * tq), F32)],
        name="diff_attention",
        compiler_params=_params(("arbitrary", "arbitrary", "arbitrary")),
    )(qt, qt, kd, vt, lqk, subln)
    if not ctx_queries:
        return oc
    ctx_blk = n_latent // CTX_LEN
    return pl.pallas_call(
        functools.partial(_diff_kernel, n_keys=CTX_LEN, lambda_init=lambda_init, latent=False),
        grid=(b, DA_HEADS),
        in_specs=[
            pl.BlockSpec((1, LANES, CTX_LEN), lambda bi, h: (bi, h, ctx_blk)),
            pl.BlockSpec((1, CTX_LEN, LANES), lambda bi, h: (bi, ctx_blk, h)),
            pl.BlockSpec((1, DA_VT_ROWS, CTX_LEN), lambda bi, h: (bi, h, ctx_blk)),
        ] + consts + [pl.BlockSpec(memory_space=pl.ANY)],
        out_specs=pl.BlockSpec((1, CTX_LEN, LANES), lambda bi, h: (bi, ctx_blk, h)),
        out_shape=jax.ShapeDtypeStruct((b, out_rows, D_C), BF16),
        input_output_aliases={5: 0},
        name="diff_attention_ctx",
        compiler_params=_params(("arbitrary", "arbitrary")),
    )(qt, kd, vt, lqk, subln, oc)


def _rope_tables(n_latent):
    tok = jnp.arange(n_latent)
    row = (tok // GRID_W).astype(F32)
    col = (tok % GRID_W).astype(F32)
    n_freq = DA_QK_DIM // 4
    freqs = ROPE_BASE ** (-jnp.arange(n_freq, dtype=F32) / n_freq)
    ang = jnp.concatenate([row[:, None] * freqs, col[:, None] * freqs], axis=-1)
    cos, sin = jnp.cos(ang), jnp.sin(ang)
    cos_t = jnp.concatenate([cos, cos, cos, cos], axis=-1)
    sin_t = jnp.concatenate([-sin, sin, -sin, sin], axis=-1)
    cos_t = jnp.concatenate([cos_t, jnp.ones((CTX_LEN, LANES), F32)], axis=0)
    sin_t = jnp.concatenate([sin_t, jnp.zeros((CTX_LEN, LANES), F32)], axis=0)
    return cos_t, sin_t


def kernel(x, c, ctx, c_ctx, w_ada, b_ada, ffn1_w1, ffn1_w3, ffn1_w2, w_in, w_out, na_rpb,
           gm_ws, gm_bs, gm_norm, da_lq1, da_lk1, da_lq2, da_lk2, da_subln,
           ffn2_w1, ffn2_w3, ffn2_w2, final_norm):
    b, n_latent, d = x.shape
    assert d == D_MODEL and ctx.shape[1] == CTX_LEN and b < 8
    assert n_latent % ATT_Q_TILE == 0

    cvec = jnp.zeros((8, d), F32).at[:b].set(c).at[b].set(c_ctx)
    mod = _ada(cvec, w_ada, b_ada).reshape(DEPTH, 8, N_ADA, d)
    cos_t, sin_t = _rope_tables(n_latent)
    gmat = jnp.asarray(np.kron(np.eye(GM_GROUPS), np.ones((GM_WIDTH, GM_WIDTH))), BF16)

    bf16 = lambda w: w.astype(BF16)
    ffn1 = (bf16(ffn1_w1), bf16(ffn1_w3), bf16(ffn1_w2))
    ffn2 = (bf16(ffn2_w1), bf16(ffn2_w3), bf16(ffn2_w2))
    w_in, w_out = bf16(w_in), bf16(w_out)

    xs = x
    for l in range(DEPTH):
        last = l == DEPTH - 1
        lambda_init = 0.8 - 0.6 * math.exp(-0.3 * l)
        modt = jnp.stack([mod[l, :b], jnp.broadcast_to(mod[l, b], (b, N_ADA, d))], axis=1)
        bias = _na_bias_tables(na_rpb[l], n_latent)
        ws_stack = gm_ws[l].reshape(GM_GROUPS * GM_CHUNK, GM_CHUNK).astype(BF16)
        bs_full = jnp.repeat(gm_bs[l].T, GM_WIDTH, axis=1)
        gn = gm_norm[l].reshape(1, D_B)
        lqk = jnp.stack([da_lq1[l], da_lk1[l], da_lq2[l], da_lk2[l]])

        xs = _ffn(xs, modt, *ffn1, layer=l, n_latent=n_latent, mi=0, ctx=ctx if l == 0 else None)
        za, kat, ob, kd, qt, vt = _proj_in(xs, modt, w_in, cos_t, sin_t, ws_stack, bs_full, gn,
                                           gmat, layer=l, n_latent=n_latent)
        oa = _na_attention(za, kat, bias, n_latent=n_latent, ctx_queries=not last)
        oc = _diff_attention(kd, qt, vt, lqk, da_subln[l].reshape(DA_V_DIM, 1),
                             n_latent=n_latent, lambda_init=lambda_init, ctx_queries=not last)
        xs = _ffn(xs, modt, *ffn2, layer=l, n_latent=n_latent, mi=6, mix=(oa, ob, oc, w_out),
                  final_gain=final_norm if last else None)
    return xs
```

```python
import functools
import math

import numpy as np
import jax
import jax.numpy as jnp
from jax import lax
from jax.experimental import pallas as pl
from jax.experimental.pallas import tpu as pltpu

F32 = jnp.float32
BF16 = jnp.bfloat16

D_MODEL = 1024
DEPTH = 2
GRID_W = 64
CTX_LEN = 256
N_ADA = 9
EPS = 1e-6
NEG_INF = -1e30
ROPE_BASE = 10000.0
D_FF = 2816
HEAD_DIM = 64
NA_HEADS = 4
NA_WIN_ROWS = 8
NA_WIN_COLS = 16
D_A = NA_HEADS * HEAD_DIM
GM_GROUPS = 4
GM_WIDTH = 64
GM_CHUNK = 128
D_B = GM_GROUPS * GM_WIDTH
DA_HEADS = 4
DA_QK_DIM = 64
DA_V_DIM = 128
DA_VT_ROWS = DA_V_DIM + 16
D_C = DA_HEADS * DA_V_DIM
D_QK_C = DA_HEADS * 2 * DA_QK_DIM
D_IN = 3 * D_A + 2 * D_B + 2 * D_QK_C + D_C
QK_SCALE = HEAD_DIM ** -0.5
LOG2_E = math.log2(math.e)

LANES = 128
MXU_DIM_V7X = 256
VMEM_BYTES_V7X = 64 * 1024 * 1024
VMEM_LIMIT = VMEM_BYTES_V7X - 8 * 1024 * 1024

TOKEN_TILE = 256
FFN_SUB_TILES = (3, 2, 1)
ATT_Q_TILE = 256
NA_Q_ROWS = ATT_Q_TILE // GRID_W
NA_K_ROWS = NA_Q_ROWS + NA_WIN_ROWS
NA_K_TOK = NA_K_ROWS * GRID_W
DA_Q_TILE = 512
DA_KV_CHUNK = 1536
DA_KV_RAMP_UP = (256, 768)
DA_KV_RAMP_DOWN = (768, 512, 256, 256)
DA_KV_SMALL_CHUNK = 384
FF_CHUNKS =((0, 768), (768, 1536), (1536, 2304), (2304, 2816))


def _params(sem):
    return pltpu.CompilerParams(dimension_semantics=sem, vmem_limit_bytes=VMEM_LIMIT)


def _resident(shape):
    nd = len(shape)
    return pl.BlockSpec(shape, lambda *_: (0,) * nd, pipeline_mode=pl.Buffered(1))


def _layer_resident(shape, layer):
    nd = len(shape)
    return pl.BlockSpec((1,) + shape, lambda *_: (layer,) + (0,) * nd,
                        pipeline_mode=pl.Buffered(1))


def _rms(x):
    return x * lax.rsqrt(jnp.mean(x * x, axis=-1, keepdims=True) + EPS)


def _mod_row(mod_ref, idx):
    return mod_ref[0, 0, idx:idx + 1, :]


def _mod_spec(tm, n_latent, d):
    assert n_latent % tm == 0
    return pl.BlockSpec((1, 1, N_ADA, d), lambda i, j: (i, (j * tm >= n_latent).astype(jnp.int32), 0, 0))


def _ada_kernel(c_ref, w_ref, b_ref, o_ref):
    c = c_ref[...]
    cs = c / (1.0 + jnp.exp(-c))
    o_ref[0] = jnp.dot(cs, w_ref[0], preferred_element_type=F32,
                       precision=lax.Precision.HIGHEST) + b_ref[0]


def _ada(cvec, w_ada, b_ada):
    depth, d, n = w_ada.shape
    tn = 1024
    return pl.pallas_call(
        _ada_kernel,
        grid=(depth, n // tn),
        in_specs=[
            pl.BlockSpec((8, d), lambda l, j: (0, 0)),
            pl.BlockSpec((1, d, tn), lambda l, j: (l, 0, j)),
            pl.BlockSpec((1, 1, tn), lambda l, j: (l, 0, j)),
        ],
        out_specs=pl.BlockSpec((1, 8, tn), lambda l, j: (l, 0, j)),
        out_shape=jax.ShapeDtypeStruct((depth, 8, n), F32),
        name="ada",
        compiler_params=_params(("arbitrary", "arbitrary")),
    )(cvec, w_ada, b_ada.reshape(depth, 1, n))


def _ffn_kernel(*refs, tm, n_sub, n_latent, mi, split_ctx, mix, final):
    refs = list(refs)
    if split_ctx:
        x_refs = refs[:n_sub]
        del refs[:n_sub]
        ctx_ref = refs.pop(0)
    else:
        x_ref = refs.pop(0)
    mod_ref = refs.pop(0)
    if mix:
        oa_ref, ob_ref, oc_ref, wout_ref = refs[:4]
        del refs[:4]
    w1_ref, w3_ref, w2_ref = refs[:3]
    del refs[:3]
    if final:
        gain_ref = refs.pop(0)
    (o_ref,) = refs
    for sub in range(n_sub):
        rows = slice(sub * tm, (sub + 1) * tm)
        is_ctx = (pl.program_id(1) * n_sub + sub) * tm >= n_latent

        def mod(idx):
            if final:
                return mod_ref[0, 0, idx:idx + 1, :]
            return jnp.where(is_ctx, mod_ref[0, 1, idx:idx + 1, :], mod_ref[0, 0, idx:idx + 1, :])

        if split_ctx:
            x = jnp.where(is_ctx, ctx_ref[0], x_refs[sub][0])
        else:
            x = x_ref[0, rows, :]
        if mix:
            cat = jnp.concatenate([oa_ref[0, rows, :], ob_ref[0, rows, :], oc_ref[0, rows, :]],
                                  axis=-1)
            x = x + mod(mi - 1) * jnp.dot(cat, wout_ref[0], preferred_element_type=F32)
        shift = mod(mi)
        scale1 = 1.0 + mod(mi + 1)
        half_gate = 0.5 * mod(mi + 2)
        h = (_rms(x) * scale1 + shift).astype(BF16)
        acc = jnp.zeros((tm, D_MODEL), F32)
        for c0, c1 in FF_CHUNKS:
            a = jnp.dot(h, w1_ref[0, :, c0:c1], preferred_element_type=F32)
            g = jnp.dot(h, w3_ref[0, :, c0:c1], preferred_element_type=F32)
            y = (a / (1.0 + jnp.exp(-a)) * g).astype(BF16)
            acc = acc + jnp.dot(y, w2_ref[0, c0:c1, :], preferred_element_type=F32)
        out = x + half_gate * acc
        if final:
            out = _rms(out) * gain_ref[...]
        o_ref[0, rows, :] = out


def _ffn(x, modt, w1, w3, w2, *, layer, n_latent, mi, ctx=None, mix=None, final_gain=None):
    b, _, d = x.shape
    t = n_latent + CTX_LEN
    tm = TOKEN_TILE
    final = final_gain is not None
    t_out = n_latent if final else t
    n_sub = next(n for n in FFN_SUB_TILES if (t_out // tm) % n == 0)
    step_rows = n_sub * tm
    tile = lambda width: pl.BlockSpec((1, step_rows, width), lambda i, j: (i, j, 0))
    if ctx is None:
        in_specs, args = [tile(d)], [x]
    else:
        assert tm == CTX_LEN and not final
        last_latent = n_latent // tm - 1
        sub_spec = lambda sub: pl.BlockSpec(
            (1, tm, d), lambda i, j: (i, jnp.minimum(j * n_sub + sub, last_latent), 0))
        in_specs = [sub_spec(sub) for sub in range(n_sub)]
        in_specs.append(pl.BlockSpec((1, CTX_LEN, d), lambda i, j: (i, 0, 0)))
        args = [x] * n_sub + [ctx]
    in_specs.append(pl.BlockSpec((1, 2, N_ADA, d), lambda i, j: (i, 0, 0, 0)))
    args.append(modt)
    if mix is not None:
        in_specs += [tile(D_A), tile(D_B), tile(D_C), _layer_resident((d, d), layer)]
        args += list(mix)
    in_specs += [_layer_resident((d, D_FF), layer), _layer_resident((d, D_FF), layer),
                 _layer_resident((D_FF, d), layer)]
    args += [w1, w3, w2]
    if final:
        in_specs.append(_resident((1, d)))
        args.append(final_gain.reshape(1, d))
    return pl.pallas_call(
        functools.partial(_ffn_kernel, tm=tm, n_sub=n_sub, n_latent=n_latent, mi=mi,
                          split_ctx=ctx is not None, mix=mix is not None, final=final),
        grid=(b, t_out // step_rows),
        in_specs=in_specs,
        out_specs=tile(d),
        out_shape=jax.ShapeDtypeStruct((b, t_out, d), F32),
        name="ffn_final" if final else ("ffn_mix" if mix is not None else "ffn"),
        compiler_params=_params(("arbitrary", "arbitrary")),
    )(*args)


def _swap_rope_halves(x, first_half):
    return jnp.where(first_half, pltpu.roll(x, 96, 1), pltpu.roll(x, 32, 1))


def _proj_in_kernel(x_ref, mod_ref, w_ref, cos_ref, sin_ref, ws_ref, bs_ref, gn_ref, gmat_ref,
                    za_ref, kat_ref, ob_ref, kd_ref, qt_ref, vt_ref, *, tm):
    x = x_ref[0]
    h = (_rms(x) * (1.0 + _mod_row(mod_ref, 4)) + _mod_row(mod_ref, 3)).astype(BF16)

    z_all = jnp.dot(h, w_ref[0], preferred_element_type=F32)
    project = lambda c0, width: z_all[:, c0:c0 + width]

    z = project(3 * D_A, 2 * D_B)
    u = z[:, 0:D_B]
    v = z[:, D_B:]
    u = 0.5 * u * (1.0 + lax.erf(u * (2.0 ** -0.5)))
    v = 0.5 * v * (1.0 + lax.erf(v * (2.0 ** -0.5)))
    v2 = v * v
    v2_hi = v2.astype(BF16)
    v2_lo = (v2 - v2_hi.astype(F32)).astype(BF16)
    ms = (jnp.dot(v2_hi, gmat_ref[...], preferred_element_type=F32)
          + jnp.dot(v2_lo, gmat_ref[...], preferred_element_type=F32)) * (1.0 / GM_WIDTH)
    vn = (v * lax.rsqrt(ms + EPS) * gn_ref[...]).astype(BF16)
    lane_group = lax.broadcasted_iota(jnp.int32, (1, D_B), 1) // GM_WIDTH
    for c in range(tm // GM_CHUNK):
        rows = slice(c * GM_CHUNK, (c + 1) * GM_CHUNK)
        r = jnp.dot(ws_ref[...], vn[rows, :], preferred_element_type=F32)
        s = bs_ref[...]
        for g in range(GM_GROUPS):
            s = s + jnp.where(lane_group == g, r[g * GM_CHUNK:(g + 1) * GM_CHUNK, :], 0.0)
        ob_ref[0, rows, :] = (u[rows, :] * s).astype(BF16)

    z = project(0, 3 * D_A)
    za_ref[0, :, 0:D_A] = (z[:, 0:D_A] * (QK_SCALE * LOG2_E)).astype(BF16)
    kat_ref[0] = z[:, D_A:2 * D_A].T.astype(BF16)
    za_ref[0, :, D_A:2 * D_A] = z[:, 2 * D_A:3 * D_A].astype(BF16)

    o = 3 * D_A + 2 * D_B
    cos = cos_ref[...]
    sin = sin_ref[...]
    first_half = (lax.broadcasted_iota(jnp.int32, (1, LANES), 1) % DA_QK_DIM) < (DA_QK_DIM // 2)
    n_q_blocks = D_QK_C // LANES
    for j in range(2 * n_q_blocks):
        if j % n_q_blocks == 0:
            z = project(o + j * LANES, D_QK_C)
        zz = z[:, (j % n_q_blocks) * LANES:(j % n_q_blocks + 1) * LANES]
        r = zz * cos + _swap_rope_halves(zz, first_half) * sin
        if j < n_q_blocks:
            qt_ref[0, j * LANES:(j + 1) * LANES, :] = (r * (QK_SCALE * LOG2_E)).T.astype(BF16)
        else:
            jk = j - n_q_blocks
            kd_ref[0, :, jk * LANES:(jk + 1) * LANES] = r.astype(BF16)
    z = project(o + 2 * D_QK_C, D_C)
    for j in range(D_C // LANES):
        r0 = j * DA_VT_ROWS
        vt_ref[0, r0:r0 + DA_V_DIM, :] = z[:, j * LANES:(j + 1) * LANES].T.astype(BF16)
        vt_ref[0, r0 + DA_V_DIM:r0 + DA_VT_ROWS, :] = jnp.ones((DA_VT_ROWS - DA_V_DIM, tm), BF16)


def _proj_in(x, modt, w_in, cos_t, sin_t, ws_stack, bs_full, gn, gmat, *, layer, n_latent):
    b, t, d = x.shape
    tm = TOKEN_TILE
    return pl.pallas_call(
        functools.partial(_proj_in_kernel, tm=tm),
        grid=(b, t // tm),
        in_specs=[
            pl.BlockSpec((1, tm, d), lambda i, j: (i, j, 0)),
            _mod_spec(tm, n_latent, d),
            _layer_resident((d, D_IN), layer),
            pl.BlockSpec((tm, LANES), lambda i, j: (j, 0)),
            pl.BlockSpec((tm, LANES), lambda i, j: (j, 0)),
            _resident((GM_GROUPS * GM_CHUNK, GM_CHUNK)),
            _resident((GM_CHUNK, D_B)),
            _resident((1, D_B)),
            _resident((D_B, D_B)),
        ],
        out_specs=[
            pl.BlockSpec((1, tm, 2 * D_A), lambda i, j: (i, j, 0)),
            pl.BlockSpec((1, D_A, tm), lambda i, j: (i, 0, j)),
            pl.BlockSpec((1, tm, D_B), lambda i, j: (i, j, 0)),
            pl.BlockSpec((1, tm, D_QK_C), lambda i, j: (i, j, 0)),
            pl.BlockSpec((1, D_QK_C, tm), lambda i, j: (i, 0, j)),
            pl.BlockSpec((1, DA_HEADS * DA_VT_ROWS, tm), lambda i, j: (i, 0, j)),
        ],
        out_shape=[
            jax.ShapeDtypeStruct((b, t, 2 * D_A), BF16),
            jax.ShapeDtypeStruct((b, D_A, t), BF16),
            jax.ShapeDtypeStruct((b, t, D_B), BF16),
            jax.ShapeDtypeStruct((b, t, D_QK_C), BF16),
            jax.ShapeDtypeStruct((b, D_QK_C, t), BF16),
            jax.ShapeDtypeStruct((b, DA_HEADS * DA_VT_ROWS, t), BF16),
        ],
        name="proj_in",
        compiler_params=_params(("arbitrary", "arbitrary")),
    )(x, modt, w_in, cos_t, sin_t, ws_stack, bs_full, gn, gmat)


def _na_kernel(q_ref, qn_ref, kt_ref, v_ref, bias_ref, biasn_ref, o_ref, sa_ref, sb_ref, *, n_latent):
    g = pl.program_id(2)
    n_groups = n_latent // ATT_Q_TILE
    rows = n_latent // GRID_W
    lane = lax.broadcasted_iota(jnp.int32, (1, LANES), 1)

    def window_start(grp):
        base_row = jnp.clip(grp * NA_Q_ROWS - NA_WIN_ROWS // 2, 0, rows - NA_K_ROWS)
        return pl.multiple_of(base_row * GRID_W, ATT_Q_TILE)

    def write_scores(s_ref, q, bias, grp):
        qs = jnp.concatenate([jnp.where(lane < HEAD_DIM, q, jnp.zeros_like(q)),
                              jnp.where(lane >= HEAD_DIM, q, jnp.zeros_like(q))], axis=0)
        sw = jnp.dot(qs, kt_ref[0, :, pl.ds(window_start(grp), NA_K_TOK)],
                     preferred_element_type=F32)
        s_ref[:, 0:NA_K_TOK] = sw + bias.reshape(2 * ATT_Q_TILE, NA_K_TOK)
        s_ref[:, NA_K_TOK:] = jnp.dot(qs, kt_ref[0, :, n_latent:], preferred_element_type=F32)

    def softmax_pv(s_ref):
        s = s_ref[...]
        p = jnp.exp2(s - s.max(axis=-1, keepdims=True))
        l = p.sum(axis=-1, keepdims=True)
        p = p.astype(BF16)
        o = (jnp.dot(p[:, 0:NA_K_TOK], v_ref[0, pl.ds(window_start(g), NA_K_TOK), :],
                     preferred_element_type=F32)
             + jnp.dot(p[:, NA_K_TOK:], v_ref[0, n_latent:, :], preferred_element_type=F32)) / l
        o_ref[0] = jnp.where(lane < HEAD_DIM, o[:ATT_Q_TILE], o[ATT_Q_TILE:]).astype(BF16)

    @pl.when(g == 0)
    def _():
        write_scores(sa_ref, q_ref[0], bias_ref[0], g)

    g_next = jnp.minimum(g + 1, n_groups)

    @pl.when(g % 2 == 0)
    def _():
        write_scores(sb_ref, qn_ref[0], biasn_ref[0], g_next)
        softmax_pv(sa_ref)

    @pl.when(g % 2 == 1)
    def _():
        write_scores(sa_ref, qn_ref[0], biasn_ref[0], g_next)
        softmax_pv(sb_ref)


def _na_attention(za, kat, bias, *, n_latent, ctx_queries):
    b, t, _ = za.shape
    n_groups = n_latent // ATT_Q_TILE
    n_steps = n_groups + 1 if ctx_queries else n_groups
    assert t == n_latent + CTX_LEN and CTX_LEN == ATT_Q_TILE
    assert n_groups >= 3 and n_latent // GRID_W >= NA_K_ROWS

    def kind(g):
        return jnp.where(g == 0, 0, jnp.where(g < n_groups - 1, 1, jnp.where(g == n_groups - 1, 2, 3)))

    nxt = lambda g: jnp.minimum(g + 1, n_groups)
    q_spec = lambda at: pl.BlockSpec((1, ATT_Q_TILE, LANES), lambda i, hp, g: (i, at(g), hp))
    bias_spec = lambda at: pl.BlockSpec((1, 2, ATT_Q_TILE, NA_K_TOK),
                                        lambda i, hp, g: (kind(at(g)), hp, 0, 0))
    same = lambda g: g
    score_buf = pltpu.VMEM((2 * ATT_Q_TILE, NA_K_TOK + CTX_LEN), F32)
    return pl.pallas_call(
        functools.partial(_na_kernel, n_latent=n_latent),
        grid=(b, NA_HEADS // 2, n_steps),
        in_specs=[
            q_spec(same), q_spec(nxt),
            pl.BlockSpec((1, LANES, t), lambda i, hp, g: (i, hp, 0)),
            pl.BlockSpec((1, t, LANES), lambda i, hp, g: (i, 0, 2 + hp)),
            bias_spec(same), bias_spec(nxt),
        ],
        out_specs=pl.BlockSpec((1, ATT_Q_TILE, LANES), lambda i, hp, g: (i, g, hp)),
        out_shape=jax.ShapeDtypeStruct((b, n_steps * ATT_Q_TILE, D_A), BF16),
        scratch_shapes=[score_buf, score_buf],
        name="na_attention",
        compiler_params=_params(("arbitrary", "arbitrary", "arbitrary")),
    )(za, za, kat, za, bias, bias)


NA_BIAS_PAD = NA_WIN_ROWS // 2
NA_BIAS_W = 11 * LANES


def _na_bias_plan(n_latent):
    rows = n_latent // GRID_W
    n_groups = n_latent // ATT_Q_TILE
    plan = []
    for g in (0, 1, n_groups - 1):
        base = int(np.clip(g * NA_Q_ROWS - NA_WIN_ROWS // 2, 0, rows - NA_K_ROWS))
        kind = []
        for i in range(NA_Q_ROWS):
            r = g * NA_Q_ROWS + i
            start = int(np.clip(r - NA_WIN_ROWS // 2, 0, rows - NA_WIN_ROWS))
            e0 = base - r + NA_WIN_ROWS - 1 + NA_BIAS_PAD
            assert 0 <= e0 and (e0 - e0 % 2) * GRID_W + NA_K_TOK <= NA_BIAS_W
            kind.append((e0, start - base, start - base + NA_WIN_ROWS - 1))
        plan.append(tuple(kind))
    return tuple(plan)


def _na_bias_kernel(r_ref, o_ref, *, plan):
    shape = (GRID_W, NA_BIAS_W)
    qc = lax.broadcasted_iota(jnp.int32, shape, 0)
    col = lax.broadcasted_iota(jnp.int32, shape, 1)
    kc = col % GRID_W
    dc = kc - qc + (NA_WIN_COLS - 1)
    w0 = jnp.clip(qc - NA_WIN_COLS // 2, 0, GRID_W - NA_WIN_COLS)
    col_ok = (kc >= w0) & (kc < w0 + NA_WIN_COLS)
    tables = []
    for p in range(2):
        a = col // GRID_W + (p - NA_BIAS_PAD)
        acc = jnp.zeros(shape, F32)
        for b in range(2 * NA_WIN_COLS - 1):
            acc = acc + jnp.where(dc == b, r_ref[0, p, b:b + 1, :], 0.0)
        ok = col_ok & (a >= 0) & (a <= 2 * NA_WIN_ROWS - 2)
        tables.append(jnp.where(ok, acc * LOG2_E, NEG_INF))
    kj = lax.broadcasted_iota(jnp.int32, (GRID_W, NA_K_TOK), 1) // GRID_W
    for k, kind in enumerate(plan):
        for i, (e0, jlo, jhi) in enumerate(kind):
            p = e0 % 2
            off = (e0 - p) * GRID_W
            slab = tables[p][:, off:off + NA_K_TOK]
            o_ref[k, 0, i * GRID_W:(i + 1) * GRID_W, :] = jnp.where(
                (kj >= jlo) & (kj <= jhi), slab, NEG_INF)
    o_ref[len(plan), 0] = jnp.full((ATT_Q_TILE, NA_K_TOK), NEG_INF, F32)


def _na_bias_tables(rpb, n_latent):
    h = rpb.shape[0]
    n_blocks = NA_BIAS_W // GRID_W + 2
    rp = jnp.pad(rpb, ((0, 0), (NA_BIAS_PAD, n_blocks - NA_BIAS_PAD - rpb.shape[1]), (0, 0)))
    rep = jnp.transpose(jnp.repeat(rp, GRID_W, axis=1), (0, 2, 1))
    r = jnp.stack([rep[:, :, p * GRID_W:p * GRID_W + NA_BIAS_W] for p in range(2)], axis=1)
    n_dc = rpb.shape[2]
    return pl.pallas_call(
        functools.partial(_na_bias_kernel, plan=_na_bias_plan(n_latent)),
        grid=(h,),
        in_specs=[pl.BlockSpec((1, 2, n_dc, NA_BIAS_W), lambda i: (i, 0, 0, 0))],
        out_specs=pl.BlockSpec((4, 1, ATT_Q_TILE, NA_K_TOK), lambda i: (0, i, 0, 0)),
        out_shape=jax.ShapeDtypeStruct((4, h, ATT_Q_TILE, NA_K_TOK), F32),
        name="na_bias",
        compiler_params=_params(("arbitrary",)),
    )(r)


def _kv_chunk_bounds(t):
    head, tail = DA_KV_RAMP_UP, DA_KV_RAMP_DOWN
    middle = t - sum(head) - sum(tail)
    if middle < 0 or middle % MXU_DIM_V7X:
        assert t % DA_KV_SMALL_CHUNK == 0
        sizes = (DA_KV_SMALL_CHUNK,) * (t // DA_KV_SMALL_CHUNK)
    else:
        units = middle // MXU_DIM_V7X
        n_mid = -(-units // (DA_KV_CHUNK // MXU_DIM_V7X))
        n_mid += n_mid % 2
        mid = tuple((units // n_mid + (j < units % n_mid)) * MXU_DIM_V7X for j in range(n_mid))
        sizes = head + mid + tail
    assert max(sizes) <= DA_KV_CHUNK and len(sizes) % 2 == 0
    return tuple(int(v) for v in np.cumsum((0,) + sizes))


def _diff_kernel(*refs, n_keys, lambda_init, latent):
    if latent:
        qt_ref, qtn_ref, k_ref, vt_ref, lqk_ref, subln_ref, o_ref, s_ref, smax_ref = refs
    else:
        qt_ref, k_ref, vt_ref, lqk_ref, subln_ref, _, o_ref = refs
    tq = qt_ref.shape[2]
    feat = lax.broadcasted_iota(jnp.int32, (LANES, 1), 0)

    def stack_components(qt):
        return jnp.concatenate([jnp.where(feat < DA_QK_DIM, qt, jnp.zeros_like(qt)),
                                jnp.where(feat >= DA_QK_DIM, qt, jnp.zeros_like(qt))], axis=1)

    qst = stack_components(qt_ref[0])
    lqk = lqk_ref[...]
    lam = (jnp.exp(jnp.sum(lqk[0:1] * lqk[1:2], axis=-1, keepdims=True))
           - jnp.exp(jnp.sum(lqk[2:3] * lqk[3:4], axis=-1, keepdims=True)) + lambda_init)

    def col_reduce(x, op):
        r, c = x.shape
        groups = 8 if r % 64 == 0 else 1
        x = x.reshape(groups, r // (8 * groups), 8, c)
        return op(op(op(x, axis=1), axis=0), axis=0, keepdims=True)

    def scores(k, q=qst):
        s = jnp.dot(k, q, preferred_element_type=F32)
        return s, col_reduce(s, jnp.max)

    def step(s, s_max, vt1, state):
        m, acc = state
        m_new = jnp.maximum(m, s_max)
        alpha = jnp.exp2(m - m_new)
        p = jnp.exp2(s - m_new)
        acc = alpha * acc + jnp.dot(vt1, p.astype(BF16), preferred_element_type=F32)
        return (m_new, acc)

    def init():
        return (jnp.full((1, 2 * tq), NEG_INF, F32), jnp.zeros((DA_VT_ROWS, 2 * tq), F32))

    def finish(state):
        _, acc = state
        o = acc[:DA_V_DIM] / acc[DA_V_DIM:DA_V_DIM + 1]
        o = o[:, :tq] - lam * o[:, tq:]
        o = o * lax.rsqrt(jnp.mean(o * o, axis=0, keepdims=True) + EPS)
        o = o * (subln_ref[...] * (1.0 - lambda_init))
        o_ref[0] = o.T.astype(BF16)

    if not latent:
        s, s_max = scores(k_ref[0])
        finish(step(s, s_max, vt_ref[0], init()))
        return

    bounds = _kv_chunk_bounds(n_keys)
    chunk = lambda c: slice(bounds[c], bounds[c + 1])
    size = lambda c: bounds[c + 1] - bounds[c]
    n_chunks = len(bounds) - 1
    assert n_chunks % 2 == 0

    @pl.when(pl.program_id(2) == 0)
    def _():
        s_ref[0, 0:size(0)], smax_ref[...] = scores(k_ref[0, chunk(0), :])

    s_max = smax_ref[...]
    state = init()
    for c in range(n_chunks):
        if c + 1 < n_chunks:
            s_ref[(c + 1) % 2, 0:size(c + 1)], next_max = scores(k_ref[0, chunk(c + 1), :])
        else:
            s_ref[0, 0:size(0)], smax_ref[...] = scores(k_ref[0, chunk(0), :],
                                                        stack_components(qtn_ref[0]))
        state = step(s_ref[c % 2, 0:size(c)], s_max, vt_ref[0, :, chunk(c)], state)
        s_max = next_max
    finish(state)


def _diff_attention(kd, qt, vt, lqk, subln, *, n_latent, lambda_init, ctx_queries):
    b, t, _ = kd.shape
    tq = DA_Q_TILE
    n_q_tiles = n_latent // tq
    consts = [pl.BlockSpec((4, DA_QK_DIM), lambda *_: (0, 0)),
              pl.BlockSpec((DA_V_DIM, 1), lambda *_: (0, 0))]
    out_rows = t if ctx_queries else n_latent
    oc = pl.pallas_call(
        functools.partial(_diff_kernel, n_keys=t, lambda_init=lambda_init, latent=True),
        grid=(b, DA_HEADS, n_q_tiles),
        in_specs=[
            pl.BlockSpec((1, LANES, tq), lambda bi, h, i: (bi, h, i)),
            pl.BlockSpec((1, LANES, tq), lambda bi, h, i: (bi, h, jnp.minimum(i + 1, n_q_tiles - 1))),
            pl.BlockSpec((1, t, LANES), lambda bi, h, i: (bi, 0, h)),
            pl.BlockSpec((1, DA_VT_ROWS, t), lambda bi, h, i: (bi, h, 0)),
        ] + consts,
        out_specs=pl.BlockSpec((1, tq, LANES), lambda bi, h, i: (bi, i, h)),
        out_shape=jax.ShapeDtypeStruct((b, out_rows, D_C), BF16),
        scratch_shapes=[pltpu.VMEM((2, DA_KV_CHUNK, 2 * tq), F32), pltpu.VMEM((1, 2 * tq), F32)],
        name="diff_attention",
        compiler_params=_params(("arbitrary", "arbitrary", "arbitrary")),
    )(qt, qt, kd, vt, lqk, subln)
    if not ctx_queries:
        return oc
    ctx_blk = n_latent // CTX_LEN
    return pl.pallas_call(
        functools.partial(_diff_kernel, n_keys=CTX_LEN, lambda_init=lambda_init, latent=False),
        grid=(b, DA_HEADS),
        in_specs=[
            pl.BlockSpec((1, LANES, CTX_LEN), lambda bi, h: (bi, h, ctx_blk)),
            pl.BlockSpec((1, CTX_LEN, LANES), lambda bi, h: (bi, ctx_blk, h)),
            pl.BlockSpec((1, DA_VT_ROWS, CTX_LEN), lambda bi, h: (bi, h, ctx_blk)),
        ] + consts + [pl.BlockSpec(memory_space=pl.ANY)],
        out_specs=pl.BlockSpec((1, CTX_LEN, LANES), lambda bi, h: (bi, ctx_blk, h)),
        out_shape=jax.ShapeDtypeStruct((b, out_rows, D_C), BF16),
        input_output_aliases={5: 0},
        name="diff_attention_ctx",
        compiler_params=_params(("arbitrary", "arbitrary")),
    )(qt, kd, vt, lqk, subln, oc)


def _rope_tables(n_latent):
    tok = jnp.arange(n_latent)
    row = (tok // GRID_W).astype(F32)
    col = (tok % GRID_W).astype(F32)
    n_freq = DA_QK_DIM // 4
    freqs = ROPE_BASE ** (-jnp.arange(n_freq, dtype=F32) / n_freq)
    ang = jnp.concatenate([row[:, None] * freqs, col[:, None] * freqs], axis=-1)
    cos, sin = jnp.cos(ang), jnp.sin(ang)
    cos_t = jnp.concatenate([cos, cos, cos, cos], axis=-1)
    sin_t = jnp.concatenate([-sin, sin, -sin, sin], axis=-1)
    cos_t = jnp.concatenate([cos_t, jnp.ones((CTX_LEN, LANES), F32)], axis=0)
    sin_t = jnp.concatenate([sin_t, jnp.zeros((CTX_LEN, LANES), F32)], axis=0)
    return cos_t, sin_t


def kernel(x, c, ctx, c_ctx, w_ada, b_ada, ffn1_w1, ffn1_w3, ffn1_w2, w_in, w_out, na_rpb,
           gm_ws, gm_bs, gm_norm, da_lq1, da_lk1, da_lq2, da_lk2, da_subln,
           ffn2_w1, ffn2_w3, ffn2_w2, final_norm):
    b, n_latent, d = x.shape
    assert d == D_MODEL and ctx.shape[1] == CTX_LEN and b < 8
    assert n_latent % ATT_Q_TILE == 0

    cvec = jnp.zeros((8, d), F32).at[:b].set(c).at[b].set(c_ctx)
    mod = _ada(cvec, w_ada, b_ada).reshape(DEPTH, 8, N_ADA, d)
    cos_t, sin_t = _rope_tables(n_latent)
    gmat = jnp.asarray(np.kron(np.eye(GM_GROUPS), np.ones((GM_WIDTH, GM_WIDTH))), BF16)

    bf16 = lambda w: w.astype(BF16)
    ffn1 = (bf16(ffn1_w1), bf16(ffn1_w3), bf16(ffn1_w2))
    ffn2 = (bf16(ffn2_w1), bf16(ffn2_w3), bf16(ffn2_w2))
    w_in, w_out = bf16(w_in), bf16(w_out)

    xs = x
    for l in range(DEPTH):
        last = l == DEPTH - 1
        lambda_init = 0.8 - 0.6 * math.exp(-0.3 * l)
        modt = jnp.stack([mod[l, :b], jnp.broadcast_to(mod[l, b], (b, N_ADA, d))], axis=1)
        bias = _na_bias_tables(na_rpb[l], n_latent)
        ws_stack = gm_ws[l].reshape(GM_GROUPS * GM_CHUNK, GM_CHUNK).astype(BF16)
        bs_full = jnp.repeat(gm_bs[l].T, GM_WIDTH, axis=1)
        gn = gm_norm[l].reshape(1, D_B)
        lqk = jnp.stack([da_lq1[l], da_lk1[l], da_lq2[l], da_lk2[l]])

        xs = _ffn(xs, modt, *ffn1, layer=l, n_latent=n_latent, mi=0, ctx=ctx if l == 0 else None)
        za, kat, ob, kd, qt, vt = _proj_in(xs, modt, w_in, cos_t, sin_t, ws_stack, bs_full, gn,
                                           gmat, layer=l, n_latent=n_latent)
        oa = _na_attention(za, kat, bias, n_latent=n_latent, ctx_queries=not last)
        oc = _diff_attention(kd, qt, vt, lqk, da_subln[l].reshape(DA_V_DIM, 1),
                             n_latent=n_latent, lambda_init=lambda_init, ctx_queries=not last)
        xs = _ffn(xs, modt, *ffn2, layer=l, n_latent=n_latent, mi=6, mix=(oa, ob, oc, w_out),
                  final_gain=final_norm if last else None)
    return xs
```

```python
import functools
import math

import numpy as np
import jax
import jax.numpy as jnp
from jax import lax
from jax.experimental import pallas as pl
from jax.experimental.pallas import tpu as pltpu

F32 = jnp.float32
BF16 = jnp.bfloat16

D_MODEL = 1024
DEPTH = 2
GRID_W = 64
CTX_LEN = 256
N_ADA = 9
EPS = 1e-6
NEG_INF = -1e30
ROPE_BASE = 10000.0
D_FF = 2816
HEAD_DIM = 64
NA_HEADS = 4
NA_WIN_ROWS = 8
NA_WIN_COLS = 16
D_A = NA_HEADS * HEAD_DIM
GM_GROUPS = 4
GM_WIDTH = 64
GM_CHUNK = 128
D_B = GM_GROUPS * GM_WIDTH
DA_HEADS = 4
DA_QK_DIM = 64
DA_V_DIM = 128
DA_VT_ROWS = DA_V_DIM + 16
D_C = DA_HEADS * DA_V_DIM
D_QK_C = DA_HEADS * 2 * DA_QK_DIM
D_IN = 3 * D_A + 2 * D_B + 2 * D_QK_C + D_C
QK_SCALE = HEAD_DIM ** -0.5
LOG2_E = math.log2(math.e)

LANES = 128
MXU_DIM_V7X = 256
VMEM_BYTES_V7X = 64 * 1024 * 1024
VMEM_LIMIT = VMEM_BYTES_V7X - 8 * 1024 * 1024

TOKEN_TILE = 256
FFN_SUB_TILES = (3, 2, 1)
NA_SUB_GROUPS = (3, 2)
ATT_Q_TILE = 256
NA_Q_ROWS = ATT_Q_TILE // GRID_W
NA_K_ROWS = NA_Q_ROWS + NA_WIN_ROWS
NA_K_TOK = NA_K_ROWS * GRID_W
DA_Q_TILE = 512
DA_KV_CHUNK = 1536
DA_KV_RAMP_UP = (256, 768)
DA_KV_RAMP_DOWN = (768, 512, 256, 256)
DA_KV_SMALL_CHUNK = 384
FF_CHUNKS =((0, 768), (768, 1536), (1536, 2304), (2304, 2816))


def _params(sem):
    return pltpu.CompilerParams(dimension_semantics=sem, vmem_limit_bytes=VMEM_LIMIT)


def _resident(shape):
    nd = len(shape)
    return pl.BlockSpec(shape, lambda *_: (0,) * nd, pipeline_mode=pl.Buffered(1))


def _layer_resident(shape, layer):
    nd = len(shape)
    return pl.BlockSpec((1,) + shape, lambda *_: (layer,) + (0,) * nd,
                        pipeline_mode=pl.Buffered(1))


def _rms(x):
    return x * lax.rsqrt(jnp.mean(x * x, axis=-1, keepdims=True) + EPS)


def _mod_row(mod_ref, idx):
    return mod_ref[0, 0, idx:idx + 1, :]


def _mod_spec(tm, n_latent, d):
    assert n_latent % tm == 0
    return pl.BlockSpec((1, 1, N_ADA, d), lambda i, j: (i, (j * tm >= n_latent).astype(jnp.int32), 0, 0))


def _ada_kernel(c_ref, w_ref, b_ref, o_ref):
    c = c_ref[...]
    cs = c / (1.0 + jnp.exp(-c))
    o_ref[0] = jnp.dot(cs, w_ref[0], preferred_element_type=F32,
                       precision=lax.Precision.HIGHEST) + b_ref[0]


def _ada(cvec, w_ada, b_ada):
    depth, d, n = w_ada.shape
    tn = 1024
    return pl.pallas_call(
        _ada_kernel,
        grid=(depth, n // tn),
        in_specs=[
            pl.BlockSpec((8, d), lambda l, j: (0, 0)),
            pl.BlockSpec((1, d, tn), lambda l, j: (l, 0, j)),
            pl.BlockSpec((1, 1, tn), lambda l, j: (l, 0, j)),
        ],
        out_specs=pl.BlockSpec((1, 8, tn), lambda l, j: (l, 0, j)),
        out_shape=jax.ShapeDtypeStruct((depth, 8, n), F32),
        name="ada",
        compiler_params=_params(("arbitrary", "arbitrary")),
    )(cvec, w_ada, b_ada.reshape(depth, 1, n))


def _ffn_kernel(*refs, tm, n_sub, n_latent, mi, split_ctx, mix, final):
    refs = list(refs)
    if split_ctx:
        x_refs = refs[:n_sub]
        del refs[:n_sub]
        ctx_ref = refs.pop(0)
    else:
        x_ref = refs.pop(0)
    mod_ref = refs.pop(0)
    if mix:
        oa_ref, ob_ref, oc_ref, wout_ref = refs[:4]
        del refs[:4]
    w1_ref, w3_ref, w2_ref = refs[:3]
    del refs[:3]
    if final:
        gain_ref = refs.pop(0)
    (o_ref,) = refs
    for sub in range(n_sub):
        rows = slice(sub * tm, (sub + 1) * tm)
        is_ctx = (pl.program_id(1) * n_sub + sub) * tm >= n_latent

        def mod(idx):
            if final:
                return mod_ref[0, 0, idx:idx + 1, :]
            return jnp.where(is_ctx, mod_ref[0, 1, idx:idx + 1, :], mod_ref[0, 0, idx:idx + 1, :])

        if split_ctx:
            x = jnp.where(is_ctx, ctx_ref[0], x_refs[sub][0])
        else:
            x = x_ref[0, rows, :]
        if mix:
            cat = jnp.concatenate([oa_ref[0, rows, :], ob_ref[0, rows, :], oc_ref[0, rows, :]],
                                  axis=-1)
            x = x + mod(mi - 1) * jnp.dot(cat, wout_ref[0], preferred_element_type=F32)
        shift = mod(mi)
        scale1 = 1.0 + mod(mi + 1)
        half_gate = 0.5 * mod(mi + 2)
        h = (_rms(x) * scale1 + shift).astype(BF16)
        acc = jnp.zeros((tm, D_MODEL), F32)
        for c0, c1 in FF_CHUNKS:
            a = jnp.dot(h, w1_ref[0, :, c0:c1], preferred_element_type=F32)
            g = jnp.dot(h, w3_ref[0, :, c0:c1], preferred_element_type=F32)
            y = (a / (1.0 + jnp.exp(-a)) * g).astype(BF16)
            acc = acc + jnp.dot(y, w2_ref[0, c0:c1, :], preferred_element_type=F32)
        out = x + half_gate * acc
        if final:
            out = _rms(out) * gain_ref[...]
        o_ref[0, rows, :] = out


def _ffn(x, modt, w1, w3, w2, *, layer, n_latent, mi, ctx=None, mix=None, final_gain=None):
    b, _, d = x.shape
    t = n_latent + CTX_LEN
    tm = TOKEN_TILE
    final = final_gain is not None
    t_out = n_latent if final else t
    n_sub = next(n for n in FFN_SUB_TILES if (t_out // tm) % n == 0)
    step_rows = n_sub * tm
    tile = lambda width: pl.BlockSpec((1, step_rows, width), lambda i, j: (i, j, 0))
    if ctx is None:
        in_specs, args = [tile(d)], [x]
    else:
        assert tm == CTX_LEN and not final
        last_latent = n_latent // tm - 1
        sub_spec = lambda sub: pl.BlockSpec(
            (1, tm, d), lambda i, j: (i, jnp.minimum(j * n_sub + sub, last_latent), 0))
        in_specs = [sub_spec(sub) for sub in range(n_sub)]
        in_specs.append(pl.BlockSpec((1, CTX_LEN, d), lambda i, j: (i, 0, 0)))
        args = [x] * n_sub + [ctx]
    in_specs.append(pl.BlockSpec((1, 2, N_ADA, d), lambda i, j: (i, 0, 0, 0)))
    args.append(modt)
    if mix is not None:
        in_specs += [tile(D_A), tile(D_B), tile(D_C), _layer_resident((d, d), layer)]
        args += list(mix)
    in_specs += [_layer_resident((d, D_FF), layer), _layer_resident((d, D_FF), layer),
                 _layer_resident((D_FF, d), layer)]
    args += [w1, w3, w2]
    if final:
        in_specs.append(_resident((1, d)))
        args.append(final_gain.reshape(1, d))
    return pl.pallas_call(
        functools.partial(_ffn_kernel, tm=tm, n_sub=n_sub, n_latent=n_latent, mi=mi,
                          split_ctx=ctx is not None, mix=mix is not None, final=final),
        grid=(b, t_out // step_rows),
        in_specs=in_specs,
        out_specs=tile(d),
        out_shape=jax.ShapeDtypeStruct((b, t_out, d), F32),
        name="ffn_final" if final else ("ffn_mix" if mix is not None else "ffn"),
        compiler_params=_params(("arbitrary", "arbitrary")),
    )(*args)


def _swap_rope_halves(x, first_half):
    return jnp.where(first_half, pltpu.roll(x, 96, 1), pltpu.roll(x, 32, 1))


def _proj_in_kernel(*refs, tm, n_sub, n_latent):
    for sub in range(n_sub):
        _proj_in_tile(*refs, tm=tm, rows=slice(sub * tm, (sub + 1) * tm),
                      is_ctx=(pl.program_id(1) * n_sub + sub) * tm >= n_latent)


def _proj_in_tile(x_ref, mod_ref, w_ref, cos_ref, sin_ref, ws_ref, bs_ref, gn_ref, gmat_ref,
                  za_ref, kat_ref, ob_ref, kd_ref, qt_ref, vt_ref, *, tm, rows, is_ctx):
    mod = lambda idx: jnp.where(is_ctx, mod_ref[0, 1, idx:idx + 1, :], mod_ref[0, 0, idx:idx + 1, :])
    x = x_ref[0, rows, :]
    h = (_rms(x) * (1.0 + mod(4)) + mod(3)).astype(BF16)

    z_all = jnp.dot(h, w_ref[0], preferred_element_type=F32)
    project = lambda c0, width: z_all[:, c0:c0 + width]

    z = project(3 * D_A, 2 * D_B)
    u = z[:, 0:D_B]
    v = z[:, D_B:]
    u = 0.5 * u * (1.0 + lax.erf(u * (2.0 ** -0.5)))
    v = 0.5 * v * (1.0 + lax.erf(v * (2.0 ** -0.5)))
    v2 = v * v
    v2_hi = v2.astype(BF16)
    v2_lo = (v2 - v2_hi.astype(F32)).astype(BF16)
    ms = (jnp.dot(v2_hi, gmat_ref[...], preferred_element_type=F32)
          + jnp.dot(v2_lo, gmat_ref[...], preferred_element_type=F32)) * (1.0 / GM_WIDTH)
    vn = (v * lax.rsqrt(ms + EPS) * gn_ref[...]).astype(BF16)
    lane_group = lax.broadcasted_iota(jnp.int32, (1, D_B), 1) // GM_WIDTH
    for c in range(tm // GM_CHUNK):
        crows = slice(c * GM_CHUNK, (c + 1) * GM_CHUNK)
        r = jnp.dot(ws_ref[...], vn[crows, :], preferred_element_type=F32)
        s = bs_ref[...]
        for g in range(GM_GROUPS):
            s = s + jnp.where(lane_group == g, r[g * GM_CHUNK:(g + 1) * GM_CHUNK, :], 0.0)
        ob_ref[0, pl.ds(rows.start + c * GM_CHUNK, GM_CHUNK), :] = (u[crows, :] * s).astype(BF16)

    z = project(0, 3 * D_A)
    za_ref[0, rows, 0:D_A] = (z[:, 0:D_A] * (QK_SCALE * LOG2_E)).astype(BF16)
    kat_ref[0, :, rows] = z[:, D_A:2 * D_A].T.astype(BF16)
    za_ref[0, rows, D_A:2 * D_A] = z[:, 2 * D_A:3 * D_A].astype(BF16)

    o = 3 * D_A + 2 * D_B
    cos = cos_ref[rows, :]
    sin = sin_ref[rows, :]
    first_half = (lax.broadcasted_iota(jnp.int32, (1, LANES), 1) % DA_QK_DIM) < (DA_QK_DIM // 2)
    n_q_blocks = D_QK_C // LANES
    for j in range(2 * n_q_blocks):
        if j % n_q_blocks == 0:
            z = project(o + j * LANES, D_QK_C)
        zz = z[:, (j % n_q_blocks) * LANES:(j % n_q_blocks + 1) * LANES]
        r = zz * cos + _swap_rope_halves(zz, first_half) * sin
        if j < n_q_blocks:
            qt_ref[0, j * LANES:(j + 1) * LANES, rows] = (r * (QK_SCALE * LOG2_E)).T.astype(BF16)
        else:
            jk = j - n_q_blocks
            kd_ref[0, rows, jk * LANES:(jk + 1) * LANES] = r.astype(BF16)
    z = project(o + 2 * D_QK_C, D_C)
    for j in range(D_C // LANES):
        r0 = j * DA_VT_ROWS
        vt_ref[0, r0:r0 + DA_V_DIM, rows] = z[:, j * LANES:(j + 1) * LANES].T.astype(BF16)
        vt_ref[0, r0 + DA_V_DIM:r0 + DA_VT_ROWS, rows] = jnp.ones((DA_VT_ROWS - DA_V_DIM, tm), BF16)


def _proj_in(x, modt, w_in, cos_t, sin_t, ws_stack, bs_full, gn, gmat, *, layer, n_latent):
    b, t, d = x.shape
    n_sub = next(n for n in FFN_SUB_TILES if (t // TOKEN_TILE) % n == 0)
    tm = n_sub * TOKEN_TILE
    return pl.pallas_call(
        functools.partial(_proj_in_kernel, tm=TOKEN_TILE, n_sub=n_sub, n_latent=n_latent),
        grid=(b, t // tm),
        in_specs=[
            pl.BlockSpec((1, tm, d), lambda i, j: (i, j, 0)),
            pl.BlockSpec((1, 2, N_ADA, d), lambda i, j: (i, 0, 0, 0)),
            _layer_resident((d, D_IN), layer),
            pl.BlockSpec((tm, LANES), lambda i, j: (j, 0)),
            pl.BlockSpec((tm, LANES), lambda i, j: (j, 0)),
            _resident((GM_GROUPS * GM_CHUNK, GM_CHUNK)),
            _resident((GM_CHUNK, D_B)),
            _resident((1, D_B)),
            _resident((D_B, D_B)),
        ],
        out_specs=[
            pl.BlockSpec((1, tm, 2 * D_A), lambda i, j: (i, j, 0)),
            pl.BlockSpec((1, D_A, tm), lambda i, j: (i, 0, j)),
            pl.BlockSpec((1, tm, D_B), lambda i, j: (i, j, 0)),
            pl.BlockSpec((1, tm, D_QK_C), lambda i, j: (i, j, 0)),
            pl.BlockSpec((1, D_QK_C, tm), lambda i, j: (i, 0, j)),
            pl.BlockSpec((1, DA_HEADS * DA_VT_ROWS, tm), lambda i, j: (i, 0, j)),
        ],
        out_shape=[
            jax.ShapeDtypeStruct((b, t, 2 * D_A), BF16),
            jax.ShapeDtypeStruct((b, D_A, t), BF16),
            jax.ShapeDtypeStruct((b, t, D_B), BF16),
            jax.ShapeDtypeStruct((b, t, D_QK_C), BF16),
            jax.ShapeDtypeStruct((b, D_QK_C, t), BF16),
            jax.ShapeDtypeStruct((b, DA_HEADS * DA_VT_ROWS, t), BF16),
        ],
        name="proj_in",
        compiler_params=_params(("arbitrary", "arbitrary")),
    )(x, modt, w_in, cos_t, sin_t, ws_stack, bs_full, gn, gmat)


def _na_kernel(*refs, n_sub, n_latent):
    q_ref, qn_ref, kt_ref, v_ref = refs[:4]
    bias_refs = refs[4:5 + n_sub]
    o_ref = refs[5 + n_sub]
    s_refs = refs[6 + n_sub:]
    assert len(s_refs) == n_sub >= 2
    first = pl.program_id(2) * n_sub
    n_groups = n_latent // ATT_Q_TILE
    rows = n_latent // GRID_W
    lane = lax.broadcasted_iota(jnp.int32, (1, LANES), 1)

    def window_start(grp):
        base_row = jnp.clip(grp * NA_Q_ROWS - NA_WIN_ROWS // 2, 0, rows - NA_K_ROWS)
        return pl.multiple_of(base_row * GRID_W, ATT_Q_TILE)

    def write_scores(s_ref, q, bias, grp):
        qs = jnp.concatenate([jnp.where(lane < HEAD_DIM, q, jnp.zeros_like(q)),
                              jnp.where(lane >= HEAD_DIM, q, jnp.zeros_like(q))], axis=0)
        sw = jnp.dot(qs, kt_ref[0, :, pl.ds(window_start(grp), NA_K_TOK)],
                     preferred_element_type=F32)
        s_ref[:, 0:NA_K_TOK] = sw + bias.reshape(2 * ATT_Q_TILE, NA_K_TOK)
        s_ref[:, NA_K_TOK:] = jnp.dot(qs, kt_ref[0, :, n_latent:], preferred_element_type=F32)

    def softmax_pv(s_ref, grp, out_rows):
        s = s_ref[...]
        p = jnp.exp2(s - s.max(axis=-1, keepdims=True))
        l = p.sum(axis=-1, keepdims=True)
        p = p.astype(BF16)
        o = (jnp.dot(p[:, 0:NA_K_TOK], v_ref[0, pl.ds(window_start(grp), NA_K_TOK), :],
                     preferred_element_type=F32)
             + jnp.dot(p[:, NA_K_TOK:], v_ref[0, n_latent:, :], preferred_element_type=F32)) / l
        o_ref[0, out_rows, :] = jnp.where(lane < HEAD_DIM, o[:ATT_Q_TILE],
                                          o[ATT_Q_TILE:]).astype(BF16)

    sub_rows = lambda sub: slice(sub * ATT_Q_TILE, (sub + 1) * ATT_Q_TILE)

    @pl.when(first == 0)
    def _():
        write_scores(s_refs[0], q_ref[0, sub_rows(0), :], bias_refs[0][0], first)

    for sub in range(n_sub):
        q_next = q_ref[0, sub_rows(sub + 1), :] if sub + 1 < n_sub else qn_ref[0]
        write_scores(s_refs[(sub + 1) % n_sub], q_next, bias_refs[sub + 1][0],
                     jnp.minimum(first + sub + 1, n_groups))
        softmax_pv(s_refs[sub], first + sub, sub_rows(sub))


def _na_attention(za, kat, bias, *, n_latent, ctx_queries):
    b, t, _ = za.shape
    n_groups = n_latent // ATT_Q_TILE
    n_total = n_groups + 1 if ctx_queries else n_groups
    n_sub = next(n for n in NA_SUB_GROUPS if n_total % n == 0)
    assert t == n_latent + CTX_LEN and CTX_LEN == ATT_Q_TILE
    assert n_groups >= 3 and n_latent // GRID_W >= NA_K_ROWS

    def kind(g):
        return jnp.where(g == 0, 0, jnp.where(g < n_groups - 1, 1, jnp.where(g == n_groups - 1, 2, 3)))

    group_at = lambda j, off: jnp.minimum(j * n_sub + off, n_groups)
    bias_spec = lambda off: pl.BlockSpec((1, 2, ATT_Q_TILE, NA_K_TOK),
                                         lambda i, hp, j: (kind(group_at(j, off)), hp, 0, 0))
    score_buf = pltpu.VMEM((2 * ATT_Q_TILE, NA_K_TOK + CTX_LEN), F32)
    return pl.pallas_call(
        functools.partial(_na_kernel, n_sub=n_sub, n_latent=n_latent),
        grid=(b, NA_HEADS // 2, n_total // n_sub),
        in_specs=[
            pl.BlockSpec((1, n_sub * ATT_Q_TILE, LANES), lambda i, hp, j: (i, j, hp)),
            pl.BlockSpec((1, ATT_Q_TILE, LANES), lambda i, hp, j: (i, group_at(j, n_sub), hp)),
            pl.BlockSpec((1, LANES, t), lambda i, hp, j: (i, hp, 0)),
            pl.BlockSpec((1, t, LANES), lambda i, hp, j: (i, 0, 2 + hp)),
        ] + [bias_spec(off) for off in range(n_sub + 1)],
        out_specs=pl.BlockSpec((1, n_sub * ATT_Q_TILE, LANES), lambda i, hp, j: (i, j, hp)),
        out_shape=jax.ShapeDtypeStruct((b, n_total * ATT_Q_TILE, D_A), BF16),
        scratch_shapes=[score_buf] * n_sub,
        name="na_attention",
        compiler_params=_params(("arbitrary", "arbitrary", "arbitrary")),
    )(za, za, kat, za, *([bias] * (n_sub + 1)))


NA_BIAS_PAD = NA_WIN_ROWS // 2
NA_BIAS_W = 11 * LANES


def _na_bias_plan(n_latent):
    rows = n_latent // GRID_W
    n_groups = n_latent // ATT_Q_TILE
    plan = []
    for g in (0, 1, n_groups - 1):
        base = int(np.clip(g * NA_Q_ROWS - NA_WIN_ROWS // 2, 0, rows - NA_K_ROWS))
        kind = []
        for i in range(NA_Q_ROWS):
            r = g * NA_Q_ROWS + i
            start = int(np.clip(r - NA_WIN_ROWS // 2, 0, rows - NA_WIN_ROWS))
            e0 = base - r + NA_WIN_ROWS - 1 + NA_BIAS_PAD
            assert 0 <= e0 and (e0 - e0 % 2) * GRID_W + NA_K_TOK <= NA_BIAS_W
            kind.append((e0, start - base, start - base + NA_WIN_ROWS - 1))
        plan.append(tuple(kind))
    return tuple(plan)


def _na_bias_kernel(r_ref, o_ref, *, plan):
    shape = (GRID_W, NA_BIAS_W)
    qc = lax.broadcasted_iota(jnp.int32, shape, 0)
    col = lax.broadcasted_iota(jnp.int32, shape, 1)
    kc = col % GRID_W
    dc = kc - qc + (NA_WIN_COLS - 1)
    w0 = jnp.clip(qc - NA_WIN_COLS // 2, 0, GRID_W - NA_WIN_COLS)
    col_ok = (kc >= w0) & (kc < w0 + NA_WIN_COLS)
    tables = []
    for p in range(2):
        a = col // GRID_W + (p - NA_BIAS_PAD)
        acc = jnp.zeros(shape, F32)
        for b in range(2 * NA_WIN_COLS - 1):
            acc = acc + jnp.where(dc == b, r_ref[0, p, b:b + 1, :], 0.0)
        ok = col_ok & (a >= 0) & (a <= 2 * NA_WIN_ROWS - 2)
        tables.append(jnp.where(ok, acc * LOG2_E, NEG_INF))
    kj = lax.broadcasted_iota(jnp.int32, (GRID_W, NA_K_TOK), 1) // GRID_W
    for k, kind in enumerate(plan):
        for i, (e0, jlo, jhi) in enumerate(kind):
            p = e0 % 2
            off = (e0 - p) * GRID_W
            slab = tables[p][:, off:off + NA_K_TOK]
            o_ref[k, 0, i * GRID_W:(i + 1) * GRID_W, :] = jnp.where(
                (kj >= jlo) & (kj <= jhi), slab, NEG_INF)
    o_ref[len(plan), 0] = jnp.full((ATT_Q_TILE, NA_K_TOK), NEG_INF, F32)


def _na_bias_tables(rpb, n_latent):
    h = rpb.shape[0]
    n_blocks = NA_BIAS_W // GRID_W + 2
    rp = jnp.pad(rpb, ((0, 0), (NA_BIAS_PAD, n_blocks - NA_BIAS_PAD - rpb.shape[1]), (0, 0)))
    rep = jnp.transpose(jnp.repeat(rp, GRID_W, axis=1), (0, 2, 1))
    r = jnp.stack([rep[:, :, p * GRID_W:p * GRID_W + NA_BIAS_W] for p in range(2)], axis=1)
    n_dc = rpb.shape[2]
    return pl.pallas_call(
        functools.partial(_na_bias_kernel, plan=_na_bias_plan(n_latent)),
        grid=(h,),
        in_specs=[pl.BlockSpec((1, 2, n_dc, NA_BIAS_W), lambda i: (i, 0, 0, 0))],
        out_specs=pl.BlockSpec((4, 1, ATT_Q_TILE, NA_K_TOK), lambda i: (0, i, 0, 0)),
        out_shape=jax.ShapeDtypeStruct((4, h, ATT_Q_TILE, NA_K_TOK), F32),
        name="na_bias",
        compiler_params=_params(("arbitrary",)),
    )(r)


def _kv_chunk_bounds(t):
    head, tail = DA_KV_RAMP_UP, DA_KV_RAMP_DOWN
    middle = t - sum(head) - sum(tail)
    if middle < 0 or middle % MXU_DIM_V7X:
        assert t % DA_KV_SMALL_CHUNK == 0
        sizes = (DA_KV_SMALL_CHUNK,) * (t // DA_KV_SMALL_CHUNK)
    else:
        units = middle // MXU_DIM_V7X
        n_mid = -(-units // (DA_KV_CHUNK // MXU_DIM_V7X))
        n_mid += n_mid % 2
        mid = tuple((units // n_mid + (j < units % n_mid)) * MXU_DIM_V7X for j in range(n_mid))
        sizes = head + mid + tail
    assert max(sizes) <= DA_KV_CHUNK and len(sizes) % 2 == 0
    return tuple(int(v) for v in np.cumsum((0,) + sizes))


def _diff_kernel(*refs, n_keys, lambda_init, latent):
    if latent:
        qt_ref, qtn_ref, k_ref, vt_ref, lqk_ref, subln_ref, o_ref, s_ref, smax_ref = refs
    else:
        qt_ref, k_ref, vt_ref, lqk_ref, subln_ref, _, o_ref = refs
    tq = qt_ref.shape[2]
    feat = lax.broadcasted_iota(jnp.int32, (LANES, 1), 0)

    def stack_components(qt):
        return jnp.concatenate([jnp.where(feat < DA_QK_DIM, qt, jnp.zeros_like(qt)),
                                jnp.where(feat >= DA_QK_DIM, qt, jnp.zeros_like(qt))], axis=1)

    qst = stack_components(qt_ref[0])
    lqk = lqk_ref[...]
    lam = (jnp.exp(jnp.sum(lqk[0:1] * lqk[1:2], axis=-1, keepdims=True))
           - jnp.exp(jnp.sum(lqk[2:3] * lqk[3:4], axis=-1, keepdims=True)) + lambda_init)

    def col_reduce(x, op):
        r, c = x.shape
        groups = 8 if r % 64 == 0 else 1
        x = x.reshape(groups, r // (8 * groups), 8, c)
        return op(op(op(x, axis=1), axis=0), axis=0, keepdims=True)

    def scores(k, q=qst):
        s = jnp.dot(k, q, preferred_element_type=F32)
        return s, col_reduce(s, jnp.max)

    def step(s, s_max, vt1, state):
        m, acc = state
        m_new = jnp.maximum(m, s_max)
        alpha = jnp.exp2(m - m_new)
        p = jnp.exp2(s - m_new)
        acc = alpha * acc + jnp.dot(vt1, p.astype(BF16), preferred_element_type=F32)
        return (m_new, acc)

    def init():
        return (jnp.full((1, 2 * tq), NEG_INF, F32), jnp.zeros((DA_VT_ROWS, 2 * tq), F32))

    def finish(state):
        _, acc = state
        o = acc[:DA_V_DIM] / acc[DA_V_DIM:DA_V_DIM + 1]
        o = o[:, :tq] - lam * o[:, tq:]
        o = o * lax.rsqrt(jnp.mean(o * o, axis=0, keepdims=True) + EPS)
        o = o * (subln_ref[...] * (1.0 - lambda_init))
        o_ref[0] = o.T.astype(BF16)

    if not latent:
        s, s_max = scores(k_ref[0])
        finish(step(s, s_max, vt_ref[0], init()))
        return

    bounds = _kv_chunk_bounds(n_keys)
    chunk = lambda c: slice(bounds[c], bounds[c + 1])
    size = lambda c: bounds[c + 1] - bounds[c]
    n_chunks = len(bounds) - 1
    assert n_chunks % 2 == 0

    @pl.when(pl.program_id(2) == 0)
    def _():
        s_ref[0, 0:size(0)], smax_ref[...] = scores(k_ref[0, chunk(0), :])

    s_max = smax_ref[...]
    state = init()
    for c in range(n_chunks):
        if c + 1 < n_chunks:
            s_ref[(c + 1) % 2, 0:size(c + 1)], next_max = scores(k_ref[0, chunk(c + 1), :])
        else:
            s_ref[0, 0:size(0)], smax_ref[...] = scores(k_ref[0, chunk(0), :],
                                                        stack_components(qtn_ref[0]))
        state = step(s_ref[c % 2, 0:size(c)], s_max, vt_ref[0, :, chunk(c)], state)
        s_max = next_max
    finish(state)


def _diff_attention(kd, qt, vt, lqk, subln, *, n_latent, lambda_init, ctx_queries):
    b, t, _ = kd.shape
    tq = DA_Q_TILE
    n_q_tiles = n_latent // tq
    consts = [pl.BlockSpec((4, DA_QK_DIM), lambda *_: (0, 0)),
              pl.BlockSpec((DA_V_DIM, 1), lambda *_: (0, 0))]
    out_rows = t if ctx_queries else n_latent
    oc = pl.pallas_call(
        functools.partial(_diff_kernel, n_keys=t, lambda_init=lambda_init, latent=True),
        grid=(b, DA_HEADS, n_q_tiles),
        in_specs=[
            pl.BlockSpec((1, LANES, tq), lambda bi, h, i: (bi, h, i)),
            pl.BlockSpec((1, LANES, tq), lambda bi, h, i: (bi, h, jnp.minimum(i + 1, n_q_tiles - 1))),
            pl.BlockSpec((1, t, LANES), lambda bi, h, i: (bi, 0, h)),
            pl.BlockSpec((1, DA_VT_ROWS, t), lambda bi, h, i: (bi, h, 0)),
        ] + consts,
        out_specs=pl.BlockSpec((1, tq, LANES), lambda bi, h, i: (bi, i, h)),
        out_shape=jax.ShapeDtypeStruct((b, out_rows, D_C), BF16),
        scratch_shapes=[pltpu.VMEM((2, DA_KV_CHUNK, 2 * tq), F32), pltpu.VMEM((1, 2 * tq), F32)],
        name="diff_attention",
        compiler_params=_params(("arbitrary", "arbitrary", "arbitrary")),
    )(qt, qt, kd, vt, lqk, subln)
    if not ctx_queries:
        return oc
    ctx_blk = n_latent // CTX_LEN
    return pl.pallas_call(
        functools.partial(_diff_kernel, n_keys=CTX_LEN, lambda_init=lambda_init, latent=False),
        grid=(b, DA_HEADS),
        in_specs=[
            pl.BlockSpec((1, LANES, CTX_LEN), lambda bi, h: (bi, h, ctx_blk)),
            pl.BlockSpec((1, CTX_LEN, LANES), lambda bi, h: (bi, ctx_blk, h)),
            pl.BlockSpec((1, DA_VT_ROWS, CTX_LEN), lambda bi, h: (bi, h, ctx_blk)),
        ] + consts + [pl.BlockSpec(memory_space=pl.ANY)],
        out_specs=pl.BlockSpec((1, CTX_LEN, LANES), lambda bi, h: (bi, ctx_blk, h)),
        out_shape=jax.ShapeDtypeStruct((b, out_rows, D_C), BF16),
        input_output_aliases={5: 0},
        name="diff_attention_ctx",
        compiler_params=_params(("arbitrary", "arbitrary")),
    )(qt, kd, vt, lqk, subln, oc)


def _rope_tables(n_latent):
    tok = jnp.arange(n_latent)
    row = (tok // GRID_W).astype(F32)
    col = (tok % GRID_W).astype(F32)
    n_freq = DA_QK_DIM // 4
    freqs = ROPE_BASE ** (-jnp.arange(n_freq, dtype=F32) / n_freq)
    ang = jnp.concatenate([row[:, None] * freqs, col[:, None] * freqs], axis=-1)
    cos, sin = jnp.cos(ang), jnp.sin(ang)
    cos_t = jnp.concatenate([cos, cos, cos, cos], axis=-1)
    sin_t = jnp.concatenate([-sin, sin, -sin, sin], axis=-1)
    cos_t = jnp.concatenate([cos_t, jnp.ones((CTX_LEN, LANES), F32)], axis=0)
    sin_t = jnp.concatenate([sin_t, jnp.zeros((CTX_LEN, LANES), F32)], axis=0)
    return cos_t, sin_t


def kernel(x, c, ctx, c_ctx, w_ada, b_ada, ffn1_w1, ffn1_w3, ffn1_w2, w_in, w_out, na_rpb,
           gm_ws, gm_bs, gm_norm, da_lq1, da_lk1, da_lq2, da_lk2, da_subln,
           ffn2_w1, ffn2_w3, ffn2_w2, final_norm):
    b, n_latent, d = x.shape
    assert d == D_MODEL and ctx.shape[1] == CTX_LEN and b < 8
    assert n_latent % ATT_Q_TILE == 0

    cvec = jnp.zeros((8, d), F32).at[:b].set(c).at[b].set(c_ctx)
    mod = _ada(cvec, w_ada, b_ada).reshape(DEPTH, 8, N_ADA, d)
    cos_t, sin_t = _rope_tables(n_latent)
    gmat = jnp.asarray(np.kron(np.eye(GM_GROUPS), np.ones((GM_WIDTH, GM_WIDTH))), BF16)

    bf16 = lambda w: w.astype(BF16)
    ffn1 = (bf16(ffn1_w1), bf16(ffn1_w3), bf16(ffn1_w2))
    ffn2 = (bf16(ffn2_w1), bf16(ffn2_w3), bf16(ffn2_w2))
    w_in, w_out = bf16(w_in), bf16(w_out)

    xs = x
    for l in range(DEPTH):
        last = l == DEPTH - 1
        lambda_init = 0.8 - 0.6 * math.exp(-0.3 * l)
        modt = jnp.stack([mod[l, :b], jnp.broadcast_to(mod[l, b], (b, N_ADA, d))], axis=1)
        bias = _na_bias_tables(na_rpb[l], n_latent)
        ws_stack = gm_ws[l].reshape(GM_GROUPS * GM_CHUNK, GM_CHUNK).astype(BF16)
        bs_full = jnp.repeat(gm_bs[l].T, GM_WIDTH, axis=1)
        gn = gm_norm[l].reshape(1, D_B)
        lqk = jnp.stack([da_lq1[l], da_lk1[l], da_lq2[l], da_lk2[l]])

        xs = _ffn(xs, modt, *ffn1, layer=l, n_latent=n_latent, mi=0, ctx=ctx if l == 0 else None)
        za, kat, ob, kd, qt, vt = _proj_in(xs, modt, w_in, cos_t, sin_t, ws_stack, bs_full, gn,
                                           gmat, layer=l, n_latent=n_latent)
        oa = _na_attention(za, kat, bias, n_latent=n_latent, ctx_queries=not last)
        oc = _diff_attention(kd, qt, vt, lqk, da_subln[l].reshape(DA_V_DIM, 1),
                             n_latent=n_latent, lambda_init=lambda_init, ctx_queries=not last)
        xs = _ffn(xs, modt, *ffn2, layer=l, n_latent=n_latent, mi=6, mix=(oa, ob, oc, w_out),
                  final_gain=final_norm if last else None)
    return xs
```

```python
import functools
import math

import numpy as np
import jax
import jax.numpy as jnp
from jax import lax
from jax.experimental import pallas as pl
from jax.experimental.pallas import tpu as pltpu

F32 = jnp.float32
BF16 = jnp.bfloat16

D_MODEL = 1024
DEPTH = 2
GRID_W = 64
CTX_LEN = 256
N_ADA = 9
EPS = 1e-6
NEG_INF = -1e30
ROPE_BASE = 10000.0
D_FF = 2816
HEAD_DIM = 64
NA_HEADS = 4
NA_WIN_ROWS = 8
NA_WIN_COLS = 16
D_A = NA_HEADS * HEAD_DIM
GM_GROUPS = 4
GM_WIDTH = 64
GM_CHUNK = 128
D_B = GM_GROUPS * GM_WIDTH
DA_HEADS = 4
DA_QK_DIM = 64
DA_V_DIM = 128
DA_VT_ROWS = DA_V_DIM + 16
D_C = DA_HEADS * DA_V_DIM
D_QK_C = DA_HEADS * 2 * DA_QK_DIM
D_IN = 3 * D_A + 2 * D_B + 2 * D_QK_C + D_C
QK_SCALE = HEAD_DIM ** -0.5
LOG2_E = math.log2(math.e)

LANES = 128
MXU_DIM_V7X = 256
VMEM_BYTES_V7X = 64 * 1024 * 1024
VMEM_LIMIT = VMEM_BYTES_V7X - 8 * 1024 * 1024

TOKEN_TILE = 256
FFN_SUB_TILES = (3, 2, 1)
NA_SUB_GROUPS = (3, 2)
ATT_Q_TILE = 256
NA_Q_ROWS = ATT_Q_TILE // GRID_W
NA_K_ROWS = NA_Q_ROWS + NA_WIN_ROWS
NA_K_TOK = NA_K_ROWS * GRID_W
DA_Q_TILE = 512
DA_TILES_PER_STEP = (2, 1)
DA_KV_CHUNK = 1536
DA_KV_RAMP_UP = (256, 768)
DA_KV_RAMP_DOWN = (768, 512, 256, 256)
DA_KV_SMALL_CHUNK = 384
FF_CHUNKS =((0, 768), (768, 1536), (1536, 2304), (2304, 2816))


def _params(sem):
    return pltpu.CompilerParams(dimension_semantics=sem, vmem_limit_bytes=VMEM_LIMIT)


def _resident(shape):
    nd = len(shape)
    return pl.BlockSpec(shape, lambda *_: (0,) * nd, pipeline_mode=pl.Buffered(1))


def _layer_resident(shape, layer):
    nd = len(shape)
    return pl.BlockSpec((1,) + shape, lambda *_: (layer,) + (0,) * nd,
                        pipeline_mode=pl.Buffered(1))


def _rms(x):
    return x * lax.rsqrt(jnp.mean(x * x, axis=-1, keepdims=True) + EPS)


def _mod_row(mod_ref, idx):
    return mod_ref[0, 0, idx:idx + 1, :]


def _mod_spec(tm, n_latent, d):
    assert n_latent % tm == 0
    return pl.BlockSpec((1, 1, N_ADA, d), lambda i, j: (i, (j * tm >= n_latent).astype(jnp.int32), 0, 0))


def _ada_kernel(c_ref, w_ref, b_ref, o_ref):
    c = c_ref[...]
    cs = c / (1.0 + jnp.exp(-c))
    o_ref[0] = jnp.dot(cs, w_ref[0], preferred_element_type=F32,
                       precision=lax.Precision.HIGHEST) + b_ref[0]


def _ada(cvec, w_ada, b_ada):
    depth, d, n = w_ada.shape
    tn = 1024
    return pl.pallas_call(
        _ada_kernel,
        grid=(depth, n // tn),
        in_specs=[
            pl.BlockSpec((8, d), lambda l, j: (0, 0)),
            pl.BlockSpec((1, d, tn), lambda l, j: (l, 0, j)),
            pl.BlockSpec((1, 1, tn), lambda l, j: (l, 0, j)),
        ],
        out_specs=pl.BlockSpec((1, 8, tn), lambda l, j: (l, 0, j)),
        out_shape=jax.ShapeDtypeStruct((depth, 8, n), F32),
        name="ada",
        compiler_params=_params(("arbitrary", "arbitrary")),
    )(cvec, w_ada, b_ada.reshape(depth, 1, n))


def _ffn_kernel(*refs, tm, n_sub, n_latent, mi, split_ctx, mix, final):
    refs = list(refs)
    if split_ctx:
        x_refs = refs[:n_sub]
        del refs[:n_sub]
        ctx_ref = refs.pop(0)
    else:
        x_ref = refs.pop(0)
    mod_ref = refs.pop(0)
    if mix:
        oa_ref, ob_ref, oc_ref, wout_ref = refs[:4]
        del refs[:4]
    w1_ref, w3_ref, w2_ref = refs[:3]
    del refs[:3]
    if final:
        gain_ref = refs.pop(0)
    (o_ref,) = refs
    for sub in range(n_sub):
        rows = slice(sub * tm, (sub + 1) * tm)
        is_ctx = (pl.program_id(1) * n_sub + sub) * tm >= n_latent

        def mod(idx):
            if final:
                return mod_ref[0, 0, idx:idx + 1, :]
            return jnp.where(is_ctx, mod_ref[0, 1, idx:idx + 1, :], mod_ref[0, 0, idx:idx + 1, :])

        if split_ctx:
            x = jnp.where(is_ctx, ctx_ref[0], x_refs[sub][0])
        else:
            x = x_ref[0, rows, :]
        if mix:
            cat = jnp.concatenate([oa_ref[0, rows, :], ob_ref[0, rows, :], oc_ref[0, rows, :]],
                                  axis=-1)
            x = x + mod(mi - 1) * jnp.dot(cat, wout_ref[0], preferred_element_type=F32)
        shift = mod(mi)
        scale1 = 1.0 + mod(mi + 1)
        half_gate = 0.5 * mod(mi + 2)
        h = (_rms(x) * scale1 + shift).astype(BF16)
        acc = jnp.zeros((tm, D_MODEL), F32)
        for c0, c1 in FF_CHUNKS:
            a = jnp.dot(h, w1_ref[0, :, c0:c1], preferred_element_type=F32)
            g = jnp.dot(h, w3_ref[0, :, c0:c1], preferred_element_type=F32)
            y = (a / (1.0 + jnp.exp(-a)) * g).astype(BF16)
            acc = acc + jnp.dot(y, w2_ref[0, c0:c1, :], preferred_element_type=F32)
        out = x + half_gate * acc
        if final:
            out = _rms(out) * gain_ref[...]
        o_ref[0, rows, :] = out


def _ffn(x, modt, w1, w3, w2, *, layer, n_latent, mi, ctx=None, mix=None, final_gain=None):
    b, _, d = x.shape
    t = n_latent + CTX_LEN
    tm = TOKEN_TILE
    final = final_gain is not None
    t_out = n_latent if final else t
    n_sub = next(n for n in FFN_SUB_TILES if (t_out // tm) % n == 0)
    step_rows = n_sub * tm
    tile = lambda width: pl.BlockSpec((1, step_rows, width), lambda i, j: (i, j, 0))
    if ctx is None:
        in_specs, args = [tile(d)], [x]
    else:
        assert tm == CTX_LEN and not final
        last_latent = n_latent // tm - 1
        sub_spec = lambda sub: pl.BlockSpec(
            (1, tm, d), lambda i, j: (i, jnp.minimum(j * n_sub + sub, last_latent), 0))
        in_specs = [sub_spec(sub) for sub in range(n_sub)]
        in_specs.append(pl.BlockSpec((1, CTX_LEN, d), lambda i, j: (i, 0, 0)))
        args = [x] * n_sub + [ctx]
    in_specs.append(pl.BlockSpec((1, 2, N_ADA, d), lambda i, j: (i, 0, 0, 0)))
    args.append(modt)
    if mix is not None:
        in_specs += [tile(D_A), tile(D_B), tile(D_C), _layer_resident((d, d), layer)]
        args += list(mix)
    in_specs += [_layer_resident((d, D_FF), layer), _layer_resident((d, D_FF), layer),
                 _layer_resident((D_FF, d), layer)]
    args += [w1, w3, w2]
    if final:
        in_specs.append(_resident((1, d)))
        args.append(final_gain.reshape(1, d))
    return pl.pallas_call(
        functools.partial(_ffn_kernel, tm=tm, n_sub=n_sub, n_latent=n_latent, mi=mi,
                          split_ctx=ctx is not None, mix=mix is not None, final=final),
        grid=(b, t_out // step_rows),
        in_specs=in_specs,
        out_specs=tile(d),
        out_shape=jax.ShapeDtypeStruct((b, t_out, d), F32),
        name="ffn_final" if final else ("ffn_mix" if mix is not None else "ffn"),
        compiler_params=_params(("arbitrary", "arbitrary")),
    )(*args)


def _swap_rope_halves(x, first_half):
    return jnp.where(first_half, pltpu.roll(x, 96, 1), pltpu.roll(x, 32, 1))


def _proj_in_kernel(*refs, tm, n_sub, n_latent):
    for sub in range(n_sub):
        _proj_in_tile(*refs, tm=tm, rows=slice(sub * tm, (sub + 1) * tm),
                      is_ctx=(pl.program_id(1) * n_sub + sub) * tm >= n_latent)


def _proj_in_tile(x_ref, mod_ref, w_ref, cos_ref, sin_ref, ws_ref, bs_ref, gn_ref, gmat_ref,
                  za_ref, kat_ref, ob_ref, kd_ref, qt_ref, vt_ref, *, tm, rows, is_ctx):
    mod = lambda idx: jnp.where(is_ctx, mod_ref[0, 1, idx:idx + 1, :], mod_ref[0, 0, idx:idx + 1, :])
    x = x_ref[0, rows, :]
    h = (_rms(x) * (1.0 + mod(4)) + mod(3)).astype(BF16)

    z_all = jnp.dot(h, w_ref[0], preferred_element_type=F32)
    project = lambda c0, width: z_all[:, c0:c0 + width]

    z = project(3 * D_A, 2 * D_B)
    u = z[:, 0:D_B]
    v = z[:, D_B:]
    u = 0.5 * u * (1.0 + lax.erf(u * (2.0 ** -0.5)))
    v = 0.5 * v * (1.0 + lax.erf(v * (2.0 ** -0.5)))
    v2 = v * v
    v2_hi = v2.astype(BF16)
    v2_lo = (v2 - v2_hi.astype(F32)).astype(BF16)
    ms = (jnp.dot(v2_hi, gmat_ref[...], preferred_element_type=F32)
          + jnp.dot(v2_lo, gmat_ref[...], preferred_element_type=F32)) * (1.0 / GM_WIDTH)
    vn = (v * lax.rsqrt(ms + EPS) * gn_ref[...]).astype(BF16)
    lane_group = lax.broadcasted_iota(jnp.int32, (1, D_B), 1) // GM_WIDTH
    for c in range(tm // GM_CHUNK):
        crows = slice(c * GM_CHUNK, (c + 1) * GM_CHUNK)
        r = jnp.dot(ws_ref[...], vn[crows, :], preferred_element_type=F32)
        s = bs_ref[...]
        for g in range(GM_GROUPS):
            s = s + jnp.where(lane_group == g, r[g * GM_CHUNK:(g + 1) * GM_CHUNK, :], 0.0)
        ob_ref[0, pl.ds(rows.start + c * GM_CHUNK, GM_CHUNK), :] = (u[crows, :] * s).astype(BF16)

    z = project(0, 3 * D_A)
    za_ref[0, rows, 0:D_A] = (z[:, 0:D_A] * (QK_SCALE * LOG2_E)).astype(BF16)
    kat_ref[0, :, rows] = z[:, D_A:2 * D_A].T.astype(BF16)
    za_ref[0, rows, D_A:2 * D_A] = z[:, 2 * D_A:3 * D_A].astype(BF16)

    o = 3 * D_A + 2 * D_B
    cos = cos_ref[rows, :]
    sin = sin_ref[rows, :]
    first_half = (lax.broadcasted_iota(jnp.int32, (1, LANES), 1) % DA_QK_DIM) < (DA_QK_DIM // 2)
    n_q_blocks = D_QK_C // LANES
    for j in range(2 * n_q_blocks):
        if j % n_q_blocks == 0:
            z = project(o + j * LANES, D_QK_C)
        zz = z[:, (j % n_q_blocks) * LANES:(j % n_q_blocks + 1) * LANES]
        r = zz * cos + _swap_rope_halves(zz, first_half) * sin
        if j < n_q_blocks:
            qt_ref[0, j * LANES:(j + 1) * LANES, rows] = (r * (QK_SCALE * LOG2_E)).T.astype(BF16)
        else:
            jk = j - n_q_blocks
            kd_ref[0, rows, jk * LANES:(jk + 1) * LANES] = r.astype(BF16)
    z = project(o + 2 * D_QK_C, D_C)
    for j in range(D_C // LANES):
        r0 = j * DA_VT_ROWS
        vt_ref[0, r0:r0 + DA_V_DIM, rows] = z[:, j * LANES:(j + 1) * LANES].T.astype(BF16)
        vt_ref[0, r0 + DA_V_DIM:r0 + DA_VT_ROWS, rows] = jnp.ones((DA_VT_ROWS - DA_V_DIM, tm), BF16)


def _proj_in(x, modt, w_in, cos_t, sin_t, ws_stack, bs_full, gn, gmat, *, layer, n_latent):
    b, t, d = x.shape
    n_sub = next(n for n in FFN_SUB_TILES if (t // TOKEN_TILE) % n == 0)
    tm = n_sub * TOKEN_TILE
    return pl.pallas_call(
        functools.partial(_proj_in_kernel, tm=TOKEN_TILE, n_sub=n_sub, n_latent=n_latent),
        grid=(b, t // tm),
        in_specs=[
            pl.BlockSpec((1, tm, d), lambda i, j: (i, j, 0)),
            pl.BlockSpec((1, 2, N_ADA, d), lambda i, j: (i, 0, 0, 0)),
            _layer_resident((d, D_IN), layer),
            pl.BlockSpec((tm, LANES), lambda i, j: (j, 0)),
            pl.BlockSpec((tm, LANES), lambda i, j: (j, 0)),
            _resident((GM_GROUPS * GM_CHUNK, GM_CHUNK)),
            _resident((GM_CHUNK, D_B)),
            _resident((1, D_B)),
            _resident((D_B, D_B)),
        ],
        out_specs=[
            pl.BlockSpec((1, tm, 2 * D_A), lambda i, j: (i, j, 0)),
            pl.BlockSpec((1, D_A, tm), lambda i, j: (i, 0, j)),
            pl.BlockSpec((1, tm, D_B), lambda i, j: (i, j, 0)),
            pl.BlockSpec((1, tm, D_QK_C), lambda i, j: (i, j, 0)),
            pl.BlockSpec((1, D_QK_C, tm), lambda i, j: (i, 0, j)),
            pl.BlockSpec((1, DA_HEADS * DA_VT_ROWS, tm), lambda i, j: (i, 0, j)),
        ],
        out_shape=[
            jax.ShapeDtypeStruct((b, t, 2 * D_A), BF16),
            jax.ShapeDtypeStruct((b, D_A, t), BF16),
            jax.ShapeDtypeStruct((b, t, D_B), BF16),
            jax.ShapeDtypeStruct((b, t, D_QK_C), BF16),
            jax.ShapeDtypeStruct((b, D_QK_C, t), BF16),
            jax.ShapeDtypeStruct((b, DA_HEADS * DA_VT_ROWS, t), BF16),
        ],
        name="proj_in",
        compiler_params=_params(("arbitrary", "arbitrary")),
    )(x, modt, w_in, cos_t, sin_t, ws_stack, bs_full, gn, gmat)


def _na_kernel(*refs, n_sub, n_latent):
    q_ref, qn_ref, kt_ref, v_ref = refs[:4]
    bias_refs = refs[4:5 + n_sub]
    o_ref = refs[5 + n_sub]
    s_refs = refs[6 + n_sub:]
    assert len(s_refs) == n_sub >= 2
    first = pl.program_id(2) * n_sub
    n_groups = n_latent // ATT_Q_TILE
    rows = n_latent // GRID_W
    lane = lax.broadcasted_iota(jnp.int32, (1, LANES), 1)

    def window_start(grp):
        base_row = jnp.clip(grp * NA_Q_ROWS - NA_WIN_ROWS // 2, 0, rows - NA_K_ROWS)
        return pl.multiple_of(base_row * GRID_W, ATT_Q_TILE)

    def write_scores(s_ref, q, bias, grp):
        qs = jnp.concatenate([jnp.where(lane < HEAD_DIM, q, jnp.zeros_like(q)),
                              jnp.where(lane >= HEAD_DIM, q, jnp.zeros_like(q))], axis=0)
        sw = jnp.dot(qs, kt_ref[0, :, pl.ds(window_start(grp), NA_K_TOK)],
                     preferred_element_type=F32)
        s_ref[:, 0:NA_K_TOK] = sw + bias.reshape(2 * ATT_Q_TILE, NA_K_TOK)
        s_ref[:, NA_K_TOK:] = jnp.dot(qs, kt_ref[0, :, n_latent:], preferred_element_type=F32)

    def softmax_pv(s_ref, grp, out_rows):
        s = s_ref[...]
        p = jnp.exp2(s - s.max(axis=-1, keepdims=True))
        l = p.sum(axis=-1, keepdims=True)
        p = p.astype(BF16)
        o = (jnp.dot(p[:, 0:NA_K_TOK], v_ref[0, pl.ds(window_start(grp), NA_K_TOK), :],
                     preferred_element_type=F32)
             + jnp.dot(p[:, NA_K_TOK:], v_ref[0, n_latent:, :], preferred_element_type=F32)) / l
        o_ref[0, out_rows, :] = jnp.where(lane < HEAD_DIM, o[:ATT_Q_TILE],
                                          o[ATT_Q_TILE:]).astype(BF16)

    sub_rows = lambda sub: slice(sub * ATT_Q_TILE, (sub + 1) * ATT_Q_TILE)

    @pl.when(first == 0)
    def _():
        write_scores(s_refs[0], q_ref[0, sub_rows(0), :], bias_refs[0][0], first)

    for sub in range(n_sub):
        q_next = q_ref[0, sub_rows(sub + 1), :] if sub + 1 < n_sub else qn_ref[0]
        write_scores(s_refs[(sub + 1) % n_sub], q_next, bias_refs[sub + 1][0],
                     jnp.minimum(first + sub + 1, n_groups))
        softmax_pv(s_refs[sub], first + sub, sub_rows(sub))


def _na_attention(za, kat, bias, *, n_latent, ctx_queries):
    b, t, _ = za.shape
    n_groups = n_latent // ATT_Q_TILE
    n_total = n_groups + 1 if ctx_queries else n_groups
    n_sub = next(n for n in NA_SUB_GROUPS if n_total % n == 0)
    assert t == n_latent + CTX_LEN and CTX_LEN == ATT_Q_TILE
    assert n_groups >= 3 and n_latent // GRID_W >= NA_K_ROWS

    def kind(g):
        return jnp.where(g == 0, 0, jnp.where(g < n_groups - 1, 1, jnp.where(g == n_groups - 1, 2, 3)))

    group_at = lambda j, off: jnp.minimum(j * n_sub + off, n_groups)
    bias_spec = lambda off: pl.BlockSpec((1, 2, ATT_Q_TILE, NA_K_TOK),
                                         lambda i, hp, j: (kind(group_at(j, off)), hp, 0, 0))
    score_buf = pltpu.VMEM((2 * ATT_Q_TILE, NA_K_TOK + CTX_LEN), F32)
    return pl.pallas_call(
        functools.partial(_na_kernel, n_sub=n_sub, n_latent=n_latent),
        grid=(b, NA_HEADS // 2, n_total // n_sub),
        in_specs=[
            pl.BlockSpec((1, n_sub * ATT_Q_TILE, LANES), lambda i, hp, j: (i, j, hp)),
            pl.BlockSpec((1, ATT_Q_TILE, LANES), lambda i, hp, j: (i, group_at(j, n_sub), hp)),
            pl.BlockSpec((1, LANES, t), lambda i, hp, j: (i, hp, 0)),
            pl.BlockSpec((1, t, LANES), lambda i, hp, j: (i, 0, 2 + hp)),
        ] + [bias_spec(off) for off in range(n_sub + 1)],
        out_specs=pl.BlockSpec((1, n_sub * ATT_Q_TILE, LANES), lambda i, hp, j: (i, j, hp)),
        out_shape=jax.ShapeDtypeStruct((b, n_total * ATT_Q_TILE, D_A), BF16),
        scratch_shapes=[score_buf] * n_sub,
        name="na_attention",
        compiler_params=_params(("arbitrary", "arbitrary", "arbitrary")),
    )(za, za, kat, za, *([bias] * (n_sub + 1)))


NA_BIAS_PAD = NA_WIN_ROWS // 2
NA_BIAS_W = 11 * LANES


def _na_bias_plan(n_latent):
    rows = n_latent // GRID_W
    n_groups = n_latent // ATT_Q_TILE
    plan = []
    for g in (0, 1, n_groups - 1):
        base = int(np.clip(g * NA_Q_ROWS - NA_WIN_ROWS // 2, 0, rows - NA_K_ROWS))
        kind = []
        for i in range(NA_Q_ROWS):
            r = g * NA_Q_ROWS + i
            start = int(np.clip(r - NA_WIN_ROWS // 2, 0, rows - NA_WIN_ROWS))
            e0 = base - r + NA_WIN_ROWS - 1 + NA_BIAS_PAD
            assert 0 <= e0 and (e0 - e0 % 2) * GRID_W + NA_K_TOK <= NA_BIAS_W
            kind.append((e0, start - base, start - base + NA_WIN_ROWS - 1))
        plan.append(tuple(kind))
    return tuple(plan)


def _na_bias_kernel(r_ref, o_ref, *, plan):
    shape = (GRID_W, NA_BIAS_W)
    qc = lax.broadcasted_iota(jnp.int32, shape, 0)
    col = lax.broadcasted_iota(jnp.int32, shape, 1)
    kc = col % GRID_W
    dc = kc - qc + (NA_WIN_COLS - 1)
    w0 = jnp.clip(qc - NA_WIN_COLS // 2, 0, GRID_W - NA_WIN_COLS)
    col_ok = (kc >= w0) & (kc < w0 + NA_WIN_COLS)
    tables = []
    for p in range(2):
        a = col // GRID_W + (p - NA_BIAS_PAD)
        acc = jnp.zeros(shape, F32)
        for b in range(2 * NA_WIN_COLS - 1):
            acc = acc + jnp.where(dc == b, r_ref[0, p, b:b + 1, :], 0.0)
        ok = col_ok & (a >= 0) & (a <= 2 * NA_WIN_ROWS - 2)
        tables.append(jnp.where(ok, acc * LOG2_E, NEG_INF))
    kj = lax.broadcasted_iota(jnp.int32, (GRID_W, NA_K_TOK), 1) // GRID_W
    for k, kind in enumerate(plan):
        for i, (e0, jlo, jhi) in enumerate(kind):
            p = e0 % 2
            off = (e0 - p) * GRID_W
            slab = tables[p][:, off:off + NA_K_TOK]
            o_ref[k, 0, i * GRID_W:(i + 1) * GRID_W, :] = jnp.where(
                (kj >= jlo) & (kj <= jhi), slab, NEG_INF)
    o_ref[len(plan), 0] = jnp.full((ATT_Q_TILE, NA_K_TOK), NEG_INF, F32)


def _na_bias_tables(rpb, n_latent):
    h = rpb.shape[0]
    n_blocks = NA_BIAS_W // GRID_W + 2
    rp = jnp.pad(rpb, ((0, 0), (NA_BIAS_PAD, n_blocks - NA_BIAS_PAD - rpb.shape[1]), (0, 0)))
    rep = jnp.transpose(jnp.repeat(rp, GRID_W, axis=1), (0, 2, 1))
    r = jnp.stack([rep[:, :, p * GRID_W:p * GRID_W + NA_BIAS_W] for p in range(2)], axis=1)
    n_dc = rpb.shape[2]
    return pl.pallas_call(
        functools.partial(_na_bias_kernel, plan=_na_bias_plan(n_latent)),
        grid=(h,),
        in_specs=[pl.BlockSpec((1, 2, n_dc, NA_BIAS_W), lambda i: (i, 0, 0, 0))],
        out_specs=pl.BlockSpec((4, 1, ATT_Q_TILE, NA_K_TOK), lambda i: (0, i, 0, 0)),
        out_shape=jax.ShapeDtypeStruct((4, h, ATT_Q_TILE, NA_K_TOK), F32),
        name="na_bias",
        compiler_params=_params(("arbitrary",)),
    )(r)


def _kv_chunk_bounds(t):
    head, tail = DA_KV_RAMP_UP, DA_KV_RAMP_DOWN
    middle = t - sum(head) - sum(tail)
    if middle < 0 or middle % MXU_DIM_V7X:
        assert t % DA_KV_SMALL_CHUNK == 0
        sizes = (DA_KV_SMALL_CHUNK,) * (t // DA_KV_SMALL_CHUNK)
    else:
        units = middle // MXU_DIM_V7X
        n_mid = -(-units // (DA_KV_CHUNK // MXU_DIM_V7X))
        n_mid += n_mid % 2
        mid = tuple((units // n_mid + (j < units % n_mid)) * MXU_DIM_V7X for j in range(n_mid))
        sizes = head + mid + tail
    assert max(sizes) <= DA_KV_CHUNK and len(sizes) % 2 == 0
    return tuple(int(v) for v in np.cumsum((0,) + sizes))


def _diff_kernel(*refs, tq, n_keys, lambda_init, latent):
    if latent:
        qt_ref, qtn_ref, k_ref, vt_ref, lqk_ref, subln_ref, o_ref, s_ref, smax_ref = refs
    else:
        qt_ref, k_ref, vt_ref, lqk_ref, subln_ref, _, o_ref = refs
    n_tiles = qt_ref.shape[2] // tq
    feat = lax.broadcasted_iota(jnp.int32, (LANES, 1), 0)

    def stack_components(qt):
        return jnp.concatenate([jnp.where(feat < DA_QK_DIM, qt, jnp.zeros_like(qt)),
                                jnp.where(feat >= DA_QK_DIM, qt, jnp.zeros_like(qt))], axis=1)

    tile_cols = lambda u: slice(u * tq, (u + 1) * tq)
    lqk = lqk_ref[...]
    lam = (jnp.exp(jnp.sum(lqk[0:1] * lqk[1:2], axis=-1, keepdims=True))
           - jnp.exp(jnp.sum(lqk[2:3] * lqk[3:4], axis=-1, keepdims=True)) + lambda_init)

    def col_reduce(x, op):
        r, c = x.shape
        groups = 8 if r % 64 == 0 else 1
        x = x.reshape(groups, r // (8 * groups), 8, c)
        return op(op(op(x, axis=1), axis=0), axis=0, keepdims=True)

    def scores(k, q):
        s = jnp.dot(k, q, preferred_element_type=F32)
        return s, col_reduce(s, jnp.max)

    def step(s, s_max, vt1, state):
        m, acc = state
        m_new = jnp.maximum(m, s_max)
        alpha = jnp.exp2(m - m_new)
        p = jnp.exp2(s - m_new)
        acc = alpha * acc + jnp.dot(vt1, p.astype(BF16), preferred_element_type=F32)
        return (m_new, acc)

    def init():
        return (jnp.full((1, 2 * tq), NEG_INF, F32), jnp.zeros((DA_VT_ROWS, 2 * tq), F32))

    def finish(state, u):
        _, acc = state
        o = acc[:DA_V_DIM] / acc[DA_V_DIM:DA_V_DIM + 1]
        o = o[:, :tq] - lam * o[:, tq:]
        o = o * lax.rsqrt(jnp.mean(o * o, axis=0, keepdims=True) + EPS)
        o = o * (subln_ref[...] * (1.0 - lambda_init))
        o_ref[0, tile_cols(u), :] = o.T.astype(BF16)

    if not latent:
        s, s_max = scores(k_ref[0], stack_components(qt_ref[0]))
        finish(step(s, s_max, vt_ref[0], init()), 0)
        return

    bounds = _kv_chunk_bounds(n_keys)
    chunk = lambda c: slice(bounds[c], bounds[c + 1])
    size = lambda c: bounds[c + 1] - bounds[c]
    n_chunks = len(bounds) - 1
    assert n_chunks % 2 == 0

    @pl.when(pl.program_id(2) == 0)
    def _():
        s_ref[0, 0:size(0)], smax_ref[...] = scores(k_ref[0, chunk(0), :],
                                                    stack_components(qt_ref[0, :, tile_cols(0)]))

    s_max = smax_ref[...]
    for u in range(n_tiles):
        qst = stack_components(qt_ref[0, :, tile_cols(u)])
        state = init()
        for c in range(n_chunks):
            if c + 1 < n_chunks:
                s_ref[(c + 1) % 2, 0:size(c + 1)], next_max = scores(k_ref[0, chunk(c + 1), :], qst)
            else:
                q_next = qt_ref[0, :, tile_cols(u + 1)] if u + 1 < n_tiles else qtn_ref[0]
                s_ref[0, 0:size(0)], next_max = scores(k_ref[0, chunk(0), :],
                                                       stack_components(q_next))
            state = step(s_ref[c % 2, 0:size(c)], s_max, vt_ref[0, :, chunk(c)], state)
            s_max = next_max
        finish(state, u)
    smax_ref[...] = s_max


def _diff_attention(kd, qt, vt, lqk, subln, *, n_latent, lambda_init, ctx_queries):
    b, t, _ = kd.shape
    tq = DA_Q_TILE
    n_q_tiles = n_latent // tq
    n_tiles = next(n for n in DA_TILES_PER_STEP if n_q_tiles % n == 0)
    step_q = n_tiles * tq
    consts = [pl.BlockSpec((4, DA_QK_DIM), lambda *_: (0, 0)),
              pl.BlockSpec((DA_V_DIM, 1), lambda *_: (0, 0))]
    out_rows = t if ctx_queries else n_latent
    oc = pl.pallas_call(
        functools.partial(_diff_kernel, tq=tq, n_keys=t, lambda_init=lambda_init, latent=True),
        grid=(b, DA_HEADS, n_q_tiles // n_tiles),
        in_specs=[
            pl.BlockSpec((1, LANES, step_q), lambda bi, h, i: (bi, h, i)),
            pl.BlockSpec((1, LANES, tq),
                         lambda bi, h, i: (bi, h, jnp.minimum((i + 1) * n_tiles, n_q_tiles - 1))),
            pl.BlockSpec((1, t, LANES), lambda bi, h, i: (bi, 0, h)),
            pl.BlockSpec((1, DA_VT_ROWS, t), lambda bi, h, i: (bi, h, 0)),
        ] + consts,
        out_specs=pl.BlockSpec((1, step_q, LANES), lambda bi, h, i: (bi, i, h)),
        out_shape=jax.ShapeDtypeStruct((b, out_rows, D_C), BF16),
        scratch_shapes=[pltpu.VMEM((2, DA_KV_CHUNK, 2 * tq), F32), pltpu.VMEM((1, 2 * tq), F32)],
        name="diff_attention",
        compiler_params=_params(("arbitrary", "arbitrary", "arbitrary")),
    )(qt, qt, kd, vt, lqk, subln)
    if not ctx_queries:
        return oc
    ctx_blk = n_latent // CTX_LEN
    return pl.pallas_call(
        functools.partial(_diff_kernel, tq=CTX_LEN, n_keys=CTX_LEN, lambda_init=lambda_init,
                          latent=False),
        grid=(b, DA_HEADS),
        in_specs=[
            pl.BlockSpec((1, LANES, CTX_LEN), lambda bi, h: (bi, h, ctx_blk)),
            pl.BlockSpec((1, CTX_LEN, LANES), lambda bi, h: (bi, ctx_blk, h)),
            pl.BlockSpec((1, DA_VT_ROWS, CTX_LEN), lambda bi, h: (bi, h, ctx_blk)),
        ] + consts + [pl.BlockSpec(memory_space=pl.ANY)],
        out_specs=pl.BlockSpec((1, CTX_LEN, LANES), lambda bi, h: (bi, ctx_blk, h)),
        out_shape=jax.ShapeDtypeStruct((b, out_rows, D_C), BF16),
        input_output_aliases={5: 0},
        name="diff_attention_ctx",
        compiler_params=_params(("arbitrary", "arbitrary")),
    )(qt, kd, vt, lqk, subln, oc)


def _rope_tables(n_latent):
    tok = jnp.arange(n_latent)
    row = (tok // GRID_W).astype(F32)
    col = (tok % GRID_W).astype(F32)
    n_freq = DA_QK_DIM // 4
    freqs = ROPE_BASE ** (-jnp.arange(n_freq, dtype=F32) / n_freq)
    ang = jnp.concatenate([row[:, None] * freqs, col[:, None] * freqs], axis=-1)
    cos, sin = jnp.cos(ang), jnp.sin(ang)
    cos_t = jnp.concatenate([cos, cos, cos, cos], axis=-1)
    sin_t = jnp.concatenate([-sin, sin, -sin, sin], axis=-1)
    cos_t = jnp.concatenate([cos_t, jnp.ones((CTX_LEN, LANES), F32)], axis=0)
    sin_t = jnp.concatenate([sin_t, jnp.zeros((CTX_LEN, LANES), F32)], axis=0)
    return cos_t, sin_t


def kernel(x, c, ctx, c_ctx, w_ada, b_ada, ffn1_w1, ffn1_w3, ffn1_w2, w_in, w_out, na_rpb,
           gm_ws, gm_bs, gm_norm, da_lq1, da_lk1, da_lq2, da_lk2, da_subln,
           ffn2_w1, ffn2_w3, ffn2_w2, final_norm):
    b, n_latent, d = x.shape
    assert d == D_MODEL and ctx.shape[1] == CTX_LEN and b < 8
    assert n_latent % ATT_Q_TILE == 0

    cvec = jnp.zeros((8, d), F32).at[:b].set(c).at[b].set(c_ctx)
    mod = _ada(cvec, w_ada, b_ada).reshape(DEPTH, 8, N_ADA, d)
    cos_t, sin_t = _rope_tables(n_latent)
    gmat = jnp.asarray(np.kron(np.eye(GM_GROUPS), np.ones((GM_WIDTH, GM_WIDTH))), BF16)

    bf16 = lambda w: w.astype(BF16)
    ffn1 = (bf16(ffn1_w1), bf16(ffn1_w3), bf16(ffn1_w2))
    ffn2 = (bf16(ffn2_w1), bf16(ffn2_w3), bf16(ffn2_w2))
    w_in, w_out = bf16(w_in), bf16(w_out)

    xs = x
    for l in range(DEPTH):
        last = l == DEPTH - 1
        lambda_init = 0.8 - 0.6 * math.exp(-0.3 * l)
        modt = jnp.stack([mod[l, :b], jnp.broadcast_to(mod[l, b], (b, N_ADA, d))], axis=1)
        bias = _na_bias_tables(na_rpb[l], n_latent)
        ws_stack = gm_ws[l].reshape(GM_GROUPS * GM_CHUNK, GM_CHUNK).astype(BF16)
        bs_full = jnp.repeat(gm_bs[l].T, GM_WIDTH, axis=1)
        gn = gm_norm[l].reshape(1, D_B)
        lqk = jnp.stack([da_lq1[l], da_lk1[l], da_lq2[l], da_lk2[l]])

        xs = _ffn(xs, modt, *ffn1, layer=l, n_latent=n_latent, mi=0, ctx=ctx if l == 0 else None)
        za, kat, ob, kd, qt, vt = _proj_in(xs, modt, w_in, cos_t, sin_t, ws_stack, bs_full, gn,
                                           gmat, layer=l, n_latent=n_latent)
        oa = _na_attention(za, kat, bias, n_latent=n_latent, ctx_queries=not last)
        oc = _diff_attention(kd, qt, vt, lqk, da_subln[l].reshape(DA_V_DIM, 1),
                             n_latent=n_latent, lambda_init=lambda_init, ctx_queries=not last)
        xs = _ffn(xs, modt, *ffn2, layer=l, n_latent=n_latent, mi=6, mix=(oa, ob, oc, w_out),
                  final_gain=final_norm if last else None)
    return xs
```

```python
import functools
import math

import numpy as np
import jax
import jax.numpy as jnp
from jax import lax
from jax.experimental import pallas as pl
from jax.experimental.pallas import tpu as pltpu

F32 = jnp.float32
BF16 = jnp.bfloat16

D_MODEL = 1024
DEPTH = 2
GRID_W = 64
CTX_LEN = 256
N_ADA = 9
EPS = 1e-6
NEG_INF = -1e30
ROPE_BASE = 10000.0
D_FF = 2816
HEAD_DIM = 64
NA_HEADS = 4
NA_WIN_ROWS = 8
NA_WIN_COLS = 16
D_A = NA_HEADS * HEAD_DIM
GM_GROUPS = 4
GM_WIDTH = 64
GM_CHUNK = 128
D_B = GM_GROUPS * GM_WIDTH
DA_HEADS = 4
DA_QK_DIM = 64
DA_V_DIM = 128
DA_VT_ROWS = DA_V_DIM + 16
D_C = DA_HEADS * DA_V_DIM
D_QK_C = DA_HEADS * 2 * DA_QK_DIM
D_IN = 3 * D_A + 2 * D_B + 2 * D_QK_C + D_C
QK_SCALE = HEAD_DIM ** -0.5
LOG2_E = math.log2(math.e)

LANES = 128
MXU_DIM_V7X = 256
SCORE_PAD_LANES = 128
VMEM_BYTES_V7X = 64 * 1024 * 1024
VMEM_LIMIT = VMEM_BYTES_V7X - 8 * 1024 * 1024

TOKEN_TILE = 256
FFN_SUB_TILES = (3, 2, 1)
NA_SUB_GROUPS = (3, 2)
ATT_Q_TILE = 256
NA_Q_ROWS = ATT_Q_TILE // GRID_W
NA_K_ROWS = NA_Q_ROWS + NA_WIN_ROWS
NA_K_TOK = NA_K_ROWS * GRID_W
DA_Q_TILE = 512
DA_TILES_PER_STEP = (2, 1)
DA_KV_CHUNK = 1536
DA_KV_RAMP_UP = (256, 768)
DA_KV_RAMP_DOWN = (768, 512, 256, 256)
DA_KV_SMALL_CHUNK = 384
FF_CHUNKS =((0, 768), (768, 1536), (1536, 2304), (2304, 2816))


def _params(sem):
    return pltpu.CompilerParams(dimension_semantics=sem, vmem_limit_bytes=VMEM_LIMIT)


def _resident(shape):
    nd = len(shape)
    return pl.BlockSpec(shape, lambda *_: (0,) * nd, pipeline_mode=pl.Buffered(1))


def _layer_resident(shape, layer):
    nd = len(shape)
    return pl.BlockSpec((1,) + shape, lambda *_: (layer,) + (0,) * nd,
                        pipeline_mode=pl.Buffered(1))


def _rms(x):
    return x * lax.rsqrt(jnp.mean(x * x, axis=-1, keepdims=True) + EPS)


def _score_scratch(lead, width):
    return pltpu.VMEM(tuple(lead) + (width + SCORE_PAD_LANES,), F32)


def _ada_kernel(c_ref, w_ref, b_ref, o_ref):
    c = c_ref[...]
    cs = c / (1.0 + jnp.exp(-c))
    o_ref[0] = jnp.dot(cs, w_ref[0], preferred_element_type=F32,
                       precision=lax.Precision.HIGHEST) + b_ref[0]


def _ada(cvec, w_ada, b_ada):
    depth, d, n = w_ada.shape
    tn = 1024
    return pl.pallas_call(
        _ada_kernel,
        grid=(depth, n // tn),
        in_specs=[
            pl.BlockSpec((8, d), lambda l, j: (0, 0)),
            pl.BlockSpec((1, d, tn), lambda l, j: (l, 0, j)),
            pl.BlockSpec((1, 1, tn), lambda l, j: (l, 0, j)),
        ],
        out_specs=pl.BlockSpec((1, 8, tn), lambda l, j: (l, 0, j)),
        out_shape=jax.ShapeDtypeStruct((depth, 8, n), F32),
        name="ada",
        compiler_params=_params(("arbitrary", "arbitrary")),
    )(cvec, w_ada, b_ada.reshape(depth, 1, n))


def _ffn_kernel(*refs, tm, n_sub, n_latent, mi, split_ctx, mix, final):
    refs = list(refs)
    if split_ctx:
        x_refs = refs[:n_sub]
        del refs[:n_sub]
        ctx_ref = refs.pop(0)
    else:
        x_ref = refs.pop(0)
    mod_ref = refs.pop(0)
    if mix:
        oa_ref, ob_ref, oc_ref, wout_ref = refs[:4]
        del refs[:4]
    w1_ref, w3_ref, w2_ref = refs[:3]
    del refs[:3]
    if final:
        gain_ref = refs.pop(0)
    (o_ref,) = refs
    for sub in range(n_sub):
        rows = slice(sub * tm, (sub + 1) * tm)
        is_ctx = (pl.program_id(1) * n_sub + sub) * tm >= n_latent

        def mod(idx):
            if final:
                return mod_ref[0, 0, idx:idx + 1, :]
            return jnp.where(is_ctx, mod_ref[0, 1, idx:idx + 1, :], mod_ref[0, 0, idx:idx + 1, :])

        if split_ctx:
            x = jnp.where(is_ctx, ctx_ref[0], x_refs[sub][0])
        else:
            x = x_ref[0, rows, :]
        if mix:
            cat = jnp.concatenate([oa_ref[0, rows, :], ob_ref[0, rows, :], oc_ref[0, rows, :]],
                                  axis=-1)
            x = x + mod(mi - 1) * jnp.dot(cat, wout_ref[0], preferred_element_type=F32)
        shift = mod(mi)
        scale1 = 1.0 + mod(mi + 1)
        half_gate = 0.5 * mod(mi + 2)
        h = (_rms(x) * scale1 + shift).astype(BF16)
        acc = jnp.zeros((tm, D_MODEL), F32)
        for c0, c1 in FF_CHUNKS:
            a = jnp.dot(h, w1_ref[0, :, c0:c1], preferred_element_type=F32)
            g = jnp.dot(h, w3_ref[0, :, c0:c1], preferred_element_type=F32)
            y = (a / (1.0 + jnp.exp(-a)) * g).astype(BF16)
            acc = acc + jnp.dot(y, w2_ref[0, c0:c1, :], preferred_element_type=F32)
        out = x + half_gate * acc
        if final:
            out = _rms(out) * gain_ref[...]
        o_ref[0, rows, :] = out


def _ffn(x, modt, w1, w3, w2, *, layer, n_latent, mi, ctx=None, mix=None, final_gain=None):
    b, _, d = x.shape
    t = n_latent + CTX_LEN
    tm = TOKEN_TILE
    final = final_gain is not None
    t_out = n_latent if final else t
    n_sub = next(n for n in FFN_SUB_TILES if (t_out // tm) % n == 0)
    step_rows = n_sub * tm
    tile = lambda width: pl.BlockSpec((1, step_rows, width), lambda i, j: (i, j, 0))
    if ctx is None:
        in_specs, args = [tile(d)], [x]
    else:
        assert tm == CTX_LEN and not final
        last_latent = n_latent // tm - 1
        sub_spec = lambda sub: pl.BlockSpec(
            (1, tm, d), lambda i, j: (i, jnp.minimum(j * n_sub + sub, last_latent), 0))
        in_specs = [sub_spec(sub) for sub in range(n_sub)]
        in_specs.append(pl.BlockSpec((1, CTX_LEN, d), lambda i, j: (i, 0, 0)))
        args = [x] * n_sub + [ctx]
    in_specs.append(pl.BlockSpec((1, 2, N_ADA, d), lambda i, j: (i, 0, 0, 0)))
    args.append(modt)
    if mix is not None:
        in_specs += [tile(D_A), tile(D_B), tile(D_C), _layer_resident((d, d), layer)]
        args += list(mix)
    in_specs += [_layer_resident((d, D_FF), layer), _layer_resident((d, D_FF), layer),
                 _layer_resident((D_FF, d), layer)]
    args += [w1, w3, w2]
    if final:
        in_specs.append(_resident((1, d)))
        args.append(final_gain.reshape(1, d))
    return pl.pallas_call(
        functools.partial(_ffn_kernel, tm=tm, n_sub=n_sub, n_latent=n_latent, mi=mi,
                          split_ctx=ctx is not None, mix=mix is not None, final=final),
        grid=(b, t_out // step_rows),
        in_specs=in_specs,
        out_specs=tile(d),
        out_shape=jax.ShapeDtypeStruct((b, t_out, d), F32),
        name="ffn_final" if final else ("ffn_mix" if mix is not None else "ffn"),
        compiler_params=_params(("arbitrary", "arbitrary")),
    )(*args)


def _swap_rope_halves(x, first_half):
    return jnp.where(first_half, pltpu.roll(x, 96, 1), pltpu.roll(x, 32, 1))


def _proj_in_kernel(*refs, tm, n_sub, n_latent):
    for sub in range(n_sub):
        _proj_in_tile(*refs, tm=tm, rows=slice(sub * tm, (sub + 1) * tm),
                      is_ctx=(pl.program_id(1) * n_sub + sub) * tm >= n_latent)


def _proj_in_tile(x_ref, mod_ref, w_ref, cos_ref, sin_ref, ws_ref, bs_ref, gn_ref, gmat_ref,
                  za_ref, kat_ref, ob_ref, kd_ref, qt_ref, vt_ref, *, tm, rows, is_ctx):
    mod = lambda idx: jnp.where(is_ctx, mod_ref[0, 1, idx:idx + 1, :], mod_ref[0, 0, idx:idx + 1, :])
    x = x_ref[0, rows, :]
    h = (_rms(x) * (1.0 + mod(4)) + mod(3)).astype(BF16)

    z_all = jnp.dot(h, w_ref[0], preferred_element_type=F32)
    project = lambda c0, width: z_all[:, c0:c0 + width]

    z = project(3 * D_A, 2 * D_B)
    u = z[:, 0:D_B]
    v = z[:, D_B:]
    u = 0.5 * u * (1.0 + lax.erf(u * (2.0 ** -0.5)))
    v = 0.5 * v * (1.0 + lax.erf(v * (2.0 ** -0.5)))
    v2 = v * v
    v2_hi = v2.astype(BF16)
    v2_lo = (v2 - v2_hi.astype(F32)).astype(BF16)
    ms = (jnp.dot(v2_hi, gmat_ref[...], preferred_element_type=F32)
          + jnp.dot(v2_lo, gmat_ref[...], preferred_element_type=F32)) * (1.0 / GM_WIDTH)
    vn = (v * lax.rsqrt(ms + EPS) * gn_ref[...]).astype(BF16)
    lane_group = lax.broadcasted_iota(jnp.int32, (1, D_B), 1) // GM_WIDTH
    for c in range(tm // GM_CHUNK):
        crows = slice(c * GM_CHUNK, (c + 1) * GM_CHUNK)
        r = jnp.dot(ws_ref[...], vn[crows, :], preferred_element_type=F32)
        s = bs_ref[...]
        for g in range(GM_GROUPS):
            s = s + jnp.where(lane_group == g, r[g * GM_CHUNK:(g + 1) * GM_CHUNK, :], 0.0)
        ob_ref[0, pl.ds(rows.start + c * GM_CHUNK, GM_CHUNK), :] = (u[crows, :] * s).astype(BF16)

    z = project(0, 3 * D_A)
    za_ref[0, rows, 0:D_A] = (z[:, 0:D_A] * (QK_SCALE * LOG2_E)).astype(BF16)
    kat_ref[0, :, rows] = z[:, D_A:2 * D_A].T.astype(BF16)
    za_ref[0, rows, D_A:2 * D_A] = z[:, 2 * D_A:3 * D_A].astype(BF16)

    o = 3 * D_A + 2 * D_B
    cos = cos_ref[rows, :]
    sin = sin_ref[rows, :]
    first_half = (lax.broadcasted_iota(jnp.int32, (1, LANES), 1) % DA_QK_DIM) < (DA_QK_DIM // 2)
    n_q_blocks = D_QK_C // LANES
    for j in range(2 * n_q_blocks):
        if j % n_q_blocks == 0:
            z = project(o + j * LANES, D_QK_C)
        zz = z[:, (j % n_q_blocks) * LANES:(j % n_q_blocks + 1) * LANES]
        r = zz * cos + _swap_rope_halves(zz, first_half) * sin
        if j < n_q_blocks:
            qt_ref[0, j * LANES:(j + 1) * LANES, rows] = (r * (QK_SCALE * LOG2_E)).T.astype(BF16)
        else:
            jk = j - n_q_blocks
            kd_ref[0, rows, jk * LANES:(jk + 1) * LANES] = r.astype(BF16)
    z = project(o + 2 * D_QK_C, D_C)
    for j in range(D_C // LANES):
        r0 = j * DA_VT_ROWS
        vt_ref[0, r0:r0 + DA_V_DIM, rows] = z[:, j * LANES:(j + 1) * LANES].T.astype(BF16)
        vt_ref[0, r0 + DA_V_DIM:r0 + DA_VT_ROWS, rows] = jnp.ones((DA_VT_ROWS - DA_V_DIM, tm), BF16)


def _proj_in(x, modt, w_in, cos_t, sin_t, ws_stack, bs_full, gn, gmat, *, layer, n_latent):
    b, t, d = x.shape
    n_sub = next(n for n in FFN_SUB_TILES if (t // TOKEN_TILE) % n == 0)
    tm = n_sub * TOKEN_TILE
    return pl.pallas_call(
        functools.partial(_proj_in_kernel, tm=TOKEN_TILE, n_sub=n_sub, n_latent=n_latent),
        grid=(b, t // tm),
        in_specs=[
            pl.BlockSpec((1, tm, d), lambda i, j: (i, j, 0)),
            pl.BlockSpec((1, 2, N_ADA, d), lambda i, j: (i, 0, 0, 0)),
            _layer_resident((d, D_IN), layer),
            pl.BlockSpec((tm, LANES), lambda i, j: (j, 0)),
            pl.BlockSpec((tm, LANES), lambda i, j: (j, 0)),
            _resident((GM_GROUPS * GM_CHUNK, GM_CHUNK)),
            _resident((GM_CHUNK, D_B)),
            _resident((1, D_B)),
            _resident((D_B, D_B)),
        ],
        out_specs=[
            pl.BlockSpec((1, tm, 2 * D_A), lambda i, j: (i, j, 0)),
            pl.BlockSpec((1, D_A, tm), lambda i, j: (i, 0, j)),
            pl.BlockSpec((1, tm, D_B), lambda i, j: (i, j, 0)),
            pl.BlockSpec((1, tm, D_QK_C), lambda i, j: (i, j, 0)),
            pl.BlockSpec((1, D_QK_C, tm), lambda i, j: (i, 0, j)),
            pl.BlockSpec((1, DA_HEADS * DA_VT_ROWS, tm), lambda i, j: (i, 0, j)),
        ],
        out_shape=[
            jax.ShapeDtypeStruct((b, t, 2 * D_A), BF16),
            jax.ShapeDtypeStruct((b, D_A, t), BF16),
            jax.ShapeDtypeStruct((b, t, D_B), BF16),
            jax.ShapeDtypeStruct((b, t, D_QK_C), BF16),
            jax.ShapeDtypeStruct((b, D_QK_C, t), BF16),
            jax.ShapeDtypeStruct((b, DA_HEADS * DA_VT_ROWS, t), BF16),
        ],
        name="proj_in",
        compiler_params=_params(("arbitrary", "arbitrary")),
    )(x, modt, w_in, cos_t, sin_t, ws_stack, bs_full, gn, gmat)


def _na_kernel(*refs, n_sub, n_latent):
    q_ref, qn_ref, kt_ref, v_ref = refs[:4]
    bias_refs = refs[4:5 + n_sub]
    o_ref = refs[5 + n_sub]
    s_refs = refs[6 + n_sub:]
    assert len(s_refs) == n_sub >= 2
    first = pl.program_id(2) * n_sub
    n_groups = n_latent // ATT_Q_TILE
    rows = n_latent // GRID_W
    lane = lax.broadcasted_iota(jnp.int32, (1, LANES), 1)

    def window_start(grp):
        base_row = jnp.clip(grp * NA_Q_ROWS - NA_WIN_ROWS // 2, 0, rows - NA_K_ROWS)
        return pl.multiple_of(base_row * GRID_W, ATT_Q_TILE)

    def write_scores(s_ref, q, bias, grp):
        qs = jnp.concatenate([jnp.where(lane < HEAD_DIM, q, jnp.zeros_like(q)),
                              jnp.where(lane >= HEAD_DIM, q, jnp.zeros_like(q))], axis=0)
        sw = jnp.dot(qs, kt_ref[0, :, pl.ds(window_start(grp), NA_K_TOK)],
                     preferred_element_type=F32)
        s_ref[:, 0:NA_K_TOK] = sw + bias.reshape(2 * ATT_Q_TILE, NA_K_TOK)
        s_ref[:, NA_K_TOK:NA_K_TOK + CTX_LEN] = jnp.dot(qs, kt_ref[0, :, n_latent:],
                                                        preferred_element_type=F32)

    def softmax_pv(s_ref, grp, out_rows):
        s = s_ref[:, 0:NA_K_TOK + CTX_LEN]
        p = jnp.exp2(s - s.max(axis=-1, keepdims=True))
        l = p.sum(axis=-1, keepdims=True)
        p = p.astype(BF16)
        o = (jnp.dot(p[:, 0:NA_K_TOK], v_ref[0, pl.ds(window_start(grp), NA_K_TOK), :],
                     preferred_element_type=F32)
             + jnp.dot(p[:, NA_K_TOK:], v_ref[0, n_latent:, :], preferred_element_type=F32)) / l
        o_ref[0, out_rows, :] = jnp.where(lane < HEAD_DIM, o[:ATT_Q_TILE],
                                          o[ATT_Q_TILE:]).astype(BF16)

    sub_rows = lambda sub: slice(sub * ATT_Q_TILE, (sub + 1) * ATT_Q_TILE)

    @pl.when(first == 0)
    def _():
        write_scores(s_refs[0], q_ref[0, sub_rows(0), :], bias_refs[0][0], first)

    for sub in range(n_sub):
        q_next = q_ref[0, sub_rows(sub + 1), :] if sub + 1 < n_sub else qn_ref[0]
        write_scores(s_refs[(sub + 1) % n_sub], q_next, bias_refs[sub + 1][0],
                     jnp.minimum(first + sub + 1, n_groups))
        softmax_pv(s_refs[sub], first + sub, sub_rows(sub))


def _na_attention(za, kat, bias, *, n_latent, ctx_queries):
    b, t, _ = za.shape
    n_groups = n_latent // ATT_Q_TILE
    n_total = n_groups + 1 if ctx_queries else n_groups
    n_sub = next(n for n in NA_SUB_GROUPS if n_total % n == 0)
    assert t == n_latent + CTX_LEN and CTX_LEN == ATT_Q_TILE
    assert n_groups >= 3 and n_latent // GRID_W >= NA_K_ROWS

    def kind(g):
        return jnp.where(g == 0, 0, jnp.where(g < n_groups - 1, 1, jnp.where(g == n_groups - 1, 2, 3)))

    group_at = lambda j, off: jnp.minimum(j * n_sub + off, n_groups)
    bias_spec = lambda off: pl.BlockSpec((1, 2, ATT_Q_TILE, NA_K_TOK),
                                         lambda i, hp, j: (kind(group_at(j, off)), hp, 0, 0))
    score_buf = _score_scratch((2 * ATT_Q_TILE,), NA_K_TOK + CTX_LEN)
    return pl.pallas_call(
        functools.partial(_na_kernel, n_sub=n_sub, n_latent=n_latent),
        grid=(b, NA_HEADS // 2, n_total // n_sub),
        in_specs=[
            pl.BlockSpec((1, n_sub * ATT_Q_TILE, LANES), lambda i, hp, j: (i, j, hp)),
            pl.BlockSpec((1, ATT_Q_TILE, LANES), lambda i, hp, j: (i, group_at(j, n_sub), hp)),
            pl.BlockSpec((1, LANES, t), lambda i, hp, j: (i, hp, 0)),
            pl.BlockSpec((1, t, LANES), lambda i, hp, j: (i, 0, 2 + hp)),
        ] + [bias_spec(off) for off in range(n_sub + 1)],
        out_specs=pl.BlockSpec((1, n_sub * ATT_Q_TILE, LANES), lambda i, hp, j: (i, j, hp)),
        out_shape=jax.ShapeDtypeStruct((b, n_total * ATT_Q_TILE, D_A), BF16),
        scratch_shapes=[score_buf] * n_sub,
        name="na_attention",
        compiler_params=_params(("arbitrary", "arbitrary", "arbitrary")),
    )(za, za, kat, za, *([bias] * (n_sub + 1)))


NA_BIAS_PAD = NA_WIN_ROWS // 2
NA_BIAS_W = 11 * LANES


def _na_bias_plan(n_latent):
    rows = n_latent // GRID_W
    n_groups = n_latent // ATT_Q_TILE
    plan = []
    for g in (0, 1, n_groups - 1):
        base = int(np.clip(g * NA_Q_ROWS - NA_WIN_ROWS // 2, 0, rows - NA_K_ROWS))
        kind = []
        for i in range(NA_Q_ROWS):
            r = g * NA_Q_ROWS + i
            start = int(np.clip(r - NA_WIN_ROWS // 2, 0, rows - NA_WIN_ROWS))
            e0 = base - r + NA_WIN_ROWS - 1 + NA_BIAS_PAD
            assert 0 <= e0 and (e0 - e0 % 2) * GRID_W + NA_K_TOK <= NA_BIAS_W
            kind.append((e0, start - base, start - base + NA_WIN_ROWS - 1))
        plan.append(tuple(kind))
    return tuple(plan)


def _na_bias_kernel(r_ref, o_ref, *, plan):
    shape = (GRID_W, NA_BIAS_W)
    qc = lax.broadcasted_iota(jnp.int32, shape, 0)
    col = lax.broadcasted_iota(jnp.int32, shape, 1)
    kc = col % GRID_W
    dc = kc - qc + (NA_WIN_COLS - 1)
    w0 = jnp.clip(qc - NA_WIN_COLS // 2, 0, GRID_W - NA_WIN_COLS)
    col_ok = (kc >= w0) & (kc < w0 + NA_WIN_COLS)
    tables = []
    for p in range(2):
        a = col // GRID_W + (p - NA_BIAS_PAD)
        acc = jnp.zeros(shape, F32)
        for b in range(2 * NA_WIN_COLS - 1):
            acc = acc + jnp.where(dc == b, r_ref[0, p, b:b + 1, :], 0.0)
        ok = col_ok & (a >= 0) & (a <= 2 * NA_WIN_ROWS - 2)
        tables.append(jnp.where(ok, acc * LOG2_E, NEG_INF))
    kj = lax.broadcasted_iota(jnp.int32, (GRID_W, NA_K_TOK), 1) // GRID_W
    for k, kind in enumerate(plan):
        for i, (e0, jlo, jhi) in enumerate(kind):
            p = e0 % 2
            off = (e0 - p) * GRID_W
            slab = tables[p][:, off:off + NA_K_TOK]
            o_ref[k, 0, i * GRID_W:(i + 1) * GRID_W, :] = jnp.where(
                (kj >= jlo) & (kj <= jhi), slab, NEG_INF)
    o_ref[len(plan), 0] = jnp.full((ATT_Q_TILE, NA_K_TOK), NEG_INF, F32)


def _na_bias_tables(rpb, n_latent):
    h = rpb.shape[0]
    n_blocks = NA_BIAS_W // GRID_W + 2
    rp = jnp.pad(rpb, ((0, 0), (NA_BIAS_PAD, n_blocks - NA_BIAS_PAD - rpb.shape[1]), (0, 0)))
    rep = jnp.transpose(jnp.repeat(rp, GRID_W, axis=1), (0, 2, 1))
    r = jnp.stack([rep[:, :, p * GRID_W:p * GRID_W + NA_BIAS_W] for p in range(2)], axis=1)
    n_dc = rpb.shape[2]
    return pl.pallas_call(
        functools.partial(_na_bias_kernel, plan=_na_bias_plan(n_latent)),
        grid=(h,),
        in_specs=[pl.BlockSpec((1, 2, n_dc, NA_BIAS_W), lambda i: (i, 0, 0, 0))],
        out_specs=pl.BlockSpec((4, 1, ATT_Q_TILE, NA_K_TOK), lambda i: (0, i, 0, 0)),
        out_shape=jax.ShapeDtypeStruct((4, h, ATT_Q_TILE, NA_K_TOK), F32),
        name="na_bias",
        compiler_params=_params(("arbitrary",)),
    )(r)


def _kv_chunk_bounds(t):
    head, tail = DA_KV_RAMP_UP, DA_KV_RAMP_DOWN
    middle = t - sum(head) - sum(tail)
    if middle < 0 or middle % MXU_DIM_V7X:
        assert t % DA_KV_SMALL_CHUNK == 0
        sizes = (DA_KV_SMALL_CHUNK,) * (t // DA_KV_SMALL_CHUNK)
    else:
        units = middle // MXU_DIM_V7X
        n_mid = -(-units // (DA_KV_CHUNK // MXU_DIM_V7X))
        n_mid += n_mid % 2
        mid = tuple((units // n_mid + (j < units % n_mid)) * MXU_DIM_V7X
                    for j in range(n_mid)) if n_mid else ()
        sizes = head + mid + tail
    assert max(sizes) <= DA_KV_CHUNK and len(sizes) % 2 == 0
    return tuple(int(v) for v in np.cumsum((0,) + sizes))


def _diff_kernel(*refs, tq, n_keys, lambda_init, latent):
    if latent:
        qt_ref, qtn_ref, k_ref, vt_ref, lqk_ref, subln_ref, o_ref, s_ref, smax_ref = refs
    else:
        qt_ref, k_ref, vt_ref, lqk_ref, subln_ref, _, o_ref = refs
    n_tiles = qt_ref.shape[2] // tq
    feat = lax.broadcasted_iota(jnp.int32, (LANES, 1), 0)

    def stack_components(qt):
        return jnp.concatenate([jnp.where(feat < DA_QK_DIM, qt, jnp.zeros_like(qt)),
                                jnp.where(feat >= DA_QK_DIM, qt, jnp.zeros_like(qt))], axis=1)

    tile_cols = lambda u: slice(u * tq, (u + 1) * tq)
    lqk = lqk_ref[...]
    lam = (jnp.exp(jnp.sum(lqk[0:1] * lqk[1:2], axis=-1, keepdims=True))
           - jnp.exp(jnp.sum(lqk[2:3] * lqk[3:4], axis=-1, keepdims=True)) + lambda_init)

    def col_reduce(x, op):
        r, c = x.shape
        groups = 8 if r % 64 == 0 else 1
        x = x.reshape(groups, r // (8 * groups), 8, c)
        return op(op(op(x, axis=1), axis=0), axis=0, keepdims=True)

    def scores(k, q):
        s = jnp.dot(k, q, preferred_element_type=F32)
        return s, col_reduce(s, jnp.max)

    def step(s, s_max, vt1, state):
        m, acc = state
        m_new = jnp.maximum(m, s_max)
        alpha = jnp.exp2(m - m_new)
        p = jnp.exp2(s - m_new)
        acc = alpha * acc + jnp.dot(vt1, p.astype(BF16), preferred_element_type=F32)
        return (m_new, acc)

    def init():
        return (jnp.full((1, 2 * tq), NEG_INF, F32), jnp.zeros((DA_VT_ROWS, 2 * tq), F32))

    def finish(state, u):
        _, acc = state
        o = acc[:DA_V_DIM] / acc[DA_V_DIM:DA_V_DIM + 1]
        o = o[:, :tq] - lam * o[:, tq:]
        o = o * lax.rsqrt(jnp.mean(o * o, axis=0, keepdims=True) + EPS)
        o = o * (subln_ref[...] * (1.0 - lambda_init))
        o_ref[0, tile_cols(u), :] = o.T.astype(BF16)

    if not latent:
        s, s_max = scores(k_ref[0], stack_components(qt_ref[0]))
        finish(step(s, s_max, vt_ref[0], init()), 0)
        return

    bounds = _kv_chunk_bounds(n_keys)
    chunk = lambda c: slice(bounds[c], bounds[c + 1])
    size = lambda c: bounds[c + 1] - bounds[c]
    n_chunks = len(bounds) - 1
    assert n_chunks % 2 == 0

    slot = lambda c, n_rows: (c % 2, slice(0, n_rows), slice(0, 2 * tq))

    @pl.when(pl.program_id(2) == 0)
    def _():
        s_ref[slot(0, size(0))], smax_ref[...] = scores(
            k_ref[0, chunk(0), :], stack_components(qt_ref[0, :, tile_cols(0)]))

    s_max = smax_ref[...]
    for u in range(n_tiles):
        qst = stack_components(qt_ref[0, :, tile_cols(u)])
        state = init()
        for c in range(n_chunks):
            if c + 1 < n_chunks:
                s_ref[slot(c + 1, size(c + 1))], next_max = scores(k_ref[0, chunk(c + 1), :], qst)
            else:
                q_next = qt_ref[0, :, tile_cols(u + 1)] if u + 1 < n_tiles else qtn_ref[0]
                s_ref[slot(0, size(0))], next_max = scores(k_ref[0, chunk(0), :],
                                                           stack_components(q_next))
            state = step(s_ref[slot(c, size(c))], s_max, vt_ref[0, :, chunk(c)], state)
            s_max = next_max
        finish(state, u)
    smax_ref[...] = s_max


def _diff_attention(kd, qt, vt, lqk, subln, *, n_latent, lambda_init, ctx_queries):
    b, t, _ = kd.shape
    tq = DA_Q_TILE
    n_q_tiles = n_latent // tq
    n_tiles = next(n for n in DA_TILES_PER_STEP if n_q_tiles % n == 0)
    step_q = n_tiles * tq
    consts = [pl.BlockSpec((4, DA_QK_DIM), lambda *_: (0, 0)),
              pl.BlockSpec((DA_V_DIM, 1), lambda *_: (0, 0))]
    out_rows = t if ctx_queries else n_latent
    oc = pl.pallas_call(
        functools.partial(_diff_kernel, tq=tq, n_keys=t, lambda_init=lambda_init, latent=True),
        grid=(b, DA_HEADS, n_q_tiles // n_tiles),
        in_specs=[
            pl.BlockSpec((1, LANES, step_q), lambda bi, h, i: (bi, h, i)),
            pl.BlockSpec((1, LANES, tq),
                         lambda bi, h, i: (bi, h, jnp.minimum((i + 1) * n_tiles, n_q_tiles - 1))),
            pl.BlockSpec((1, t, LANES), lambda bi, h, i: (bi, 0, h)),
            pl.BlockSpec((1, DA_VT_ROWS, t), lambda bi, h, i: (bi, h, 0)),
        ] + consts,
        out_specs=pl.BlockSpec((1, step_q, LANES), lambda bi, h, i: (bi, i, h)),
        out_shape=jax.ShapeDtypeStruct((b, out_rows, D_C), BF16),
        scratch_shapes=[_score_scratch((2, DA_KV_CHUNK), 2 * tq), pltpu.VMEM((1, 2 * tq), F32)],
        name="diff_attention",
        compiler_params=_params(("arbitrary", "arbitrary", "arbitrary")),
    )(qt, qt, kd, vt, lqk, subln)
    if not ctx_queries:
        return oc
    ctx_blk = n_latent // CTX_LEN
    return pl.pallas_call(
        functools.partial(_diff_kernel, tq=CTX_LEN, n_keys=CTX_LEN, lambda_init=lambda_init,
                          latent=False),
        grid=(b, DA_HEADS),
        in_specs=[
            pl.BlockSpec((1, LANES, CTX_LEN), lambda bi, h: (bi, h, ctx_blk)),
            pl.BlockSpec((1, CTX_LEN, LANES), lambda bi, h: (bi, ctx_blk, h)),
            pl.BlockSpec((1, DA_VT_ROWS, CTX_LEN), lambda bi, h: (bi, h, ctx_blk)),
        ] + consts + [pl.BlockSpec(memory_space=pl.ANY)],
        out_specs=pl.BlockSpec((1, CTX_LEN, LANES), lambda bi, h: (bi, ctx_blk, h)),
        out_shape=jax.ShapeDtypeStruct((b, out_rows, D_C), BF16),
        input_output_aliases={5: 0},
        name="diff_attention_ctx",
        compiler_params=_params(("arbitrary", "arbitrary")),
    )(qt, kd, vt, lqk, subln, oc)


def _rope_tables(n_latent):
    tok = jnp.arange(n_latent)
    row = (tok // GRID_W).astype(F32)
    col = (tok % GRID_W).astype(F32)
    n_freq = DA_QK_DIM // 4
    freqs = ROPE_BASE ** (-jnp.arange(n_freq, dtype=F32) / n_freq)
    ang = jnp.concatenate([row[:, None] * freqs, col[:, None] * freqs], axis=-1)
    cos, sin = jnp.cos(ang), jnp.sin(ang)
    cos_t = jnp.concatenate([cos, cos, cos, cos], axis=-1)
    sin_t = jnp.concatenate([-sin, sin, -sin, sin], axis=-1)
    cos_t = jnp.concatenate([cos_t, jnp.ones((CTX_LEN, LANES), F32)], axis=0)
    sin_t = jnp.concatenate([sin_t, jnp.zeros((CTX_LEN, LANES), F32)], axis=0)
    return cos_t, sin_t


def kernel(x, c, ctx, c_ctx, w_ada, b_ada, ffn1_w1, ffn1_w3, ffn1_w2, w_in, w_out, na_rpb,
           gm_ws, gm_bs, gm_norm, da_lq1, da_lk1, da_lq2, da_lk2, da_subln,
           ffn2_w1, ffn2_w3, ffn2_w2, final_norm):
    b, n_latent, d = x.shape
    assert d == D_MODEL and ctx.shape[1] == CTX_LEN and b < 8
    assert n_latent % ATT_Q_TILE == 0

    cvec = jnp.zeros((8, d), F32).at[:b].set(c).at[b].set(c_ctx)
    mod = _ada(cvec, w_ada, b_ada).reshape(DEPTH, 8, N_ADA, d)
    cos_t, sin_t = _rope_tables(n_latent)
    gmat = jnp.asarray(np.kron(np.eye(GM_GROUPS), np.ones((GM_WIDTH, GM_WIDTH))), BF16)

    bf16 = lambda w: w.astype(BF16)
    ffn1 = (bf16(ffn1_w1), bf16(ffn1_w3), bf16(ffn1_w2))
    ffn2 = (bf16(ffn2_w1), bf16(ffn2_w3), bf16(ffn2_w2))
    w_in, w_out = bf16(w_in), bf16(w_out)

    xs = x
    for l in range(DEPTH):
        last = l == DEPTH - 1
        lambda_init = 0.8 - 0.6 * math.exp(-0.3 * l)
        modt = jnp.stack([mod[l, :b], jnp.broadcast_to(mod[l, b], (b, N_ADA, d))], axis=1)
        bias = _na_bias_tables(na_rpb[l], n_latent)
        ws_stack = gm_ws[l].reshape(GM_GROUPS * GM_CHUNK, GM_CHUNK).astype(BF16)
        bs_full = jnp.repeat(gm_bs[l].T, GM_WIDTH, axis=1)
        gn = gm_norm[l].reshape(1, D_B)
        lqk = jnp.stack([da_lq1[l], da_lk1[l], da_lq2[l], da_lk2[l]])

        xs = _ffn(xs, modt, *ffn1, layer=l, n_latent=n_latent, mi=0, ctx=ctx if l == 0 else None)
        za, kat, ob, kd, qt, vt = _proj_in(xs, modt, w_in, cos_t, sin_t, ws_stack, bs_full, gn,
                                           gmat, layer=l, n_latent=n_latent)
        oa = _na_attention(za, kat, bias, n_latent=n_latent, ctx_queries=not last)
        oc = _diff_attention(kd, qt, vt, lqk, da_subln[l].reshape(DA_V_DIM, 1),
                             n_latent=n_latent, lambda_init=lambda_init, ctx_queries=not last)
        xs = _ffn(xs, modt, *ffn2, layer=l, n_latent=n_latent, mi=6, mix=(oa, ob, oc, w_out),
                  final_gain=final_norm if last else None)
    return xs
```

```python
import functools
import math

import numpy as np
import jax
import jax.numpy as jnp
from jax import lax
from jax.experimental import pallas as pl
from jax.experimental.pallas import tpu as pltpu

F32 = jnp.float32
BF16 = jnp.bfloat16

D_MODEL = 1024
DEPTH = 2
GRID_W = 64
CTX_LEN = 256
N_ADA = 9
EPS = 1e-6
NEG_INF = -1e30
ROPE_BASE = 10000.0
D_FF = 2816
HEAD_DIM = 64
NA_HEADS = 4
NA_WIN_ROWS = 8
NA_WIN_COLS = 16
D_A = NA_HEADS * HEAD_DIM
GM_GROUPS = 4
GM_WIDTH = 64
GM_CHUNK = 128
D_B = GM_GROUPS * GM_WIDTH
DA_HEADS = 4
DA_QK_DIM = 64
DA_V_DIM = 128
DA_VT_ROWS = DA_V_DIM + 16
D_C = DA_HEADS * DA_V_DIM
D_QK_C = DA_HEADS * 2 * DA_QK_DIM
D_IN = 3 * D_A + 2 * D_B + 2 * D_QK_C + D_C
QK_SCALE = HEAD_DIM ** -0.5
LOG2_E = math.log2(math.e)

LANES = 128
MXU_DIM_V7X = 256
VMEM_BYTES_V7X = 64 * 1024 * 1024
VMEM_LIMIT = VMEM_BYTES_V7X - 8 * 1024 * 1024

TOKEN_TILE = 256
FFN_SUB_TILES = (3, 2, 1)
NA_SUB_GROUPS = (3, 2)
ATT_Q_TILE = 256
NA_Q_ROWS = ATT_Q_TILE // GRID_W
NA_K_ROWS = NA_Q_ROWS + NA_WIN_ROWS
NA_K_TOK = NA_K_ROWS * GRID_W
DA_Q_TILE = 512
DA_TILES_PER_STEP = (2, 1)
DA_KV_CHUNK = 1536
DA_KV_RAMP_UP = (256, 768)
DA_KV_RAMP_DOWN = (768, 512, 256, 256)
DA_KV_SMALL_CHUNK = 384
FF_CHUNKS = ((0, 768), (768, 1536), (1536, 2304), (2304, D_FF))
ADA_ROWS = 8
ADA_COL_TILE = 1024


def _params(sem):
    return pltpu.CompilerParams(dimension_semantics=sem, vmem_limit_bytes=VMEM_LIMIT)


def _resident(shape):
    nd = len(shape)
    return pl.BlockSpec(shape, lambda *_: (0,) * nd, pipeline_mode=pl.Buffered(1))


def _layer_resident(shape, layer):
    nd = len(shape)
    return pl.BlockSpec((1,) + shape, lambda *_: (layer,) + (0,) * nd,
                        pipeline_mode=pl.Buffered(1))


def _rms(x):
    return x * lax.rsqrt(jnp.mean(x * x, axis=-1, keepdims=True) + EPS)


def _ada_kernel(c_ref, w_ref, b_ref, o_ref):
    c = c_ref[...]
    cs = c / (1.0 + jnp.exp(-c))
    o_ref[0] = jnp.dot(cs, w_ref[0], preferred_element_type=F32,
                       precision=lax.Precision.HIGHEST) + b_ref[0]


def _ada(cvec, w_ada, b_ada):
    depth, d, n = w_ada.shape
    tn = ADA_COL_TILE
    return pl.pallas_call(
        _ada_kernel,
        grid=(depth, n // tn),
        in_specs=[
            pl.BlockSpec((ADA_ROWS, d), lambda l, j: (0, 0)),
            pl.BlockSpec((1, d, tn), lambda l, j: (l, 0, j)),
            pl.BlockSpec((1, 1, tn), lambda l, j: (l, 0, j)),
        ],
        out_specs=pl.BlockSpec((1, ADA_ROWS, tn), lambda l, j: (l, 0, j)),
        out_shape=jax.ShapeDtypeStruct((depth, ADA_ROWS, n), F32),
        name="ada",
        compiler_params=_params(("arbitrary", "arbitrary")),
    )(cvec, w_ada, b_ada.reshape(depth, 1, n))


def _ffn_kernel(*refs, tm, n_sub, n_latent, mi, split_ctx, mix, final):
    refs = list(refs)
    if split_ctx:
        x_refs = refs[:n_sub]
        del refs[:n_sub]
        ctx_ref = refs.pop(0)
    else:
        x_ref = refs.pop(0)
    mod_ref = refs.pop(0)
    if mix:
        oa_ref, ob_ref, oc_ref, wout_ref = refs[:4]
        del refs[:4]
    w1_ref, w3_ref, w2_ref = refs[:3]
    del refs[:3]
    if final:
        gain_ref = refs.pop(0)
    (o_ref,) = refs
    for sub in range(n_sub):
        rows = slice(sub * tm, (sub + 1) * tm)
        is_ctx = (pl.program_id(1) * n_sub + sub) * tm >= n_latent

        def mod(idx):
            if final:
                return mod_ref[0, 0, idx:idx + 1, :]
            return jnp.where(is_ctx, mod_ref[0, 1, idx:idx + 1, :], mod_ref[0, 0, idx:idx + 1, :])

        if split_ctx:
            x = jnp.where(is_ctx, ctx_ref[0], x_refs[sub][0])
        else:
            x = x_ref[0, rows, :]
        if mix:
            cat = jnp.concatenate([oa_ref[0, rows, :], ob_ref[0, rows, :], oc_ref[0, rows, :]],
                                  axis=-1)
            x = x + mod(mi - 1) * jnp.dot(cat, wout_ref[0], preferred_element_type=F32)
        shift = mod(mi)
        scale1 = 1.0 + mod(mi + 1)
        half_gate = 0.5 * mod(mi + 2)
        h = (_rms(x) * scale1 + shift).astype(BF16)
        acc = jnp.zeros((tm, D_MODEL), F32)
        for c0, c1 in FF_CHUNKS:
            a = jnp.dot(h, w1_ref[0, :, c0:c1], preferred_element_type=F32)
            g = jnp.dot(h, w3_ref[0, :, c0:c1], preferred_element_type=F32)
            y = (a / (1.0 + jnp.exp(-a)) * g).astype(BF16)
            acc = acc + jnp.dot(y, w2_ref[0, c0:c1, :], preferred_element_type=F32)
        out = x + half_gate * acc
        if final:
            out = _rms(out) * gain_ref[...]
        o_ref[0, rows, :] = out


def _ffn(x, modt, w1, w3, w2, *, layer, n_latent, mi, ctx=None, mix=None, final_gain=None):
    b, _, d = x.shape
    t = n_latent + CTX_LEN
    tm = TOKEN_TILE
    final = final_gain is not None
    t_out = n_latent if final else t
    n_sub = next(n for n in FFN_SUB_TILES if (t_out // tm) % n == 0)
    step_rows = n_sub * tm
    tile = lambda width: pl.BlockSpec((1, step_rows, width), lambda i, j: (i, j, 0))
    if ctx is None:
        in_specs, args = [tile(d)], [x]
    else:
        assert tm == CTX_LEN and not final
        last_latent = n_latent // tm - 1
        sub_spec = lambda sub: pl.BlockSpec(
            (1, tm, d), lambda i, j: (i, jnp.minimum(j * n_sub + sub, last_latent), 0))
        in_specs = [sub_spec(sub) for sub in range(n_sub)]
        in_specs.append(pl.BlockSpec((1, CTX_LEN, d), lambda i, j: (i, 0, 0)))
        args = [x] * n_sub + [ctx]
    in_specs.append(pl.BlockSpec((1, 2, N_ADA, d), lambda i, j: (i, 0, 0, 0)))
    args.append(modt)
    if mix is not None:
        in_specs += [tile(D_A), tile(D_B), tile(D_C), _layer_resident((d, d), layer)]
        args += list(mix)
    in_specs += [_layer_resident((d, D_FF), layer), _layer_resident((d, D_FF), layer),
                 _layer_resident((D_FF, d), layer)]
    args += [w1, w3, w2]
    if final:
        in_specs.append(_resident((1, d)))
        args.append(final_gain.reshape(1, d))
    return pl.pallas_call(
        functools.partial(_ffn_kernel, tm=tm, n_sub=n_sub, n_latent=n_latent, mi=mi,
                          split_ctx=ctx is not None, mix=mix is not None, final=final),
        grid=(b, t_out // step_rows),
        in_specs=in_specs,
        out_specs=tile(d),
        out_shape=jax.ShapeDtypeStruct((b, t_out, d), F32),
        name="ffn_final" if final else ("ffn_mix" if mix is not None else "ffn"),
        compiler_params=_params(("arbitrary", "arbitrary")),
    )(*args)


def _swap_rope_halves(x, first_half):
    return jnp.where(first_half, pltpu.roll(x, 96, 1), pltpu.roll(x, 32, 1))


def _proj_in_kernel(*refs, tm, n_sub, n_latent):
    for sub in range(n_sub):
        _proj_in_tile(*refs, tm=tm, rows=slice(sub * tm, (sub + 1) * tm),
                      is_ctx=(pl.program_id(1) * n_sub + sub) * tm >= n_latent)


def _proj_in_tile(x_ref, mod_ref, w_ref, cos_ref, sin_ref, ws_ref, bs_ref, gn_ref, gmat_ref,
                  za_ref, kat_ref, ob_ref, kd_ref, qt_ref, vt_ref, *, tm, rows, is_ctx):
    mod = lambda idx: jnp.where(is_ctx, mod_ref[0, 1, idx:idx + 1, :], mod_ref[0, 0, idx:idx + 1, :])
    x = x_ref[0, rows, :]
    h = (_rms(x) * (1.0 + mod(4)) + mod(3)).astype(BF16)

    z_all = jnp.dot(h, w_ref[0], preferred_element_type=F32)
    project = lambda c0, width: z_all[:, c0:c0 + width]

    z = project(3 * D_A, 2 * D_B)
    u = z[:, 0:D_B]
    v = z[:, D_B:]
    u = 0.5 * u * (1.0 + lax.erf(u * (2.0 ** -0.5)))
    v = 0.5 * v * (1.0 + lax.erf(v * (2.0 ** -0.5)))
    v2 = v * v
    v2_hi = v2.astype(BF16)
    v2_lo = (v2 - v2_hi.astype(F32)).astype(BF16)
    ms = (jnp.dot(v2_hi, gmat_ref[...], preferred_element_type=F32)
          + jnp.dot(v2_lo, gmat_ref[...], preferred_element_type=F32)) * (1.0 / GM_WIDTH)
    vn = (v * lax.rsqrt(ms + EPS) * gn_ref[...]).astype(BF16)
    lane_group = lax.broadcasted_iota(jnp.int32, (1, D_B), 1) // GM_WIDTH
    for c in range(tm // GM_CHUNK):
        crows = slice(c * GM_CHUNK, (c + 1) * GM_CHUNK)
        r = jnp.dot(ws_ref[...], vn[crows, :], preferred_element_type=F32)
        s = bs_ref[...]
        for g in range(GM_GROUPS):
            s = s + jnp.where(lane_group == g, r[g * GM_CHUNK:(g + 1) * GM_CHUNK, :], 0.0)
        ob_ref[0, pl.ds(rows.start + c * GM_CHUNK, GM_CHUNK), :] = (u[crows, :] * s).astype(BF16)

    z = project(0, 3 * D_A)
    za_ref[0, rows, 0:D_A] = (z[:, 0:D_A] * (QK_SCALE * LOG2_E)).astype(BF16)
    kat_ref[0, :, rows] = z[:, D_A:2 * D_A].T.astype(BF16)
    za_ref[0, rows, D_A:2 * D_A] = z[:, 2 * D_A:3 * D_A].astype(BF16)

    o = 3 * D_A + 2 * D_B
    cos = cos_ref[rows, :]
    sin = sin_ref[rows, :]
    first_half = (lax.broadcasted_iota(jnp.int32, (1, LANES), 1) % DA_QK_DIM) < (DA_QK_DIM // 2)
    n_q_blocks = D_QK_C // LANES
    for j in range(2 * n_q_blocks):
        if j % n_q_blocks == 0:
            z = project(o + j * LANES, D_QK_C)
        zz = z[:, (j % n_q_blocks) * LANES:(j % n_q_blocks + 1) * LANES]
        r = zz * cos + _swap_rope_halves(zz, first_half) * sin
        if j < n_q_blocks:
            qt_ref[0, j * LANES:(j + 1) * LANES, rows] = (r * (QK_SCALE * LOG2_E)).T.astype(BF16)
        else:
            jk = j - n_q_blocks
            kd_ref[0, rows, jk * LANES:(jk + 1) * LANES] = r.astype(BF16)
    z = project(o + 2 * D_QK_C, D_C)
    for j in range(D_C // LANES):
        r0 = j * DA_VT_ROWS
        vt_ref[0, r0:r0 + DA_V_DIM, rows] = z[:, j * LANES:(j + 1) * LANES].T.astype(BF16)
        vt_ref[0, r0 + DA_V_DIM:r0 + DA_VT_ROWS, rows] = jnp.ones((DA_VT_ROWS - DA_V_DIM, tm), BF16)


def _proj_in(x, modt, w_in, cos_t, sin_t, ws_stack, bs_full, gn, gmat, *, layer, n_latent):
    b, t, d = x.shape
    n_sub = next(n for n in FFN_SUB_TILES if (t // TOKEN_TILE) % n == 0)
    tm = n_sub * TOKEN_TILE
    return pl.pallas_call(
        functools.partial(_proj_in_kernel, tm=TOKEN_TILE, n_sub=n_sub, n_latent=n_latent),
        grid=(b, t // tm),
        in_specs=[
            pl.BlockSpec((1, tm, d), lambda i, j: (i, j, 0)),
            pl.BlockSpec((1, 2, N_ADA, d), lambda i, j: (i, 0, 0, 0)),
            _layer_resident((d, D_IN), layer),
            pl.BlockSpec((tm, LANES), lambda i, j: (j, 0)),
            pl.BlockSpec((tm, LANES), lambda i, j: (j, 0)),
            _resident((GM_GROUPS * GM_CHUNK, GM_CHUNK)),
            _resident((GM_CHUNK, D_B)),
            _resident((1, D_B)),
            _resident((D_B, D_B)),
        ],
        out_specs=[
            pl.BlockSpec((1, tm, 2 * D_A), lambda i, j: (i, j, 0)),
            pl.BlockSpec((1, D_A, tm), lambda i, j: (i, 0, j)),
            pl.BlockSpec((1, tm, D_B), lambda i, j: (i, j, 0)),
            pl.BlockSpec((1, tm, D_QK_C), lambda i, j: (i, j, 0)),
            pl.BlockSpec((1, D_QK_C, tm), lambda i, j: (i, 0, j)),
            pl.BlockSpec((1, DA_HEADS * DA_VT_ROWS, tm), lambda i, j: (i, 0, j)),
        ],
        out_shape=[
            jax.ShapeDtypeStruct((b, t, 2 * D_A), BF16),
            jax.ShapeDtypeStruct((b, D_A, t), BF16),
            jax.ShapeDtypeStruct((b, t, D_B), BF16),
            jax.ShapeDtypeStruct((b, t, D_QK_C), BF16),
            jax.ShapeDtypeStruct((b, D_QK_C, t), BF16),
            jax.ShapeDtypeStruct((b, DA_HEADS * DA_VT_ROWS, t), BF16),
        ],
        name="proj_in",
        compiler_params=_params(("arbitrary", "arbitrary")),
    )(x, modt, w_in, cos_t, sin_t, ws_stack, bs_full, gn, gmat)


def _na_kernel(*refs, n_sub, n_latent):
    q_ref, qn_ref, kt_ref, v_ref = refs[:4]
    bias_refs = refs[4:5 + n_sub]
    o_ref = refs[5 + n_sub]
    s_refs = refs[6 + n_sub:]
    assert len(s_refs) == n_sub >= 2
    first = pl.program_id(2) * n_sub
    n_groups = n_latent // ATT_Q_TILE
    rows = n_latent // GRID_W
    lane = lax.broadcasted_iota(jnp.int32, (1, LANES), 1)

    def window_start(grp):
        base_row = jnp.clip(grp * NA_Q_ROWS - NA_WIN_ROWS // 2, 0, rows - NA_K_ROWS)
        return pl.multiple_of(base_row * GRID_W, ATT_Q_TILE)

    def write_scores(s_ref, q, bias, grp):
        qs = jnp.concatenate([jnp.where(lane < HEAD_DIM, q, jnp.zeros_like(q)),
                              jnp.where(lane >= HEAD_DIM, q, jnp.zeros_like(q))], axis=0)
        sw = jnp.dot(qs, kt_ref[0, :, pl.ds(window_start(grp), NA_K_TOK)],
                     preferred_element_type=F32)
        s_ref[:, 0:NA_K_TOK] = sw + bias.reshape(2 * ATT_Q_TILE, NA_K_TOK)
        s_ref[:, NA_K_TOK:] = jnp.dot(qs, kt_ref[0, :, n_latent:], preferred_element_type=F32)

    def softmax_pv(s_ref, grp, out_rows):
        s = s_ref[...]
        p = jnp.exp2(s - s.max(axis=-1, keepdims=True))
        l = p.sum(axis=-1, keepdims=True)
        p = p.astype(BF16)
        o = (jnp.dot(p[:, 0:NA_K_TOK], v_ref[0, pl.ds(window_start(grp), NA_K_TOK), :],
                     preferred_element_type=F32)
             + jnp.dot(p[:, NA_K_TOK:], v_ref[0, n_latent:, :], preferred_element_type=F32)) / l
        o_ref[0, out_rows, :] = jnp.where(lane < HEAD_DIM, o[:ATT_Q_TILE],
                                          o[ATT_Q_TILE:]).astype(BF16)

    sub_rows = lambda sub: slice(sub * ATT_Q_TILE, (sub + 1) * ATT_Q_TILE)

    @pl.when(first == 0)
    def _():
        write_scores(s_refs[0], q_ref[0, sub_rows(0), :], bias_refs[0][0], first)

    for sub in range(n_sub):
        q_next = q_ref[0, sub_rows(sub + 1), :] if sub + 1 < n_sub else qn_ref[0]
        write_scores(s_refs[(sub + 1) % n_sub], q_next, bias_refs[sub + 1][0],
                     jnp.minimum(first + sub + 1, n_groups))
        softmax_pv(s_refs[sub], first + sub, sub_rows(sub))


def _na_attention(za, kat, bias, *, n_latent, ctx_queries):
    b, t, _ = za.shape
    n_groups = n_latent // ATT_Q_TILE
    n_total = n_groups + 1 if ctx_queries else n_groups
    n_sub = next(n for n in NA_SUB_GROUPS if n_total % n == 0)
    assert t == n_latent + CTX_LEN and CTX_LEN == ATT_Q_TILE
    assert n_groups >= 3 and n_latent // GRID_W >= NA_K_ROWS

    def kind(g):
        return jnp.where(g == 0, 0, jnp.where(g < n_groups - 1, 1, jnp.where(g == n_groups - 1, 2, 3)))

    group_at = lambda j, off: jnp.minimum(j * n_sub + off, n_groups)
    bias_spec = lambda off: pl.BlockSpec((1, 2, ATT_Q_TILE, NA_K_TOK),
                                         lambda i, hp, j: (kind(group_at(j, off)), hp, 0, 0))
    score_buf = pltpu.VMEM((2 * ATT_Q_TILE, NA_K_TOK + CTX_LEN), F32)
    return pl.pallas_call(
        functools.partial(_na_kernel, n_sub=n_sub, n_latent=n_latent),
        grid=(b, NA_HEADS // 2, n_total // n_sub),
        in_specs=[
            pl.BlockSpec((1, n_sub * ATT_Q_TILE, LANES), lambda i, hp, j: (i, j, hp)),
            pl.BlockSpec((1, ATT_Q_TILE, LANES), lambda i, hp, j: (i, group_at(j, n_sub), hp)),
            pl.BlockSpec((1, LANES, t), lambda i, hp, j: (i, hp, 0)),
            pl.BlockSpec((1, t, LANES), lambda i, hp, j: (i, 0, 2 + hp)),
        ] + [bias_spec(off) for off in range(n_sub + 1)],
        out_specs=pl.BlockSpec((1, n_sub * ATT_Q_TILE, LANES), lambda i, hp, j: (i, j, hp)),
        out_shape=jax.ShapeDtypeStruct((b, n_total * ATT_Q_TILE, D_A), BF16),
        scratch_shapes=[score_buf] * n_sub,
        name="na_attention",
        compiler_params=_params(("arbitrary", "arbitrary", "arbitrary")),
    )(za, za, kat, za, *([bias] * (n_sub + 1)))


NA_BIAS_PAD = NA_WIN_ROWS // 2
NA_BIAS_W = 11 * LANES


def _na_bias_plan(n_latent):
    rows = n_latent // GRID_W
    n_groups = n_latent // ATT_Q_TILE
    plan = []
    for g in (0, 1, n_groups - 1):
        base = int(np.clip(g * NA_Q_ROWS - NA_WIN_ROWS // 2, 0, rows - NA_K_ROWS))
        kind = []
        for i in range(NA_Q_ROWS):
            r = g * NA_Q_ROWS + i
            start = int(np.clip(r - NA_WIN_ROWS // 2, 0, rows - NA_WIN_ROWS))
            e0 = base - r + NA_WIN_ROWS - 1 + NA_BIAS_PAD
            assert 0 <= e0 and (e0 - e0 % 2) * GRID_W + NA_K_TOK <= NA_BIAS_W
            kind.append((e0, start - base, start - base + NA_WIN_ROWS - 1))
        plan.append(tuple(kind))
    return tuple(plan)


def _na_bias_kernel(r_ref, o_ref, *, plan):
    shape = (GRID_W, NA_BIAS_W)
    qc = lax.broadcasted_iota(jnp.int32, shape, 0)
    col = lax.broadcasted_iota(jnp.int32, shape, 1)
    kc = col % GRID_W
    dc = kc - qc + (NA_WIN_COLS - 1)
    w0 = jnp.clip(qc - NA_WIN_COLS // 2, 0, GRID_W - NA_WIN_COLS)
    col_ok = (kc >= w0) & (kc < w0 + NA_WIN_COLS)
    tables = []
    for p in range(2):
        a = col // GRID_W + (p - NA_BIAS_PAD)
        acc = jnp.zeros(shape, F32)
        for b in range(2 * NA_WIN_COLS - 1):
            acc = acc + jnp.where(dc == b, r_ref[0, p, b:b + 1, :], 0.0)
        ok = col_ok & (a >= 0) & (a <= 2 * NA_WIN_ROWS - 2)
        tables.append(jnp.where(ok, acc * LOG2_E, NEG_INF))
    kj = lax.broadcasted_iota(jnp.int32, (GRID_W, NA_K_TOK), 1) // GRID_W
    for k, kind in enumerate(plan):
        for i, (e0, jlo, jhi) in enumerate(kind):
            p = e0 % 2
            off = (e0 - p) * GRID_W
            slab = tables[p][:, off:off + NA_K_TOK]
            o_ref[k, 0, i * GRID_W:(i + 1) * GRID_W, :] = jnp.where(
                (kj >= jlo) & (kj <= jhi), slab, NEG_INF)
    o_ref[len(plan), 0] = jnp.full((ATT_Q_TILE, NA_K_TOK), NEG_INF, F32)


def _na_bias_tables(rpb, n_latent):
    h = rpb.shape[0]
    n_blocks = NA_BIAS_W // GRID_W + 2
    rp = jnp.pad(rpb, ((0, 0), (NA_BIAS_PAD, n_blocks - NA_BIAS_PAD - rpb.shape[1]), (0, 0)))
    rep = jnp.transpose(jnp.repeat(rp, GRID_W, axis=1), (0, 2, 1))
    r = jnp.stack([rep[:, :, p * GRID_W:p * GRID_W + NA_BIAS_W] for p in range(2)], axis=1)
    n_dc = rpb.shape[2]
    return pl.pallas_call(
        functools.partial(_na_bias_kernel, plan=_na_bias_plan(n_latent)),
        grid=(h,),
        in_specs=[pl.BlockSpec((1, 2, n_dc, NA_BIAS_W), lambda i: (i, 0, 0, 0))],
        out_specs=pl.BlockSpec((4, 1, ATT_Q_TILE, NA_K_TOK), lambda i: (0, i, 0, 0)),
        out_shape=jax.ShapeDtypeStruct((4, h, ATT_Q_TILE, NA_K_TOK), F32),
        name="na_bias",
        compiler_params=_params(("arbitrary",)),
    )(r)


def _kv_chunk_bounds(t):
    head, tail = DA_KV_RAMP_UP, DA_KV_RAMP_DOWN
    middle = t - sum(head) - sum(tail)
    if middle < 0 or middle % MXU_DIM_V7X:
        assert t % DA_KV_SMALL_CHUNK == 0
        sizes = (DA_KV_SMALL_CHUNK,) * (t // DA_KV_SMALL_CHUNK)
    else:
        units = middle // MXU_DIM_V7X
        n_mid = -(-units // (DA_KV_CHUNK // MXU_DIM_V7X))
        n_mid += n_mid % 2
        mid = tuple((units // n_mid + (j < units % n_mid)) * MXU_DIM_V7X
                    for j in range(n_mid)) if n_mid else ()
        sizes = head + mid + tail
    assert max(sizes) <= DA_KV_CHUNK and len(sizes) % 2 == 0
    return tuple(int(v) for v in np.cumsum((0,) + sizes))


def _diff_kernel(*refs, tq, n_keys, lambda_init, latent):
    if latent:
        qt_ref, qtn_ref, k_ref, vt_ref, lqk_ref, subln_ref, o_ref, s_ref, smax_ref = refs
    else:
        qt_ref, k_ref, vt_ref, lqk_ref, subln_ref, _, o_ref = refs
    n_tiles = qt_ref.shape[2] // tq
    feat = lax.broadcasted_iota(jnp.int32, (LANES, 1), 0)

    def stack_components(qt):
        return jnp.concatenate([jnp.where(feat < DA_QK_DIM, qt, jnp.zeros_like(qt)),
                                jnp.where(feat >= DA_QK_DIM, qt, jnp.zeros_like(qt))], axis=1)

    tile_cols = lambda u: slice(u * tq, (u + 1) * tq)
    lqk = lqk_ref[...]
    lam = (jnp.exp(jnp.sum(lqk[0:1] * lqk[1:2], axis=-1, keepdims=True))
           - jnp.exp(jnp.sum(lqk[2:3] * lqk[3:4], axis=-1, keepdims=True)) + lambda_init)

    def col_reduce(x, op):
        r, c = x.shape
        groups = 8 if r % 64 == 0 else 1
        x = x.reshape(groups, r // (8 * groups), 8, c)
        return op(op(op(x, axis=1), axis=0), axis=0, keepdims=True)

    def scores(k, q):
        s = jnp.dot(k, q, preferred_element_type=F32)
        return s, col_reduce(s, jnp.max)

    def step(s, s_max, vt1, state):
        m, acc = state
        m_new = jnp.maximum(m, s_max)
        alpha = jnp.exp2(m - m_new)
        p = jnp.exp2(s - m_new)
        acc = alpha * acc + jnp.dot(vt1, p.astype(BF16), preferred_element_type=F32)
        return (m_new, acc)

    def init():
        return (jnp.full((1, 2 * tq), NEG_INF, F32), jnp.zeros((DA_VT_ROWS, 2 * tq), F32))

    def finish(state, u):
        _, acc = state
        o = acc[:DA_V_DIM] / acc[DA_V_DIM:DA_V_DIM + 1]
        o = o[:, :tq] - lam * o[:, tq:]
        o = o * lax.rsqrt(jnp.mean(o * o, axis=0, keepdims=True) + EPS)
        o = o * (subln_ref[...] * (1.0 - lambda_init))
        o_ref[0, tile_cols(u), :] = o.T.astype(BF16)

    if not latent:
        s, s_max = scores(k_ref[0], stack_components(qt_ref[0]))
        finish(step(s, s_max, vt_ref[0], init()), 0)
        return

    bounds = _kv_chunk_bounds(n_keys)
    chunk = lambda c: slice(bounds[c], bounds[c + 1])
    size = lambda c: bounds[c + 1] - bounds[c]
    n_chunks = len(bounds) - 1
    assert n_chunks % 2 == 0

    slot = lambda c, n_rows: (c % 2, slice(0, n_rows))

    @pl.when(pl.program_id(2) == 0)
    def _():
        s_ref[slot(0, size(0))], smax_ref[...] = scores(
            k_ref[0, chunk(0), :], stack_components(qt_ref[0, :, tile_cols(0)]))

    s_max = smax_ref[...]
    for u in range(n_tiles):
        qst = stack_components(qt_ref[0, :, tile_cols(u)])
        state = init()
        for c in range(n_chunks):
            if c + 1 < n_chunks:
                s_ref[slot(c + 1, size(c + 1))], next_max = scores(k_ref[0, chunk(c + 1), :], qst)
            else:
                q_next = qt_ref[0, :, tile_cols(u + 1)] if u + 1 < n_tiles else qtn_ref[0]
                s_ref[slot(0, size(0))], next_max = scores(k_ref[0, chunk(0), :],
                                                           stack_components(q_next))
            state = step(s_ref[slot(c, size(c))], s_max, vt_ref[0, :, chunk(c)], state)
            s_max = next_max
        finish(state, u)
    smax_ref[...] = s_max


def _diff_attention(kd, qt, vt, lqk, subln, *, n_latent, lambda_init, ctx_queries):
    b, t, _ = kd.shape
    tq = DA_Q_TILE
    n_q_tiles = n_latent // tq
    n_tiles = next(n for n in DA_TILES_PER_STEP if n_q_tiles % n == 0)
    step_q = n_tiles * tq
    consts = [pl.BlockSpec((4, DA_QK_DIM), lambda *_: (0, 0)),
              pl.BlockSpec((DA_V_DIM, 1), lambda *_: (0, 0))]
    out_rows = t if ctx_queries else n_latent
    oc = pl.pallas_call(
        functools.partial(_diff_kernel, tq=tq, n_keys=t, lambda_init=lambda_init, latent=True),
        grid=(b, DA_HEADS, n_q_tiles // n_tiles),
        in_specs=[
            pl.BlockSpec((1, LANES, step_q), lambda bi, h, i: (bi, h, i)),
            pl.BlockSpec((1, LANES, tq),
                         lambda bi, h, i: (bi, h, jnp.minimum((i + 1) * n_tiles, n_q_tiles - 1))),
            pl.BlockSpec((1, t, LANES), lambda bi, h, i: (bi, 0, h)),
            pl.BlockSpec((1, DA_VT_ROWS, t), lambda bi, h, i: (bi, h, 0)),
        ] + consts,
        out_specs=pl.BlockSpec((1, step_q, LANES), lambda bi, h, i: (bi, i, h)),
        out_shape=jax.ShapeDtypeStruct((b, out_rows, D_C), BF16),
        scratch_shapes=[pltpu.VMEM((2, DA_KV_CHUNK, 2 * tq), F32), pltpu.VMEM((1, 2 * tq), F32)],
        name="diff_attention",
        compiler_params=_params(("arbitrary", "arbitrary", "arbitrary")),
    )(qt, qt, kd, vt, lqk, subln)
    if not ctx_queries:
        return oc
    ctx_blk = n_latent // CTX_LEN
    return pl.pallas_call(
        functools.partial(_diff_kernel, tq=CTX_LEN, n_keys=CTX_LEN, lambda_init=lambda_init,
                          latent=False),
        grid=(b, DA_HEADS),
        in_specs=[
            pl.BlockSpec((1, LANES, CTX_LEN), lambda bi, h: (bi, h, ctx_blk)),
            pl.BlockSpec((1, CTX_LEN, LANES), lambda bi, h: (bi, ctx_blk, h)),
            pl.BlockSpec((1, DA_VT_ROWS, CTX_LEN), lambda bi, h: (bi, h, ctx_blk)),
        ] + consts + [pl.BlockSpec(memory_space=pl.ANY)],
        out_specs=pl.BlockSpec((1, CTX_LEN, LANES), lambda bi, h: (bi, ctx_blk, h)),
        out_shape=jax.ShapeDtypeStruct((b, out_rows, D_C), BF16),
        input_output_aliases={5: 0},
        name="diff_attention_ctx",
        compiler_params=_params(("arbitrary", "arbitrary")),
    )(qt, kd, vt, lqk, subln, oc)


def _rope_tables(n_latent):
    tok = jnp.arange(n_latent)
    row = (tok // GRID_W).astype(F32)
    col = (tok % GRID_W).astype(F32)
    n_freq = DA_QK_DIM // 4
    freqs = ROPE_BASE ** (-jnp.arange(n_freq, dtype=F32) / n_freq)
    ang = jnp.concatenate([row[:, None] * freqs, col[:, None] * freqs], axis=-1)
    cos, sin = jnp.cos(ang), jnp.sin(ang)
    cos_t = jnp.concatenate([cos, cos, cos, cos], axis=-1)
    sin_t = jnp.concatenate([-sin, sin, -sin, sin], axis=-1)
    cos_t = jnp.concatenate([cos_t, jnp.ones((CTX_LEN, LANES), F32)], axis=0)
    sin_t = jnp.concatenate([sin_t, jnp.zeros((CTX_LEN, LANES), F32)], axis=0)
    return cos_t, sin_t


def kernel(x, c, ctx, c_ctx, w_ada, b_ada, ffn1_w1, ffn1_w3, ffn1_w2, w_in, w_out, na_rpb,
           gm_ws, gm_bs, gm_norm, da_lq1, da_lk1, da_lq2, da_lk2, da_subln,
           ffn2_w1, ffn2_w3, ffn2_w2, final_norm):
    b, n_latent, d = x.shape
    assert d == D_MODEL and ctx.shape[1] == CTX_LEN and b < ADA_ROWS
    assert n_latent % DA_Q_TILE == 0

    cvec = jnp.zeros((ADA_ROWS, d), F32).at[:b].set(c).at[b].set(c_ctx)
    mod = _ada(cvec, w_ada, b_ada).reshape(DEPTH, ADA_ROWS, N_ADA, d)
    cos_t, sin_t = _rope_tables(n_latent)
    gmat = jnp.asarray(np.kron(np.eye(GM_GROUPS), np.ones((GM_WIDTH, GM_WIDTH))), BF16)

    bf16 = lambda w: w.astype(BF16)
    ffn1 = (bf16(ffn1_w1), bf16(ffn1_w3), bf16(ffn1_w2))
    ffn2 = (bf16(ffn2_w1), bf16(ffn2_w3), bf16(ffn2_w2))
    w_in, w_out = bf16(w_in), bf16(w_out)

    xs = x
    for l in range(DEPTH):
        last = l == DEPTH - 1
        lambda_init = 0.8 - 0.6 * math.exp(-0.3 * l)
        modt = jnp.stack([mod[l, :b], jnp.broadcast_to(mod[l, b], (b, N_ADA, d))], axis=1)
        bias = _na_bias_tables(na_rpb[l], n_latent)
        ws_stack = gm_ws[l].reshape(GM_GROUPS * GM_CHUNK, GM_CHUNK).astype(BF16)
        bs_full = jnp.repeat(gm_bs[l].T, GM_WIDTH, axis=1)
        gn = gm_norm[l].reshape(1, D_B)
        lqk = jnp.stack([da_lq1[l], da_lk1[l], da_lq2[l], da_lk2[l]])

        xs = _ffn(xs, modt, *ffn1, layer=l, n_latent=n_latent, mi=0, ctx=ctx if l == 0 else None)
        za, kat, ob, kd, qt, vt = _proj_in(xs, modt, w_in, cos_t, sin_t, ws_stack, bs_full, gn,
                                           gmat, layer=l, n_latent=n_latent)
        oa = _na_attention(za, kat, bias, n_latent=n_latent, ctx_queries=not last)
        oc = _diff_attention(kd, qt, vt, lqk, da_subln[l].reshape(DA_V_DIM, 1),
                             n_latent=n_latent, lambda_init=lambda_init, ctx_queries=not last)
        xs = _ffn(xs, modt, *ffn2, layer=l, n_latent=n_latent, mi=6, mix=(oa, ob, oc, w_out),
                  final_gain=final_norm if last else None)
    return xs
```

```python
import functools
import math

import numpy as np
import jax
import jax.numpy as jnp
from jax import lax
from jax.experimental import pallas as pl
from jax.experimental.pallas import tpu as pltpu

F32 = jnp.float32
BF16 = jnp.bfloat16

D_MODEL = 1024
DEPTH = 2
GRID_W = 64
CTX_LEN = 256
N_ADA = 9
EPS = 1e-6
NEG_INF = -1e30
ROPE_BASE = 10000.0
D_FF = 2816
HEAD_DIM = 64
NA_HEADS = 4
NA_WIN_ROWS = 8
NA_WIN_COLS = 16
D_A = NA_HEADS * HEAD_DIM
GM_GROUPS = 4
GM_WIDTH = 64
GM_CHUNK = 128
D_B = GM_GROUPS * GM_WIDTH
DA_HEADS = 4
DA_QK_DIM = 64
DA_V_DIM = 128
DA_VT_ROWS = DA_V_DIM + 16
D_C = DA_HEADS * DA_V_DIM
D_QK_C = DA_HEADS * 2 * DA_QK_DIM
D_IN = 3 * D_A + 2 * D_B + 2 * D_QK_C + D_C
QK_SCALE = HEAD_DIM ** -0.5
LOG2_E = math.log2(math.e)

LANES = 128
MXU_DIM_V7X = 256
VMEM_BYTES_V7X = 64 * 1024 * 1024
VMEM_LIMIT = VMEM_BYTES_V7X - 8 * 1024 * 1024

TOKEN_TILE = 256
FFN_SUB_TILES = (3, 2, 1)
NA_SUB_GROUPS = (3, 2)
ATT_Q_TILE = 256
NA_Q_ROWS = ATT_Q_TILE // GRID_W
NA_K_ROWS = NA_Q_ROWS + NA_WIN_ROWS
NA_K_TOK = NA_K_ROWS * GRID_W
DA_Q_TILE = 512
DA_TILES_PER_STEP = (2, 1)
DA_KV_CHUNK = 1536
DA_KV_RAMP_UP = (256, 768)
DA_KV_RAMP_DOWN = (768, 512, 512)
DA_PREFETCH = 2
DA_KV_SMALL_CHUNK = 384
FF_CHUNKS = ((0, 768), (768, 1536), (1536, 2304), (2304, D_FF))
ADA_ROWS = 8
ADA_COL_TILE = 1024


def _params(sem):
    return pltpu.CompilerParams(dimension_semantics=sem, vmem_limit_bytes=VMEM_LIMIT)


def _resident(shape):
    nd = len(shape)
    return pl.BlockSpec(shape, lambda *_: (0,) * nd, pipeline_mode=pl.Buffered(1))


def _layer_resident(shape, layer):
    nd = len(shape)
    return pl.BlockSpec((1,) + shape, lambda *_: (layer,) + (0,) * nd,
                        pipeline_mode=pl.Buffered(1))


def _rms(x):
    return x * lax.rsqrt(jnp.mean(x * x, axis=-1, keepdims=True) + EPS)


def _ada_kernel(c_ref, w_ref, b_ref, o_ref):
    c = c_ref[...]
    cs = c / (1.0 + jnp.exp(-c))
    o_ref[0] = jnp.dot(cs, w_ref[0], preferred_element_type=F32,
                       precision=lax.Precision.HIGHEST) + b_ref[0]


def _ada(cvec, w_ada, b_ada):
    depth, d, n = w_ada.shape
    tn = ADA_COL_TILE
    return pl.pallas_call(
        _ada_kernel,
        grid=(depth, n // tn),
        in_specs=[
            pl.BlockSpec((ADA_ROWS, d), lambda l, j: (0, 0)),
            pl.BlockSpec((1, d, tn), lambda l, j: (l, 0, j)),
            pl.BlockSpec((1, 1, tn), lambda l, j: (l, 0, j)),
        ],
        out_specs=pl.BlockSpec((1, ADA_ROWS, tn), lambda l, j: (l, 0, j)),
        out_shape=jax.ShapeDtypeStruct((depth, ADA_ROWS, n), F32),
        name="ada",
        compiler_params=_params(("arbitrary", "arbitrary")),
    )(cvec, w_ada, b_ada.reshape(depth, 1, n))


def _ffn_kernel(*refs, tm, n_sub, n_latent, mi, split_ctx, mix, final):
    refs = list(refs)
    if split_ctx:
        x_refs = refs[:n_sub]
        del refs[:n_sub]
        ctx_ref = refs.pop(0)
    else:
        x_ref = refs.pop(0)
    mod_ref = refs.pop(0)
    if mix:
        oa_ref, ob_ref, oc_ref, wout_ref = refs[:4]
        del refs[:4]
    w1_ref, w3_ref, w2_ref = refs[:3]
    del refs[:3]
    if final:
        gain_ref = refs.pop(0)
    (o_ref,) = refs
    for sub in range(n_sub):
        rows = slice(sub * tm, (sub + 1) * tm)
        is_ctx = (pl.program_id(1) * n_sub + sub) * tm >= n_latent

        def mod(idx):
            if final:
                return mod_ref[0, 0, idx:idx + 1, :]
            return jnp.where(is_ctx, mod_ref[0, 1, idx:idx + 1, :], mod_ref[0, 0, idx:idx + 1, :])

        if split_ctx:
            x = jnp.where(is_ctx, ctx_ref[0], x_refs[sub][0])
        else:
            x = x_ref[0, rows, :]
        if mix:
            cat = jnp.concatenate([oa_ref[0, rows, :], ob_ref[0, rows, :], oc_ref[0, rows, :]],
                                  axis=-1)
            x = x + mod(mi - 1) * jnp.dot(cat, wout_ref[0], preferred_element_type=F32)
        shift = mod(mi)
        scale1 = 1.0 + mod(mi + 1)
        half_gate = 0.5 * mod(mi + 2)
        h = (_rms(x) * scale1 + shift).astype(BF16)
        acc = jnp.zeros((tm, D_MODEL), F32)
        for c0, c1 in FF_CHUNKS:
            a = jnp.dot(h, w1_ref[0, :, c0:c1], preferred_element_type=F32)
            g = jnp.dot(h, w3_ref[0, :, c0:c1], preferred_element_type=F32)
            y = (a / (1.0 + jnp.exp(-a)) * g).astype(BF16)
            acc = acc + jnp.dot(y, w2_ref[0, c0:c1, :], preferred_element_type=F32)
        out = x + half_gate * acc
        if final:
            out = _rms(out) * gain_ref[...]
        o_ref[0, rows, :] = out


def _ffn(x, modt, w1, w3, w2, *, layer, n_latent, mi, ctx=None, mix=None, final_gain=None):
    b, _, d = x.shape
    t = n_latent + CTX_LEN
    tm = TOKEN_TILE
    final = final_gain is not None
    t_out = n_latent if final else t
    n_sub = next(n for n in FFN_SUB_TILES if (t_out // tm) % n == 0)
    step_rows = n_sub * tm
    tile = lambda width: pl.BlockSpec((1, step_rows, width), lambda i, j: (i, j, 0))
    if ctx is None:
        in_specs, args = [tile(d)], [x]
    else:
        assert tm == CTX_LEN and not final
        last_latent = n_latent // tm - 1
        sub_spec = lambda sub: pl.BlockSpec(
            (1, tm, d), lambda i, j: (i, jnp.minimum(j * n_sub + sub, last_latent), 0))
        in_specs = [sub_spec(sub) for sub in range(n_sub)]
        in_specs.append(pl.BlockSpec((1, CTX_LEN, d), lambda i, j: (i, 0, 0)))
        args = [x] * n_sub + [ctx]
    in_specs.append(pl.BlockSpec((1, 2, N_ADA, d), lambda i, j: (i, 0, 0, 0)))
    args.append(modt)
    if mix is not None:
        in_specs += [tile(D_A), tile(D_B), tile(D_C), _layer_resident((d, d), layer)]
        args += list(mix)
    in_specs += [_layer_resident((d, D_FF), layer), _layer_resident((d, D_FF), layer),
                 _layer_resident((D_FF, d), layer)]
    args += [w1, w3, w2]
    if final:
        in_specs.append(_resident((1, d)))
        args.append(final_gain.reshape(1, d))
    return pl.pallas_call(
        functools.partial(_ffn_kernel, tm=tm, n_sub=n_sub, n_latent=n_latent, mi=mi,
                          split_ctx=ctx is not None, mix=mix is not None, final=final),
        grid=(b, t_out // step_rows),
        in_specs=in_specs,
        out_specs=tile(d),
        out_shape=jax.ShapeDtypeStruct((b, t_out, d), F32),
        name="ffn_final" if final else ("ffn_mix" if mix is not None else "ffn"),
        compiler_params=_params(("arbitrary", "arbitrary")),
    )(*args)


def _swap_rope_halves(x, first_half):
    return jnp.where(first_half, pltpu.roll(x, 96, 1), pltpu.roll(x, 32, 1))


def _proj_in_kernel(*refs, tm, n_sub, n_latent):
    for sub in range(n_sub):
        _proj_in_tile(*refs, tm=tm, rows=slice(sub * tm, (sub + 1) * tm),
                      is_ctx=(pl.program_id(1) * n_sub + sub) * tm >= n_latent)


def _proj_in_tile(x_ref, mod_ref, w_ref, cos_ref, sin_ref, ws_ref, bs_ref, gn_ref, gmat_ref,
                  za_ref, kat_ref, ob_ref, kd_ref, qt_ref, vt_ref, *, tm, rows, is_ctx):
    mod = lambda idx: jnp.where(is_ctx, mod_ref[0, 1, idx:idx + 1, :], mod_ref[0, 0, idx:idx + 1, :])
    x = x_ref[0, rows, :]
    h = (_rms(x) * (1.0 + mod(4)) + mod(3)).astype(BF16)

    z_all = jnp.dot(h, w_ref[0], preferred_element_type=F32)
    project = lambda c0, width: z_all[:, c0:c0 + width]

    z = project(3 * D_A, 2 * D_B)
    u = z[:, 0:D_B]
    v = z[:, D_B:]
    u = 0.5 * u * (1.0 + lax.erf(u * (2.0 ** -0.5)))
    v = 0.5 * v * (1.0 + lax.erf(v * (2.0 ** -0.5)))
    v2 = v * v
    v2_hi = v2.astype(BF16)
    v2_lo = (v2 - v2_hi.astype(F32)).astype(BF16)
    ms = (jnp.dot(v2_hi, gmat_ref[...], preferred_element_type=F32)
          + jnp.dot(v2_lo, gmat_ref[...], preferred_element_type=F32)) * (1.0 / GM_WIDTH)
    vn = (v * lax.rsqrt(ms + EPS) * gn_ref[...]).astype(BF16)
    lane_group = lax.broadcasted_iota(jnp.int32, (1, D_B), 1) // GM_WIDTH
    for c in range(tm // GM_CHUNK):
        crows = slice(c * GM_CHUNK, (c + 1) * GM_CHUNK)
        r = jnp.dot(ws_ref[...], vn[crows, :], preferred_element_type=F32)
        s = bs_ref[...]
        for g in range(GM_GROUPS):
            s = s + jnp.where(lane_group == g, r[g * GM_CHUNK:(g + 1) * GM_CHUNK, :], 0.0)
        ob_ref[0, pl.ds(rows.start + c * GM_CHUNK, GM_CHUNK), :] = (u[crows, :] * s).astype(BF16)

    z = project(0, 3 * D_A)
    za_ref[0, rows, 0:D_A] = (z[:, 0:D_A] * (QK_SCALE * LOG2_E)).astype(BF16)
    kat_ref[0, :, rows] = z[:, D_A:2 * D_A].T.astype(BF16)
    za_ref[0, rows, D_A:2 * D_A] = z[:, 2 * D_A:3 * D_A].astype(BF16)

    o = 3 * D_A + 2 * D_B
    cos = cos_ref[rows, :]
    sin = sin_ref[rows, :]
    first_half = (lax.broadcasted_iota(jnp.int32, (1, LANES), 1) % DA_QK_DIM) < (DA_QK_DIM // 2)
    n_q_blocks = D_QK_C // LANES
    for j in range(2 * n_q_blocks):
        if j % n_q_blocks == 0:
            z = project(o + j * LANES, D_QK_C)
        zz = z[:, (j % n_q_blocks) * LANES:(j % n_q_blocks + 1) * LANES]
        r = zz * cos + _swap_rope_halves(zz, first_half) * sin
        if j < n_q_blocks:
            qt_ref[0, j * LANES:(j + 1) * LANES, rows] = (r * (QK_SCALE * LOG2_E)).T.astype(BF16)
        else:
            jk = j - n_q_blocks
            kd_ref[0, rows, jk * LANES:(jk + 1) * LANES] = r.astype(BF16)
    z = project(o + 2 * D_QK_C, D_C)
    for j in range(D_C // LANES):
        r0 = j * DA_VT_ROWS
        vt_ref[0, r0:r0 + DA_V_DIM, rows] = z[:, j * LANES:(j + 1) * LANES].T.astype(BF16)
        vt_ref[0, r0 + DA_V_DIM:r0 + DA_VT_ROWS, rows] = jnp.ones((DA_VT_ROWS - DA_V_DIM, tm), BF16)


def _proj_in(x, modt, w_in, cos_t, sin_t, ws_stack, bs_full, gn, gmat, *, layer, n_latent):
    b, t, d = x.shape
    n_sub = next(n for n in FFN_SUB_TILES if (t // TOKEN_TILE) % n == 0)
    tm = n_sub * TOKEN_TILE
    return pl.pallas_call(
        functools.partial(_proj_in_kernel, tm=TOKEN_TILE, n_sub=n_sub, n_latent=n_latent),
        grid=(b, t // tm),
        in_specs=[
            pl.BlockSpec((1, tm, d), lambda i, j: (i, j, 0)),
            pl.BlockSpec((1, 2, N_ADA, d), lambda i, j: (i, 0, 0, 0)),
            _layer_resident((d, D_IN), layer),
            pl.BlockSpec((tm, LANES), lambda i, j: (j, 0)),
            pl.BlockSpec((tm, LANES), lambda i, j: (j, 0)),
            _resident((GM_GROUPS * GM_CHUNK, GM_CHUNK)),
            _resident((GM_CHUNK, D_B)),
            _resident((1, D_B)),
            _resident((D_B, D_B)),
        ],
        out_specs=[
            pl.BlockSpec((1, tm, 2 * D_A), lambda i, j: (i, j, 0)),
            pl.BlockSpec((1, D_A, tm), lambda i, j: (i, 0, j)),
            pl.BlockSpec((1, tm, D_B), lambda i, j: (i, j, 0)),
            pl.BlockSpec((1, tm, D_QK_C), lambda i, j: (i, j, 0)),
            pl.BlockSpec((1, D_QK_C, tm), lambda i, j: (i, 0, j)),
            pl.BlockSpec((1, DA_HEADS * DA_VT_ROWS, tm), lambda i, j: (i, 0, j)),
        ],
        out_shape=[
            jax.ShapeDtypeStruct((b, t, 2 * D_A), BF16),
            jax.ShapeDtypeStruct((b, D_A, t), BF16),
            jax.ShapeDtypeStruct((b, t, D_B), BF16),
            jax.ShapeDtypeStruct((b, t, D_QK_C), BF16),
            jax.ShapeDtypeStruct((b, D_QK_C, t), BF16),
            jax.ShapeDtypeStruct((b, DA_HEADS * DA_VT_ROWS, t), BF16),
        ],
        name="proj_in",
        compiler_params=_params(("arbitrary", "arbitrary")),
    )(x, modt, w_in, cos_t, sin_t, ws_stack, bs_full, gn, gmat)


def _na_kernel(*refs, n_sub, n_latent):
    q_ref, qn_ref, kt_ref, v_ref = refs[:4]
    bias_refs = refs[4:5 + n_sub]
    o_ref = refs[5 + n_sub]
    s_refs = refs[6 + n_sub:]
    assert len(s_refs) == n_sub >= 2
    first = pl.program_id(2) * n_sub
    n_groups = n_latent // ATT_Q_TILE
    rows = n_latent // GRID_W
    lane = lax.broadcasted_iota(jnp.int32, (1, LANES), 1)

    def window_start(grp):
        base_row = jnp.clip(grp * NA_Q_ROWS - NA_WIN_ROWS // 2, 0, rows - NA_K_ROWS)
        return pl.multiple_of(base_row * GRID_W, ATT_Q_TILE)

    def write_scores(s_ref, q, bias, grp):
        qs = jnp.concatenate([jnp.where(lane < HEAD_DIM, q, jnp.zeros_like(q)),
                              jnp.where(lane >= HEAD_DIM, q, jnp.zeros_like(q))], axis=0)
        sw = jnp.dot(qs, kt_ref[0, :, pl.ds(window_start(grp), NA_K_TOK)],
                     preferred_element_type=F32)
        s_ref[:, 0:NA_K_TOK] = sw + bias.reshape(2 * ATT_Q_TILE, NA_K_TOK)
        s_ref[:, NA_K_TOK:] = jnp.dot(qs, kt_ref[0, :, n_latent:], preferred_element_type=F32)

    def softmax_pv(s_ref, grp, out_rows):
        s = s_ref[...]
        p = jnp.exp2(s - s.max(axis=-1, keepdims=True))
        l = p.sum(axis=-1, keepdims=True)
        p = p.astype(BF16)
        o = (jnp.dot(p[:, 0:NA_K_TOK], v_ref[0, pl.ds(window_start(grp), NA_K_TOK), :],
                     preferred_element_type=F32)
             + jnp.dot(p[:, NA_K_TOK:], v_ref[0, n_latent:, :], preferred_element_type=F32)) / l
        o_ref[0, out_rows, :] = jnp.where(lane < HEAD_DIM, o[:ATT_Q_TILE],
                                          o[ATT_Q_TILE:]).astype(BF16)

    sub_rows = lambda sub: slice(sub * ATT_Q_TILE, (sub + 1) * ATT_Q_TILE)

    @pl.when(first == 0)
    def _():
        write_scores(s_refs[0], q_ref[0, sub_rows(0), :], bias_refs[0][0], first)

    for sub in range(n_sub):
        q_next = q_ref[0, sub_rows(sub + 1), :] if sub + 1 < n_sub else qn_ref[0]
        write_scores(s_refs[(sub + 1) % n_sub], q_next, bias_refs[sub + 1][0],
                     jnp.minimum(first + sub + 1, n_groups))
        softmax_pv(s_refs[sub], first + sub, sub_rows(sub))


def _na_attention(za, kat, bias, *, n_latent, ctx_queries):
    b, t, _ = za.shape
    n_groups = n_latent // ATT_Q_TILE
    n_total = n_groups + 1 if ctx_queries else n_groups
    n_sub = next(n for n in NA_SUB_GROUPS if n_total % n == 0)
    assert t == n_latent + CTX_LEN and CTX_LEN == ATT_Q_TILE
    assert n_groups >= 3 and n_latent // GRID_W >= NA_K_ROWS

    def kind(g):
        return jnp.where(g == 0, 0, jnp.where(g < n_groups - 1, 1, jnp.where(g == n_groups - 1, 2, 3)))

    group_at = lambda j, off: jnp.minimum(j * n_sub + off, n_groups)
    bias_spec = lambda off: pl.BlockSpec((1, 2, ATT_Q_TILE, NA_K_TOK),
                                         lambda i, hp, j: (kind(group_at(j, off)), hp, 0, 0))
    score_buf = pltpu.VMEM((2 * ATT_Q_TILE, NA_K_TOK + CTX_LEN), F32)
    return pl.pallas_call(
        functools.partial(_na_kernel, n_sub=n_sub, n_latent=n_latent),
        grid=(b, NA_HEADS // 2, n_total // n_sub),
        in_specs=[
            pl.BlockSpec((1, n_sub * ATT_Q_TILE, LANES), lambda i, hp, j: (i, j, hp)),
            pl.BlockSpec((1, ATT_Q_TILE, LANES), lambda i, hp, j: (i, group_at(j, n_sub), hp)),
            pl.BlockSpec((1, LANES, t), lambda i, hp, j: (i, hp, 0)),
            pl.BlockSpec((1, t, LANES), lambda i, hp, j: (i, 0, 2 + hp)),
        ] + [bias_spec(off) for off in range(n_sub + 1)],
        out_specs=pl.BlockSpec((1, n_sub * ATT_Q_TILE, LANES), lambda i, hp, j: (i, j, hp)),
        out_shape=jax.ShapeDtypeStruct((b, n_total * ATT_Q_TILE, D_A), BF16),
        scratch_shapes=[score_buf] * n_sub,
        name="na_attention",
        compiler_params=_params(("arbitrary", "arbitrary", "arbitrary")),
    )(za, za, kat, za, *([bias] * (n_sub + 1)))


NA_BIAS_PAD = NA_WIN_ROWS // 2
NA_BIAS_W = 11 * LANES


def _na_bias_plan(n_latent):
    rows = n_latent // GRID_W
    n_groups = n_latent // ATT_Q_TILE
    plan = []
    for g in (0, 1, n_groups - 1):
        base = int(np.clip(g * NA_Q_ROWS - NA_WIN_ROWS // 2, 0, rows - NA_K_ROWS))
        kind = []
        for i in range(NA_Q_ROWS):
            r = g * NA_Q_ROWS + i
            start = int(np.clip(r - NA_WIN_ROWS // 2, 0, rows - NA_WIN_ROWS))
            e0 = base - r + NA_WIN_ROWS - 1 + NA_BIAS_PAD
            assert 0 <= e0 and (e0 - e0 % 2) * GRID_W + NA_K_TOK <= NA_BIAS_W
            kind.append((e0, start - base, start - base + NA_WIN_ROWS - 1))
        plan.append(tuple(kind))
    return tuple(plan)


def _na_bias_kernel(r_ref, o_ref, *, plan):
    shape = (GRID_W, NA_BIAS_W)
    qc = lax.broadcasted_iota(jnp.int32, shape, 0)
    col = lax.broadcasted_iota(jnp.int32, shape, 1)
    kc = col % GRID_W
    dc = kc - qc + (NA_WIN_COLS - 1)
    w0 = jnp.clip(qc - NA_WIN_COLS // 2, 0, GRID_W - NA_WIN_COLS)
    col_ok = (kc >= w0) & (kc < w0 + NA_WIN_COLS)
    tables = []
    for p in range(2):
        a = col // GRID_W + (p - NA_BIAS_PAD)
        acc = jnp.zeros(shape, F32)
        for b in range(2 * NA_WIN_COLS - 1):
            acc = acc + jnp.where(dc == b, r_ref[0, p, b:b + 1, :], 0.0)
        ok = col_ok & (a >= 0) & (a <= 2 * NA_WIN_ROWS - 2)
        tables.append(jnp.where(ok, acc * LOG2_E, NEG_INF))
    kj = lax.broadcasted_iota(jnp.int32, (GRID_W, NA_K_TOK), 1) // GRID_W
    for k, kind in enumerate(plan):
        for i, (e0, jlo, jhi) in enumerate(kind):
            p = e0 % 2
            off = (e0 - p) * GRID_W
            slab = tables[p][:, off:off + NA_K_TOK]
            o_ref[k, 0, i * GRID_W:(i + 1) * GRID_W, :] = jnp.where(
                (kj >= jlo) & (kj <= jhi), slab, NEG_INF)
    o_ref[len(plan), 0] = jnp.full((ATT_Q_TILE, NA_K_TOK), NEG_INF, F32)


def _na_bias_tables(rpb, n_latent):
    h = rpb.shape[0]
    n_blocks = NA_BIAS_W // GRID_W + 2
    rp = jnp.pad(rpb, ((0, 0), (NA_BIAS_PAD, n_blocks - NA_BIAS_PAD - rpb.shape[1]), (0, 0)))
    rep = jnp.transpose(jnp.repeat(rp, GRID_W, axis=1), (0, 2, 1))
    r = jnp.stack([rep[:, :, p * GRID_W:p * GRID_W + NA_BIAS_W] for p in range(2)], axis=1)
    n_dc = rpb.shape[2]
    return pl.pallas_call(
        functools.partial(_na_bias_kernel, plan=_na_bias_plan(n_latent)),
        grid=(h,),
        in_specs=[pl.BlockSpec((1, 2, n_dc, NA_BIAS_W), lambda i: (i, 0, 0, 0))],
        out_specs=pl.BlockSpec((4, 1, ATT_Q_TILE, NA_K_TOK), lambda i: (0, i, 0, 0)),
        out_shape=jax.ShapeDtypeStruct((4, h, ATT_Q_TILE, NA_K_TOK), F32),
        name="na_bias",
        compiler_params=_params(("arbitrary",)),
    )(r)


def _kv_chunk_bounds(t):
    head, tail = DA_KV_RAMP_UP, DA_KV_RAMP_DOWN
    n_slots = DA_PREFETCH + 1
    middle = t - sum(head) - sum(tail)
    units = middle // MXU_DIM_V7X
    n_mid = -(-units // (DA_KV_CHUNK // MXU_DIM_V7X)) if middle > 0 else 0
    n_mid += -(len(head) + n_mid + len(tail)) % n_slots
    if middle < 0 or middle % MXU_DIM_V7X or n_mid > units:
        assert t % DA_KV_SMALL_CHUNK == 0
        sizes = (DA_KV_SMALL_CHUNK,) * (t // DA_KV_SMALL_CHUNK)
    else:
        mid = tuple((units // n_mid + (j < units % n_mid)) * MXU_DIM_V7X for j in range(n_mid))
        sizes = head + mid + tail
    assert max(sizes) <= DA_KV_CHUNK and len(sizes) % n_slots == 0
    return tuple(int(v) for v in np.cumsum((0,) + sizes))


def _diff_kernel(*refs, tq, n_keys, lambda_init, latent):
    if latent:
        qt_ref, qtn_ref, k_ref, vt_ref, lqk_ref, subln_ref, o_ref, s_ref, smax_ref = refs
    else:
        qt_ref, k_ref, vt_ref, lqk_ref, subln_ref, _, o_ref = refs
    n_tiles = qt_ref.shape[2] // tq
    feat = lax.broadcasted_iota(jnp.int32, (LANES, 1), 0)

    def stack_components(qt):
        return jnp.concatenate([jnp.where(feat < DA_QK_DIM, qt, jnp.zeros_like(qt)),
                                jnp.where(feat >= DA_QK_DIM, qt, jnp.zeros_like(qt))], axis=1)

    tile_cols = lambda u: slice(u * tq, (u + 1) * tq)
    lqk = lqk_ref[...]
    lam = (jnp.exp(jnp.sum(lqk[0:1] * lqk[1:2], axis=-1, keepdims=True))
           - jnp.exp(jnp.sum(lqk[2:3] * lqk[3:4], axis=-1, keepdims=True)) + lambda_init)

    def col_reduce(x, op):
        r, c = x.shape
        groups = 8 if r % 64 == 0 else 1
        x = x.reshape(groups, r // (8 * groups), 8, c)
        return op(op(op(x, axis=1), axis=0), axis=0, keepdims=True)

    def scores(k, q):
        s = jnp.dot(k, q, preferred_element_type=F32)
        return s, col_reduce(s, jnp.max)

    def step(s, s_max, vt1, state):
        m, acc = state
        m_new = jnp.maximum(m, s_max)
        alpha = jnp.exp2(m - m_new)
        p = jnp.exp2(s - m_new)
        acc = alpha * acc + jnp.dot(vt1, p.astype(BF16), preferred_element_type=F32)
        return (m_new, acc)

    def init():
        return (jnp.full((1, 2 * tq), NEG_INF, F32), jnp.zeros((DA_VT_ROWS, 2 * tq), F32))

    def finish(state, u):
        _, acc = state
        o = acc[:DA_V_DIM] / acc[DA_V_DIM:DA_V_DIM + 1]
        o = o[:, :tq] - lam * o[:, tq:]
        o = o * lax.rsqrt(jnp.mean(o * o, axis=0, keepdims=True) + EPS)
        o = o * (subln_ref[...] * (1.0 - lambda_init))
        o_ref[0, tile_cols(u), :] = o.T.astype(BF16)

    if not latent:
        s, s_max = scores(k_ref[0], stack_components(qt_ref[0]))
        finish(step(s, s_max, vt_ref[0], init()), 0)
        return

    bounds = _kv_chunk_bounds(n_keys)
    chunk = lambda c: slice(bounds[c], bounds[c + 1])
    size = lambda c: bounds[c + 1] - bounds[c]
    n_chunks = len(bounds) - 1
    ahead = DA_PREFETCH
    n_slots = ahead + 1
    assert n_chunks % n_slots == 0 and n_chunks > ahead
    slot = lambda c: (c % n_slots, slice(0, size(c)))

    @pl.when(pl.program_id(2) == 0)
    def _():
        q0 = stack_components(qt_ref[0, :, tile_cols(0)])
        for c in range(ahead):
            s_ref[slot(c)], smax_ref[c:c + 1, :] = scores(k_ref[0, chunk(c), :], q0)

    maxes = [smax_ref[c:c + 1, :] for c in range(ahead)]
    for u in range(n_tiles):
        qst = stack_components(qt_ref[0, :, tile_cols(u)])
        q_next = stack_components(qt_ref[0, :, tile_cols(u + 1)] if u + 1 < n_tiles
                                  else qtn_ref[0])
        state = init()
        for c in range(n_chunks):
            nxt = c + ahead
            if nxt < n_chunks:
                s_ref[slot(nxt)], new_max = scores(k_ref[0, chunk(nxt), :], qst)
            else:
                s_ref[slot(nxt - n_chunks)], new_max = scores(k_ref[0, chunk(nxt - n_chunks), :],
                                                              q_next)
            state = step(s_ref[slot(c)], maxes[0], vt_ref[0, :, chunk(c)], state)
            maxes = maxes[1:] + [new_max]
        finish(state, u)
    for c in range(ahead):
        smax_ref[c:c + 1, :] = maxes[c]


def _diff_attention(kd, qt, vt, lqk, subln, *, n_latent, lambda_init, ctx_queries):
    b, t, _ = kd.shape
    tq = DA_Q_TILE
    n_q_tiles = n_latent // tq
    n_tiles = next(n for n in DA_TILES_PER_STEP if n_q_tiles % n == 0)
    step_q = n_tiles * tq
    consts = [pl.BlockSpec((4, DA_QK_DIM), lambda *_: (0, 0)),
              pl.BlockSpec((DA_V_DIM, 1), lambda *_: (0, 0))]
    out_rows = t if ctx_queries else n_latent
    oc = pl.pallas_call(
        functools.partial(_diff_kernel, tq=tq, n_keys=t, lambda_init=lambda_init, latent=True),
        grid=(b, DA_HEADS, n_q_tiles // n_tiles),
        in_specs=[
            pl.BlockSpec((1, LANES, step_q), lambda bi, h, i: (bi, h, i)),
            pl.BlockSpec((1, LANES, tq),
                         lambda bi, h, i: (bi, h, jnp.minimum((i + 1) * n_tiles, n_q_tiles - 1))),
            pl.BlockSpec((1, t, LANES), lambda bi, h, i: (bi, 0, h)),
            pl.BlockSpec((1, DA_VT_ROWS, t), lambda bi, h, i: (bi, h, 0)),
        ] + consts,
        out_specs=pl.BlockSpec((1, step_q, LANES), lambda bi, h, i: (bi, i, h)),
        out_shape=jax.ShapeDtypeStruct((b, out_rows, D_C), BF16),
        scratch_shapes=[pltpu.VMEM((DA_PREFETCH + 1, DA_KV_CHUNK, 2 * tq), F32),
                        pltpu.VMEM((DA_PREFETCH, 2 * tq), F32)],
        name="diff_attention",
        compiler_params=_params(("arbitrary", "arbitrary", "arbitrary")),
    )(qt, qt, kd, vt, lqk, subln)
    if not ctx_queries:
        return oc
    ctx_blk = n_latent // CTX_LEN
    return pl.pallas_call(
        functools.partial(_diff_kernel, tq=CTX_LEN, n_keys=CTX_LEN, lambda_init=lambda_init,
                          latent=False),
        grid=(b, DA_HEADS),
        in_specs=[
            pl.BlockSpec((1, LANES, CTX_LEN), lambda bi, h: (bi, h, ctx_blk)),
            pl.BlockSpec((1, CTX_LEN, LANES), lambda bi, h: (bi, ctx_blk, h)),
            pl.BlockSpec((1, DA_VT_ROWS, CTX_LEN), lambda bi, h: (bi, h, ctx_blk)),
        ] + consts + [pl.BlockSpec(memory_space=pl.ANY)],
        out_specs=pl.BlockSpec((1, CTX_LEN, LANES), lambda bi, h: (bi, ctx_blk, h)),
        out_shape=jax.ShapeDtypeStruct((b, out_rows, D_C), BF16),
        input_output_aliases={5: 0},
        name="diff_attention_ctx",
        compiler_params=_params(("arbitrary", "arbitrary")),
    )(qt, kd, vt, lqk, subln, oc)


def _rope_tables(n_latent):
    tok = jnp.arange(n_latent)
    row = (tok // GRID_W).astype(F32)
    col = (tok % GRID_W).astype(F32)
    n_freq = DA_QK_DIM // 4
    freqs = ROPE_BASE ** (-jnp.arange(n_freq, dtype=F32) / n_freq)
    ang = jnp.concatenate([row[:, None] * freqs, col[:, None] * freqs], axis=-1)
    cos, sin = jnp.cos(ang), jnp.sin(ang)
    cos_t = jnp.concatenate([cos, cos, cos, cos], axis=-1)
    sin_t = jnp.concatenate([-sin, sin, -sin, sin], axis=-1)
    cos_t = jnp.concatenate([cos_t, jnp.ones((CTX_LEN, LANES), F32)], axis=0)
    sin_t = jnp.concatenate([sin_t, jnp.zeros((CTX_LEN, LANES), F32)], axis=0)
    return cos_t, sin_t


def kernel(x, c, ctx, c_ctx, w_ada, b_ada, ffn1_w1, ffn1_w3, ffn1_w2, w_in, w_out, na_rpb,
           gm_ws, gm_bs, gm_norm, da_lq1, da_lk1, da_lq2, da_lk2, da_subln,
           ffn2_w1, ffn2_w3, ffn2_w2, final_norm):
    b, n_latent, d = x.shape
    assert d == D_MODEL and ctx.shape[1] == CTX_LEN and b < ADA_ROWS
    assert n_latent % DA_Q_TILE == 0

    cvec = jnp.zeros((ADA_ROWS, d), F32).at[:b].set(c).at[b].set(c_ctx)
    mod = _ada(cvec, w_ada, b_ada).reshape(DEPTH, ADA_ROWS, N_ADA, d)
    cos_t, sin_t = _rope_tables(n_latent)
    gmat = jnp.asarray(np.kron(np.eye(GM_GROUPS), np.ones((GM_WIDTH, GM_WIDTH))), BF16)

    bf16 = lambda w: w.astype(BF16)
    ffn1 = (bf16(ffn1_w1), bf16(ffn1_w3), bf16(ffn1_w2))
    ffn2 = (bf16(ffn2_w1), bf16(ffn2_w3), bf16(ffn2_w2))
    w_in, w_out = bf16(w_in), bf16(w_out)

    xs = x
    for l in range(DEPTH):
        last = l == DEPTH - 1
        lambda_init = 0.8 - 0.6 * math.exp(-0.3 * l)
        modt = jnp.stack([mod[l, :b], jnp.broadcast_to(mod[l, b], (b, N_ADA, d))], axis=1)
        bias = _na_bias_tables(na_rpb[l], n_latent)
        ws_stack = gm_ws[l].reshape(GM_GROUPS * GM_CHUNK, GM_CHUNK).astype(BF16)
        bs_full = jnp.repeat(gm_bs[l].T, GM_WIDTH, axis=1)
        gn = gm_norm[l].reshape(1, D_B)
        lqk = jnp.stack([da_lq1[l], da_lk1[l], da_lq2[l], da_lk2[l]])

        xs = _ffn(xs, modt, *ffn1, layer=l, n_latent=n_latent, mi=0, ctx=ctx if l == 0 else None)
        za, kat, ob, kd, qt, vt = _proj_in(xs, modt, w_in, cos_t, sin_t, ws_stack, bs_full, gn,
                                           gmat, layer=l, n_latent=n_latent)
        oa = _na_attention(za, kat, bias, n_latent=n_latent, ctx_queries=not last)
        oc = _diff_attention(kd, qt, vt, lqk, da_subln[l].reshape(DA_V_DIM, 1),
                             n_latent=n_latent, lambda_init=lambda_init, ctx_queries=not last)
        xs = _ffn(xs, modt, *ffn2, layer=l, n_latent=n_latent, mi=6, mix=(oa, ob, oc, w_out),
                  final_gain=final_norm if last else None)
    return xs
```

```python
import functools
import math

import numpy as np
import jax
import jax.numpy as jnp
from jax import lax
from jax.experimental import pallas as pl
from jax.experimental.pallas import tpu as pltpu

F32 = jnp.float32
BF16 = jnp.bfloat16

D_MODEL = 1024
DEPTH = 2
GRID_W = 64
CTX_LEN = 256
N_ADA = 9
EPS = 1e-6
NEG_INF = -1e30
ROPE_BASE = 10000.0
D_FF = 2816
HEAD_DIM = 64
NA_HEADS = 4
NA_WIN_ROWS = 8
NA_WIN_COLS = 16
D_A = NA_HEADS * HEAD_DIM
GM_GROUPS = 4
GM_WIDTH = 64
GM_CHUNK = 128
D_B = GM_GROUPS * GM_WIDTH
DA_HEADS = 4
DA_QK_DIM = 64
DA_V_DIM = 128
DA_VT_ROWS = DA_V_DIM + 16
D_C = DA_HEADS * DA_V_DIM
D_QK_C = DA_HEADS * 2 * DA_QK_DIM
D_IN = 3 * D_A + 2 * D_B + 2 * D_QK_C + D_C
QK_SCALE = HEAD_DIM ** -0.5
LOG2_E = math.log2(math.e)

LANES = 128
MXU_DIM_V7X = 256
VMEM_BYTES_V7X = 64 * 1024 * 1024
VMEM_LIMIT = VMEM_BYTES_V7X - 8 * 1024 * 1024

TOKEN_TILE = 256
FFN_SUB_TILES = (3, 2, 1)
NA_SUB_GROUPS = (3, 2)
ATT_Q_TILE = 256
NA_Q_ROWS = ATT_Q_TILE // GRID_W
NA_K_ROWS = NA_Q_ROWS + NA_WIN_ROWS
NA_K_TOK = NA_K_ROWS * GRID_W
DA_Q_TILE = 512
DA_TILES_PER_STEP = (2, 1)
DA_KV_CHUNK = 1536
DA_KV_RAMP_UP = (256, 768)
DA_KV_RAMP_DOWN = (768, 512, 256, 256)
DA_PREFETCH = 3
DA_KV_SMALL_CHUNK = 384
FF_CHUNKS = ((0, 768), (768, 1536), (1536, 2304), (2304, D_FF))
ADA_ROWS = 8
ADA_COL_TILE = 1024


def _params(sem):
    return pltpu.CompilerParams(dimension_semantics=sem, vmem_limit_bytes=VMEM_LIMIT)


def _resident(shape):
    nd = len(shape)
    return pl.BlockSpec(shape, lambda *_: (0,) * nd, pipeline_mode=pl.Buffered(1))


def _layer_resident(shape, layer):
    nd = len(shape)
    return pl.BlockSpec((1,) + shape, lambda *_: (layer,) + (0,) * nd,
                        pipeline_mode=pl.Buffered(1))


def _rms(x):
    return x * lax.rsqrt(jnp.mean(x * x, axis=-1, keepdims=True) + EPS)


def _ada_kernel(c_ref, w_ref, b_ref, o_ref):
    c = c_ref[...]
    cs = c / (1.0 + jnp.exp(-c))
    o_ref[0] = jnp.dot(cs, w_ref[0], preferred_element_type=F32,
                       precision=lax.Precision.HIGHEST) + b_ref[0]


def _ada(cvec, w_ada, b_ada):
    depth, d, n = w_ada.shape
    tn = ADA_COL_TILE
    return pl.pallas_call(
        _ada_kernel,
        grid=(depth, n // tn),
        in_specs=[
            pl.BlockSpec((ADA_ROWS, d), lambda l, j: (0, 0)),
            pl.BlockSpec((1, d, tn), lambda l, j: (l, 0, j)),
            pl.BlockSpec((1, 1, tn), lambda l, j: (l, 0, j)),
        ],
        out_specs=pl.BlockSpec((1, ADA_ROWS, tn), lambda l, j: (l, 0, j)),
        out_shape=jax.ShapeDtypeStruct((depth, ADA_ROWS, n), F32),
        name="ada",
        compiler_params=_params(("arbitrary", "arbitrary")),
    )(cvec, w_ada, b_ada.reshape(depth, 1, n))


def _ffn_kernel(*refs, tm, n_sub, n_latent, mi, split_ctx, mix, final):
    refs = list(refs)
    if split_ctx:
        x_refs = refs[:n_sub]
        del refs[:n_sub]
        ctx_ref = refs.pop(0)
    else:
        x_ref = refs.pop(0)
    mod_ref = refs.pop(0)
    if mix:
        oa_ref, ob_ref, oc_ref, wout_ref = refs[:4]
        del refs[:4]
    w1_ref, w3_ref, w2_ref = refs[:3]
    del refs[:3]
    if final:
        gain_ref = refs.pop(0)
    (o_ref,) = refs
    for sub in range(n_sub):
        rows = slice(sub * tm, (sub + 1) * tm)
        is_ctx = (pl.program_id(1) * n_sub + sub) * tm >= n_latent

        def mod(idx):
            if final:
                return mod_ref[0, 0, idx:idx + 1, :]
            return jnp.where(is_ctx, mod_ref[0, 1, idx:idx + 1, :], mod_ref[0, 0, idx:idx + 1, :])

        if split_ctx:
            x = jnp.where(is_ctx, ctx_ref[0], x_refs[sub][0])
        else:
            x = x_ref[0, rows, :]
        if mix:
            cat = jnp.concatenate([oa_ref[0, rows, :], ob_ref[0, rows, :], oc_ref[0, rows, :]],
                                  axis=-1)
            x = x + mod(mi - 1) * jnp.dot(cat, wout_ref[0], preferred_element_type=F32)
        shift = mod(mi)
        scale1 = 1.0 + mod(mi + 1)
        half_gate = 0.5 * mod(mi + 2)
        h = (_rms(x) * scale1 + shift).astype(BF16)
        acc = jnp.zeros((tm, D_MODEL), F32)
        for c0, c1 in FF_CHUNKS:
            a = jnp.dot(h, w1_ref[0, :, c0:c1], preferred_element_type=F32)
            g = jnp.dot(h, w3_ref[0, :, c0:c1], preferred_element_type=F32)
            y = (a / (1.0 + jnp.exp(-a)) * g).astype(BF16)
            acc = acc + jnp.dot(y, w2_ref[0, c0:c1, :], preferred_element_type=F32)
        out = x + half_gate * acc
        if final:
            out = _rms(out) * gain_ref[...]
        o_ref[0, rows, :] = out


def _ffn(x, modt, w1, w3, w2, *, layer, n_latent, mi, ctx=None, mix=None, final_gain=None):
    b, _, d = x.shape
    t = n_latent + CTX_LEN
    tm = TOKEN_TILE
    final = final_gain is not None
    t_out = n_latent if final else t
    n_sub = next(n for n in FFN_SUB_TILES if (t_out // tm) % n == 0)
    step_rows = n_sub * tm
    tile = lambda width: pl.BlockSpec((1, step_rows, width), lambda i, j: (i, j, 0))
    if ctx is None:
        in_specs, args = [tile(d)], [x]
    else:
        assert tm == CTX_LEN and not final
        last_latent = n_latent // tm - 1
        sub_spec = lambda sub: pl.BlockSpec(
            (1, tm, d), lambda i, j: (i, jnp.minimum(j * n_sub + sub, last_latent), 0))
        in_specs = [sub_spec(sub) for sub in range(n_sub)]
        in_specs.append(pl.BlockSpec((1, CTX_LEN, d), lambda i, j: (i, 0, 0)))
        args = [x] * n_sub + [ctx]
    in_specs.append(pl.BlockSpec((1, 2, N_ADA, d), lambda i, j: (i, 0, 0, 0)))
    args.append(modt)
    if mix is not None:
        in_specs += [tile(D_A), tile(D_B), tile(D_C), _layer_resident((d, d), layer)]
        args += list(mix)
    in_specs += [_layer_resident((d, D_FF), layer), _layer_resident((d, D_FF), layer),
                 _layer_resident((D_FF, d), layer)]
    args += [w1, w3, w2]
    if final:
        in_specs.append(_resident((1, d)))
        args.append(final_gain.reshape(1, d))
    return pl.pallas_call(
        functools.partial(_ffn_kernel, tm=tm, n_sub=n_sub, n_latent=n_latent, mi=mi,
                          split_ctx=ctx is not None, mix=mix is not None, final=final),
        grid=(b, t_out // step_rows),
        in_specs=in_specs,
        out_specs=tile(d),
        out_shape=jax.ShapeDtypeStruct((b, t_out, d), F32),
        name="ffn_final" if final else ("ffn_mix" if mix is not None else "ffn"),
        compiler_params=_params(("arbitrary", "arbitrary")),
    )(*args)


def _swap_rope_halves(x, first_half):
    return jnp.where(first_half, pltpu.roll(x, 96, 1), pltpu.roll(x, 32, 1))


def _proj_in_kernel(*refs, tm, n_sub, n_latent):
    for sub in range(n_sub):
        _proj_in_tile(*refs, tm=tm, rows=slice(sub * tm, (sub + 1) * tm),
                      is_ctx=(pl.program_id(1) * n_sub + sub) * tm >= n_latent)


def _proj_in_tile(x_ref, mod_ref, w_ref, cos_ref, sin_ref, ws_ref, bs_ref, gn_ref, gmat_ref,
                  za_ref, kat_ref, ob_ref, kd_ref, qt_ref, vt_ref, *, tm, rows, is_ctx):
    mod = lambda idx: jnp.where(is_ctx, mod_ref[0, 1, idx:idx + 1, :], mod_ref[0, 0, idx:idx + 1, :])
    x = x_ref[0, rows, :]
    h = (_rms(x) * (1.0 + mod(4)) + mod(3)).astype(BF16)

    z_all = jnp.dot(h, w_ref[0], preferred_element_type=F32)
    project = lambda c0, width: z_all[:, c0:c0 + width]

    z = project(3 * D_A, 2 * D_B)
    u = z[:, 0:D_B]
    v = z[:, D_B:]
    u = 0.5 * u * (1.0 + lax.erf(u * (2.0 ** -0.5)))
    v = 0.5 * v * (1.0 + lax.erf(v * (2.0 ** -0.5)))
    v2 = v * v
    v2_hi = v2.astype(BF16)
    v2_lo = (v2 - v2_hi.astype(F32)).astype(BF16)
    ms = (jnp.dot(v2_hi, gmat_ref[...], preferred_element_type=F32)
          + jnp.dot(v2_lo, gmat_ref[...], preferred_element_type=F32)) * (1.0 / GM_WIDTH)
    vn = (v * lax.rsqrt(ms + EPS) * gn_ref[...]).astype(BF16)
    lane_group = lax.broadcasted_iota(jnp.int32, (1, D_B), 1) // GM_WIDTH
    for c in range(tm // GM_CHUNK):
        crows = slice(c * GM_CHUNK, (c + 1) * GM_CHUNK)
        r = jnp.dot(ws_ref[...], vn[crows, :], preferred_element_type=F32)
        s = bs_ref[...]
        for g in range(GM_GROUPS):
            s = s + jnp.where(lane_group == g, r[g * GM_CHUNK:(g + 1) * GM_CHUNK, :], 0.0)
        ob_ref[0, pl.ds(rows.start + c * GM_CHUNK, GM_CHUNK), :] = (u[crows, :] * s).astype(BF16)

    z = project(0, 3 * D_A)
    za_ref[0, rows, 0:D_A] = (z[:, 0:D_A] * (QK_SCALE * LOG2_E)).astype(BF16)
    kat_ref[0, :, rows] = z[:, D_A:2 * D_A].T.astype(BF16)
    za_ref[0, rows, D_A:2 * D_A] = z[:, 2 * D_A:3 * D_A].astype(BF16)

    o = 3 * D_A + 2 * D_B
    cos = cos_ref[rows, :]
    sin = sin_ref[rows, :]
    first_half = (lax.broadcasted_iota(jnp.int32, (1, LANES), 1) % DA_QK_DIM) < (DA_QK_DIM // 2)
    n_q_blocks = D_QK_C // LANES
    for j in range(2 * n_q_blocks):
        if j % n_q_blocks == 0:
            z = project(o + j * LANES, D_QK_C)
        zz = z[:, (j % n_q_blocks) * LANES:(j % n_q_blocks + 1) * LANES]
        r = zz * cos + _swap_rope_halves(zz, first_half) * sin
        if j < n_q_blocks:
            qt_ref[0, j * LANES:(j + 1) * LANES, rows] = (r * (QK_SCALE * LOG2_E)).T.astype(BF16)
        else:
            jk = j - n_q_blocks
            kd_ref[0, rows, jk * LANES:(jk + 1) * LANES] = r.astype(BF16)
    z = project(o + 2 * D_QK_C, D_C)
    for j in range(D_C // LANES):
        r0 = j * DA_VT_ROWS
        vt_ref[0, r0:r0 + DA_V_DIM, rows] = z[:, j * LANES:(j + 1) * LANES].T.astype(BF16)
        vt_ref[0, r0 + DA_V_DIM:r0 + DA_VT_ROWS, rows] = jnp.ones((DA_VT_ROWS - DA_V_DIM, tm), BF16)


def _proj_in(x, modt, w_in, cos_t, sin_t, ws_stack, bs_full, gn, gmat, *, layer, n_latent):
    b, t, d = x.shape
    n_sub = next(n for n in FFN_SUB_TILES if (t // TOKEN_TILE) % n == 0)
    tm = n_sub * TOKEN_TILE
    return pl.pallas_call(
        functools.partial(_proj_in_kernel, tm=TOKEN_TILE, n_sub=n_sub, n_latent=n_latent),
        grid=(b, t // tm),
        in_specs=[
            pl.BlockSpec((1, tm, d), lambda i, j: (i, j, 0)),
            pl.BlockSpec((1, 2, N_ADA, d), lambda i, j: (i, 0, 0, 0)),
            _layer_resident((d, D_IN), layer),
            pl.BlockSpec((tm, LANES), lambda i, j: (j, 0)),
            pl.BlockSpec((tm, LANES), lambda i, j: (j, 0)),
            _resident((GM_GROUPS * GM_CHUNK, GM_CHUNK)),
            _resident((GM_CHUNK, D_B)),
            _resident((1, D_B)),
            _resident((D_B, D_B)),
        ],
        out_specs=[
            pl.BlockSpec((1, tm, 2 * D_A), lambda i, j: (i, j, 0)),
            pl.BlockSpec((1, D_A, tm), lambda i, j: (i, 0, j)),
            pl.BlockSpec((1, tm, D_B), lambda i, j: (i, j, 0)),
            pl.BlockSpec((1, tm, D_QK_C), lambda i, j: (i, j, 0)),
            pl.BlockSpec((1, D_QK_C, tm), lambda i, j: (i, 0, j)),
            pl.BlockSpec((1, DA_HEADS * DA_VT_ROWS, tm), lambda i, j: (i, 0, j)),
        ],
        out_shape=[
            jax.ShapeDtypeStruct((b, t, 2 * D_A), BF16),
            jax.ShapeDtypeStruct((b, D_A, t), BF16),
            jax.ShapeDtypeStruct((b, t, D_B), BF16),
            jax.ShapeDtypeStruct((b, t, D_QK_C), BF16),
            jax.ShapeDtypeStruct((b, D_QK_C, t), BF16),
            jax.ShapeDtypeStruct((b, DA_HEADS * DA_VT_ROWS, t), BF16),
        ],
        name="proj_in",
        compiler_params=_params(("arbitrary", "arbitrary")),
    )(x, modt, w_in, cos_t, sin_t, ws_stack, bs_full, gn, gmat)


def _na_kernel(*refs, n_sub, n_latent):
    q_ref, qn_ref, kt_ref, v_ref = refs[:4]
    bias_refs = refs[4:5 + n_sub]
    o_ref = refs[5 + n_sub]
    s_refs = refs[6 + n_sub:]
    assert len(s_refs) == n_sub >= 2
    first = pl.program_id(2) * n_sub
    n_groups = n_latent // ATT_Q_TILE
    rows = n_latent // GRID_W
    lane = lax.broadcasted_iota(jnp.int32, (1, LANES), 1)

    def window_start(grp):
        base_row = jnp.clip(grp * NA_Q_ROWS - NA_WIN_ROWS // 2, 0, rows - NA_K_ROWS)
        return pl.multiple_of(base_row * GRID_W, ATT_Q_TILE)

    def write_scores(s_ref, q, bias, grp):
        qs = jnp.concatenate([jnp.where(lane < HEAD_DIM, q, jnp.zeros_like(q)),
                              jnp.where(lane >= HEAD_DIM, q, jnp.zeros_like(q))], axis=0)
        sw = jnp.dot(qs, kt_ref[0, :, pl.ds(window_start(grp), NA_K_TOK)],
                     preferred_element_type=F32)
        s_ref[:, 0:NA_K_TOK] = sw + bias.reshape(2 * ATT_Q_TILE, NA_K_TOK)
        s_ref[:, NA_K_TOK:] = jnp.dot(qs, kt_ref[0, :, n_latent:], preferred_element_type=F32)

    def softmax_pv(s_ref, grp, out_rows):
        s = s_ref[...]
        p = jnp.exp2(s - s.max(axis=-1, keepdims=True))
        l = p.sum(axis=-1, keepdims=True)
        p = p.astype(BF16)
        o = (jnp.dot(p[:, 0:NA_K_TOK], v_ref[0, pl.ds(window_start(grp), NA_K_TOK), :],
                     preferred_element_type=F32)
             + jnp.dot(p[:, NA_K_TOK:], v_ref[0, n_latent:, :], preferred_element_type=F32)) / l
        o_ref[0, out_rows, :] = jnp.where(lane < HEAD_DIM, o[:ATT_Q_TILE],
                                          o[ATT_Q_TILE:]).astype(BF16)

    sub_rows = lambda sub: slice(sub * ATT_Q_TILE, (sub + 1) * ATT_Q_TILE)

    @pl.when(first == 0)
    def _():
        write_scores(s_refs[0], q_ref[0, sub_rows(0), :], bias_refs[0][0], first)

    for sub in range(n_sub):
        q_next = q_ref[0, sub_rows(sub + 1), :] if sub + 1 < n_sub else qn_ref[0]
        write_scores(s_refs[(sub + 1) % n_sub], q_next, bias_refs[sub + 1][0],
                     jnp.minimum(first + sub + 1, n_groups))
        softmax_pv(s_refs[sub], first + sub, sub_rows(sub))


def _na_attention(za, kat, bias, *, n_latent, ctx_queries):
    b, t, _ = za.shape
    n_groups = n_latent // ATT_Q_TILE
    n_total = n_groups + 1 if ctx_queries else n_groups
    n_sub = next(n for n in NA_SUB_GROUPS if n_total % n == 0)
    assert t == n_latent + CTX_LEN and CTX_LEN == ATT_Q_TILE
    assert n_groups >= 3 and n_latent // GRID_W >= NA_K_ROWS

    def kind(g):
        return jnp.where(g == 0, 0, jnp.where(g < n_groups - 1, 1, jnp.where(g == n_groups - 1, 2, 3)))

    group_at = lambda j, off: jnp.minimum(j * n_sub + off, n_groups)
    bias_spec = lambda off: pl.BlockSpec((1, 2, ATT_Q_TILE, NA_K_TOK),
                                         lambda i, hp, j: (kind(group_at(j, off)), hp, 0, 0))
    score_buf = pltpu.VMEM((2 * ATT_Q_TILE, NA_K_TOK + CTX_LEN), F32)
    return pl.pallas_call(
        functools.partial(_na_kernel, n_sub=n_sub, n_latent=n_latent),
        grid=(b, NA_HEADS // 2, n_total // n_sub),
        in_specs=[
            pl.BlockSpec((1, n_sub * ATT_Q_TILE, LANES), lambda i, hp, j: (i, j, hp)),
            pl.BlockSpec((1, ATT_Q_TILE, LANES), lambda i, hp, j: (i, group_at(j, n_sub), hp)),
            pl.BlockSpec((1, LANES, t), lambda i, hp, j: (i, hp, 0)),
            pl.BlockSpec((1, t, LANES), lambda i, hp, j: (i, 0, 2 + hp)),
        ] + [bias_spec(off) for off in range(n_sub + 1)],
        out_specs=pl.BlockSpec((1, n_sub * ATT_Q_TILE, LANES), lambda i, hp, j: (i, j, hp)),
        out_shape=jax.ShapeDtypeStruct((b, n_total * ATT_Q_TILE, D_A), BF16),
        scratch_shapes=[score_buf] * n_sub,
        name="na_attention",
        compiler_params=_params(("arbitrary", "arbitrary", "arbitrary")),
    )(za, za, kat, za, *([bias] * (n_sub + 1)))


NA_BIAS_PAD = NA_WIN_ROWS // 2
NA_BIAS_W = 11 * LANES


def _na_bias_plan(n_latent):
    rows = n_latent // GRID_W
    n_groups = n_latent // ATT_Q_TILE
    plan = []
    for g in (0, 1, n_groups - 1):
        base = int(np.clip(g * NA_Q_ROWS - NA_WIN_ROWS // 2, 0, rows - NA_K_ROWS))
        kind = []
        for i in range(NA_Q_ROWS):
            r = g * NA_Q_ROWS + i
            start = int(np.clip(r - NA_WIN_ROWS // 2, 0, rows - NA_WIN_ROWS))
            e0 = base - r + NA_WIN_ROWS - 1 + NA_BIAS_PAD
            assert 0 <= e0 and (e0 - e0 % 2) * GRID_W + NA_K_TOK <= NA_BIAS_W
            kind.append((e0, start - base, start - base + NA_WIN_ROWS - 1))
        plan.append(tuple(kind))
    return tuple(plan)


def _na_bias_kernel(r_ref, o_ref, *, plan):
    shape = (GRID_W, NA_BIAS_W)
    qc = lax.broadcasted_iota(jnp.int32, shape, 0)
    col = lax.broadcasted_iota(jnp.int32, shape, 1)
    kc = col % GRID_W
    dc = kc - qc + (NA_WIN_COLS - 1)
    w0 = jnp.clip(qc - NA_WIN_COLS // 2, 0, GRID_W - NA_WIN_COLS)
    col_ok = (kc >= w0) & (kc < w0 + NA_WIN_COLS)
    tables = []
    for p in range(2):
        a = col // GRID_W + (p - NA_BIAS_PAD)
        acc = jnp.zeros(shape, F32)
        for b in range(2 * NA_WIN_COLS - 1):
            acc = acc + jnp.where(dc == b, r_ref[0, p, b:b + 1, :], 0.0)
        ok = col_ok & (a >= 0) & (a <= 2 * NA_WIN_ROWS - 2)
        tables.append(jnp.where(ok, acc * LOG2_E, NEG_INF))
    kj = lax.broadcasted_iota(jnp.int32, (GRID_W, NA_K_TOK), 1) // GRID_W
    for k, kind in enumerate(plan):
        for i, (e0, jlo, jhi) in enumerate(kind):
            p = e0 % 2
            off = (e0 - p) * GRID_W
            slab = tables[p][:, off:off + NA_K_TOK]
            o_ref[k, 0, i * GRID_W:(i + 1) * GRID_W, :] = jnp.where(
                (kj >= jlo) & (kj <= jhi), slab, NEG_INF)
    o_ref[len(plan), 0] = jnp.full((ATT_Q_TILE, NA_K_TOK), NEG_INF, F32)


def _na_bias_tables(rpb, n_latent):
    h = rpb.shape[0]
    n_blocks = NA_BIAS_W // GRID_W + 2
    rp = jnp.pad(rpb, ((0, 0), (NA_BIAS_PAD, n_blocks - NA_BIAS_PAD - rpb.shape[1]), (0, 0)))
    rep = jnp.transpose(jnp.repeat(rp, GRID_W, axis=1), (0, 2, 1))
    r = jnp.stack([rep[:, :, p * GRID_W:p * GRID_W + NA_BIAS_W] for p in range(2)], axis=1)
    n_dc = rpb.shape[2]
    return pl.pallas_call(
        functools.partial(_na_bias_kernel, plan=_na_bias_plan(n_latent)),
        grid=(h,),
        in_specs=[pl.BlockSpec((1, 2, n_dc, NA_BIAS_W), lambda i: (i, 0, 0, 0))],
        out_specs=pl.BlockSpec((4, 1, ATT_Q_TILE, NA_K_TOK), lambda i: (0, i, 0, 0)),
        out_shape=jax.ShapeDtypeStruct((4, h, ATT_Q_TILE, NA_K_TOK), F32),
        name="na_bias",
        compiler_params=_params(("arbitrary",)),
    )(r)


def _kv_chunk_bounds(t):
    head, tail = DA_KV_RAMP_UP, DA_KV_RAMP_DOWN
    n_slots = DA_PREFETCH + 1
    middle = t - sum(head) - sum(tail)
    units = middle // MXU_DIM_V7X
    n_mid = -(-units // (DA_KV_CHUNK // MXU_DIM_V7X)) if middle > 0 else 0
    n_mid += -(len(head) + n_mid + len(tail)) % n_slots
    if middle < 0 or middle % MXU_DIM_V7X or n_mid > units:
        assert t % DA_KV_SMALL_CHUNK == 0
        sizes = (DA_KV_SMALL_CHUNK,) * (t // DA_KV_SMALL_CHUNK)
    else:
        mid = tuple((units // n_mid + (j < units % n_mid)) * MXU_DIM_V7X for j in range(n_mid))
        sizes = head + mid + tail
    assert max(sizes) <= DA_KV_CHUNK and len(sizes) % n_slots == 0
    return tuple(int(v) for v in np.cumsum((0,) + sizes))


def _diff_kernel(*refs, tq, n_keys, lambda_init, latent):
    if latent:
        qt_ref, qtn_ref, k_ref, vt_ref, lqk_ref, subln_ref, o_ref, s_ref, smax_ref = refs
    else:
        qt_ref, k_ref, vt_ref, lqk_ref, subln_ref, _, o_ref = refs
    n_tiles = qt_ref.shape[2] // tq
    feat = lax.broadcasted_iota(jnp.int32, (LANES, 1), 0)

    def stack_components(qt):
        return jnp.concatenate([jnp.where(feat < DA_QK_DIM, qt, jnp.zeros_like(qt)),
                                jnp.where(feat >= DA_QK_DIM, qt, jnp.zeros_like(qt))], axis=1)

    tile_cols = lambda u: slice(u * tq, (u + 1) * tq)
    lqk = lqk_ref[...]
    lam = (jnp.exp(jnp.sum(lqk[0:1] * lqk[1:2], axis=-1, keepdims=True))
           - jnp.exp(jnp.sum(lqk[2:3] * lqk[3:4], axis=-1, keepdims=True)) + lambda_init)

    def col_reduce(x, op):
        r, c = x.shape
        groups = 8 if r % 64 == 0 else 1
        x = x.reshape(groups, r // (8 * groups), 8, c)
        return op(op(op(x, axis=1), axis=0), axis=0, keepdims=True)

    def scores(k, q):
        s = jnp.dot(k, q, preferred_element_type=F32)
        return s, col_reduce(s, jnp.max)

    def step(s, s_max, vt1, state):
        m, acc = state
        m_new = jnp.maximum(m, s_max)
        alpha = jnp.exp2(m - m_new)
        p = jnp.exp2(s - m_new)
        acc = alpha * acc + jnp.dot(vt1, p.astype(BF16), preferred_element_type=F32)
        return (m_new, acc)

    def init():
        return (jnp.full((1, 2 * tq), NEG_INF, F32), jnp.zeros((DA_VT_ROWS, 2 * tq), F32))

    def finish(state, u):
        _, acc = state
        o = acc[:DA_V_DIM] / acc[DA_V_DIM:DA_V_DIM + 1]
        o = o[:, :tq] - lam * o[:, tq:]
        o = o * lax.rsqrt(jnp.mean(o * o, axis=0, keepdims=True) + EPS)
        o = o * (subln_ref[...] * (1.0 - lambda_init))
        o_ref[0, tile_cols(u), :] = o.T.astype(BF16)

    if not latent:
        s, s_max = scores(k_ref[0], stack_components(qt_ref[0]))
        finish(step(s, s_max, vt_ref[0], init()), 0)
        return

    bounds = _kv_chunk_bounds(n_keys)
    chunk = lambda c: slice(bounds[c], bounds[c + 1])
    size = lambda c: bounds[c + 1] - bounds[c]
    n_chunks = len(bounds) - 1
    ahead = DA_PREFETCH
    n_slots = ahead + 1
    assert n_chunks % n_slots == 0 and n_chunks > ahead
    slot = lambda c: (c % n_slots, slice(0, size(c)))

    @pl.when(pl.program_id(2) == 0)
    def _():
        q0 = stack_components(qt_ref[0, :, tile_cols(0)])
        for c in range(ahead):
            s_ref[slot(c)], smax_ref[c:c + 1, :] = scores(k_ref[0, chunk(c), :], q0)

    maxes = [smax_ref[c:c + 1, :] for c in range(ahead)]
    for u in range(n_tiles):
        qst = stack_components(qt_ref[0, :, tile_cols(u)])
        q_next = stack_components(qt_ref[0, :, tile_cols(u + 1)] if u + 1 < n_tiles
                                  else qtn_ref[0])
        state = init()
        for c in range(n_chunks):
            nxt = c + ahead
            if nxt < n_chunks:
                s_ref[slot(nxt)], new_max = scores(k_ref[0, chunk(nxt), :], qst)
            else:
                s_ref[slot(nxt - n_chunks)], new_max = scores(k_ref[0, chunk(nxt - n_chunks), :],
                                                              q_next)
            state = step(s_ref[slot(c)], maxes[0], vt_ref[0, :, chunk(c)], state)
            maxes = maxes[1:] + [new_max]
        finish(state, u)
    for c in range(ahead):
        smax_ref[c:c + 1, :] = maxes[c]


def _diff_attention(kd, qt, vt, lqk, subln, *, n_latent, lambda_init, ctx_queries):
    b, t, _ = kd.shape
    tq = DA_Q_TILE
    n_q_tiles = n_latent // tq
    n_tiles = next(n for n in DA_TILES_PER_STEP if n_q_tiles % n == 0)
    step_q = n_tiles * tq
    consts = [pl.BlockSpec((4, DA_QK_DIM), lambda *_: (0, 0)),
              pl.BlockSpec((DA_V_DIM, 1), lambda *_: (0, 0))]
    out_rows = t if ctx_queries else n_latent
    oc = pl.pallas_call(
        functools.partial(_diff_kernel, tq=tq, n_keys=t, lambda_init=lambda_init, latent=True),
        grid=(b, DA_HEADS, n_q_tiles // n_tiles),
        in_specs=[
            pl.BlockSpec((1, LANES, step_q), lambda bi, h, i: (bi, h, i)),
            pl.BlockSpec((1, LANES, tq),
                         lambda bi, h, i: (bi, h, jnp.minimum((i + 1) * n_tiles, n_q_tiles - 1))),
            pl.BlockSpec((1, t, LANES), lambda bi, h, i: (bi, 0, h)),
            pl.BlockSpec((1, DA_VT_ROWS, t), lambda bi, h, i: (bi, h, 0)),
        ] + consts,
        out_specs=pl.BlockSpec((1, step_q, LANES), lambda bi, h, i: (bi, i, h)),
        out_shape=jax.ShapeDtypeStruct((b, out_rows, D_C), BF16),
        scratch_shapes=[pltpu.VMEM((DA_PREFETCH + 1, DA_KV_CHUNK, 2 * tq), F32),
                        pltpu.VMEM((DA_PREFETCH, 2 * tq), F32)],
        name="diff_attention",
        compiler_params=_params(("arbitrary", "arbitrary", "arbitrary")),
    )(qt, qt, kd, vt, lqk, subln)
    if not ctx_queries:
        return oc
    ctx_blk = n_latent // CTX_LEN
    return pl.pallas_call(
        functools.partial(_diff_kernel, tq=CTX_LEN, n_keys=CTX_LEN, lambda_init=lambda_init,
                          latent=False),
        grid=(b, DA_HEADS),
        in_specs=[
            pl.BlockSpec((1, LANES, CTX_LEN), lambda bi, h: (bi, h, ctx_blk)),
            pl.BlockSpec((1, CTX_LEN, LANES), lambda bi, h: (bi, ctx_blk, h)),
            pl.BlockSpec((1, DA_VT_ROWS, CTX_LEN), lambda bi, h: (bi, h, ctx_blk)),
        ] + consts + [pl.BlockSpec(memory_space=pl.ANY)],
        out_specs=pl.BlockSpec((1, CTX_LEN, LANES), lambda bi, h: (bi, ctx_blk, h)),
        out_shape=jax.ShapeDtypeStruct((b, out_rows, D_C), BF16),
        input_output_aliases={5: 0},
        name="diff_attention_ctx",
        compiler_params=_params(("arbitrary", "arbitrary")),
    )(qt, kd, vt, lqk, subln, oc)


def _rope_tables(n_latent):
    tok = jnp.arange(n_latent)
    row = (tok // GRID_W).astype(F32)
    col = (tok % GRID_W).astype(F32)
    n_freq = DA_QK_DIM // 4
    freqs = ROPE_BASE ** (-jnp.arange(n_freq, dtype=F32) / n_freq)
    ang = jnp.concatenate([row[:, None] * freqs, col[:, None] * freqs], axis=-1)
    cos, sin = jnp.cos(ang), jnp.sin(ang)
    cos_t = jnp.concatenate([cos, cos, cos, cos], axis=-1)
    sin_t = jnp.concatenate([-sin, sin, -sin, sin], axis=-1)
    cos_t = jnp.concatenate([cos_t, jnp.ones((CTX_LEN, LANES), F32)], axis=0)
    sin_t = jnp.concatenate([sin_t, jnp.zeros((CTX_LEN, LANES), F32)], axis=0)
    return cos_t, sin_t


def kernel(x, c, ctx, c_ctx, w_ada, b_ada, ffn1_w1, ffn1_w3, ffn1_w2, w_in, w_out, na_rpb,
           gm_ws, gm_bs, gm_norm, da_lq1, da_lk1, da_lq2, da_lk2, da_subln,
           ffn2_w1, ffn2_w3, ffn2_w2, final_norm):
    b, n_latent, d = x.shape
    assert d == D_MODEL and ctx.shape[1] == CTX_LEN and b < ADA_ROWS
    assert n_latent % DA_Q_TILE == 0

    cvec = jnp.zeros((ADA_ROWS, d), F32).at[:b].set(c).at[b].set(c_ctx)
    mod = _ada(cvec, w_ada, b_ada).reshape(DEPTH, ADA_ROWS, N_ADA, d)
    cos_t, sin_t = _rope_tables(n_latent)
    gmat = jnp.asarray(np.kron(np.eye(GM_GROUPS), np.ones((GM_WIDTH, GM_WIDTH))), BF16)

    bf16 = lambda w: w.astype(BF16)
    ffn1 = (bf16(ffn1_w1), bf16(ffn1_w3), bf16(ffn1_w2))
    ffn2 = (bf16(ffn2_w1), bf16(ffn2_w3), bf16(ffn2_w2))
    w_in, w_out = bf16(w_in), bf16(w_out)

    xs = x
    for l in range(DEPTH):
        last = l == DEPTH - 1
        lambda_init = 0.8 - 0.6 * math.exp(-0.3 * l)
        modt = jnp.stack([mod[l, :b], jnp.broadcast_to(mod[l, b], (b, N_ADA, d))], axis=1)
        bias = _na_bias_tables(na_rpb[l], n_latent)
        ws_stack = gm_ws[l].reshape(GM_GROUPS * GM_CHUNK, GM_CHUNK).astype(BF16)
        bs_full = jnp.repeat(gm_bs[l].T, GM_WIDTH, axis=1)
        gn = gm_norm[l].reshape(1, D_B)
        lqk = jnp.stack([da_lq1[l], da_lk1[l], da_lq2[l], da_lk2[l]])

        xs = _ffn(xs, modt, *ffn1, layer=l, n_latent=n_latent, mi=0, ctx=ctx if l == 0 else None)
        za, kat, ob, kd, qt, vt = _proj_in(xs, modt, w_in, cos_t, sin_t, ws_stack, bs_full, gn,
                                           gmat, layer=l, n_latent=n_latent)
        oa = _na_attention(za, kat, bias, n_latent=n_latent, ctx_queries=not last)
        oc = _diff_attention(kd, qt, vt, lqk, da_subln[l].reshape(DA_V_DIM, 1),
                             n_latent=n_latent, lambda_init=lambda_init, ctx_queries=not last)
        xs = _ffn(xs, modt, *ffn2, layer=l, n_latent=n_latent, mi=6, mix=(oa, ob, oc, w_out),
                  final_gain=final_norm if last else None)
    return xs
```

```python
import functools
import math

import numpy as np
import jax
import jax.numpy as jnp
from jax import lax
from jax.experimental import pallas as pl
from jax.experimental.pallas import tpu as pltpu

F32 = jnp.float32
BF16 = jnp.bfloat16

D_MODEL = 1024
DEPTH = 2
GRID_W = 64
CTX_LEN = 256
N_ADA = 9
EPS = 1e-6
NEG_INF = -1e30
ROPE_BASE = 10000.0
D_FF = 2816
HEAD_DIM = 64
NA_HEADS = 4
NA_WIN_ROWS = 8
NA_WIN_COLS = 16
D_A = NA_HEADS * HEAD_DIM
GM_GROUPS = 4
GM_WIDTH = 64
GM_CHUNK = 128
D_B = GM_GROUPS * GM_WIDTH
DA_HEADS = 4
DA_QK_DIM = 64
DA_V_DIM = 128
DA_VT_ROWS = DA_V_DIM + 16
D_C = DA_HEADS * DA_V_DIM
D_QK_C = DA_HEADS * 2 * DA_QK_DIM
D_IN = 3 * D_A + 2 * D_B + 2 * D_QK_C + D_C
QK_SCALE = HEAD_DIM ** -0.5
LOG2_E = math.log2(math.e)

LANES = 128
MXU_DIM_V7X = 256
VMEM_BYTES_V7X = 64 * 1024 * 1024
VMEM_LIMIT = VMEM_BYTES_V7X - 8 * 1024 * 1024

TOKEN_TILE = 256
FFN_SUB_TILES = (3, 2, 1)
NA_SUB_GROUPS = (3, 2)
ATT_Q_TILE = 256
NA_Q_ROWS = ATT_Q_TILE // GRID_W
NA_K_ROWS = NA_Q_ROWS + NA_WIN_ROWS
NA_K_TOK = NA_K_ROWS * GRID_W
DA_Q_TILE = 512
DA_TILES_PER_STEP = (2, 1)
DA_KV_CHUNK = 1536
DA_KV_RAMP_UP = (256, 768)
DA_KV_RAMP_DOWN = (768, 512, 256, 256)
DA_PREFETCH = 4
DA_KV_SMALL_CHUNK = 384
FF_CHUNKS = ((0, 768), (768, 1536), (1536, 2304), (2304, D_FF))
ADA_ROWS = 8
ADA_COL_TILE = 1024


def _params(sem):
    return pltpu.CompilerParams(dimension_semantics=sem, vmem_limit_bytes=VMEM_LIMIT)


def _resident(shape):
    nd = len(shape)
    return pl.BlockSpec(shape, lambda *_: (0,) * nd, pipeline_mode=pl.Buffered(1))


def _layer_resident(shape, layer):
    nd = len(shape)
    return pl.BlockSpec((1,) + shape, lambda *_: (layer,) + (0,) * nd,
                        pipeline_mode=pl.Buffered(1))


def _rms(x):
    return x * lax.rsqrt(jnp.mean(x * x, axis=-1, keepdims=True) + EPS)


def _ada_kernel(c_ref, w_ref, b_ref, o_ref):
    c = c_ref[...]
    cs = c / (1.0 + jnp.exp(-c))
    o_ref[0] = jnp.dot(cs, w_ref[0], preferred_element_type=F32,
                       precision=lax.Precision.HIGHEST) + b_ref[0]


def _ada(cvec, w_ada, b_ada):
    depth, d, n = w_ada.shape
    tn = ADA_COL_TILE
    return pl.pallas_call(
        _ada_kernel,
        grid=(depth, n // tn),
        in_specs=[
            pl.BlockSpec((ADA_ROWS, d), lambda l, j: (0, 0)),
            pl.BlockSpec((1, d, tn), lambda l, j: (l, 0, j)),
            pl.BlockSpec((1, 1, tn), lambda l, j: (l, 0, j)),
        ],
        out_specs=pl.BlockSpec((1, ADA_ROWS, tn), lambda l, j: (l, 0, j)),
        out_shape=jax.ShapeDtypeStruct((depth, ADA_ROWS, n), F32),
        name="ada",
        compiler_params=_params(("arbitrary", "arbitrary")),
    )(cvec, w_ada, b_ada.reshape(depth, 1, n))


def _ffn_kernel(*refs, tm, n_sub, n_latent, mi, split_ctx, mix, final):
    refs = list(refs)
    if split_ctx:
        x_refs = refs[:n_sub]
        del refs[:n_sub]
        ctx_ref = refs.pop(0)
    else:
        x_ref = refs.pop(0)
    mod_ref = refs.pop(0)
    if mix:
        oa_ref, ob_ref, oc_ref, wout_ref = refs[:4]
        del refs[:4]
    w1_ref, w3_ref, w2_ref = refs[:3]
    del refs[:3]
    if final:
        gain_ref = refs.pop(0)
    (o_ref,) = refs
    for sub in range(n_sub):
        rows = slice(sub * tm, (sub + 1) * tm)
        is_ctx = (pl.program_id(1) * n_sub + sub) * tm >= n_latent

        def mod(idx):
            if final:
                return mod_ref[0, 0, idx:idx + 1, :]
            return jnp.where(is_ctx, mod_ref[0, 1, idx:idx + 1, :], mod_ref[0, 0, idx:idx + 1, :])

        if split_ctx:
            x = jnp.where(is_ctx, ctx_ref[0], x_refs[sub][0])
        else:
            x = x_ref[0, rows, :]
        if mix:
            cat = jnp.concatenate([oa_ref[0, rows, :], ob_ref[0, rows, :], oc_ref[0, rows, :]],
                                  axis=-1)
            x = x + mod(mi - 1) * jnp.dot(cat, wout_ref[0], preferred_element_type=F32)
        shift = mod(mi)
        scale1 = 1.0 + mod(mi + 1)
        half_gate = 0.5 * mod(mi + 2)
        h = (_rms(x) * scale1 + shift).astype(BF16)
        acc = jnp.zeros((tm, D_MODEL), F32)
        for c0, c1 in FF_CHUNKS:
            a = jnp.dot(h, w1_ref[0, :, c0:c1], preferred_element_type=F32)
            g = jnp.dot(h, w3_ref[0, :, c0:c1], preferred_element_type=F32)
            y = (a / (1.0 + jnp.exp(-a)) * g).astype(BF16)
            acc = acc + jnp.dot(y, w2_ref[0, c0:c1, :], preferred_element_type=F32)
        out = x + half_gate * acc
        if final:
            out = _rms(out) * gain_ref[...]
        o_ref[0, rows, :] = out


def _ffn(x, modt, w1, w3, w2, *, layer, n_latent, mi, ctx=None, mix=None, final_gain=None):
    b, _, d = x.shape
    t = n_latent + CTX_LEN
    tm = TOKEN_TILE
    final = final_gain is not None
    t_out = n_latent if final else t
    n_sub = next(n for n in FFN_SUB_TILES if (t_out // tm) % n == 0)
    step_rows = n_sub * tm
    tile = lambda width: pl.BlockSpec((1, step_rows, width), lambda i, j: (i, j, 0))
    if ctx is None:
        in_specs, args = [tile(d)], [x]
    else:
        assert tm == CTX_LEN and not final
        last_latent = n_latent // tm - 1
        sub_spec = lambda sub: pl.BlockSpec(
            (1, tm, d), lambda i, j: (i, jnp.minimum(j * n_sub + sub, last_latent), 0))
        in_specs = [sub_spec(sub) for sub in range(n_sub)]
        in_specs.append(pl.BlockSpec((1, CTX_LEN, d), lambda i, j: (i, 0, 0)))
        args = [x] * n_sub + [ctx]
    in_specs.append(pl.BlockSpec((1, 2, N_ADA, d), lambda i, j: (i, 0, 0, 0)))
    args.append(modt)
    if mix is not None:
        in_specs += [tile(D_A), tile(D_B), tile(D_C), _layer_resident((d, d), layer)]
        args += list(mix)
    in_specs += [_layer_resident((d, D_FF), layer), _layer_resident((d, D_FF), layer),
                 _layer_resident((D_FF, d), layer)]
    args += [w1, w3, w2]
    if final:
        in_specs.append(_resident((1, d)))
        args.append(final_gain.reshape(1, d))
    return pl.pallas_call(
        functools.partial(_ffn_kernel, tm=tm, n_sub=n_sub, n_latent=n_latent, mi=mi,
                          split_ctx=ctx is not None, mix=mix is not None, final=final),
        grid=(b, t_out // step_rows),
        in_specs=in_specs,
        out_specs=tile(d),
        out_shape=jax.ShapeDtypeStruct((b, t_out, d), F32),
        name="ffn_final" if final else ("ffn_mix" if mix is not None else "ffn"),
        compiler_params=_params(("arbitrary", "arbitrary")),
    )(*args)


def _swap_rope_halves(x, first_half):
    return jnp.where(first_half, pltpu.roll(x, 96, 1), pltpu.roll(x, 32, 1))


def _proj_in_kernel(*refs, tm, n_sub, n_latent):
    for sub in range(n_sub):
        _proj_in_tile(*refs, tm=tm, rows=slice(sub * tm, (sub + 1) * tm),
                      is_ctx=(pl.program_id(1) * n_sub + sub) * tm >= n_latent)


def _proj_in_tile(x_ref, mod_ref, w_ref, cos_ref, sin_ref, ws_ref, bs_ref, gn_ref, gmat_ref,
                  za_ref, kat_ref, ob_ref, kd_ref, qt_ref, vt_ref, *, tm, rows, is_ctx):
    mod = lambda idx: jnp.where(is_ctx, mod_ref[0, 1, idx:idx + 1, :], mod_ref[0, 0, idx:idx + 1, :])
    x = x_ref[0, rows, :]
    h = (_rms(x) * (1.0 + mod(4)) + mod(3)).astype(BF16)

    z_all = jnp.dot(h, w_ref[0], preferred_element_type=F32)
    project = lambda c0, width: z_all[:, c0:c0 + width]

    z = project(3 * D_A, 2 * D_B)
    u = z[:, 0:D_B]
    v = z[:, D_B:]
    u = 0.5 * u * (1.0 + lax.erf(u * (2.0 ** -0.5)))
    v = 0.5 * v * (1.0 + lax.erf(v * (2.0 ** -0.5)))
    v2 = v * v
    v2_hi = v2.astype(BF16)
    v2_lo = (v2 - v2_hi.astype(F32)).astype(BF16)
    ms = (jnp.dot(v2_hi, gmat_ref[...], preferred_element_type=F32)
          + jnp.dot(v2_lo, gmat_ref[...], preferred_element_type=F32)) * (1.0 / GM_WIDTH)
    vn = (v * lax.rsqrt(ms + EPS) * gn_ref[...]).astype(BF16)
    lane_group = lax.broadcasted_iota(jnp.int32, (1, D_B), 1) // GM_WIDTH
    for c in range(tm // GM_CHUNK):
        crows = slice(c * GM_CHUNK, (c + 1) * GM_CHUNK)
        r = jnp.dot(ws_ref[...], vn[crows, :], preferred_element_type=F32)
        s = bs_ref[...]
        for g in range(GM_GROUPS):
            s = s + jnp.where(lane_group == g, r[g * GM_CHUNK:(g + 1) * GM_CHUNK, :], 0.0)
        ob_ref[0, pl.ds(rows.start + c * GM_CHUNK, GM_CHUNK), :] = (u[crows, :] * s).astype(BF16)

    z = project(0, 3 * D_A)
    za_ref[0, rows, 0:D_A] = (z[:, 0:D_A] * (QK_SCALE * LOG2_E)).astype(BF16)
    kat_ref[0, :, rows] = z[:, D_A:2 * D_A].T.astype(BF16)
    za_ref[0, rows, D_A:2 * D_A] = z[:, 2 * D_A:3 * D_A].astype(BF16)

    o = 3 * D_A + 2 * D_B
    cos = cos_ref[rows, :]
    sin = sin_ref[rows, :]
    first_half = (lax.broadcasted_iota(jnp.int32, (1, LANES), 1) % DA_QK_DIM) < (DA_QK_DIM // 2)
    n_q_blocks = D_QK_C // LANES
    for j in range(2 * n_q_blocks):
        if j % n_q_blocks == 0:
            z = project(o + j * LANES, D_QK_C)
        zz = z[:, (j % n_q_blocks) * LANES:(j % n_q_blocks + 1) * LANES]
        r = zz * cos + _swap_rope_halves(zz, first_half) * sin
        if j < n_q_blocks:
            qt_ref[0, j * LANES:(j + 1) * LANES, rows] = (r * (QK_SCALE * LOG2_E)).T.astype(BF16)
        else:
            jk = j - n_q_blocks
            kd_ref[0, rows, jk * LANES:(jk + 1) * LANES] = r.astype(BF16)
    z = project(o + 2 * D_QK_C, D_C)
    for j in range(D_C // LANES):
        r0 = j * DA_VT_ROWS
        vt_ref[0, r0:r0 + DA_V_DIM, rows] = z[:, j * LANES:(j + 1) * LANES].T.astype(BF16)
        vt_ref[0, r0 + DA_V_DIM:r0 + DA_VT_ROWS, rows] = jnp.ones((DA_VT_ROWS - DA_V_DIM, tm), BF16)


def _proj_in(x, modt, w_in, cos_t, sin_t, ws_stack, bs_full, gn, gmat, *, layer, n_latent):
    b, t, d = x.shape
    n_sub = next(n for n in FFN_SUB_TILES if (t // TOKEN_TILE) % n == 0)
    tm = n_sub * TOKEN_TILE
    return pl.pallas_call(
        functools.partial(_proj_in_kernel, tm=TOKEN_TILE, n_sub=n_sub, n_latent=n_latent),
        grid=(b, t // tm),
        in_specs=[
            pl.BlockSpec((1, tm, d), lambda i, j: (i, j, 0)),
            pl.BlockSpec((1, 2, N_ADA, d), lambda i, j: (i, 0, 0, 0)),
            _layer_resident((d, D_IN), layer),
            pl.BlockSpec((tm, LANES), lambda i, j: (j, 0)),
            pl.BlockSpec((tm, LANES), lambda i, j: (j, 0)),
            _resident((GM_GROUPS * GM_CHUNK, GM_CHUNK)),
            _resident((GM_CHUNK, D_B)),
            _resident((1, D_B)),
            _resident((D_B, D_B)),
        ],
        out_specs=[
            pl.BlockSpec((1, tm, 2 * D_A), lambda i, j: (i, j, 0)),
            pl.BlockSpec((1, D_A, tm), lambda i, j: (i, 0, j)),
            pl.BlockSpec((1, tm, D_B), lambda i, j: (i, j, 0)),
            pl.BlockSpec((1, tm, D_QK_C), lambda i, j: (i, j, 0)),
            pl.BlockSpec((1, D_QK_C, tm), lambda i, j: (i, 0, j)),
            pl.BlockSpec((1, DA_HEADS * DA_VT_ROWS, tm), lambda i, j: (i, 0, j)),
        ],
        out_shape=[
            jax.ShapeDtypeStruct((b, t, 2 * D_A), BF16),
            jax.ShapeDtypeStruct((b, D_A, t), BF16),
            jax.ShapeDtypeStruct((b, t, D_B), BF16),
            jax.ShapeDtypeStruct((b, t, D_QK_C), BF16),
            jax.ShapeDtypeStruct((b, D_QK_C, t), BF16),
            jax.ShapeDtypeStruct((b, DA_HEADS * DA_VT_ROWS, t), BF16),
        ],
        name="proj_in",
        compiler_params=_params(("arbitrary", "arbitrary")),
    )(x, modt, w_in, cos_t, sin_t, ws_stack, bs_full, gn, gmat)


def _na_kernel(*refs, n_sub, n_latent):
    q_ref, qn_ref, kt_ref, v_ref = refs[:4]
    bias_refs = refs[4:5 + n_sub]
    o_ref = refs[5 + n_sub]
    s_refs = refs[6 + n_sub:]
    assert len(s_refs) == n_sub >= 2
    first = pl.program_id(2) * n_sub
    n_groups = n_latent // ATT_Q_TILE
    rows = n_latent // GRID_W
    lane = lax.broadcasted_iota(jnp.int32, (1, LANES), 1)

    def window_start(grp):
        base_row = jnp.clip(grp * NA_Q_ROWS - NA_WIN_ROWS // 2, 0, rows - NA_K_ROWS)
        return pl.multiple_of(base_row * GRID_W, ATT_Q_TILE)

    def write_scores(s_ref, q, bias, grp):
        qs = jnp.concatenate([jnp.where(lane < HEAD_DIM, q, jnp.zeros_like(q)),
                              jnp.where(lane >= HEAD_DIM, q, jnp.zeros_like(q))], axis=0)
        sw = jnp.dot(qs, kt_ref[0, :, pl.ds(window_start(grp), NA_K_TOK)],
                     preferred_element_type=F32)
        s_ref[:, 0:NA_K_TOK] = sw + bias.reshape(2 * ATT_Q_TILE, NA_K_TOK)
        s_ref[:, NA_K_TOK:] = jnp.dot(qs, kt_ref[0, :, n_latent:], preferred_element_type=F32)

    def softmax_pv(s_ref, grp, out_rows):
        s = s_ref[...]
        p = jnp.exp2(s - s.max(axis=-1, keepdims=True))
        l = p.sum(axis=-1, keepdims=True)
        p = p.astype(BF16)
        o = (jnp.dot(p[:, 0:NA_K_TOK], v_ref[0, pl.ds(window_start(grp), NA_K_TOK), :],
                     preferred_element_type=F32)
             + jnp.dot(p[:, NA_K_TOK:], v_ref[0, n_latent:, :], preferred_element_type=F32)) / l
        o_ref[0, out_rows, :] = jnp.where(lane < HEAD_DIM, o[:ATT_Q_TILE],
                                          o[ATT_Q_TILE:]).astype(BF16)

    sub_rows = lambda sub: slice(sub * ATT_Q_TILE, (sub + 1) * ATT_Q_TILE)

    @pl.when(first == 0)
    def _():
        write_scores(s_refs[0], q_ref[0, sub_rows(0), :], bias_refs[0][0], first)

    for sub in range(n_sub):
        q_next = q_ref[0, sub_rows(sub + 1), :] if sub + 1 < n_sub else qn_ref[0]
        write_scores(s_refs[(sub + 1) % n_sub], q_next, bias_refs[sub + 1][0],
                     jnp.minimum(first + sub + 1, n_groups))
        softmax_pv(s_refs[sub], first + sub, sub_rows(sub))


def _na_attention(za, kat, bias, *, n_latent, ctx_queries):
    b, t, _ = za.shape
    n_groups = n_latent // ATT_Q_TILE
    n_total = n_groups + 1 if ctx_queries else n_groups
    n_sub = next(n for n in NA_SUB_GROUPS if n_total % n == 0)
    assert t == n_latent + CTX_LEN and CTX_LEN == ATT_Q_TILE
    assert n_groups >= 3 and n_latent // GRID_W >= NA_K_ROWS

    def kind(g):
        return jnp.where(g == 0, 0, jnp.where(g < n_groups - 1, 1, jnp.where(g == n_groups - 1, 2, 3)))

    group_at = lambda j, off: jnp.minimum(j * n_sub + off, n_groups)
    bias_spec = lambda off: pl.BlockSpec((1, 2, ATT_Q_TILE, NA_K_TOK),
                                         lambda i, hp, j: (kind(group_at(j, off)), hp, 0, 0))
    score_buf = pltpu.VMEM((2 * ATT_Q_TILE, NA_K_TOK + CTX_LEN), F32)
    return pl.pallas_call(
        functools.partial(_na_kernel, n_sub=n_sub, n_latent=n_latent),
        grid=(b, NA_HEADS // 2, n_total // n_sub),
        in_specs=[
            pl.BlockSpec((1, n_sub * ATT_Q_TILE, LANES), lambda i, hp, j: (i, j, hp)),
            pl.BlockSpec((1, ATT_Q_TILE, LANES), lambda i, hp, j: (i, group_at(j, n_sub), hp)),
            pl.BlockSpec((1, LANES, t), lambda i, hp, j: (i, hp, 0)),
            pl.BlockSpec((1, t, LANES), lambda i, hp, j: (i, 0, 2 + hp)),
        ] + [bias_spec(off) for off in range(n_sub + 1)],
        out_specs=pl.BlockSpec((1, n_sub * ATT_Q_TILE, LANES), lambda i, hp, j: (i, j, hp)),
        out_shape=jax.ShapeDtypeStruct((b, n_total * ATT_Q_TILE, D_A), BF16),
        scratch_shapes=[score_buf] * n_sub,
        name="na_attention",
        compiler_params=_params(("arbitrary", "arbitrary", "arbitrary")),
    )(za, za, kat, za, *([bias] * (n_sub + 1)))


NA_BIAS_PAD = NA_WIN_ROWS // 2
NA_BIAS_W = 11 * LANES


def _na_bias_plan(n_latent):
    rows = n_latent // GRID_W
    n_groups = n_latent // ATT_Q_TILE
    plan = []
    for g in (0, 1, n_groups - 1):
        base = int(np.clip(g * NA_Q_ROWS - NA_WIN_ROWS // 2, 0, rows - NA_K_ROWS))
        kind = []
        for i in range(NA_Q_ROWS):
            r = g * NA_Q_ROWS + i
            start = int(np.clip(r - NA_WIN_ROWS // 2, 0, rows - NA_WIN_ROWS))
            e0 = base - r + NA_WIN_ROWS - 1 + NA_BIAS_PAD
            assert 0 <= e0 and (e0 - e0 % 2) * GRID_W + NA_K_TOK <= NA_BIAS_W
            kind.append((e0, start - base, start - base + NA_WIN_ROWS - 1))
        plan.append(tuple(kind))
    return tuple(plan)


def _na_bias_kernel(r_ref, o_ref, *, plan):
    shape = (GRID_W, NA_BIAS_W)
    qc = lax.broadcasted_iota(jnp.int32, shape, 0)
    col = lax.broadcasted_iota(jnp.int32, shape, 1)
    kc = col % GRID_W
    dc = kc - qc + (NA_WIN_COLS - 1)
    w0 = jnp.clip(qc - NA_WIN_COLS // 2, 0, GRID_W - NA_WIN_COLS)
    col_ok = (kc >= w0) & (kc < w0 + NA_WIN_COLS)
    tables = []
    for p in range(2):
        a = col // GRID_W + (p - NA_BIAS_PAD)
        acc = jnp.zeros(shape, F32)
        for b in range(2 * NA_WIN_COLS - 1):
            acc = acc + jnp.where(dc == b, r_ref[0, p, b:b + 1, :], 0.0)
        ok = col_ok & (a >= 0) & (a <= 2 * NA_WIN_ROWS - 2)
        tables.append(jnp.where(ok, acc * LOG2_E, NEG_INF))
    kj = lax.broadcasted_iota(jnp.int32, (GRID_W, NA_K_TOK), 1) // GRID_W
    for k, kind in enumerate(plan):
        for i, (e0, jlo, jhi) in enumerate(kind):
            p = e0 % 2
            off = (e0 - p) * GRID_W
            slab = tables[p][:, off:off + NA_K_TOK]
            o_ref[k, 0, i * GRID_W:(i + 1) * GRID_W, :] = jnp.where(
                (kj >= jlo) & (kj <= jhi), slab, NEG_INF)
    o_ref[len(plan), 0] = jnp.full((ATT_Q_TILE, NA_K_TOK), NEG_INF, F32)


def _na_bias_tables(rpb, n_latent):
    h = rpb.shape[0]
    n_blocks = NA_BIAS_W // GRID_W + 2
    rp = jnp.pad(rpb, ((0, 0), (NA_BIAS_PAD, n_blocks - NA_BIAS_PAD - rpb.shape[1]), (0, 0)))
    rep = jnp.transpose(jnp.repeat(rp, GRID_W, axis=1), (0, 2, 1))
    r = jnp.stack([rep[:, :, p * GRID_W:p * GRID_W + NA_BIAS_W] for p in range(2)], axis=1)
    n_dc = rpb.shape[2]
    return pl.pallas_call(
        functools.partial(_na_bias_kernel, plan=_na_bias_plan(n_latent)),
        grid=(h,),
        in_specs=[pl.BlockSpec((1, 2, n_dc, NA_BIAS_W), lambda i: (i, 0, 0, 0))],
        out_specs=pl.BlockSpec((4, 1, ATT_Q_TILE, NA_K_TOK), lambda i: (0, i, 0, 0)),
        out_shape=jax.ShapeDtypeStruct((4, h, ATT_Q_TILE, NA_K_TOK), F32),
        name="na_bias",
        compiler_params=_params(("arbitrary",)),
    )(r)


def _kv_chunk_bounds(t):
    head, tail = DA_KV_RAMP_UP, DA_KV_RAMP_DOWN
    n_slots = DA_PREFETCH + 1
    middle = t - sum(head) - sum(tail)
    units = middle // MXU_DIM_V7X
    n_mid = -(-units // (DA_KV_CHUNK // MXU_DIM_V7X)) if middle > 0 else 0
    n_mid += -(len(head) + n_mid + len(tail)) % n_slots
    if middle < 0 or middle % MXU_DIM_V7X or n_mid > units:
        assert t % DA_KV_SMALL_CHUNK == 0
        sizes = (DA_KV_SMALL_CHUNK,) * (t // DA_KV_SMALL_CHUNK)
    else:
        mid = tuple((units // n_mid + (j < units % n_mid)) * MXU_DIM_V7X for j in range(n_mid))
        sizes = head + mid + tail
    assert max(sizes) <= DA_KV_CHUNK and len(sizes) % n_slots == 0
    return tuple(int(v) for v in np.cumsum((0,) + sizes))


def _diff_kernel(*refs, tq, n_keys, lambda_init, latent):
    if latent:
        qt_ref, qtn_ref, k_ref, vt_ref, lqk_ref, subln_ref, o_ref, s_ref, smax_ref = refs
    else:
        qt_ref, k_ref, vt_ref, lqk_ref, subln_ref, _, o_ref = refs
    n_tiles = qt_ref.shape[2] // tq
    feat = lax.broadcasted_iota(jnp.int32, (LANES, 1), 0)

    def stack_components(qt):
        return jnp.concatenate([jnp.where(feat < DA_QK_DIM, qt, jnp.zeros_like(qt)),
                                jnp.where(feat >= DA_QK_DIM, qt, jnp.zeros_like(qt))], axis=1)

    tile_cols = lambda u: slice(u * tq, (u + 1) * tq)
    lqk = lqk_ref[...]
    lam = (jnp.exp(jnp.sum(lqk[0:1] * lqk[1:2], axis=-1, keepdims=True))
           - jnp.exp(jnp.sum(lqk[2:3] * lqk[3:4], axis=-1, keepdims=True)) + lambda_init)

    def col_reduce(x, op):
        r, c = x.shape
        groups = 8 if r % 64 == 0 else 1
        x = x.reshape(groups, r // (8 * groups), 8, c)
        return op(op(op(x, axis=1), axis=0), axis=0, keepdims=True)

    def scores(k, q):
        s = jnp.dot(k, q, preferred_element_type=F32)
        return s, col_reduce(s, jnp.max)

    def step(s, s_max, vt1, state):
        m, acc = state
        m_new = jnp.maximum(m, s_max)
        alpha = jnp.exp2(m - m_new)
        p = jnp.exp2(s - m_new)
        acc = alpha * acc + jnp.dot(vt1, p.astype(BF16), preferred_element_type=F32)
        return (m_new, acc)

    def init():
        return (jnp.full((1, 2 * tq), NEG_INF, F32), jnp.zeros((DA_VT_ROWS, 2 * tq), F32))

    def finish(state, u):
        _, acc = state
        o = acc[:DA_V_DIM] / acc[DA_V_DIM:DA_V_DIM + 1]
        o = o[:, :tq] - lam * o[:, tq:]
        o = o * lax.rsqrt(jnp.mean(o * o, axis=0, keepdims=True) + EPS)
        o = o * (subln_ref[...] * (1.0 - lambda_init))
        o_ref[0, tile_cols(u), :] = o.T.astype(BF16)

    if not latent:
        s, s_max = scores(k_ref[0], stack_components(qt_ref[0]))
        finish(step(s, s_max, vt_ref[0], init()), 0)
        return

    bounds = _kv_chunk_bounds(n_keys)
    chunk = lambda c: slice(bounds[c], bounds[c + 1])
    size = lambda c: bounds[c + 1] - bounds[c]
    n_chunks = len(bounds) - 1
    ahead = DA_PREFETCH
    n_slots = ahead + 1
    assert n_chunks % n_slots == 0 and n_chunks > ahead
    slot = lambda c: (c % n_slots, slice(0, size(c)))

    @pl.when(pl.program_id(2) == 0)
    def _():
        q0 = stack_components(qt_ref[0, :, tile_cols(0)])
        for c in range(ahead):
            s_ref[slot(c)], smax_ref[c:c + 1, :] = scores(k_ref[0, chunk(c), :], q0)

    maxes = [smax_ref[c:c + 1, :] for c in range(ahead)]
    for u in range(n_tiles):
        qst = stack_components(qt_ref[0, :, tile_cols(u)])
        q_next = stack_components(qt_ref[0, :, tile_cols(u + 1)] if u + 1 < n_tiles
                                  else qtn_ref[0])
        state = init()
        for c in range(n_chunks):
            nxt = c + ahead
            if nxt < n_chunks:
                s_ref[slot(nxt)], new_max = scores(k_ref[0, chunk(nxt), :], qst)
            else:
                s_ref[slot(nxt - n_chunks)], new_max = scores(k_ref[0, chunk(nxt - n_chunks), :],
                                                              q_next)
            state = step(s_ref[slot(c)], maxes[0], vt_ref[0, :, chunk(c)], state)
            maxes = maxes[1:] + [new_max]
        finish(state, u)
    for c in range(ahead):
        smax_ref[c:c + 1, :] = maxes[c]


def _diff_attention(kd, qt, vt, lqk, subln, *, n_latent, lambda_init, ctx_queries):
    b, t, _ = kd.shape
    tq = DA_Q_TILE
    n_q_tiles = n_latent // tq
    n_tiles = next(n for n in DA_TILES_PER_STEP if n_q_tiles % n == 0)
    step_q = n_tiles * tq
    consts = [pl.BlockSpec((4, DA_QK_DIM), lambda *_: (0, 0)),
              pl.BlockSpec((DA_V_DIM, 1), lambda *_: (0, 0))]
    out_rows = t if ctx_queries else n_latent
    oc = pl.pallas_call(
        functools.partial(_diff_kernel, tq=tq, n_keys=t, lambda_init=lambda_init, latent=True),
        grid=(b, DA_HEADS, n_q_tiles // n_tiles),
        in_specs=[
            pl.BlockSpec((1, LANES, step_q), lambda bi, h, i: (bi, h, i)),
            pl.BlockSpec((1, LANES, tq),
                         lambda bi, h, i: (bi, h, jnp.minimum((i + 1) * n_tiles, n_q_tiles - 1))),
            pl.BlockSpec((1, t, LANES), lambda bi, h, i: (bi, 0, h)),
            pl.BlockSpec((1, DA_VT_ROWS, t), lambda bi, h, i: (bi, h, 0)),
        ] + consts,
        out_specs=pl.BlockSpec((1, step_q, LANES), lambda bi, h, i: (bi, i, h)),
        out_shape=jax.ShapeDtypeStruct((b, out_rows, D_C), BF16),
        scratch_shapes=[pltpu.VMEM((DA_PREFETCH + 1, DA_KV_CHUNK, 2 * tq), F32),
                        pltpu.VMEM((DA_PREFETCH, 2 * tq), F32)],
        name="diff_attention",
        compiler_params=_params(("arbitrary", "arbitrary", "arbitrary")),
    )(qt, qt, kd, vt, lqk, subln)
    if not ctx_queries:
        return oc
    ctx_blk = n_latent // CTX_LEN
    return pl.pallas_call(
        functools.partial(_diff_kernel, tq=CTX_LEN, n_keys=CTX_LEN, lambda_init=lambda_init,
                          latent=False),
        grid=(b, DA_HEADS),
        in_specs=[
            pl.BlockSpec((1, LANES, CTX_LEN), lambda bi, h: (bi, h, ctx_blk)),
            pl.BlockSpec((1, CTX_LEN, LANES), lambda bi, h: (bi, ctx_blk, h)),
            pl.BlockSpec((1, DA_VT_ROWS, CTX_LEN), lambda bi, h: (bi, h, ctx_blk)),
        ] + consts + [pl.BlockSpec(memory_space=pl.ANY)],
        out_specs=pl.BlockSpec((1, CTX_LEN, LANES), lambda bi, h: (bi, ctx_blk, h)),
        out_shape=jax.ShapeDtypeStruct((b, out_rows, D_C), BF16),
        input_output_aliases={5: 0},
        name="diff_attention_ctx",
        compiler_params=_params(("arbitrary", "arbitrary")),
    )(qt, kd, vt, lqk, subln, oc)


def _rope_tables(n_latent):
    tok = jnp.arange(n_latent)
    row = (tok // GRID_W).astype(F32)
    col = (tok % GRID_W).astype(F32)
    n_freq = DA_QK_DIM // 4
    freqs = ROPE_BASE ** (-jnp.arange(n_freq, dtype=F32) / n_freq)
    ang = jnp.concatenate([row[:, None] * freqs, col[:, None] * freqs], axis=-1)
    cos, sin = jnp.cos(ang), jnp.sin(ang)
    cos_t = jnp.concatenate([cos, cos, cos, cos], axis=-1)
    sin_t = jnp.concatenate([-sin, sin, -sin, sin], axis=-1)
    cos_t = jnp.concatenate([cos_t, jnp.ones((CTX_LEN, LANES), F32)], axis=0)
    sin_t = jnp.concatenate([sin_t, jnp.zeros((CTX_LEN, LANES), F32)], axis=0)
    return cos_t, sin_t


def kernel(x, c, ctx, c_ctx, w_ada, b_ada, ffn1_w1, ffn1_w3, ffn1_w2, w_in, w_out, na_rpb,
           gm_ws, gm_bs, gm_norm, da_lq1, da_lk1, da_lq2, da_lk2, da_subln,
           ffn2_w1, ffn2_w3, ffn2_w2, final_norm):
    b, n_latent, d = x.shape
    assert d == D_MODEL and ctx.shape[1] == CTX_LEN and b < ADA_ROWS
    assert n_latent % DA_Q_TILE == 0

    cvec = jnp.zeros((ADA_ROWS, d), F32).at[:b].set(c).at[b].set(c_ctx)
    mod = _ada(cvec, w_ada, b_ada).reshape(DEPTH, ADA_ROWS, N_ADA, d)
    cos_t, sin_t = _rope_tables(n_latent)
    gmat = jnp.asarray(np.kron(np.eye(GM_GROUPS), np.ones((GM_WIDTH, GM_WIDTH))), BF16)

    bf16 = lambda w: w.astype(BF16)
    ffn1 = (bf16(ffn1_w1), bf16(ffn1_w3), bf16(ffn1_w2))
    ffn2 = (bf16(ffn2_w1), bf16(ffn2_w3), bf16(ffn2_w2))
    w_in, w_out = bf16(w_in), bf16(w_out)

    xs = x
    for l in range(DEPTH):
        last = l == DEPTH - 1
        lambda_init = 0.8 - 0.6 * math.exp(-0.3 * l)
        modt = jnp.stack([mod[l, :b], jnp.broadcast_to(mod[l, b], (b, N_ADA, d))], axis=1)
        bias = _na_bias_tables(na_rpb[l], n_latent)
        ws_stack = gm_ws[l].reshape(GM_GROUPS * GM_CHUNK, GM_CHUNK).astype(BF16)
        bs_full = jnp.repeat(gm_bs[l].T, GM_WIDTH, axis=1)
        gn = gm_norm[l].reshape(1, D_B)
        lqk = jnp.stack([da_lq1[l], da_lk1[l], da_lq2[l], da_lk2[l]])

        xs = _ffn(xs, modt, *ffn1, layer=l, n_latent=n_latent, mi=0, ctx=ctx if l == 0 else None)
        za, kat, ob, kd, qt, vt = _proj_in(xs, modt, w_in, cos_t, sin_t, ws_stack, bs_full, gn,
                                           gmat, layer=l, n_latent=n_latent)
        oa = _na_attention(za, kat, bias, n_latent=n_latent, ctx_queries=not last)
        oc = _diff_attention(kd, qt, vt, lqk, da_subln[l].reshape(DA_V_DIM, 1),
                             n_latent=n_latent, lambda_init=lambda_init, ctx_queries=not last)
        xs = _ffn(xs, modt, *ffn2, layer=l, n_latent=n_latent, mi=6, mix=(oa, ob, oc, w_out),
                  final_gain=final_norm if last else None)
    return xs
```

```python
import functools
import math

import numpy as np
import jax
import jax.numpy as jnp
from jax import lax
from jax.experimental import pallas as pl
from jax.experimental.pallas import tpu as pltpu

F32 = jnp.float32
BF16 = jnp.bfloat16

D_MODEL = 1024
DEPTH = 2
GRID_W = 64
CTX_LEN = 256
N_ADA = 9
EPS = 1e-6
NEG_INF = -1e30
ROPE_BASE = 10000.0
D_FF = 2816
HEAD_DIM = 64
NA_HEADS = 4
NA_WIN_ROWS = 8
NA_WIN_COLS = 16
D_A = NA_HEADS * HEAD_DIM
GM_GROUPS = 4
GM_WIDTH = 64
GM_CHUNK = 128
D_B = GM_GROUPS * GM_WIDTH
DA_HEADS = 4
DA_QK_DIM = 64
DA_V_DIM = 128
DA_VT_ROWS = DA_V_DIM + 16
D_C = DA_HEADS * DA_V_DIM
D_QK_C = DA_HEADS * 2 * DA_QK_DIM
D_IN = 3 * D_A + 2 * D_B + 2 * D_QK_C + D_C
QK_SCALE = HEAD_DIM ** -0.5
LOG2_E = math.log2(math.e)

LANES = 128
MXU_DIM_V7X = 256
VMEM_BYTES_V7X = 64 * 1024 * 1024
VMEM_LIMIT = VMEM_BYTES_V7X - 8 * 1024 * 1024

TOKEN_TILE = 256
FFN_SUB_TILES = (3, 2, 1)
NA_SUB_GROUPS = (3, 2)
ATT_Q_TILE = 256
NA_Q_ROWS = ATT_Q_TILE // GRID_W
NA_K_ROWS = NA_Q_ROWS + NA_WIN_ROWS
NA_K_TOK = NA_K_ROWS * GRID_W
DA_Q_TILE = 512
DA_TILES_PER_STEP = (2, 1)
DA_KV_CHUNK = 1536
DA_KV_RAMP_UP = (256, 768)
DA_KV_RAMP_DOWN = (768, 512, 256, 256)
DA_PREFETCH = 3
DA_KV_SMALL_CHUNK = 384
FF_CHUNKS = ((0, 768), (768, 1536), (1536, 2304), (2304, D_FF))
ADA_ROWS = 8
ADA_COL_TILE = 1024


def _params(sem):
    return pltpu.CompilerParams(dimension_semantics=sem, vmem_limit_bytes=VMEM_LIMIT)


def _resident(shape):
    nd = len(shape)
    return pl.BlockSpec(shape, lambda *_: (0,) * nd, pipeline_mode=pl.Buffered(1))


def _layer_resident(shape, layer):
    nd = len(shape)
    return pl.BlockSpec((1,) + shape, lambda *_: (layer,) + (0,) * nd,
                        pipeline_mode=pl.Buffered(1))


def _rms(x):
    return x * lax.rsqrt(jnp.mean(x * x, axis=-1, keepdims=True) + EPS)


def _ada_kernel(c_ref, w_ref, b_ref, o_ref):
    c = c_ref[...]
    cs = c / (1.0 + jnp.exp(-c))
    o_ref[0] = jnp.dot(cs, w_ref[0], preferred_element_type=F32,
                       precision=lax.Precision.HIGHEST) + b_ref[0]


def _ada(cvec, w_ada, b_ada):
    depth, d, n = w_ada.shape
    tn = ADA_COL_TILE
    return pl.pallas_call(
        _ada_kernel,
        grid=(depth, n // tn),
        in_specs=[
            pl.BlockSpec((ADA_ROWS, d), lambda l, j: (0, 0)),
            pl.BlockSpec((1, d, tn), lambda l, j: (l, 0, j)),
            pl.BlockSpec((1, 1, tn), lambda l, j: (l, 0, j)),
        ],
        out_specs=pl.BlockSpec((1, ADA_ROWS, tn), lambda l, j: (l, 0, j)),
        out_shape=jax.ShapeDtypeStruct((depth, ADA_ROWS, n), F32),
        name="ada",
        compiler_params=_params(("arbitrary", "arbitrary")),
    )(cvec, w_ada, b_ada.reshape(depth, 1, n))


def _ffn_kernel(*refs, tm, n_sub, n_latent, mi, split_ctx, mix, final):
    refs = list(refs)
    if split_ctx:
        x_refs = refs[:n_sub]
        del refs[:n_sub]
        ctx_ref = refs.pop(0)
    else:
        x_ref = refs.pop(0)
    mod_ref = refs.pop(0)
    if mix:
        oa_ref, ob_ref, oc_ref, wout_ref = refs[:4]
        del refs[:4]
    w1_ref, w3_ref, w2_ref = refs[:3]
    del refs[:3]
    if final:
        gain_ref = refs.pop(0)
    (o_ref,) = refs
    for sub in range(n_sub):
        rows = slice(sub * tm, (sub + 1) * tm)
        is_ctx = (pl.program_id(1) * n_sub + sub) * tm >= n_latent

        def mod(idx):
            if final:
                return mod_ref[0, 0, idx:idx + 1, :]
            return jnp.where(is_ctx, mod_ref[0, 1, idx:idx + 1, :], mod_ref[0, 0, idx:idx + 1, :])

        if split_ctx:
            x = jnp.where(is_ctx, ctx_ref[0], x_refs[sub][0])
        else:
            x = x_ref[0, rows, :]
        if mix:
            cat = jnp.concatenate([oa_ref[0, rows, :], ob_ref[0, rows, :], oc_ref[0, rows, :]],
                                  axis=-1)
            x = x + mod(mi - 1) * jnp.dot(cat, wout_ref[0], preferred_element_type=F32)
        shift = mod(mi)
        scale1 = 1.0 + mod(mi + 1)
        half_gate = 0.5 * mod(mi + 2)
        h = (_rms(x) * scale1 + shift).astype(BF16)
        acc = jnp.zeros((tm, D_MODEL), F32)
        for c0, c1 in FF_CHUNKS:
            a = jnp.dot(h, w1_ref[0, :, c0:c1], preferred_element_type=F32)
            g = jnp.dot(h, w3_ref[0, :, c0:c1], preferred_element_type=F32)
            y = (a / (1.0 + jnp.exp(-a)) * g).astype(BF16)
            acc = acc + jnp.dot(y, w2_ref[0, c0:c1, :], preferred_element_type=F32)
        out = x + half_gate * acc
        if final:
            out = _rms(out) * gain_ref[...]
        o_ref[0, rows, :] = out


def _ffn(x, modt, w1, w3, w2, *, layer, n_latent, mi, ctx=None, mix=None, final_gain=None):
    b, _, d = x.shape
    t = n_latent + CTX_LEN
    tm = TOKEN_TILE
    final = final_gain is not None
    t_out = n_latent if final else t
    n_sub = next(n for n in FFN_SUB_TILES if (t_out // tm) % n == 0)
    step_rows = n_sub * tm
    tile = lambda width: pl.BlockSpec((1, step_rows, width), lambda i, j: (i, j, 0))
    if ctx is None:
        in_specs, args = [tile(d)], [x]
    else:
        assert tm == CTX_LEN and not final
        last_latent = n_latent // tm - 1
        sub_spec = lambda sub: pl.BlockSpec(
            (1, tm, d), lambda i, j: (i, jnp.minimum(j * n_sub + sub, last_latent), 0))
        in_specs = [sub_spec(sub) for sub in range(n_sub)]
        in_specs.append(pl.BlockSpec((1, CTX_LEN, d), lambda i, j: (i, 0, 0)))
        args = [x] * n_sub + [ctx]
    in_specs.append(pl.BlockSpec((1, 2, N_ADA, d), lambda i, j: (i, 0, 0, 0)))
    args.append(modt)
    if mix is not None:
        in_specs += [tile(D_A), tile(D_B), tile(D_C), _layer_resident((d, d), layer)]
        args += list(mix)
    in_specs += [_layer_resident((d, D_FF), layer), _layer_resident((d, D_FF), layer),
                 _layer_resident((D_FF, d), layer)]
    args += [w1, w3, w2]
    if final:
        in_specs.append(_resident((1, d)))
        args.append(final_gain.reshape(1, d))
    return pl.pallas_call(
        functools.partial(_ffn_kernel, tm=tm, n_sub=n_sub, n_latent=n_latent, mi=mi,
                          split_ctx=ctx is not None, mix=mix is not None, final=final),
        grid=(b, t_out // step_rows),
        in_specs=in_specs,
        out_specs=tile(d),
        out_shape=jax.ShapeDtypeStruct((b, t_out, d), F32),
        name="ffn_final" if final else ("ffn_mix" if mix is not None else "ffn"),
        compiler_params=_params(("arbitrary", "arbitrary")),
    )(*args)


def _swap_rope_halves(x, first_half):
    return jnp.where(first_half, pltpu.roll(x, 96, 1), pltpu.roll(x, 32, 1))


def _proj_in_kernel(*refs, tm, n_sub, n_latent):
    for sub in range(n_sub):
        _proj_in_tile(*refs, tm=tm, rows=slice(sub * tm, (sub + 1) * tm),
                      is_ctx=(pl.program_id(1) * n_sub + sub) * tm >= n_latent)


def _proj_in_tile(x_ref, mod_ref, w_ref, cos_ref, sin_ref, ws_ref, bs_ref, gn_ref, gmat_ref,
                  za_ref, kat_ref, ob_ref, kd_ref, qt_ref, vt_ref, *, tm, rows, is_ctx):
    mod = lambda idx: jnp.where(is_ctx, mod_ref[0, 1, idx:idx + 1, :], mod_ref[0, 0, idx:idx + 1, :])
    x = x_ref[0, rows, :]
    h = (_rms(x) * (1.0 + mod(4)) + mod(3)).astype(BF16)

    z_all = jnp.dot(h, w_ref[0], preferred_element_type=F32)
    project = lambda c0, width: z_all[:, c0:c0 + width]

    z = project(3 * D_A, 2 * D_B)
    u = z[:, 0:D_B]
    v = z[:, D_B:]
    u = 0.5 * u * (1.0 + lax.erf(u * (2.0 ** -0.5)))
    v = 0.5 * v * (1.0 + lax.erf(v * (2.0 ** -0.5)))
    v2 = v * v
    v2_hi = v2.astype(BF16)
    v2_lo = (v2 - v2_hi.astype(F32)).astype(BF16)
    ms = (jnp.dot(v2_hi, gmat_ref[...], preferred_element_type=F32)
          + jnp.dot(v2_lo, gmat_ref[...], preferred_element_type=F32)) * (1.0 / GM_WIDTH)
    vn = (v * lax.rsqrt(ms + EPS) * gn_ref[...]).astype(BF16)
    lane_group = lax.broadcasted_iota(jnp.int32, (1, D_B), 1) // GM_WIDTH
    for c in range(tm // GM_CHUNK):
        crows = slice(c * GM_CHUNK, (c + 1) * GM_CHUNK)
        r = jnp.dot(ws_ref[...], vn[crows, :], preferred_element_type=F32)
        s = bs_ref[...]
        for g in range(GM_GROUPS):
            s = s + jnp.where(lane_group == g, r[g * GM_CHUNK:(g + 1) * GM_CHUNK, :], 0.0)
        ob_ref[0, pl.ds(rows.start + c * GM_CHUNK, GM_CHUNK), :] = (u[crows, :] * s).astype(BF16)

    z = project(0, 3 * D_A)
    za_ref[0, rows, 0:D_A] = (z[:, 0:D_A] * (QK_SCALE * LOG2_E)).astype(BF16)
    kat_ref[0, :, rows] = z[:, D_A:2 * D_A].T.astype(BF16)
    za_ref[0, rows, D_A:2 * D_A] = z[:, 2 * D_A:3 * D_A].astype(BF16)

    o = 3 * D_A + 2 * D_B
    cos = cos_ref[rows, :]
    sin = sin_ref[rows, :]
    first_half = (lax.broadcasted_iota(jnp.int32, (1, LANES), 1) % DA_QK_DIM) < (DA_QK_DIM // 2)
    n_q_blocks = D_QK_C // LANES
    for j in range(2 * n_q_blocks):
        if j % n_q_blocks == 0:
            z = project(o + j * LANES, D_QK_C)
        zz = z[:, (j % n_q_blocks) * LANES:(j % n_q_blocks + 1) * LANES]
        r = zz * cos + _swap_rope_halves(zz, first_half) * sin
        if j < n_q_blocks:
            qt_ref[0, j * LANES:(j + 1) * LANES, rows] = (r * (QK_SCALE * LOG2_E)).T.astype(BF16)
        else:
            jk = j - n_q_blocks
            kd_ref[0, rows, jk * LANES:(jk + 1) * LANES] = r.astype(BF16)
    z = project(o + 2 * D_QK_C, D_C)
    for j in range(D_C // LANES):
        r0 = j * DA_VT_ROWS
        vt_ref[0, r0:r0 + DA_V_DIM, rows] = z[:, j * LANES:(j + 1) * LANES].T.astype(BF16)
        vt_ref[0, r0 + DA_V_DIM:r0 + DA_VT_ROWS, rows] = jnp.ones((DA_VT_ROWS - DA_V_DIM, tm), BF16)


def _proj_in(x, modt, w_in, cos_t, sin_t, ws_stack, bs_full, gn, gmat, *, layer, n_latent):
    b, t, d = x.shape
    n_sub = next(n for n in FFN_SUB_TILES if (t // TOKEN_TILE) % n == 0)
    tm = n_sub * TOKEN_TILE
    return pl.pallas_call(
        functools.partial(_proj_in_kernel, tm=TOKEN_TILE, n_sub=n_sub, n_latent=n_latent),
        grid=(b, t // tm),
        in_specs=[
            pl.BlockSpec((1, tm, d), lambda i, j: (i, j, 0)),
            pl.BlockSpec((1, 2, N_ADA, d), lambda i, j: (i, 0, 0, 0)),
            _layer_resident((d, D_IN), layer),
            pl.BlockSpec((tm, LANES), lambda i, j: (j, 0)),
            pl.BlockSpec((tm, LANES), lambda i, j: (j, 0)),
            _resident((GM_GROUPS * GM_CHUNK, GM_CHUNK)),
            _resident((GM_CHUNK, D_B)),
            _resident((1, D_B)),
            _resident((D_B, D_B)),
        ],
        out_specs=[
            pl.BlockSpec((1, tm, 2 * D_A), lambda i, j: (i, j, 0)),
            pl.BlockSpec((1, D_A, tm), lambda i, j: (i, 0, j)),
            pl.BlockSpec((1, tm, D_B), lambda i, j: (i, j, 0)),
            pl.BlockSpec((1, tm, D_QK_C), lambda i, j: (i, j, 0)),
            pl.BlockSpec((1, D_QK_C, tm), lambda i, j: (i, 0, j)),
            pl.BlockSpec((1, DA_HEADS * DA_VT_ROWS, tm), lambda i, j: (i, 0, j)),
        ],
        out_shape=[
            jax.ShapeDtypeStruct((b, t, 2 * D_A), BF16),
            jax.ShapeDtypeStruct((b, D_A, t), BF16),
            jax.ShapeDtypeStruct((b, t, D_B), BF16),
            jax.ShapeDtypeStruct((b, t, D_QK_C), BF16),
            jax.ShapeDtypeStruct((b, D_QK_C, t), BF16),
            jax.ShapeDtypeStruct((b, DA_HEADS * DA_VT_ROWS, t), BF16),
        ],
        name="proj_in",
        compiler_params=_params(("arbitrary", "arbitrary")),
    )(x, modt, w_in, cos_t, sin_t, ws_stack, bs_full, gn, gmat)


def _na_kernel(*refs, n_sub, n_latent):
    q_ref, qn_ref, kt_ref, v_ref = refs[:4]
    bias_refs = refs[4:5 + n_sub]
    o_ref = refs[5 + n_sub]
    s_refs = refs[6 + n_sub:]
    assert len(s_refs) == n_sub >= 2
    first = pl.program_id(2) * n_sub
    n_groups = n_latent // ATT_Q_TILE
    rows = n_latent // GRID_W
    lane = lax.broadcasted_iota(jnp.int32, (1, LANES), 1)

    def window_start(grp):
        base_row = jnp.clip(grp * NA_Q_ROWS - NA_WIN_ROWS // 2, 0, rows - NA_K_ROWS)
        return pl.multiple_of(base_row * GRID_W, ATT_Q_TILE)

    def write_scores(s_ref, q, bias, grp):
        qs = jnp.concatenate([jnp.where(lane < HEAD_DIM, q, jnp.zeros_like(q)),
                              jnp.where(lane >= HEAD_DIM, q, jnp.zeros_like(q))], axis=0)
        sw = jnp.dot(qs, kt_ref[0, :, pl.ds(window_start(grp), NA_K_TOK)],
                     preferred_element_type=F32)
        s_ref[:, 0:NA_K_TOK] = sw + bias.reshape(2 * ATT_Q_TILE, NA_K_TOK)
        s_ref[:, NA_K_TOK:] = jnp.dot(qs, kt_ref[0, :, n_latent:], preferred_element_type=F32)

    def softmax_pv(s_ref, grp, out_rows):
        s = s_ref[...]
        p = jnp.exp2(s - s.max(axis=-1, keepdims=True)).astype(BF16)

        def with_ones(v):
            return jnp.concatenate([v, jnp.ones_like(v)], axis=-1)

        o = (jnp.dot(p[:, 0:NA_K_TOK], with_ones(v_ref[0, pl.ds(window_start(grp), NA_K_TOK), :]),
                     preferred_element_type=F32)
             + jnp.dot(p[:, NA_K_TOK:], with_ones(v_ref[0, n_latent:, :]),
                       preferred_element_type=F32))
        o = o[:, 0:LANES] / o[:, LANES:LANES + 1]
        o_ref[0, out_rows, :] = jnp.where(lane < HEAD_DIM, o[:ATT_Q_TILE],
                                          o[ATT_Q_TILE:]).astype(BF16)

    sub_rows = lambda sub: slice(sub * ATT_Q_TILE, (sub + 1) * ATT_Q_TILE)

    @pl.when(first == 0)
    def _():
        write_scores(s_refs[0], q_ref[0, sub_rows(0), :], bias_refs[0][0], first)

    for sub in range(n_sub):
        q_next = q_ref[0, sub_rows(sub + 1), :] if sub + 1 < n_sub else qn_ref[0]
        write_scores(s_refs[(sub + 1) % n_sub], q_next, bias_refs[sub + 1][0],
                     jnp.minimum(first + sub + 1, n_groups))
        softmax_pv(s_refs[sub], first + sub, sub_rows(sub))


def _na_attention(za, kat, bias, *, n_latent, ctx_queries):
    b, t, _ = za.shape
    n_groups = n_latent // ATT_Q_TILE
    n_total = n_groups + 1 if ctx_queries else n_groups
    n_sub = next(n for n in NA_SUB_GROUPS if n_total % n == 0)
    assert t == n_latent + CTX_LEN and CTX_LEN == ATT_Q_TILE
    assert n_groups >= 3 and n_latent // GRID_W >= NA_K_ROWS

    def kind(g):
        return jnp.where(g == 0, 0, jnp.where(g < n_groups - 1, 1, jnp.where(g == n_groups - 1, 2, 3)))

    group_at = lambda j, off: jnp.minimum(j * n_sub + off, n_groups)
    bias_spec = lambda off: pl.BlockSpec((1, 2, ATT_Q_TILE, NA_K_TOK),
                                         lambda i, hp, j: (kind(group_at(j, off)), hp, 0, 0))
    score_buf = pltpu.VMEM((2 * ATT_Q_TILE, NA_K_TOK + CTX_LEN), F32)
    return pl.pallas_call(
        functools.partial(_na_kernel, n_sub=n_sub, n_latent=n_latent),
        grid=(b, NA_HEADS // 2, n_total // n_sub),
        in_specs=[
            pl.BlockSpec((1, n_sub * ATT_Q_TILE, LANES), lambda i, hp, j: (i, j, hp)),
            pl.BlockSpec((1, ATT_Q_TILE, LANES), lambda i, hp, j: (i, group_at(j, n_sub), hp)),
            pl.BlockSpec((1, LANES, t), lambda i, hp, j: (i, hp, 0)),
            pl.BlockSpec((1, t, LANES), lambda i, hp, j: (i, 0, 2 + hp)),
        ] + [bias_spec(off) for off in range(n_sub + 1)],
        out_specs=pl.BlockSpec((1, n_sub * ATT_Q_TILE, LANES), lambda i, hp, j: (i, j, hp)),
        out_shape=jax.ShapeDtypeStruct((b, n_total * ATT_Q_TILE, D_A), BF16),
        scratch_shapes=[score_buf] * n_sub,
        name="na_attention",
        compiler_params=_params(("arbitrary", "arbitrary", "arbitrary")),
    )(za, za, kat, za, *([bias] * (n_sub + 1)))


NA_BIAS_PAD = NA_WIN_ROWS // 2
NA_BIAS_W = 11 * LANES


def _na_bias_plan(n_latent):
    rows = n_latent // GRID_W
    n_groups = n_latent // ATT_Q_TILE
    plan = []
    for g in (0, 1, n_groups - 1):
        base = int(np.clip(g * NA_Q_ROWS - NA_WIN_ROWS // 2, 0, rows - NA_K_ROWS))
        kind = []
        for i in range(NA_Q_ROWS):
            r = g * NA_Q_ROWS + i
            start = int(np.clip(r - NA_WIN_ROWS // 2, 0, rows - NA_WIN_ROWS))
            e0 = base - r + NA_WIN_ROWS - 1 + NA_BIAS_PAD
            assert 0 <= e0 and (e0 - e0 % 2) * GRID_W + NA_K_TOK <= NA_BIAS_W
            kind.append((e0, start - base, start - base + NA_WIN_ROWS - 1))
        plan.append(tuple(kind))
    return tuple(plan)


def _na_bias_kernel(r_ref, o_ref, *, plan):
    shape = (GRID_W, NA_BIAS_W)
    qc = lax.broadcasted_iota(jnp.int32, shape, 0)
    col = lax.broadcasted_iota(jnp.int32, shape, 1)
    kc = col % GRID_W
    dc = kc - qc + (NA_WIN_COLS - 1)
    w0 = jnp.clip(qc - NA_WIN_COLS // 2, 0, GRID_W - NA_WIN_COLS)
    col_ok = (kc >= w0) & (kc < w0 + NA_WIN_COLS)
    tables = []
    for p in range(2):
        a = col // GRID_W + (p - NA_BIAS_PAD)
        acc = jnp.zeros(shape, F32)
        for b in range(2 * NA_WIN_COLS - 1):
            acc = acc + jnp.where(dc == b, r_ref[0, p, b:b + 1, :], 0.0)
        ok = col_ok & (a >= 0) & (a <= 2 * NA_WIN_ROWS - 2)
        tables.append(jnp.where(ok, acc * LOG2_E, NEG_INF))
    kj = lax.broadcasted_iota(jnp.int32, (GRID_W, NA_K_TOK), 1) // GRID_W
    for k, kind in enumerate(plan):
        for i, (e0, jlo, jhi) in enumerate(kind):
            p = e0 % 2
            off = (e0 - p) * GRID_W
            slab = tables[p][:, off:off + NA_K_TOK]
            o_ref[k, 0, i * GRID_W:(i + 1) * GRID_W, :] = jnp.where(
                (kj >= jlo) & (kj <= jhi), slab, NEG_INF)
    o_ref[len(plan), 0] = jnp.full((ATT_Q_TILE, NA_K_TOK), NEG_INF, F32)


def _na_bias_tables(rpb, n_latent):
    h = rpb.shape[0]
    n_blocks = NA_BIAS_W // GRID_W + 2
    rp = jnp.pad(rpb, ((0, 0), (NA_BIAS_PAD, n_blocks - NA_BIAS_PAD - rpb.shape[1]), (0, 0)))
    rep = jnp.transpose(jnp.repeat(rp, GRID_W, axis=1), (0, 2, 1))
    r = jnp.stack([rep[:, :, p * GRID_W:p * GRID_W + NA_BIAS_W] for p in range(2)], axis=1)
    n_dc = rpb.shape[2]
    return pl.pallas_call(
        functools.partial(_na_bias_kernel, plan=_na_bias_plan(n_latent)),
        grid=(h,),
        in_specs=[pl.BlockSpec((1, 2, n_dc, NA_BIAS_W), lambda i: (i, 0, 0, 0))],
        out_specs=pl.BlockSpec((4, 1, ATT_Q_TILE, NA_K_TOK), lambda i: (0, i, 0, 0)),
        out_shape=jax.ShapeDtypeStruct((4, h, ATT_Q_TILE, NA_K_TOK), F32),
        name="na_bias",
        compiler_params=_params(("arbitrary",)),
    )(r)


def _kv_chunk_bounds(t):
    head, tail = DA_KV_RAMP_UP, DA_KV_RAMP_DOWN
    n_slots = DA_PREFETCH + 1
    middle = t - sum(head) - sum(tail)
    units = middle // MXU_DIM_V7X
    n_mid = -(-units // (DA_KV_CHUNK // MXU_DIM_V7X)) if middle > 0 else 0
    n_mid += -(len(head) + n_mid + len(tail)) % n_slots
    if middle < 0 or middle % MXU_DIM_V7X or n_mid > units:
        assert t % DA_KV_SMALL_CHUNK == 0
        sizes = (DA_KV_SMALL_CHUNK,) * (t // DA_KV_SMALL_CHUNK)
    else:
        mid = tuple((units // n_mid + (j < units % n_mid)) * MXU_DIM_V7X for j in range(n_mid))
        sizes = head + mid + tail
    assert max(sizes) <= DA_KV_CHUNK and len(sizes) % n_slots == 0
    return tuple(int(v) for v in np.cumsum((0,) + sizes))


def _diff_kernel(*refs, tq, n_keys, lambda_init, latent):
    if latent:
        qt_ref, qtn_ref, k_ref, vt_ref, lqk_ref, subln_ref, o_ref, s_ref, smax_ref = refs
    else:
        qt_ref, k_ref, vt_ref, lqk_ref, subln_ref, _, o_ref = refs
    n_tiles = qt_ref.shape[2] // tq
    feat = lax.broadcasted_iota(jnp.int32, (LANES, 1), 0)

    def stack_components(qt):
        return jnp.concatenate([jnp.where(feat < DA_QK_DIM, qt, jnp.zeros_like(qt)),
                                jnp.where(feat >= DA_QK_DIM, qt, jnp.zeros_like(qt))], axis=1)

    tile_cols = lambda u: slice(u * tq, (u + 1) * tq)
    lqk = lqk_ref[...]
    lam = (jnp.exp(jnp.sum(lqk[0:1] * lqk[1:2], axis=-1, keepdims=True))
           - jnp.exp(jnp.sum(lqk[2:3] * lqk[3:4], axis=-1, keepdims=True)) + lambda_init)

    def col_reduce(x, op):
        r, c = x.shape
        groups = 8 if r % 64 == 0 else 1
        x = x.reshape(groups, r // (8 * groups), 8, c)
        return op(op(op(x, axis=1), axis=0), axis=0, keepdims=True)

    def scores(k, q):
        s = jnp.dot(k, q, preferred_element_type=F32)
        return s, col_reduce(s, jnp.max)

    def step(s, s_max, vt1, state):
        m, acc = state
        m_new = jnp.maximum(m, s_max)
        alpha = jnp.exp2(m - m_new)
        p = jnp.exp2(s - m_new)
        acc = alpha * acc + jnp.dot(vt1, p.astype(BF16), preferred_element_type=F32)
        return (m_new, acc)

    def init():
        return (jnp.full((1, 2 * tq), NEG_INF, F32), jnp.zeros((DA_VT_ROWS, 2 * tq), F32))

    def finish(state, u):
        _, acc = state
        o = acc[:DA_V_DIM] / acc[DA_V_DIM:DA_V_DIM + 1]
        o = o[:, :tq] - lam * o[:, tq:]
        o = o * lax.rsqrt(jnp.mean(o * o, axis=0, keepdims=True) + EPS)
        o = o * (subln_ref[...] * (1.0 - lambda_init))
        o_ref[0, tile_cols(u), :] = o.T.astype(BF16)

    if not latent:
        s, s_max = scores(k_ref[0], stack_components(qt_ref[0]))
        finish(step(s, s_max, vt_ref[0], init()), 0)
        return

    bounds = _kv_chunk_bounds(n_keys)
    chunk = lambda c: slice(bounds[c], bounds[c + 1])
    size = lambda c: bounds[c + 1] - bounds[c]
    n_chunks = len(bounds) - 1
    ahead = DA_PREFETCH
    n_slots = ahead + 1
    assert n_chunks % n_slots == 0 and n_chunks > ahead
    slot = lambda c: (c % n_slots, slice(0, size(c)))

    @pl.when(pl.program_id(2) == 0)
    def _():
        q0 = stack_components(qt_ref[0, :, tile_cols(0)])
        for c in range(ahead):
            s_ref[slot(c)], smax_ref[c:c + 1, :] = scores(k_ref[0, chunk(c), :], q0)

    maxes = [smax_ref[c:c + 1, :] for c in range(ahead)]
    for u in range(n_tiles):
        qst = stack_components(qt_ref[0, :, tile_cols(u)])
        q_next = stack_components(qt_ref[0, :, tile_cols(u + 1)] if u + 1 < n_tiles
                                  else qtn_ref[0])
        state = init()
        for c in range(n_chunks):
            nxt = c + ahead
            if nxt < n_chunks:
                s_ref[slot(nxt)], new_max = scores(k_ref[0, chunk(nxt), :], qst)
            else:
                s_ref[slot(nxt - n_chunks)], new_max = scores(k_ref[0, chunk(nxt - n_chunks), :],
                                                              q_next)
            state = step(s_ref[slot(c)], maxes[0], vt_ref[0, :, chunk(c)], state)
            maxes = maxes[1:] + [new_max]
        finish(state, u)
    for c in range(ahead):
        smax_ref[c:c + 1, :] = maxes[c]


def _diff_attention(kd, qt, vt, lqk, subln, *, n_latent, lambda_init, ctx_queries):
    b, t, _ = kd.shape
    tq = DA_Q_TILE
    n_q_tiles = n_latent // tq
    n_tiles = next(n for n in DA_TILES_PER_STEP if n_q_tiles % n == 0)
    step_q = n_tiles * tq
    consts = [pl.BlockSpec((4, DA_QK_DIM), lambda *_: (0, 0)),
              pl.BlockSpec((DA_V_DIM, 1), lambda *_: (0, 0))]
    out_rows = t if ctx_queries else n_latent
    oc = pl.pallas_call(
        functools.partial(_diff_kernel, tq=tq, n_keys=t, lambda_init=lambda_init, latent=True),
        grid=(b, DA_HEADS, n_q_tiles // n_tiles),
        in_specs=[
            pl.BlockSpec((1, LANES, step_q), lambda bi, h, i: (bi, h, i)),
            pl.BlockSpec((1, LANES, tq),
                         lambda bi, h, i: (bi, h, jnp.minimum((i + 1) * n_tiles, n_q_tiles - 1))),
            pl.BlockSpec((1, t, LANES), lambda bi, h, i: (bi, 0, h)),
            pl.BlockSpec((1, DA_VT_ROWS, t), lambda bi, h, i: (bi, h, 0)),
        ] + consts,
        out_specs=pl.BlockSpec((1, step_q, LANES), lambda bi, h, i: (bi, i, h)),
        out_shape=jax.ShapeDtypeStruct((b, out_rows, D_C), BF16),
        scratch_shapes=[pltpu.VMEM((DA_PREFETCH + 1, DA_KV_CHUNK, 2 * tq), F32),
                        pltpu.VMEM((DA_PREFETCH, 2 * tq), F32)],
        name="diff_attention",
        compiler_params=_params(("arbitrary", "arbitrary", "arbitrary")),
    )(qt, qt, kd, vt, lqk, subln)
    if not ctx_queries:
        return oc
    ctx_blk = n_latent // CTX_LEN
    return pl.pallas_call(
        functools.partial(_diff_kernel, tq=CTX_LEN, n_keys=CTX_LEN, lambda_init=lambda_init,
                          latent=False),
        grid=(b, DA_HEADS),
        in_specs=[
            pl.BlockSpec((1, LANES, CTX_LEN), lambda bi, h: (bi, h, ctx_blk)),
            pl.BlockSpec((1, CTX_LEN, LANES), lambda bi, h: (bi, ctx_blk, h)),
            pl.BlockSpec((1, DA_VT_ROWS, CTX_LEN), lambda bi, h: (bi, h, ctx_blk)),
        ] + consts + [pl.BlockSpec(memory_space=pl.ANY)],
        out_specs=pl.BlockSpec((1, CTX_LEN, LANES), lambda bi, h: (bi, ctx_blk, h)),
        out_shape=jax.ShapeDtypeStruct((b, out_rows, D_C), BF16),
        input_output_aliases={5: 0},
        name="diff_attention_ctx",
        compiler_params=_params(("arbitrary", "arbitrary")),
    )(qt, kd, vt, lqk, subln, oc)


def _rope_tables(n_latent):
    tok = jnp.arange(n_latent)
    row = (tok // GRID_W).astype(F32)
    col = (tok % GRID_W).astype(F32)
    n_freq = DA_QK_DIM // 4
    freqs = ROPE_BASE ** (-jnp.arange(n_freq, dtype=F32) / n_freq)
    ang = jnp.concatenate([row[:, None] * freqs, col[:, None] * freqs], axis=-1)
    cos, sin = jnp.cos(ang), jnp.sin(ang)
    cos_t = jnp.concatenate([cos, cos, cos, cos], axis=-1)
    sin_t = jnp.concatenate([-sin, sin, -sin, sin], axis=-1)
    cos_t = jnp.concatenate([cos_t, jnp.ones((CTX_LEN, LANES), F32)], axis=0)
    sin_t = jnp.concatenate([sin_t, jnp.zeros((CTX_LEN, LANES), F32)], axis=0)
    return cos_t, sin_t


def kernel(x, c, ctx, c_ctx, w_ada, b_ada, ffn1_w1, ffn1_w3, ffn1_w2, w_in, w_out, na_rpb,
           gm_ws, gm_bs, gm_norm, da_lq1, da_lk1, da_lq2, da_lk2, da_subln,
           ffn2_w1, ffn2_w3, ffn2_w2, final_norm):
    b, n_latent, d = x.shape
    assert d == D_MODEL and ctx.shape[1] == CTX_LEN and b < ADA_ROWS
    assert n_latent % DA_Q_TILE == 0

    cvec = jnp.zeros((ADA_ROWS, d), F32).at[:b].set(c).at[b].set(c_ctx)
    mod = _ada(cvec, w_ada, b_ada).reshape(DEPTH, ADA_ROWS, N_ADA, d)
    cos_t, sin_t = _rope_tables(n_latent)
    gmat = jnp.asarray(np.kron(np.eye(GM_GROUPS), np.ones((GM_WIDTH, GM_WIDTH))), BF16)

    bf16 = lambda w: w.astype(BF16)
    ffn1 = (bf16(ffn1_w1), bf16(ffn1_w3), bf16(ffn1_w2))
    ffn2 = (bf16(ffn2_w1), bf16(ffn2_w3), bf16(ffn2_w2))
    w_in, w_out = bf16(w_in), bf16(w_out)

    xs = x
    for l in range(DEPTH):
        last = l == DEPTH - 1
        lambda_init = 0.8 - 0.6 * math.exp(-0.3 * l)
        modt = jnp.stack([mod[l, :b], jnp.broadcast_to(mod[l, b], (b, N_ADA, d))], axis=1)
        bias = _na_bias_tables(na_rpb[l], n_latent)
        ws_stack = gm_ws[l].reshape(GM_GROUPS * GM_CHUNK, GM_CHUNK).astype(BF16)
        bs_full = jnp.repeat(gm_bs[l].T, GM_WIDTH, axis=1)
        gn = gm_norm[l].reshape(1, D_B)
        lqk = jnp.stack([da_lq1[l], da_lk1[l], da_lq2[l], da_lk2[l]])

        xs = _ffn(xs, modt, *ffn1, layer=l, n_latent=n_latent, mi=0, ctx=ctx if l == 0 else None)
        za, kat, ob, kd, qt, vt = _proj_in(xs, modt, w_in, cos_t, sin_t, ws_stack, bs_full, gn,
                                           gmat, layer=l, n_latent=n_latent)
        oa = _na_attention(za, kat, bias, n_latent=n_latent, ctx_queries=not last)
        oc = _diff_attention(kd, qt, vt, lqk, da_subln[l].reshape(DA_V_DIM, 1),
                             n_latent=n_latent, lambda_init=lambda_init, ctx_queries=not last)
        xs = _ffn(xs, modt, *ffn2, layer=l, n_latent=n_latent, mi=6, mix=(oa, ob, oc, w_out),
                  final_gain=final_norm if last else None)
    return xs
```

```python
import functools
import math

import numpy as np
import jax
import jax.numpy as jnp
from jax import lax
from jax.experimental import pallas as pl
from jax.experimental.pallas import tpu as pltpu

F32 = jnp.float32
BF16 = jnp.bfloat16

D_MODEL = 1024
DEPTH = 2
GRID_W = 64
CTX_LEN = 256
N_ADA = 9
EPS = 1e-6
NEG_INF = -1e30
ROPE_BASE = 10000.0
D_FF = 2816
HEAD_DIM = 64
NA_HEADS = 4
NA_WIN_ROWS = 8
NA_WIN_COLS = 16
D_A = NA_HEADS * HEAD_DIM
GM_GROUPS = 4
GM_WIDTH = 64
GM_CHUNK = 128
D_B = GM_GROUPS * GM_WIDTH
DA_HEADS = 4
DA_QK_DIM = 64
DA_V_DIM = 128
DA_VT_ROWS = DA_V_DIM + 16
D_C = DA_HEADS * DA_V_DIM
D_QK_C = DA_HEADS * 2 * DA_QK_DIM
D_IN = 3 * D_A + 2 * D_B + 2 * D_QK_C + D_C
QK_SCALE = HEAD_DIM ** -0.5
LOG2_E = math.log2(math.e)

LANES = 128
MXU_DIM_V7X = 256
VMEM_BYTES_V7X = 64 * 1024 * 1024
VMEM_LIMIT = VMEM_BYTES_V7X - 8 * 1024 * 1024

TOKEN_TILE = 256
FFN_SUB_TILES = (3, 2, 1)
NA_SUB_GROUPS = (3, 2)
ATT_Q_TILE = 256
NA_Q_ROWS = ATT_Q_TILE // GRID_W
NA_K_ROWS = NA_Q_ROWS + NA_WIN_ROWS
NA_K_TOK = NA_K_ROWS * GRID_W
DA_Q_TILE = 512
DA_TILES_PER_STEP = (2, 1)
DA_KV_CHUNK = 1536
DA_KV_RAMP_UP = (256, 768)
DA_KV_RAMP_DOWN = (768, 512, 256, 256)
DA_PREFETCH = 3
DA_KV_SMALL_CHUNK = 384
FF_CHUNKS = ((0, 768), (768, 1536), (1536, 2304), (2304, D_FF))
ADA_ROWS = 8
ADA_COL_TILE = 1024


def _params(sem):
    return pltpu.CompilerParams(dimension_semantics=sem, vmem_limit_bytes=VMEM_LIMIT)


def _resident(shape):
    nd = len(shape)
    return pl.BlockSpec(shape, lambda *_: (0,) * nd, pipeline_mode=pl.Buffered(1))


def _layer_resident(shape, layer):
    nd = len(shape)
    return pl.BlockSpec((1,) + shape, lambda *_: (layer,) + (0,) * nd,
                        pipeline_mode=pl.Buffered(1))


def _rms(x):
    return x * lax.rsqrt(jnp.mean(x * x, axis=-1, keepdims=True) + EPS)


def _ada_kernel(c_ref, w_ref, b_ref, o_ref):
    c = c_ref[...]
    cs = c / (1.0 + jnp.exp(-c))
    o_ref[0] = jnp.dot(cs, w_ref[0], preferred_element_type=F32,
                       precision=lax.Precision.HIGHEST) + b_ref[0]


def _ada(cvec, w_ada, b_ada):
    depth, d, n = w_ada.shape
    tn = ADA_COL_TILE
    return pl.pallas_call(
        _ada_kernel,
        grid=(depth, n // tn),
        in_specs=[
            pl.BlockSpec((ADA_ROWS, d), lambda l, j: (0, 0)),
            pl.BlockSpec((1, d, tn), lambda l, j: (l, 0, j)),
            pl.BlockSpec((1, 1, tn), lambda l, j: (l, 0, j)),
        ],
        out_specs=pl.BlockSpec((1, ADA_ROWS, tn), lambda l, j: (l, 0, j)),
        out_shape=jax.ShapeDtypeStruct((depth, ADA_ROWS, n), F32),
        name="ada",
        compiler_params=_params(("arbitrary", "arbitrary")),
    )(cvec, w_ada, b_ada.reshape(depth, 1, n))


def _ffn_kernel(*refs, tm, n_sub, n_latent, mi, split_ctx, mix, final):
    refs = list(refs)
    if split_ctx:
        x_refs = refs[:n_sub]
        del refs[:n_sub]
        ctx_ref = refs.pop(0)
    else:
        x_ref = refs.pop(0)
    mod_ref = refs.pop(0)
    if mix:
        oa_ref, ob_ref, oc_ref, wout_ref = refs[:4]
        del refs[:4]
    w1_ref, w3_ref, w2_ref = refs[:3]
    del refs[:3]
    if final:
        gain_ref = refs.pop(0)
    (o_ref,) = refs
    for sub in range(n_sub):
        rows = slice(sub * tm, (sub + 1) * tm)
        is_ctx = (pl.program_id(1) * n_sub + sub) * tm >= n_latent

        def mod(idx):
            if final:
                return mod_ref[0, 0, idx:idx + 1, :]
            return jnp.where(is_ctx, mod_ref[0, 1, idx:idx + 1, :], mod_ref[0, 0, idx:idx + 1, :])

        if split_ctx:
            x = jnp.where(is_ctx, ctx_ref[0], x_refs[sub][0])
        else:
            x = x_ref[0, rows, :]
        if mix:
            cat = jnp.concatenate([oa_ref[0, rows, :], ob_ref[0, rows, :], oc_ref[0, rows, :]],
                                  axis=-1)
            x = x + mod(mi - 1) * jnp.dot(cat, wout_ref[0], preferred_element_type=F32)
        shift = mod(mi)
        scale1 = 1.0 + mod(mi + 1)
        half_gate = 0.5 * mod(mi + 2)
        h = (_rms(x) * scale1 + shift).astype(BF16)
        acc = jnp.zeros((tm, D_MODEL), F32)
        for c0, c1 in FF_CHUNKS:
            a = jnp.dot(h, w1_ref[0, :, c0:c1], preferred_element_type=F32)
            g = jnp.dot(h, w3_ref[0, :, c0:c1], preferred_element_type=F32)
            y = (a / (1.0 + jnp.exp(-a)) * g).astype(BF16)
            acc = acc + jnp.dot(y, w2_ref[0, c0:c1, :], preferred_element_type=F32)
        out = x + half_gate * acc
        if final:
            out = _rms(out) * gain_ref[...]
        o_ref[0, rows, :] = out


def _ffn(x, modt, w1, w3, w2, *, layer, n_latent, mi, ctx=None, mix=None, final_gain=None):
    b, _, d = x.shape
    t = n_latent + CTX_LEN
    tm = TOKEN_TILE
    final = final_gain is not None
    t_out = n_latent if final else t
    n_sub = next(n for n in FFN_SUB_TILES if (t_out // tm) % n == 0)
    step_rows = n_sub * tm
    tile = lambda width: pl.BlockSpec((1, step_rows, width), lambda i, j: (i, j, 0))
    if ctx is None:
        in_specs, args = [tile(d)], [x]
    else:
        assert tm == CTX_LEN and not final
        last_latent = n_latent // tm - 1
        sub_spec = lambda sub: pl.BlockSpec(
            (1, tm, d), lambda i, j: (i, jnp.minimum(j * n_sub + sub, last_latent), 0))
        in_specs = [sub_spec(sub) for sub in range(n_sub)]
        in_specs.append(pl.BlockSpec((1, CTX_LEN, d), lambda i, j: (i, 0, 0)))
        args = [x] * n_sub + [ctx]
    in_specs.append(pl.BlockSpec((1, 2, N_ADA, d), lambda i, j: (i, 0, 0, 0)))
    args.append(modt)
    if mix is not None:
        in_specs += [tile(D_A), tile(D_B), tile(D_C), _layer_resident((d, d), layer)]
        args += list(mix)
    in_specs += [_layer_resident((d, D_FF), layer), _layer_resident((d, D_FF), layer),
                 _layer_resident((D_FF, d), layer)]
    args += [w1, w3, w2]
    if final:
        in_specs.append(_resident((1, d)))
        args.append(final_gain.reshape(1, d))
    return pl.pallas_call(
        functools.partial(_ffn_kernel, tm=tm, n_sub=n_sub, n_latent=n_latent, mi=mi,
                          split_ctx=ctx is not None, mix=mix is not None, final=final),
        grid=(b, t_out // step_rows),
        in_specs=in_specs,
        out_specs=tile(d),
        out_shape=jax.ShapeDtypeStruct((b, t_out, d), F32),
        name="ffn_final" if final else ("ffn_mix" if mix is not None else "ffn"),
        compiler_params=_params(("arbitrary", "arbitrary")),
    )(*args)


def _swap_rope_halves(x, first_half):
    return jnp.where(first_half, pltpu.roll(x, 96, 1), pltpu.roll(x, 32, 1))


def _proj_in_kernel(*refs, tm, n_sub, n_latent):
    for sub in range(n_sub):
        _proj_in_tile(*refs, tm=tm, rows=slice(sub * tm, (sub + 1) * tm),
                      is_ctx=(pl.program_id(1) * n_sub + sub) * tm >= n_latent)


def _proj_in_tile(x_ref, mod_ref, w_ref, cos_ref, sin_ref, ws_ref, bs_ref, gn_ref, gmat_ref,
                  za_ref, kat_ref, ob_ref, kd_ref, qt_ref, vt_ref, *, tm, rows, is_ctx):
    mod = lambda idx: jnp.where(is_ctx, mod_ref[0, 1, idx:idx + 1, :], mod_ref[0, 0, idx:idx + 1, :])
    x = x_ref[0, rows, :]
    h = (_rms(x) * (1.0 + mod(4)) + mod(3)).astype(BF16)

    z_all = jnp.dot(h, w_ref[0], preferred_element_type=F32)
    project = lambda c0, width: z_all[:, c0:c0 + width]

    z = project(3 * D_A, 2 * D_B)
    u = z[:, 0:D_B]
    v = z[:, D_B:]
    u = 0.5 * u * (1.0 + lax.erf(u * (2.0 ** -0.5)))
    v = 0.5 * v * (1.0 + lax.erf(v * (2.0 ** -0.5)))
    v2 = v * v
    v2_hi = v2.astype(BF16)
    v2_lo = (v2 - v2_hi.astype(F32)).astype(BF16)
    ms = (jnp.dot(v2_hi, gmat_ref[...], preferred_element_type=F32)
          + jnp.dot(v2_lo, gmat_ref[...], preferred_element_type=F32)) * (1.0 / GM_WIDTH)
    vn = (v * lax.rsqrt(ms + EPS) * gn_ref[...]).astype(BF16)
    lane_group = lax.broadcasted_iota(jnp.int32, (1, D_B), 1) // GM_WIDTH
    for c in range(tm // GM_CHUNK):
        crows = slice(c * GM_CHUNK, (c + 1) * GM_CHUNK)
        r = jnp.dot(ws_ref[...], vn[crows, :], preferred_element_type=F32)
        s = bs_ref[...]
        for g in range(GM_GROUPS):
            s = s + jnp.where(lane_group == g, r[g * GM_CHUNK:(g + 1) * GM_CHUNK, :], 0.0)
        ob_ref[0, pl.ds(rows.start + c * GM_CHUNK, GM_CHUNK), :] = (u[crows, :] * s).astype(BF16)

    z = project(0, 3 * D_A)
    za_ref[0, rows, 0:D_A] = (z[:, 0:D_A] * (QK_SCALE * LOG2_E)).astype(BF16)
    kat_ref[0, :, rows] = z[:, D_A:2 * D_A].T.astype(BF16)
    za_ref[0, rows, D_A:2 * D_A] = z[:, 2 * D_A:3 * D_A].astype(BF16)

    o = 3 * D_A + 2 * D_B
    cos = cos_ref[rows, :]
    sin = sin_ref[rows, :]
    first_half = (lax.broadcasted_iota(jnp.int32, (1, LANES), 1) % DA_QK_DIM) < (DA_QK_DIM // 2)
    n_q_blocks = D_QK_C // LANES
    for j in range(2 * n_q_blocks):
        if j % n_q_blocks == 0:
            z = project(o + j * LANES, D_QK_C)
        zz = z[:, (j % n_q_blocks) * LANES:(j % n_q_blocks + 1) * LANES]
        r = zz * cos + _swap_rope_halves(zz, first_half) * sin
        if j < n_q_blocks:
            qt_ref[0, j * LANES:(j + 1) * LANES, rows] = (r * (QK_SCALE * LOG2_E)).T.astype(BF16)
        else:
            jk = j - n_q_blocks
            kd_ref[0, rows, jk * LANES:(jk + 1) * LANES] = r.astype(BF16)
    z = project(o + 2 * D_QK_C, D_C)
    for j in range(D_C // LANES):
        r0 = j * DA_VT_ROWS
        vt_ref[0, r0:r0 + DA_V_DIM, rows] = z[:, j * LANES:(j + 1) * LANES].T.astype(BF16)
        vt_ref[0, r0 + DA_V_DIM:r0 + DA_VT_ROWS, rows] = jnp.ones((DA_VT_ROWS - DA_V_DIM, tm), BF16)


def _proj_in(x, modt, w_in, cos_t, sin_t, ws_stack, bs_full, gn, gmat, *, layer, n_latent):
    b, t, d = x.shape
    n_sub = next(n for n in FFN_SUB_TILES if (t // TOKEN_TILE) % n == 0)
    tm = n_sub * TOKEN_TILE
    return pl.pallas_call(
        functools.partial(_proj_in_kernel, tm=TOKEN_TILE, n_sub=n_sub, n_latent=n_latent),
        grid=(b, t // tm),
        in_specs=[
            pl.BlockSpec((1, tm, d), lambda i, j: (i, j, 0)),
            pl.BlockSpec((1, 2, N_ADA, d), lambda i, j: (i, 0, 0, 0)),
            _layer_resident((d, D_IN), layer),
            pl.BlockSpec((tm, LANES), lambda i, j: (j, 0)),
            pl.BlockSpec((tm, LANES), lambda i, j: (j, 0)),
            _resident((GM_GROUPS * GM_CHUNK, GM_CHUNK)),
            _resident((GM_CHUNK, D_B)),
            _resident((1, D_B)),
            _resident((D_B, D_B)),
        ],
        out_specs=[
            pl.BlockSpec((1, tm, 2 * D_A), lambda i, j: (i, j, 0)),
            pl.BlockSpec((1, D_A, tm), lambda i, j: (i, 0, j)),
            pl.BlockSpec((1, tm, D_B), lambda i, j: (i, j, 0)),
            pl.BlockSpec((1, tm, D_QK_C), lambda i, j: (i, j, 0)),
            pl.BlockSpec((1, D_QK_C, tm), lambda i, j: (i, 0, j)),
            pl.BlockSpec((1, DA_HEADS * DA_VT_ROWS, tm), lambda i, j: (i, 0, j)),
        ],
        out_shape=[
            jax.ShapeDtypeStruct((b, t, 2 * D_A), BF16),
            jax.ShapeDtypeStruct((b, D_A, t), BF16),
            jax.ShapeDtypeStruct((b, t, D_B), BF16),
            jax.ShapeDtypeStruct((b, t, D_QK_C), BF16),
            jax.ShapeDtypeStruct((b, D_QK_C, t), BF16),
            jax.ShapeDtypeStruct((b, DA_HEADS * DA_VT_ROWS, t), BF16),
        ],
        name="proj_in",
        compiler_params=_params(("arbitrary", "arbitrary")),
    )(x, modt, w_in, cos_t, sin_t, ws_stack, bs_full, gn, gmat)


def _na_kernel(*refs, n_sub, n_latent):
    q_ref, qn_ref, kt_ref, v_ref = refs[:4]
    bias_refs = refs[4:5 + n_sub]
    o_ref = refs[5 + n_sub]
    s_refs = refs[6 + n_sub:]
    assert len(s_refs) == n_sub >= 2
    first = pl.program_id(2) * n_sub
    n_groups = n_latent // ATT_Q_TILE
    rows = n_latent // GRID_W
    lane = lax.broadcasted_iota(jnp.int32, (1, LANES), 1)

    def window_start(grp):
        base_row = jnp.clip(grp * NA_Q_ROWS - NA_WIN_ROWS // 2, 0, rows - NA_K_ROWS)
        return pl.multiple_of(base_row * GRID_W, ATT_Q_TILE)

    def write_scores(s_ref, q, bias, grp):
        qs = jnp.concatenate([jnp.where(lane < HEAD_DIM, q, jnp.zeros_like(q)),
                              jnp.where(lane >= HEAD_DIM, q, jnp.zeros_like(q))], axis=0)
        sw = jnp.dot(qs, kt_ref[0, :, pl.ds(window_start(grp), NA_K_TOK)],
                     preferred_element_type=F32)
        s_ref[:, 0:NA_K_TOK] = sw + bias.reshape(2 * ATT_Q_TILE, NA_K_TOK)
        s_ref[:, NA_K_TOK:] = jnp.dot(qs, kt_ref[0, :, n_latent:], preferred_element_type=F32)

    def softmax_pv(s_ref, grp, out_rows):
        s = s_ref[...]
        p = jnp.exp2(s - s.max(axis=-1, keepdims=True)).astype(BF16)

        def with_ones(v):
            return jnp.concatenate([v, jnp.ones_like(v)], axis=-1)

        o = (jnp.dot(p[:, 0:NA_K_TOK], with_ones(v_ref[0, pl.ds(window_start(grp), NA_K_TOK), :]),
                     preferred_element_type=F32)
             + jnp.dot(p[:, NA_K_TOK:], with_ones(v_ref[0, n_latent:, :]),
                       preferred_element_type=F32))
        o = o[:, 0:LANES] * (1.0 / o[:, LANES:LANES + 1])
        o_ref[0, out_rows, :] = jnp.where(lane < HEAD_DIM, o[:ATT_Q_TILE],
                                          o[ATT_Q_TILE:]).astype(BF16)

    sub_rows = lambda sub: slice(sub * ATT_Q_TILE, (sub + 1) * ATT_Q_TILE)

    @pl.when(first == 0)
    def _():
        write_scores(s_refs[0], q_ref[0, sub_rows(0), :], bias_refs[0][0], first)

    for sub in range(n_sub):
        q_next = q_ref[0, sub_rows(sub + 1), :] if sub + 1 < n_sub else qn_ref[0]
        write_scores(s_refs[(sub + 1) % n_sub], q_next, bias_refs[sub + 1][0],
                     jnp.minimum(first + sub + 1, n_groups))
        softmax_pv(s_refs[sub], first + sub, sub_rows(sub))


def _na_attention(za, kat, bias, *, n_latent, ctx_queries):
    b, t, _ = za.shape
    n_groups = n_latent // ATT_Q_TILE
    n_total = n_groups + 1 if ctx_queries else n_groups
    n_sub = next(n for n in NA_SUB_GROUPS if n_total % n == 0)
    assert t == n_latent + CTX_LEN and CTX_LEN == ATT_Q_TILE
    assert n_groups >= 3 and n_latent // GRID_W >= NA_K_ROWS

    def kind(g):
        return jnp.where(g == 0, 0, jnp.where(g < n_groups - 1, 1, jnp.where(g == n_groups - 1, 2, 3)))

    group_at = lambda j, off: jnp.minimum(j * n_sub + off, n_groups)
    bias_spec = lambda off: pl.BlockSpec((1, 2, ATT_Q_TILE, NA_K_TOK),
                                         lambda i, hp, j: (kind(group_at(j, off)), hp, 0, 0))
    score_buf = pltpu.VMEM((2 * ATT_Q_TILE, NA_K_TOK + CTX_LEN), F32)
    return pl.pallas_call(
        functools.partial(_na_kernel, n_sub=n_sub, n_latent=n_latent),
        grid=(b, NA_HEADS // 2, n_total // n_sub),
        in_specs=[
            pl.BlockSpec((1, n_sub * ATT_Q_TILE, LANES), lambda i, hp, j: (i, j, hp)),
            pl.BlockSpec((1, ATT_Q_TILE, LANES), lambda i, hp, j: (i, group_at(j, n_sub), hp)),
            pl.BlockSpec((1, LANES, t), lambda i, hp, j: (i, hp, 0)),
            pl.BlockSpec((1, t, LANES), lambda i, hp, j: (i, 0, 2 + hp)),
        ] + [bias_spec(off) for off in range(n_sub + 1)],
        out_specs=pl.BlockSpec((1, n_sub * ATT_Q_TILE, LANES), lambda i, hp, j: (i, j, hp)),
        out_shape=jax.ShapeDtypeStruct((b, n_total * ATT_Q_TILE, D_A), BF16),
        scratch_shapes=[score_buf] * n_sub,
        name="na_attention",
        compiler_params=_params(("arbitrary", "arbitrary", "arbitrary")),
    )(za, za, kat, za, *([bias] * (n_sub + 1)))


NA_BIAS_PAD = NA_WIN_ROWS // 2
NA_BIAS_W = 11 * LANES


def _na_bias_plan(n_latent):
    rows = n_latent // GRID_W
    n_groups = n_latent // ATT_Q_TILE
    plan = []
    for g in (0, 1, n_groups - 1):
        base = int(np.clip(g * NA_Q_ROWS - NA_WIN_ROWS // 2, 0, rows - NA_K_ROWS))
        kind = []
        for i in range(NA_Q_ROWS):
            r = g * NA_Q_ROWS + i
            start = int(np.clip(r - NA_WIN_ROWS // 2, 0, rows - NA_WIN_ROWS))
            e0 = base - r + NA_WIN_ROWS - 1 + NA_BIAS_PAD
            assert 0 <= e0 and (e0 - e0 % 2) * GRID_W + NA_K_TOK <= NA_BIAS_W
            kind.append((e0, start - base, start - base + NA_WIN_ROWS - 1))
        plan.append(tuple(kind))
    return tuple(plan)


def _na_bias_kernel(r_ref, o_ref, *, plan):
    shape = (GRID_W, NA_BIAS_W)
    qc = lax.broadcasted_iota(jnp.int32, shape, 0)
    col = lax.broadcasted_iota(jnp.int32, shape, 1)
    kc = col % GRID_W
    dc = kc - qc + (NA_WIN_COLS - 1)
    w0 = jnp.clip(qc - NA_WIN_COLS // 2, 0, GRID_W - NA_WIN_COLS)
    col_ok = (kc >= w0) & (kc < w0 + NA_WIN_COLS)
    tables = []
    for p in range(2):
        a = col // GRID_W + (p - NA_BIAS_PAD)
        acc = jnp.zeros(shape, F32)
        for b in range(2 * NA_WIN_COLS - 1):
            acc = acc + jnp.where(dc == b, r_ref[0, p, b:b + 1, :], 0.0)
        ok = col_ok & (a >= 0) & (a <= 2 * NA_WIN_ROWS - 2)
        tables.append(jnp.where(ok, acc * LOG2_E, NEG_INF))
    kj = lax.broadcasted_iota(jnp.int32, (GRID_W, NA_K_TOK), 1) // GRID_W
    for k, kind in enumerate(plan):
        for i, (e0, jlo, jhi) in enumerate(kind):
            p = e0 % 2
            off = (e0 - p) * GRID_W
            slab = tables[p][:, off:off + NA_K_TOK]
            o_ref[k, 0, i * GRID_W:(i + 1) * GRID_W, :] = jnp.where(
                (kj >= jlo) & (kj <= jhi), slab, NEG_INF)
    o_ref[len(plan), 0] = jnp.full((ATT_Q_TILE, NA_K_TOK), NEG_INF, F32)


def _na_bias_tables(rpb, n_latent):
    h = rpb.shape[0]
    n_blocks = NA_BIAS_W // GRID_W + 2
    rp = jnp.pad(rpb, ((0, 0), (NA_BIAS_PAD, n_blocks - NA_BIAS_PAD - rpb.shape[1]), (0, 0)))
    rep = jnp.transpose(jnp.repeat(rp, GRID_W, axis=1), (0, 2, 1))
    r = jnp.stack([rep[:, :, p * GRID_W:p * GRID_W + NA_BIAS_W] for p in range(2)], axis=1)
    n_dc = rpb.shape[2]
    return pl.pallas_call(
        functools.partial(_na_bias_kernel, plan=_na_bias_plan(n_latent)),
        grid=(h,),
        in_specs=[pl.BlockSpec((1, 2, n_dc, NA_BIAS_W), lambda i: (i, 0, 0, 0))],
        out_specs=pl.BlockSpec((4, 1, ATT_Q_TILE, NA_K_TOK), lambda i: (0, i, 0, 0)),
        out_shape=jax.ShapeDtypeStruct((4, h, ATT_Q_TILE, NA_K_TOK), F32),
        name="na_bias",
        compiler_params=_params(("arbitrary",)),
    )(r)


def _kv_chunk_bounds(t):
    head, tail = DA_KV_RAMP_UP, DA_KV_RAMP_DOWN
    n_slots = DA_PREFETCH + 1
    middle = t - sum(head) - sum(tail)
    units = middle // MXU_DIM_V7X
    n_mid = -(-units // (DA_KV_CHUNK // MXU_DIM_V7X)) if middle > 0 else 0
    n_mid += -(len(head) + n_mid + len(tail)) % n_slots
    if middle < 0 or middle % MXU_DIM_V7X or n_mid > units:
        assert t % DA_KV_SMALL_CHUNK == 0
        sizes = (DA_KV_SMALL_CHUNK,) * (t // DA_KV_SMALL_CHUNK)
    else:
        mid = tuple((units // n_mid + (j < units % n_mid)) * MXU_DIM_V7X for j in range(n_mid))
        sizes = head + mid + tail
    assert max(sizes) <= DA_KV_CHUNK and len(sizes) % n_slots == 0
    return tuple(int(v) for v in np.cumsum((0,) + sizes))


def _diff_kernel(*refs, tq, n_keys, lambda_init, latent):
    if latent:
        qt_ref, qtn_ref, k_ref, vt_ref, lqk_ref, subln_ref, o_ref, s_ref, smax_ref = refs
    else:
        qt_ref, k_ref, vt_ref, lqk_ref, subln_ref, _, o_ref = refs
    n_tiles = qt_ref.shape[2] // tq
    feat = lax.broadcasted_iota(jnp.int32, (LANES, 1), 0)

    def stack_components(qt):
        return jnp.concatenate([jnp.where(feat < DA_QK_DIM, qt, jnp.zeros_like(qt)),
                                jnp.where(feat >= DA_QK_DIM, qt, jnp.zeros_like(qt))], axis=1)

    tile_cols = lambda u: slice(u * tq, (u + 1) * tq)
    lqk = lqk_ref[...]
    lam = (jnp.exp(jnp.sum(lqk[0:1] * lqk[1:2], axis=-1, keepdims=True))
           - jnp.exp(jnp.sum(lqk[2:3] * lqk[3:4], axis=-1, keepdims=True)) + lambda_init)

    def col_reduce(x, op):
        r, c = x.shape
        groups = 8 if r % 64 == 0 else 1
        x = x.reshape(groups, r // (8 * groups), 8, c)
        return op(op(op(x, axis=1), axis=0), axis=0, keepdims=True)

    def scores(k, q):
        s = jnp.dot(k, q, preferred_element_type=F32)
        return s, col_reduce(s, jnp.max)

    def step(s, s_max, vt1, state):
        m, acc = state
        m_new = jnp.maximum(m, s_max)
        alpha = jnp.exp2(m - m_new)
        p = jnp.exp2(s - m_new)
        acc = alpha * acc + jnp.dot(vt1, p.astype(BF16), preferred_element_type=F32)
        return (m_new, acc)

    def init():
        return (jnp.full((1, 2 * tq), NEG_INF, F32), jnp.zeros((DA_VT_ROWS, 2 * tq), F32))

    def finish(state, u):
        _, acc = state
        o = acc[:DA_V_DIM] * (1.0 / acc[DA_V_DIM:DA_V_DIM + 1])
        o = o[:, :tq] - lam * o[:, tq:]
        o = o * lax.rsqrt(jnp.mean(o * o, axis=0, keepdims=True) + EPS)
        o = o * (subln_ref[...] * (1.0 - lambda_init))
        o_ref[0, tile_cols(u), :] = o.T.astype(BF16)

    if not latent:
        s, s_max = scores(k_ref[0], stack_components(qt_ref[0]))
        finish(step(s, s_max, vt_ref[0], init()), 0)
        return

    bounds = _kv_chunk_bounds(n_keys)
    chunk = lambda c: slice(bounds[c], bounds[c + 1])
    size = lambda c: bounds[c + 1] - bounds[c]
    n_chunks = len(bounds) - 1
    ahead = DA_PREFETCH
    n_slots = ahead + 1
    assert n_chunks % n_slots == 0 and n_chunks > ahead
    slot = lambda c: (c % n_slots, slice(0, size(c)))

    @pl.when(pl.program_id(2) == 0)
    def _():
        q0 = stack_components(qt_ref[0, :, tile_cols(0)])
        for c in range(ahead):
            s_ref[slot(c)], smax_ref[c:c + 1, :] = scores(k_ref[0, chunk(c), :], q0)

    maxes = [smax_ref[c:c + 1, :] for c in range(ahead)]
    for u in range(n_tiles):
        qst = stack_components(qt_ref[0, :, tile_cols(u)])
        q_next = stack_components(qt_ref[0, :, tile_cols(u + 1)] if u + 1 < n_tiles
                                  else qtn_ref[0])
        state = init()
        for c in range(n_chunks):
            nxt = c + ahead
            if nxt < n_chunks:
                s_ref[slot(nxt)], new_max = scores(k_ref[0, chunk(nxt), :], qst)
            else:
                s_ref[slot(nxt - n_chunks)], new_max = scores(k_ref[0, chunk(nxt - n_chunks), :],
                                                              q_next)
            state = step(s_ref[slot(c)], maxes[0], vt_ref[0, :, chunk(c)], state)
            maxes = maxes[1:] + [new_max]
        finish(state, u)
    for c in range(ahead):
        smax_ref[c:c + 1, :] = maxes[c]


def _diff_attention(kd, qt, vt, lqk, subln, *, n_latent, lambda_init, ctx_queries):
    b, t, _ = kd.shape
    tq = DA_Q_TILE
    n_q_tiles = n_latent // tq
    n_tiles = next(n for n in DA_TILES_PER_STEP if n_q_tiles % n == 0)
    step_q = n_tiles * tq
    consts = [pl.BlockSpec((4, DA_QK_DIM), lambda *_: (0, 0)),
              pl.BlockSpec((DA_V_DIM, 1), lambda *_: (0, 0))]
    out_rows = t if ctx_queries else n_latent
    oc = pl.pallas_call(
        functools.partial(_diff_kernel, tq=tq, n_keys=t, lambda_init=lambda_init, latent=True),
        grid=(b, DA_HEADS, n_q_tiles // n_tiles),
        in_specs=[
            pl.BlockSpec((1, LANES, step_q), lambda bi, h, i: (bi, h, i)),
            pl.BlockSpec((1, LANES, tq),
                         lambda bi, h, i: (bi, h, jnp.minimum((i + 1) * n_tiles, n_q_tiles - 1))),
            pl.BlockSpec((1, t, LANES), lambda bi, h, i: (bi, 0, h)),
            pl.BlockSpec((1, DA_VT_ROWS, t), lambda bi, h, i: (bi, h, 0)),
        ] + consts,
        out_specs=pl.BlockSpec((1, step_q, LANES), lambda bi, h, i: (bi, i, h)),
        out_shape=jax.ShapeDtypeStruct((b, out_rows, D_C), BF16),
        scratch_shapes=[pltpu.VMEM((DA_PREFETCH + 1, DA_KV_CHUNK, 2 * tq), F32),
                        pltpu.VMEM((DA_PREFETCH, 2 * tq), F32)],
        name="diff_attention",
        compiler_params=_params(("arbitrary", "arbitrary", "arbitrary")),
    )(qt, qt, kd, vt, lqk, subln)
    if not ctx_queries:
        return oc
    ctx_blk = n_latent // CTX_LEN
    return pl.pallas_call(
        functools.partial(_diff_kernel, tq=CTX_LEN, n_keys=CTX_LEN, lambda_init=lambda_init,
                          latent=False),
        grid=(b, DA_HEADS),
        in_specs=[
            pl.BlockSpec((1, LANES, CTX_LEN), lambda bi, h: (bi, h, ctx_blk)),
            pl.BlockSpec((1, CTX_LEN, LANES), lambda bi, h: (bi, ctx_blk, h)),
            pl.BlockSpec((1, DA_VT_ROWS, CTX_LEN), lambda bi, h: (bi, h, ctx_blk)),
        ] + consts + [pl.BlockSpec(memory_space=pl.ANY)],
        out_specs=pl.BlockSpec((1, CTX_LEN, LANES), lambda bi, h: (bi, ctx_blk, h)),
        out_shape=jax.ShapeDtypeStruct((b, out_rows, D_C), BF16),
        input_output_aliases={5: 0},
        name="diff_attention_ctx",
        compiler_params=_params(("arbitrary", "arbitrary")),
    )(qt, kd, vt, lqk, subln, oc)


def _rope_tables(n_latent):
    tok = jnp.arange(n_latent)
    row = (tok // GRID_W).astype(F32)
    col = (tok % GRID_W).astype(F32)
    n_freq = DA_QK_DIM // 4
    freqs = ROPE_BASE ** (-jnp.arange(n_freq, dtype=F32) / n_freq)
    ang = jnp.concatenate([row[:, None] * freqs, col[:, None] * freqs], axis=-1)
    cos, sin = jnp.cos(ang), jnp.sin(ang)
    cos_t = jnp.concatenate([cos, cos, cos, cos], axis=-1)
    sin_t = jnp.concatenate([-sin, sin, -sin, sin], axis=-1)
    cos_t = jnp.concatenate([cos_t, jnp.ones((CTX_LEN, LANES), F32)], axis=0)
    sin_t = jnp.concatenate([sin_t, jnp.zeros((CTX_LEN, LANES), F32)], axis=0)
    return cos_t, sin_t


def kernel(x, c, ctx, c_ctx, w_ada, b_ada, ffn1_w1, ffn1_w3, ffn1_w2, w_in, w_out, na_rpb,
           gm_ws, gm_bs, gm_norm, da_lq1, da_lk1, da_lq2, da_lk2, da_subln,
           ffn2_w1, ffn2_w3, ffn2_w2, final_norm):
    b, n_latent, d = x.shape
    assert d == D_MODEL and ctx.shape[1] == CTX_LEN and b < ADA_ROWS
    assert n_latent % DA_Q_TILE == 0

    cvec = jnp.zeros((ADA_ROWS, d), F32).at[:b].set(c).at[b].set(c_ctx)
    mod = _ada(cvec, w_ada, b_ada).reshape(DEPTH, ADA_ROWS, N_ADA, d)
    cos_t, sin_t = _rope_tables(n_latent)
    gmat = jnp.asarray(np.kron(np.eye(GM_GROUPS), np.ones((GM_WIDTH, GM_WIDTH))), BF16)

    bf16 = lambda w: w.astype(BF16)
    ffn1 = (bf16(ffn1_w1), bf16(ffn1_w3), bf16(ffn1_w2))
    ffn2 = (bf16(ffn2_w1), bf16(ffn2_w3), bf16(ffn2_w2))
    w_in, w_out = bf16(w_in), bf16(w_out)

    xs = x
    for l in range(DEPTH):
        last = l == DEPTH - 1
        lambda_init = 0.8 - 0.6 * math.exp(-0.3 * l)
        modt = jnp.stack([mod[l, :b], jnp.broadcast_to(mod[l, b], (b, N_ADA, d))], axis=1)
        bias = _na_bias_tables(na_rpb[l], n_latent)
        ws_stack = gm_ws[l].reshape(GM_GROUPS * GM_CHUNK, GM_CHUNK).astype(BF16)
        bs_full = jnp.repeat(gm_bs[l].T, GM_WIDTH, axis=1)
        gn = gm_norm[l].reshape(1, D_B)
        lqk = jnp.stack([da_lq1[l], da_lk1[l], da_lq2[l], da_lk2[l]])

        xs = _ffn(xs, modt, *ffn1, layer=l, n_latent=n_latent, mi=0, ctx=ctx if l == 0 else None)
        za, kat, ob, kd, qt, vt = _proj_in(xs, modt, w_in, cos_t, sin_t, ws_stack, bs_full, gn,
                                           gmat, layer=l, n_latent=n_latent)
        oa = _na_attention(za, kat, bias, n_latent=n_latent, ctx_queries=not last)
        oc = _diff_attention(kd, qt, vt, lqk, da_subln[l].reshape(DA_V_DIM, 1),
                             n_latent=n_latent, lambda_init=lambda_init, ctx_queries=not last)
        xs = _ffn(xs, modt, *ffn2, layer=l, n_latent=n_latent, mi=6, mix=(oa, ob, oc, w_out),
                  final_gain=final_norm if last else None)
    return xs
```
